```python
import math
import jax, jax.numpy as jnp
from jax import lax
import numpy as np

D_MODEL = 1024
BATCH = 8
SEQ = 4096
DEPTH = 4

CONV_WIDTH = D_MODEL
CONV_K = 3
POOL_WIDTH = D_MODEL
POOL_GROUPS = 4
POOL_WINDOWS = (2, 4, 8, 16)
N_HEADS = 16
N_KV_HEADS = 4
HEAD_DIM = D_MODEL // N_HEADS
WINDOW = 128
BLOCK = 128
N_BUCKETS = 32
MAX_DISTANCE = 128
N_BRANCHES = 3
D_FF = -(-8 * D_MODEL // (3 * 256)) * 256

EPS = 1e-6
NEG_INF = -1e30

Q_WIDTH = N_HEADS * HEAD_DIM
KV_WIDTH = N_KV_HEADS * HEAD_DIM
IN_SIZES = (CONV_WIDTH, CONV_WIDTH, CONV_WIDTH, POOL_WIDTH, Q_WIDTH, KV_WIDTH, KV_WIDTH,
            D_MODEL, D_MODEL, D_MODEL)
IN_TOTAL = sum(IN_SIZES)

kernel_name = "hybrid_conv_pool_swa_encoder"


def rms_norm(x, g):
    xf = x.astype(jnp.float32)
    y = xf * lax.rsqrt(jnp.mean(xf * xf, axis=-1, keepdims=True) + EPS)
    return (y * g.astype(jnp.float32)).astype(x.dtype)


def split_points():
    pts, acc = [], 0
    for s in IN_SIZES[:-1]:
        acc += s
        pts.append(acc)
    return pts


def t5_bucket(rel):
    half = N_BUCKETS // 2
    max_exact = half // 2
    ret = jnp.where(rel > 0, half, 0)
    n = jnp.abs(rel)
    nf = jnp.maximum(n, 1).astype(jnp.float32)
    large = max_exact + (jnp.log(nf / max_exact) / math.log(MAX_DISTANCE / max_exact)
                         * (half - max_exact)).astype(jnp.int32)
    large = jnp.minimum(large, half - 1)
    return ret + jnp.where(n < max_exact, n, large)


def short_conv_mixer(b_gate, c_gate, xin, conv_w, w_out):
    u = c_gate * xin
    y = lax.conv_general_dilated(u, conv_w, window_strides=(1,),
                                 padding=[(CONV_K // 2, CONV_K // 2)],
                                 dimension_numbers=("NWC", "WIO", "NWC"),
                                 feature_group_count=u.shape[-1])
    return (b_gate * y) @ w_out


def multiscale_pool_mixer(u, w_pool, pool_scale):
    B, S, W = u.shape
    cg = W // POOL_GROUPS
    uf = u.astype(jnp.float32).reshape(B, S, POOL_GROUPS, cg)
    cs = jnp.pad(jnp.cumsum(uf, axis=1), ((0, 0), (1, 0), (0, 0), (0, 0)))
    t = jnp.arange(S)
    outs = []
    for gi, w in enumerate(POOL_WINDOWS):
        lo = jnp.maximum(t - w // 2, 0)
        hi = jnp.minimum(t + (w - 1 - w // 2), S - 1)
        csg = cs[:, :, gi]
        s = jnp.take(csg, hi + 1, axis=1) - jnp.take(csg, lo, axis=1)
        cnt = (hi - lo + 1).astype(jnp.float32)[None, :, None]
        outs.append(s / cnt - uf[:, :, gi])
    p = jnp.stack(outs, axis=2).astype(u.dtype)
    y = jnp.einsum("bsgc,gcd->bsgd", p, w_pool).reshape(B, S, W)
    return y * pool_scale


def windowed_gqa(q, k, v, rel_bias, sink):
    B, S, _ = q.shape
    nb = S // BLOCK
    G = N_HEADS // N_KV_HEADS
    qb = q.reshape(B, nb, BLOCK, N_KV_HEADS, G, HEAD_DIM)
    pad = ((0, 0), (BLOCK, BLOCK), (0, 0))
    kp = jnp.pad(k, pad).reshape(B, nb + 2, BLOCK, N_KV_HEADS, HEAD_DIM)
    vp = jnp.pad(v, pad).reshape(B, nb + 2, BLOCK, N_KV_HEADS, HEAD_DIM)
    kb = jnp.concatenate([kp[:, :-2], kp[:, 1:-1], kp[:, 2:]], axis=2)
    vb = jnp.concatenate([vp[:, :-2], vp[:, 1:-1], vp[:, 2:]], axis=2)
    scores = jnp.einsum("bnqhgd,bnkhd->bhgnqk", qb, kb,
                        preferred_element_type=jnp.float32) * (HEAD_DIM ** -0.5)
    qi = jnp.arange(BLOCK)[:, None]
    kj = jnp.arange(3 * BLOCK)[None, :]
    rel = kj - BLOCK - qi
    bias = rel_bias[t5_bucket(rel)].astype(jnp.float32)
    bias = jnp.transpose(bias, (2, 0, 1)).reshape(N_KV_HEADS, G, 1, BLOCK, 3 * BLOCK)
    kabs = jnp.arange(nb)[:, None, None] * BLOCK + kj[None] - BLOCK
    valid = (jnp.abs(rel)[None] <= WINDOW) & (kabs >= 0) & (kabs < S)
    scores = jnp.where(valid, scores + bias, NEG_INF)
    sink_l = sink.astype(jnp.float32).reshape(N_KV_HEADS, G, 1, 1, 1)
    m = jnp.maximum(jnp.max(scores, axis=-1, keepdims=True), sink_l)
    p = jnp.exp(scores - m)
    denom = jnp.sum(p, axis=-1, keepdims=True) + jnp.exp(sink_l - m)
    p = (p / denom).astype(v.dtype)
    out = jnp.einsum("bhgnqk,bnkhd->bnqhgd", p, vb)
    return out.reshape(B, S, N_HEADS * HEAD_DIM)


def hybrid_layer(x, w_in, conv_w, w_a_out, w_pool, pool_scale, w_attn_out, sink, w_o,
                 g_mix, g_ffn, w_gu, w_down, rel_bias):
    h = rms_norm(x, g_mix)
    proj = h @ w_in
    b_a, c_a, x_a, u_p, q, k, v, ga, gp, gt = jnp.split(proj, split_points(), axis=-1)
    y_a = short_conv_mixer(b_a, c_a, x_a, conv_w, w_a_out)
    y_p = multiscale_pool_mixer(u_p, w_pool, pool_scale)
    y_t = windowed_gqa(q, k, v, rel_bias, sink) @ w_attn_out
    merged = jax.nn.sigmoid(ga) * y_a + jax.nn.sigmoid(gp) * y_p + jax.nn.sigmoid(gt) * y_t
    x = x + merged @ w_o
    h2 = rms_norm(x, g_ffn)
    gate, up = jnp.split(h2 @ w_gu, [D_FF], axis=-1)
    return x + (jax.nn.silu(gate) * up) @ w_down


def _fwd_setup_inputs(seed: int = 0) -> dict:
    key = jax.random.key(seed)
    ks = jax.random.split(key, 16)
    nrm = lambda k, shape, scale: jax.random.normal(k, shape, jnp.float32) * scale
    cg = POOL_WIDTH // POOL_GROUPS
    return {
        "x": nrm(ks[0], (BATCH, SEQ, D_MODEL), 1.0),
        "w_in": nrm(ks[1], (DEPTH, D_MODEL, IN_TOTAL), D_MODEL ** -0.5),
        "conv_w": nrm(ks[2], (DEPTH, CONV_K, 1, CONV_WIDTH), CONV_K ** -0.5),
        "w_a_out": nrm(ks[3], (DEPTH, CONV_WIDTH, D_MODEL), CONV_WIDTH ** -0.5),
        "w_pool": nrm(ks[4], (DEPTH, POOL_GROUPS, cg, cg), cg ** -0.5),
        "pool_scale": 1.0 + nrm(ks[5], (DEPTH, POOL_WIDTH), 0.02),
        "w_attn_out": nrm(ks[6], (DEPTH, Q_WIDTH, D_MODEL), Q_WIDTH ** -0.5),
        "attn_sink": nrm(ks[7], (DEPTH, N_HEADS), 0.5),
        "w_o": nrm(ks[8], (DEPTH, D_MODEL, D_MODEL), D_MODEL ** -0.5),
        "g_mix": 1.0 + nrm(ks[9], (DEPTH, D_MODEL), 0.02),
        "g_ffn": 1.0 + nrm(ks[10], (DEPTH, D_MODEL), 0.02),
        "w_gu": nrm(ks[11], (DEPTH, D_MODEL, 2 * D_FF), D_MODEL ** -0.5),
        "w_down": nrm(ks[12], (DEPTH, D_FF, D_MODEL), D_FF ** -0.5),
        "rel_bias": nrm(ks[13], (N_BUCKETS, N_HEADS), 0.5),
        "g_final": 1.0 + nrm(ks[14], (D_MODEL,), 0.02),
    }


def _fwd_reference(x, w_in, conv_w, w_a_out, w_pool, pool_scale, w_attn_out, attn_sink, w_o,
              g_mix, g_ffn, w_gu, w_down, rel_bias, g_final):
    for layer in range(DEPTH):
        x = hybrid_layer(x, w_in[layer], conv_w[layer], w_a_out[layer], w_pool[layer],
                         pool_scale[layer], w_attn_out[layer], attn_sink[layer], w_o[layer],
                         g_mix[layer], g_ffn[layer], w_gu[layer], w_down[layer], rel_bias)
    return rms_norm(x, g_final)


import jax as _jax
import jax.numpy as _jnp

TWIN_FORMAT = 'train_step'
FWD_PARAMS = ['x', 'w_in', 'conv_w', 'w_a_out', 'w_pool', 'pool_scale', 'w_attn_out', 'attn_sink', 'w_o', 'g_mix', 'g_ffn', 'w_gu', 'w_down', 'rel_bias', 'g_final']
TWIN_WEIGHTS = ['w_in', 'conv_w', 'w_a_out', 'w_pool', 'pool_scale', 'w_attn_out', 'attn_sink', 'w_o', 'g_mix', 'g_ffn', 'w_gu', 'w_down', 'rel_bias', 'g_final']
TWIN_DIFF_INPUT = 'x'
TWIN_INPUTS = ['x', 'w_in', 'conv_w', 'w_a_out', 'w_pool', 'pool_scale', 'w_attn_out', 'attn_sink', 'w_o', 'g_mix', 'g_ffn', 'w_gu', 'w_down', 'rel_bias', 'g_final', 'loss_target', 'm_w_in', 'm_conv_w', 'm_w_a_out', 'm_w_pool', 'm_pool_scale', 'm_w_attn_out', 'm_attn_sink', 'm_w_o', 'm_g_mix', 'm_g_ffn', 'm_w_gu', 'm_w_down', 'm_rel_bias', 'm_g_final', 'v_w_in', 'v_conv_w', 'v_w_a_out', 'v_w_pool', 'v_pool_scale', 'v_w_attn_out', 'v_attn_sink', 'v_w_o', 'v_g_mix', 'v_g_ffn', 'v_w_gu', 'v_w_down', 'v_rel_bias', 'v_g_final']
TWIN_OUTPUTS = ['loss', 'grad_x', 'grad_w_in', 'grad_conv_w', 'grad_w_a_out', 'grad_w_pool', 'grad_pool_scale', 'grad_w_attn_out', 'grad_attn_sink', 'grad_w_o', 'grad_g_mix', 'grad_g_ffn', 'grad_w_gu', 'grad_w_down', 'grad_rel_bias', 'grad_g_final', 'delta_w_in', 'delta_conv_w', 'delta_w_a_out', 'delta_w_pool', 'delta_pool_scale', 'delta_w_attn_out', 'delta_attn_sink', 'delta_w_o', 'delta_g_mix', 'delta_g_ffn', 'delta_w_gu', 'delta_w_down', 'delta_rel_bias', 'delta_g_final', 'new_m_w_in', 'new_m_conv_w', 'new_m_w_a_out', 'new_m_w_pool', 'new_m_pool_scale', 'new_m_w_attn_out', 'new_m_attn_sink', 'new_m_w_o', 'new_m_g_mix', 'new_m_g_ffn', 'new_m_w_gu', 'new_m_w_down', 'new_m_rel_bias', 'new_m_g_final', 'new_v_w_in', 'new_v_conv_w', 'new_v_w_a_out', 'new_v_w_pool', 'new_v_pool_scale', 'new_v_w_attn_out', 'new_v_attn_sink', 'new_v_w_o', 'new_v_g_mix', 'new_v_g_ffn', 'new_v_w_gu', 'new_v_w_down', 'new_v_rel_bias', 'new_v_g_final']
TWIN_LEAF_KINDS = {'loss': 'loss', 'grad_x': 'grad_x', 'grad_w_in': 'grad_w', 'grad_conv_w': 'grad_w', 'grad_w_a_out': 'grad_w', 'grad_w_pool': 'grad_w', 'grad_pool_scale': 'grad_w', 'grad_w_attn_out': 'grad_w', 'grad_attn_sink': 'grad_w', 'grad_w_o': 'grad_w', 'grad_g_mix': 'grad_w', 'grad_g_ffn': 'grad_w', 'grad_w_gu': 'grad_w', 'grad_w_down': 'grad_w', 'grad_rel_bias': 'grad_w', 'grad_g_final': 'grad_w', 'delta_w_in': 'delta_w', 'delta_conv_w': 'delta_w', 'delta_w_a_out': 'delta_w', 'delta_w_pool': 'delta_w', 'delta_pool_scale': 'delta_w', 'delta_w_attn_out': 'delta_w', 'delta_attn_sink': 'delta_w', 'delta_w_o': 'delta_w', 'delta_g_mix': 'delta_w', 'delta_g_ffn': 'delta_w', 'delta_w_gu': 'delta_w', 'delta_w_down': 'delta_w', 'delta_rel_bias': 'delta_w', 'delta_g_final': 'delta_w', 'new_m_w_in': 'new_m', 'new_m_conv_w': 'new_m', 'new_m_w_a_out': 'new_m', 'new_m_w_pool': 'new_m', 'new_m_pool_scale': 'new_m', 'new_m_w_attn_out': 'new_m', 'new_m_attn_sink': 'new_m', 'new_m_w_o': 'new_m', 'new_m_g_mix': 'new_m', 'new_m_g_ffn': 'new_m', 'new_m_w_gu': 'new_m', 'new_m_w_down': 'new_m', 'new_m_rel_bias': 'new_m', 'new_m_g_final': 'new_m', 'new_v_w_in': 'new_v', 'new_v_conv_w': 'new_v', 'new_v_w_a_out': 'new_v', 'new_v_w_pool': 'new_v', 'new_v_pool_scale': 'new_v', 'new_v_w_attn_out': 'new_v', 'new_v_attn_sink': 'new_v', 'new_v_w_o': 'new_v', 'new_v_g_mix': 'new_v', 'new_v_g_ffn': 'new_v', 'new_v_w_gu': 'new_v', 'new_v_w_down': 'new_v', 'new_v_rel_bias': 'new_v', 'new_v_g_final': 'new_v'}


def _forward(args):
    return _fwd_reference(*[args[k] for k in FWD_PARAMS])


def _output_shape():
    def fwd():
        inp = _fwd_setup_inputs(0)
        return _fwd_reference(*[inp[k] for k in FWD_PARAMS])
    out = _jax.eval_shape(fwd)
    return out.shape, out.dtype

N_MICROBATCH = 1
ADAM_LR = 0.001
ADAM_B1 = 0.9
ADAM_B2 = 0.999
ADAM_EPS = 1e-08
ADAM_WD = 0.01
ADAM_STEP = 10
PER_EXAMPLE_BATCH_AXIS = {'x': 0, 'loss_target': 0}
SHARED_INPUTS = []
_WEIGHT_DTYPES = {'w_in': _jnp.float32, 'conv_w': _jnp.float32, 'w_a_out': _jnp.float32, 'w_pool': _jnp.float32, 'pool_scale': _jnp.float32, 'w_attn_out': _jnp.float32, 'attn_sink': _jnp.float32, 'w_o': _jnp.float32, 'g_mix': _jnp.float32, 'g_ffn': _jnp.float32, 'w_gu': _jnp.float32, 'w_down': _jnp.float32, 'rel_bias': _jnp.float32, 'g_final': _jnp.float32}
MOMENT_SCALE = {'w_in': 6.183811e-02, 'conv_w': 8.846165e-02, 'w_a_out': 8.909513e-02, 'w_pool': 7.899286e-02, 'pool_scale': 7.777825e-02, 'w_attn_out': 1.186500e-02, 'attn_sink': 4.567152e-04, 'w_o': 1.194164e-01, 'g_mix': 1.799731e-01, 'g_ffn': 1.202743e-01, 'w_gu': 5.107218e-02, 'w_down': 8.338691e-02, 'rel_bias': 2.840694e-02, 'g_final': 3.200687e+01}


def _to_microbatches(a, axis):
    t = _jnp.moveaxis(a, axis, 0)
    t = t.reshape((N_MICROBATCH, t.shape[0] // N_MICROBATCH) + t.shape[1:])
    return _jnp.moveaxis(t, 1, axis + 1)


def setup_inputs(seed: int = 0) -> dict:
    inp = _fwd_setup_inputs(seed)
    key = _jax.random.fold_in(_jax.random.key(seed), 7919)
    shape, _ = _output_shape()
    out = dict(inp)
    out["loss_target"] = _jax.random.normal(_jax.random.fold_in(key, 0), shape, _jnp.float32)
    for i, name in enumerate(TWIN_WEIGHTS):
        w = inp[name].astype(_jnp.float32)
        if MOMENT_SCALE is None:
            s = _jnp.sqrt(_jnp.mean(_jnp.square(w)) + 1e-30)
        else:
            s = MOMENT_SCALE[name]
        km, kv = _jax.random.split(_jax.random.fold_in(key, i + 1))
        out[name] = w
        out["m_" + name] = s * _jax.random.normal(km, w.shape, _jnp.float32)
        out["v_" + name] = (s * s) * _jax.random.uniform(kv, w.shape, _jnp.float32, 0.5, 1.5)
    if N_MICROBATCH > 1:
        for name, axis in PER_EXAMPLE_BATCH_AXIS.items():
            out[name] = _to_microbatches(out[name], axis)
    return {'x': out['x'], 'w_in': out['w_in'], 'conv_w': out['conv_w'], 'w_a_out': out['w_a_out'], 'w_pool': out['w_pool'], 'pool_scale': out['pool_scale'], 'w_attn_out': out['w_attn_out'], 'attn_sink': out['attn_sink'], 'w_o': out['w_o'], 'g_mix': out['g_mix'], 'g_ffn': out['g_ffn'], 'w_gu': out['w_gu'], 'w_down': out['w_down'], 'rel_bias': out['rel_bias'], 'g_final': out['g_final'], 'loss_target': out['loss_target'], 'm_w_in': out['m_w_in'], 'm_conv_w': out['m_conv_w'], 'm_w_a_out': out['m_w_a_out'], 'm_w_pool': out['m_w_pool'], 'm_pool_scale': out['m_pool_scale'], 'm_w_attn_out': out['m_w_attn_out'], 'm_attn_sink': out['m_attn_sink'], 'm_w_o': out['m_w_o'], 'm_g_mix': out['m_g_mix'], 'm_g_ffn': out['m_g_ffn'], 'm_w_gu': out['m_w_gu'], 'm_w_down': out['m_w_down'], 'm_rel_bias': out['m_rel_bias'], 'm_g_final': out['m_g_final'], 'v_w_in': out['v_w_in'], 'v_conv_w': out['v_conv_w'], 'v_w_a_out': out['v_w_a_out'], 'v_w_pool': out['v_w_pool'], 'v_pool_scale': out['v_pool_scale'], 'v_w_attn_out': out['v_w_attn_out'], 'v_attn_sink': out['v_attn_sink'], 'v_w_o': out['v_w_o'], 'v_g_mix': out['v_g_mix'], 'v_g_ffn': out['v_g_ffn'], 'v_w_gu': out['v_w_gu'], 'v_w_down': out['v_w_down'], 'v_rel_bias': out['v_rel_bias'], 'v_g_final': out['v_g_final']}


def _loss(weights, diff, rest, loss_target):
    with _jax.named_scope("forward"):
        args = {**rest, TWIN_DIFF_INPUT: diff, **{k: w.astype(_WEIGHT_DTYPES[k]) for k, w in weights.items()}}
        y = _forward(args)
    with _jax.named_scope("loss_head"):
        err = _jnp.square(y.astype(_jnp.float32) - loss_target)
        return 0.5 * _jnp.sum(_jnp.mean(err, axis=-1)) if err.ndim else 0.5 * err


def _adamw(w, g, m, v):
    m = ADAM_B1 * m + (1.0 - ADAM_B1) * g
    v = ADAM_B2 * v + (1.0 - ADAM_B2) * _jnp.square(g)
    m_hat = m / (1.0 - ADAM_B1 ** ADAM_STEP)
    v_hat = v / (1.0 - ADAM_B2 ** ADAM_STEP)
    delta = -ADAM_LR * (m_hat / (_jnp.sqrt(v_hat) + ADAM_EPS) + ADAM_WD * w)
    return delta, m, v


def reference(x, w_in, conv_w, w_a_out, w_pool, pool_scale, w_attn_out, attn_sink, w_o, g_mix, g_ffn, w_gu, w_down, rel_bias, g_final, loss_target, m_w_in, m_conv_w, m_w_a_out, m_w_pool, m_pool_scale, m_w_attn_out, m_attn_sink, m_w_o, m_g_mix, m_g_ffn, m_w_gu, m_w_down, m_rel_bias, m_g_final, v_w_in, v_conv_w, v_w_a_out, v_w_pool, v_pool_scale, v_w_attn_out, v_attn_sink, v_w_o, v_g_mix, v_g_ffn, v_w_gu, v_w_down, v_rel_bias, v_g_final):
    given = dict(x=x, w_in=w_in, conv_w=conv_w, w_a_out=w_a_out, w_pool=w_pool, pool_scale=pool_scale, w_attn_out=w_attn_out, attn_sink=attn_sink, w_o=w_o, g_mix=g_mix, g_ffn=g_ffn, w_gu=w_gu, w_down=w_down, rel_bias=rel_bias, g_final=g_final, loss_target=loss_target, m_w_in=m_w_in, m_conv_w=m_conv_w, m_w_a_out=m_w_a_out, m_w_pool=m_w_pool, m_pool_scale=m_pool_scale, m_w_attn_out=m_w_attn_out, m_attn_sink=m_attn_sink, m_w_o=m_w_o, m_g_mix=m_g_mix, m_g_ffn=m_g_ffn, m_w_gu=m_w_gu, m_w_down=m_w_down, m_rel_bias=m_rel_bias, m_g_final=m_g_final, v_w_in=v_w_in, v_conv_w=v_conv_w, v_w_a_out=v_w_a_out, v_w_pool=v_w_pool, v_pool_scale=v_pool_scale, v_w_attn_out=v_w_attn_out, v_attn_sink=v_attn_sink, v_w_o=v_w_o, v_g_mix=v_g_mix, v_g_ffn=v_g_ffn, v_w_gu=v_w_gu, v_w_down=v_w_down, v_rel_bias=v_rel_bias, v_g_final=v_g_final)
    weights = {n: given[n] for n in TWIN_WEIGHTS}
    shared = {n: given[n] for n in SHARED_INPUTS}
    per_example = {n: given[n] for n in ['x']}
    grad_fn = _jax.value_and_grad(_loss, argnums=(0, 1))

    def one_microbatch(ex, loss_target):
        ex = dict(ex)
        diff = ex.pop(TWIN_DIFF_INPUT)
        return grad_fn(weights, diff, {**shared, **ex}, loss_target)

    if N_MICROBATCH == 1:
        loss, (grad_w, grad_x) = one_microbatch(per_example, given["loss_target"])
    else:
        def body(carry, xs):
            loss_sum, grad_sum = carry
            l_k, (gw_k, gx_k) = one_microbatch(xs[0], xs[1])
            with _jax.named_scope("update"):
                return (loss_sum + l_k, _jax.tree.map(_jnp.add, grad_sum, gw_k)), gx_k

        init = (_jnp.zeros((), _jnp.float32), _jax.tree.map(_jnp.zeros_like, weights))
        (loss, grad_w), grad_x = _jax.lax.scan(body, init, (per_example, given["loss_target"]))
    with _jax.named_scope("update"):
        delta_w, new_m, new_v = {}, {}, {}
        for n in TWIN_WEIGHTS:
            delta_w[n], new_m[n], new_v[n] = _adamw(weights[n], grad_w[n], given["m_" + n], given["v_" + n])
    return (loss, grad_x, *[grad_w[n] for n in TWIN_WEIGHTS], *[delta_w[n] for n in TWIN_WEIGHTS],
            *[new_m[n] for n in TWIN_WEIGHTS], *[new_v[n] for n in TWIN_WEIGHTS])
```

```python
import functools
import math

import numpy as np
import jax
import jax.numpy as jnp
from jax import lax
from jax.experimental import pallas as pl
from jax.experimental.pallas import tpu as pltpu

F32 = jnp.float32
BF16 = jnp.bfloat16

D_MODEL = 1024
DEPTH = 4
N_HEADS = 16
N_KV_HEADS = 4
HEAD_DIM = 64
GQA = N_HEADS // N_KV_HEADS
WINDOW = 128
BLOCK = 128
N_BUCKETS = 32
MAX_DISTANCE = 128
POOL_GROUPS = 4
POOL_CG = D_MODEL // POOL_GROUPS
POOL_WINDOWS = (2, 4, 8, 16)
D_FF = 2816
IN_TOTAL = 8704
OFF_B, OFF_C, OFF_X, OFF_U, OFF_Q, OFF_K, OFF_V, OFF_GA, OFF_GP, OFF_GT = (
    0, 1024, 2048, 3072, 4096, 5120, 5376, 5632, 6656, 7680)
EPS = 1e-6
NEG_INF = -1e30
SM_SCALE = HEAD_DIM ** -0.5

ADAM_LR = 0.001
ADAM_B1 = 0.9
ADAM_B2 = 0.999
ADAM_EPS = 1e-08
ADAM_WD = 0.01
ADAM_STEP = 10

N_CHIPS = 4
HALO = 8
V7X_VMEM_LIMIT = 56 * 1024 * 1024
MESH = pl.DeviceIdType.MESH
ANY = pl.BlockSpec(memory_space=pl.ANY)


def _params(*sem):
    return pltpu.CompilerParams(dimension_semantics=tuple(sem) if sem else None,
                                vmem_limit_bytes=V7X_VMEM_LIMIT)


def _tile(n, pref):
    t = min(pref, n)
    while n % t or t % 128:
        t -= 128
    return t


def _nt(a, b):
    return lax.dot_general(a, b, (((1,), (1,)), ((), ())), preferred_element_type=F32)


def _tn(a, b):
    return lax.dot_general(a, b, (((0,), (0,)), ((), ())), preferred_element_type=F32)


def _nn(a, b):
    return jnp.dot(a, b, preferred_element_type=F32)


def _sigmoid(v):
    return 1.0 / (1.0 + jnp.exp(-v))


def _norm_matmul(x, g, w, name):
    S, Dm = x.shape
    N = w.shape[1]
    tm, tn = _tile(S, 512), _tile(N, 512)

    def body(x_ref, g_ref, w_ref, h_ref, o_ref):
        @pl.when(pl.program_id(1) == 0)
        def _():
            xv = x_ref[...]
            r = lax.rsqrt(jnp.mean(xv * xv, axis=-1, keepdims=True) + EPS)
            h_ref[...] = (xv * r * g_ref[...]).astype(BF16)
        o_ref[...] = _nn(h_ref[...], w_ref[...]).astype(BF16)

    return pl.pallas_call(
        body, name=name, grid=(S // tm, N // tn),
        in_specs=[pl.BlockSpec((tm, Dm), lambda i, j: (i, 0)),
                  pl.BlockSpec((1, Dm), lambda i, j: (0, 0)),
                  pl.BlockSpec((Dm, tn), lambda i, j: (0, j))],
        out_specs=[pl.BlockSpec((tm, Dm), lambda i, j: (i, 0)),
                   pl.BlockSpec((tm, tn), lambda i, j: (i, j))],
        out_shape=[jax.ShapeDtypeStruct((S, Dm), BF16), jax.ShapeDtypeStruct((S, N), BF16)],
        compiler_params=_params("parallel", "arbitrary"),
    )(x, g, w)


CB = 128


def _fill_padded(pad_ref, v, S):
    z = jnp.zeros((HALO, v.shape[1]), F32)
    pad_ref[pl.ds(0, HALO), :] = z
    pad_ref[pl.ds(S + HALO, HALO), :] = z
    pad_ref[pl.ds(HALO, S), :] = v


def _shifted(pad_ref, off, S):
    return pad_ref[pl.ds(HALO + off, S), :]


def _conv_fwd(proj, cw8):
    S = proj.shape[0]
    nblk = D_MODEL // CB

    def body(b_ref, c_ref, x_ref, w_ref, o_ref, pad):
        u = c_ref[...].astype(F32) * x_ref[...].astype(F32)
        _fill_padded(pad, u, S)
        cv = w_ref[0:1, :] * _shifted(pad, -1, S) + w_ref[1:2, :] * u + w_ref[2:3, :] * _shifted(pad, 1, S)
        o_ref[...] = (b_ref[...].astype(F32) * cv).astype(BF16)

    col = lambda base: pl.BlockSpec((S, CB), lambda j: (0, base // CB + j))
    return pl.pallas_call(
        body, name="conv_fwd", grid=(nblk,),
        in_specs=[col(OFF_B), col(OFF_C), col(OFF_X), pl.BlockSpec((8, CB), lambda j: (0, j))],
        out_specs=pl.BlockSpec((S, CB), lambda j: (0, j)),
        out_shape=jax.ShapeDtypeStruct((S, D_MODEL), BF16),
        scratch_shapes=[pltpu.VMEM((S + 2 * HALO, CB), F32)],
        compiler_params=_params("parallel"),
    )(proj, proj, proj, cw8)


def _pool_count(S, lo, hi):
    t = lax.broadcasted_iota(jnp.int32, (S, CB), 0)
    return (jnp.minimum(t + hi, S - 1) - jnp.maximum(t - lo, 0) + 1).astype(F32)


def _pool_fwd(proj):
    S = proj.shape[0]
    nblk = D_MODEL // CB
    per_group = POOL_CG // CB

    def body(u_ref, o_ref, pad):
        u = u_ref[...].astype(F32)
        _fill_padded(pad, u, S)
        grp = pl.program_id(0) // per_group
        for gi, w in enumerate(POOL_WINDOWS):
            @pl.when(grp == gi)
            def _(w=w):
                lo, hi = w // 2, w - 1 - w // 2
                acc = _shifted(pad, -lo, S)
                for off in range(-lo + 1, hi + 1):
                    acc = acc + _shifted(pad, off, S)
                o_ref[...] = (acc / _pool_count(S, lo, hi) - u).astype(BF16)

    return pl.pallas_call(
        body, name="pool_fwd", grid=(nblk,),
        in_specs=[pl.BlockSpec((S, CB), lambda j: (0, OFF_U // CB + j))],
        out_specs=pl.BlockSpec((S, CB), lambda j: (0, j)),
        out_shape=jax.ShapeDtypeStruct((S, D_MODEL), BF16),
        scratch_shapes=[pltpu.VMEM((S + 2 * HALO, CB), F32)],
        compiler_params=_params("parallel"),
    )(proj)


def _attn_specs(S):
    nb = S // BLOCK
    kcol, vcol = OFF_K // (N_KV_HEADS * HEAD_DIM), OFF_V // (N_KV_HEADS * HEAD_DIM)
    kvw = N_KV_HEADS * HEAD_DIM
    prev = lambda i: jnp.maximum(i - 1, 0)
    nxt = lambda i: jnp.minimum(i + 1, nb - 1)
    return [
        pl.BlockSpec((BLOCK, D_MODEL), lambda i: (i, OFF_Q // D_MODEL)),
        pl.BlockSpec((BLOCK, kvw), lambda i: (prev(i), kcol)),
        pl.BlockSpec((BLOCK, kvw), lambda i: (i, kcol)),
        pl.BlockSpec((BLOCK, kvw), lambda i: (nxt(i), kcol)),
        pl.BlockSpec((BLOCK, kvw), lambda i: (prev(i), vcol)),
        pl.BlockSpec((BLOCK, kvw), lambda i: (i, vcol)),
        pl.BlockSpec((BLOCK, kvw), lambda i: (nxt(i), vcol)),
    ]


def _heads_rows(ref_or_val, hk):
    return jnp.concatenate(
        [ref_or_val[:, (GQA * hk + g) * HEAD_DIM:(GQA * hk + g + 1) * HEAD_DIM] for g in range(GQA)], axis=0)


def _kv_rows(p_ref, c_ref, n_ref, hk):
    sl = slice(hk * HEAD_DIM, (hk + 1) * HEAD_DIM)
    return jnp.concatenate([p_ref[:, sl], c_ref[:, sl], n_ref[:, sl]], axis=0)


def _softmax_parts(q4, kc, bias_blk, sink_blk, in_range):
    s = _nt(q4, kc) * SM_SCALE + bias_blk
    s = jnp.where(in_range, s, NEG_INF)
    m = jnp.maximum(jnp.max(s, axis=-1, keepdims=True), sink_blk)
    p = jnp.exp(s - m)
    e_sink = jnp.exp(sink_blk - m)
    den = jnp.sum(p, axis=-1, keepdims=True) + e_sink
    return p / den, e_sink / den


def _key_in_range(i, S):
    kabs = i * BLOCK + lax.broadcasted_iota(jnp.int32, (1, 3 * BLOCK), 1) - BLOCK
    return (kabs >= 0) & (kabs < S)


def _attn_fwd(proj, bias_tab, sink_col):
    S = proj.shape[0]
    nb = S // BLOCK
    rows = GQA * BLOCK

    def body(q_ref, kp, kc_, kn, vp, vc_, vn, bias_ref, sink_ref, o_ref):
        in_range = _key_in_range(pl.program_id(0), S)
        outs = []
        for hk in range(N_KV_HEADS):
            q4 = _heads_rows(q_ref, hk)
            kc = _kv_rows(kp, kc_, kn, hk)
            vc = _kv_rows(vp, vc_, vn, hk)
            pn, _ = _softmax_parts(q4, kc, bias_ref[pl.ds(hk * rows, rows), :],
                                   sink_ref[pl.ds(hk * rows, rows), :], in_range)
            o4 = _nn(pn.astype(BF16), vc)
            outs += [o4[g * BLOCK:(g + 1) * BLOCK, :] for g in range(GQA)]
        o_ref[...] = jnp.concatenate(outs, axis=1).astype(BF16)

    const = lambda shape: pl.BlockSpec(shape, lambda i: (0, 0))
    return pl.pallas_call(
        body, name="attn_fwd", grid=(nb,),
        in_specs=_attn_specs(S) + [const((N_HEADS * BLOCK, 3 * BLOCK)), const((N_HEADS * BLOCK, 1))],
        out_specs=pl.BlockSpec((BLOCK, D_MODEL), lambda i: (i, 0)),
        out_shape=jax.ShapeDtypeStruct((S, D_MODEL), BF16),
        compiler_params=_params("parallel"),
    )(*([proj] * 7), bias_tab, sink_col)


GATE_HALF = D_MODEL // 2


def _gate_specs(tm):
    return [pl.BlockSpec((tm, GATE_HALF), lambda i, c=off // GATE_HALF + k: (i, c))
            for off in (OFF_GA, OFF_GP, OFF_GT) for k in (0, 1)]


def _gate(lo_ref, hi_ref):
    return _sigmoid(jnp.concatenate([lo_ref[...], hi_ref[...]], axis=1).astype(F32))


def _pool_mix(p, wp):
    return jnp.concatenate(
        [_nn(p[:, g * POOL_CG:(g + 1) * POOL_CG], wp[g]) for g in range(POOL_GROUPS)], axis=1)


def _mix_fwd(za, p, att, proj, x, wa, wp, ps, wt, wo):
    S = x.shape[0]
    tm = _tile(S, 256)

    def body(za_ref, p_ref, att_ref, ga0, ga1, gp0, gp1, gt0, gt1, x_ref, wa_ref, wp_ref, ps_ref, wt_ref, wo_ref,
             ya_ref, yp_ref, yt_ref, mg_ref, x2_ref):
        ya = _nn(za_ref[...], wa_ref[...])
        ypr = _pool_mix(p_ref[...], wp_ref)
        yt = _nn(att_ref[...], wt_ref[...])
        merged = _gate(ga0, ga1) * ya + _gate(gp0, gp1) * (ypr * ps_ref[...]) + _gate(gt0, gt1) * yt
        mb = merged.astype(BF16)
        ya_ref[...] = ya.astype(BF16)
        yp_ref[...] = ypr.astype(BF16)
        yt_ref[...] = yt.astype(BF16)
        mg_ref[...] = mb
        x2_ref[...] = x_ref[...] + _nn(mb, wo_ref[...])

    row = lambda c=0: pl.BlockSpec((tm, D_MODEL), lambda i: (i, c))
    whole = lambda a: pl.BlockSpec(a.shape, lambda i: (0,) * a.ndim)
    act = jax.ShapeDtypeStruct((S, D_MODEL), BF16)
    return pl.pallas_call(
        body, name="mix_fwd", grid=(S // tm,),
        in_specs=[row(), row(), row()] + _gate_specs(tm) + [row(), whole(wa), whole(wp), whole(ps), whole(wt), whole(wo)],
        out_specs=[row(), row(), row(), row(), row()],
        out_shape=[act, act, act, act, jax.ShapeDtypeStruct((S, D_MODEL), F32)],
        compiler_params=_params("parallel"),
    )(za, p, att, *([proj] * 6), x, wa, wp, ps, wt, wo)


def _ffn_fwd(gu, x2, wd):
    S = x2.shape[0]
    tm = _tile(S, 256)

    def body(g_ref, u_ref, x_ref, w_ref, a_ref, o_ref):
        g = g_ref[...].astype(F32)
        a = (g * _sigmoid(g) * u_ref[...].astype(F32)).astype(BF16)
        a_ref[...] = a
        o_ref[...] = x_ref[...] + _nn(a, w_ref[...])

    return pl.pallas_call(
        body, name="ffn_fwd", grid=(S // tm,),
        in_specs=[pl.BlockSpec((tm, D_FF), lambda i: (i, 0)), pl.BlockSpec((tm, D_FF), lambda i: (i, 1)),
                  pl.BlockSpec((tm, D_MODEL), lambda i: (i, 0)), pl.BlockSpec((D_FF, D_MODEL), lambda i: (0, 0))],
        out_specs=[pl.BlockSpec((tm, D_FF), lambda i: (i, 0)), pl.BlockSpec((tm, D_MODEL), lambda i: (i, 0))],
        out_shape=[jax.ShapeDtypeStruct((S, D_FF), BF16), jax.ShapeDtypeStruct((S, D_MODEL), F32)],
        compiler_params=_params("parallel"),
    )(gu, gu, x2, wd)


def _loss_bwd(x, g, tgt):
    S, Dm = x.shape
    tm = _tile(S, 512)

    def body(x_ref, g_ref, t_ref, l_ref, dx_ref, dg_ref):
        @pl.when(pl.program_id(0) == 0)
        def _():
            l_ref[...] = jnp.zeros_like(l_ref)
            dg_ref[...] = jnp.zeros_like(dg_ref)
        xv, gv = x_ref[...], g_ref[...]
        r = lax.rsqrt(jnp.mean(xv * xv, axis=-1, keepdims=True) + EPS)
        n = xv * r
        err = n * gv - t_ref[...]
        l_ref[...] += 0.5 * jnp.sum(jnp.mean(err * err, axis=-1, keepdims=True), axis=0, keepdims=True)
        dy = err * (1.0 / Dm)
        dn = dy * gv
        dx_ref[...] = r * (dn - n * jnp.mean(dn * n, axis=-1, keepdims=True))
        dg_ref[...] += jnp.sum(dy * n, axis=0, keepdims=True)

    return pl.pallas_call(
        body, name="loss_bwd", grid=(S // tm,),
        in_specs=[pl.BlockSpec((tm, Dm), lambda i: (i, 0)), pl.BlockSpec((1, Dm), lambda i: (0, 0)),
                  pl.BlockSpec((tm, Dm), lambda i: (i, 0))],
        out_specs=[pl.BlockSpec((8, 128), lambda i: (0, 0)), pl.BlockSpec((tm, Dm), lambda i: (i, 0)),
                   pl.BlockSpec((1, Dm), lambda i: (0, 0))],
        out_shape=[jax.ShapeDtypeStruct((8, 128), F32), jax.ShapeDtypeStruct((S, Dm), F32),
                   jax.ShapeDtypeStruct((1, Dm), F32)],
        compiler_params=_params("arbitrary"),
    )(x, g, tgt)


def _ffn_bwd(dx3, gu, wd):
    S = dx3.shape[0]
    tm = _tile(S, 256)

    def body(d_ref, g_ref, u_ref, w_ref, o_ref):
        dact = _nt(d_ref[...].astype(BF16), w_ref[...])
        g, u = g_ref[...].astype(F32), u_ref[...].astype(F32)
        sg = _sigmoid(g)
        o_ref[:, 0:D_FF] = (dact * u * (sg * (1.0 + g * (1.0 - sg)))).astype(BF16)
        o_ref[:, D_FF:2 * D_FF] = (dact * (g * sg)).astype(BF16)

    return pl.pallas_call(
        body, name="ffn_bwd", grid=(S // tm,),
        in_specs=[pl.BlockSpec((tm, D_MODEL), lambda i: (i, 0)),
                  pl.BlockSpec((tm, D_FF), lambda i: (i, 0)), pl.BlockSpec((tm, D_FF), lambda i: (i, 1)),
                  pl.BlockSpec((D_FF, D_MODEL), lambda i: (0, 0))],
        out_specs=pl.BlockSpec((tm, 2 * D_FF), lambda i: (i, 0)),
        out_shape=jax.ShapeDtypeStruct((S, 2 * D_FF), BF16),
        compiler_params=_params("parallel"),
    )(dx3, gu, gu, wd)


def _wgrad(a, b, name, tk=512, tn=512, out_dtype=BF16):
    S, K = a.shape
    N = b.shape[1]
    tk, tn, ts = _tile(K, tk), _tile(N, tn), _tile(S, 512)
    n_s = S // ts

    def body(a_ref, b_ref, o_ref, acc):
        s = pl.program_id(2)

        @pl.when(s == 0)
        def _():
            acc[...] = jnp.zeros_like(acc)
        acc[...] += _tn(a_ref[...].astype(BF16), b_ref[...].astype(BF16))

        @pl.when(s == n_s - 1)
        def _():
            o_ref[...] = acc[...].astype(out_dtype)

    return pl.pallas_call(
        body, name=name, grid=(K // tk, N // tn, n_s),
        in_specs=[pl.BlockSpec((ts, tk), lambda k, n, s: (s, k)), pl.BlockSpec((ts, tn), lambda k, n, s: (s, n))],
        out_specs=pl.BlockSpec((tk, tn), lambda k, n, s: (k, n)),
        out_shape=jax.ShapeDtypeStruct((K, N), out_dtype),
        scratch_shapes=[pltpu.VMEM((tk, tn), F32)],
        compiler_params=_params("parallel", "parallel", "arbitrary"),
    )(a, b)


def _wgrad_pool(p, dyps):
    S = p.shape[0]
    ts = _tile(S, 512)
    n_s = S // ts

    def body(a_ref, b_ref, o_ref, acc):
        s = pl.program_id(1)

        @pl.when(s == 0)
        def _():
            acc[...] = jnp.zeros_like(acc)
        acc[...] += _tn(a_ref[...], b_ref[...])

        @pl.when(s == n_s - 1)
        def _():
            o_ref[...] = acc[...].astype(BF16)

    return pl.pallas_call(
        body, name="wgrad_pool", grid=(POOL_GROUPS, n_s),
        in_specs=[pl.BlockSpec((ts, POOL_CG), lambda g, s: (s, g)), pl.BlockSpec((ts, POOL_CG), lambda g, s: (s, g))],
        out_specs=pl.BlockSpec((None, POOL_CG, POOL_CG), lambda g, s: (g, 0, 0)),
        out_shape=jax.ShapeDtypeStruct((POOL_GROUPS, POOL_CG, POOL_CG), BF16),
        scratch_shapes=[pltpu.VMEM((POOL_CG, POOL_CG), F32)],
        compiler_params=_params("parallel", "arbitrary"),
    )(p, dyps)


def _dgrad_norm_bwd(dy, w, x, g, dres, name, tk):
    S, K = dy.shape
    Dm = x.shape[1]
    tm, tk = _tile(S, 512), _tile(K, tk)
    n_k = K // tk

    def body(dy_ref, w_ref, x_ref, g_ref, r_ref, dx_ref, dg_ref, acc):
        i, k = pl.program_id(0), pl.program_id(1)

        @pl.when((i == 0) & (k == 0))
        def _():
            dg_ref[...] = jnp.zeros_like(dg_ref)

        @pl.when(k == 0)
        def _():
            acc[...] = jnp.zeros_like(acc)
        acc[...] += _nt(dy_ref[...], w_ref[...])

        @pl.when(k == n_k - 1)
        def _():
            dh, xv = acc[...], x_ref[...]
            r = lax.rsqrt(jnp.mean(xv * xv, axis=-1, keepdims=True) + EPS)
            n = xv * r
            dn = dh * g_ref[...]
            dx_ref[...] = r_ref[...] + r * (dn - n * jnp.mean(dn * n, axis=-1, keepdims=True))
            dg_ref[...] += jnp.sum(dh * n, axis=0, keepdims=True)

    rowblk = pl.BlockSpec((tm, Dm), lambda i, k: (i, 0))
    vec = pl.BlockSpec((1, Dm), lambda i, k: (0, 0))
    return pl.pallas_call(
        body, name=name, grid=(S // tm, n_k),
        in_specs=[pl.BlockSpec((tm, tk), lambda i, k: (i, k)), pl.BlockSpec((Dm, tk), lambda i, k: (0, k)),
                  rowblk, vec, rowblk],
        out_specs=[rowblk, vec],
        out_shape=[jax.ShapeDtypeStruct((S, Dm), F32), jax.ShapeDtypeStruct((1, Dm), F32)],
        scratch_shapes=[pltpu.VMEM((tm, Dm), F32)],
        compiler_params=_params("arbitrary", "arbitrary"),
    )(dy, w, x, g, dres)


def _mix_bwd(dx2, ya, ypr, yt, proj, ps, wa, wp, wt, wo):
    S = dx2.shape[0]
    tm = _tile(S, 256)

    def body(dx_ref, ya_ref, yp_ref, yt_ref, ga0, ga1, gp0, gp1, gt0, gt1, ps_ref, wa_ref, wp_ref, wt_ref, wo_ref,
             dya_ref, dyt_ref, dyps_ref, dza_ref, datt_ref, dp_ref, dga_ref, dgp_ref, dgt_ref, dps_ref):
        @pl.when(pl.program_id(0) == 0)
        def _():
            dps_ref[...] = jnp.zeros_like(dps_ref)
        dm = _nt(dx_ref[...].astype(BF16), wo_ref[...])
        sa, sp, st = _gate(ga0, ga1), _gate(gp0, gp1), _gate(gt0, gt1)
        psv = ps_ref[...]
        ypr_v = yp_ref[...].astype(F32)
        dya = (sa * dm).astype(BF16)
        dyt = (st * dm).astype(BF16)
        dyp = sp * dm
        dyps = (dyp * psv).astype(BF16)
        dya_ref[...] = dya
        dyt_ref[...] = dyt
        dyps_ref[...] = dyps
        dga_ref[...] = (dm * ya_ref[...].astype(F32) * (sa * (1.0 - sa))).astype(BF16)
        dgp_ref[...] = (dm * (ypr_v * psv) * (sp * (1.0 - sp))).astype(BF16)
        dgt_ref[...] = (dm * yt_ref[...].astype(F32) * (st * (1.0 - st))).astype(BF16)
        dps_ref[...] += jnp.sum(dyp * ypr_v, axis=0, keepdims=True)
        dza_ref[...] = _nt(dya, wa_ref[...]).astype(BF16)
        datt_ref[...] = _nt(dyt, wt_ref[...]).astype(BF16)
        dp_ref[...] = jnp.concatenate(
            [_nt(dyps[:, g * POOL_CG:(g + 1) * POOL_CG], wp_ref[g]) for g in range(POOL_GROUPS)], axis=1).astype(BF16)

    row = lambda c=0: pl.BlockSpec((tm, D_MODEL), lambda i: (i, c))
    whole = lambda a: pl.BlockSpec(a.shape, lambda i: (0,) * a.ndim)
    act = jax.ShapeDtypeStruct((S, D_MODEL), BF16)
    return pl.pallas_call(
        body, name="mix_bwd", grid=(S // tm,),
        in_specs=[row(), row(), row(), row()] + _gate_specs(tm) + [whole(ps), whole(wa), whole(wp), whole(wt), whole(wo)],
        out_specs=[row()] * 9 + [pl.BlockSpec((1, D_MODEL), lambda i: (0, 0))],
        out_shape=[act] * 9 + [jax.ShapeDtypeStruct((1, D_MODEL), F32)],
        compiler_params=_params("arbitrary"),
    )(dx2, ya, ypr, yt, *([proj] * 6), ps, wa, wp, wt, wo)


def _conv_bwd(dza, proj, cw8):
    S = proj.shape[0]
    nblk = D_MODEL // CB

    def body(d_ref, b_ref, c_ref, x_ref, w_ref, db_ref, dc_ref, dxa_ref, dw_ref, pad_u, pad_d):
        c, xa = c_ref[...].astype(F32), x_ref[...].astype(F32)
        u = c * xa
        _fill_padded(pad_u, u, S)
        u_prev, u_next = _shifted(pad_u, -1, S), _shifted(pad_u, 1, S)
        cv = w_ref[0:1, :] * u_prev + w_ref[1:2, :] * u + w_ref[2:3, :] * u_next
        dza_v = d_ref[...].astype(F32)
        db_ref[...] = (dza_v * cv).astype(BF16)
        dcv = dza_v * b_ref[...].astype(F32)
        _fill_padded(pad_d, dcv, S)
        du = w_ref[0:1, :] * _shifted(pad_d, 1, S) + w_ref[1:2, :] * dcv + w_ref[2:3, :] * _shifted(pad_d, -1, S)
        dc_ref[...] = (du * xa).astype(BF16)
        dxa_ref[...] = (du * c).astype(BF16)
        dw_ref[...] = jnp.concatenate(
            [jnp.sum(dcv * u_prev, axis=0, keepdims=True), jnp.sum(dcv * u, axis=0, keepdims=True),
             jnp.sum(dcv * u_next, axis=0, keepdims=True), jnp.zeros((5, CB), F32)], axis=0)

    col = lambda base: pl.BlockSpec((S, CB), lambda j: (0, base // CB + j))
    act = jax.ShapeDtypeStruct((S, D_MODEL), BF16)
    return pl.pallas_call(
        body, name="conv_bwd", grid=(nblk,),
        in_specs=[col(0), col(OFF_B), col(OFF_C), col(OFF_X), pl.BlockSpec((8, CB), lambda j: (0, j))],
        out_specs=[col(0), col(0), col(0), pl.BlockSpec((8, CB), lambda j: (0, j))],
        out_shape=[act, act, act, jax.ShapeDtypeStruct((8, D_MODEL), F32)],
        scratch_shapes=[pltpu.VMEM((S + 2 * HALO, CB), F32), pltpu.VMEM((S + 2 * HALO, CB), F32)],
        compiler_params=_params("parallel"),
    )(dza, proj, proj, proj, cw8)


def _pool_bwd(dp):
    S = dp.shape[0]
    nblk = D_MODEL // CB
    per_group = POOL_CG // CB

    def body(d_ref, o_ref, pad):
        d = d_ref[...].astype(F32)
        grp = pl.program_id(0) // per_group
        for gi, w in enumerate(POOL_WINDOWS):
            @pl.when(grp == gi)
            def _(w=w):
                lo, hi = w // 2, w - 1 - w // 2
                _fill_padded(pad, d / _pool_count(S, lo, hi), S)
                acc = _shifted(pad, -hi, S)
                for off in range(-hi + 1, lo + 1):
                    acc = acc + _shifted(pad, off, S)
                o_ref[...] = (acc - d).astype(BF16)

    return pl.pallas_call(
        body, name="pool_bwd", grid=(nblk,),
        in_specs=[pl.BlockSpec((S, CB), lambda j: (0, j))],
        out_specs=pl.BlockSpec((S, CB), lambda j: (0, j)),
        out_shape=jax.ShapeDtypeStruct((S, D_MODEL), BF16),
        scratch_shapes=[pltpu.VMEM((S + 2 * HALO, CB), F32)],
        compiler_params=_params("parallel"),
    )(dp)


def _attn_bwd(proj, datt, bias_tab, sink_col, dbias_in):
    S = proj.shape[0]
    nb = S // BLOCK
    rows = GQA * BLOCK
    kvw = N_KV_HEADS * HEAD_DIM

    def body(q_ref, kp, kc_, kn, vp, vc_, vn, do_ref, bias_ref, sink_ref, dbin_ref,
             dq_ref, dk_ref, dv_ref, db_ref, ds_ref):
        i = pl.program_id(0)

        @pl.when(i == 0)
        def _():
            dk_ref[...] = jnp.zeros_like(dk_ref)
            dv_ref[...] = jnp.zeros_like(dv_ref)
            db_ref[...] = dbin_ref[...]
            ds_ref[...] = jnp.zeros_like(ds_ref)
        in_range = _key_in_range(i, S)
        dqs, dks, dvs = [], [], []
        for hk in range(N_KV_HEADS):
            q4 = _heads_rows(q_ref, hk)
            do4 = _heads_rows(do_ref, hk)
            kc = _kv_rows(kp, kc_, kn, hk)
            vc = _kv_rows(vp, vc_, vn, hk)
            blk = pl.ds(hk * rows, rows)
            pn, p_sink = _softmax_parts(q4, kc, bias_ref[blk, :], sink_ref[blk, :], in_range)
            dpm = _nt(do4, vc)
            delta = jnp.sum(pn * dpm, axis=-1, keepdims=True)
            dsc = pn * (dpm - delta)
            db_ref[blk, :] += dsc
            ds_ref[blk, :] += -p_sink * delta
            dsb = dsc.astype(BF16)
            dq4 = _nn(dsb, kc) * SM_SCALE
            dqs += [dq4[g * BLOCK:(g + 1) * BLOCK, :] for g in range(GQA)]
            dks.append(_tn(dsb, q4) * SM_SCALE)
            dvs.append(_tn(pn.astype(BF16), do4))
        dq_ref[...] = jnp.concatenate(dqs, axis=1).astype(BF16)
        r0 = pl.multiple_of(i * BLOCK, BLOCK)
        dk_ref[pl.ds(r0, 3 * BLOCK), :] += jnp.concatenate(dks, axis=1)
        dv_ref[pl.ds(r0, 3 * BLOCK), :] += jnp.concatenate(dvs, axis=1)

    const = lambda shape: pl.BlockSpec(shape, lambda i: (0, 0))
    tab = (N_HEADS * BLOCK, 3 * BLOCK)
    return pl.pallas_call(
        body, name="attn_bwd", grid=(nb,),
        in_specs=_attn_specs(S) + [pl.BlockSpec((BLOCK, D_MODEL), lambda i: (i, 0)),
                                   const(tab), const((N_HEADS * BLOCK, 1)), const(tab)],
        out_specs=[pl.BlockSpec((BLOCK, D_MODEL), lambda i: (i, 0)),
                   const((S + 2 * BLOCK, kvw)), const((S + 2 * BLOCK, kvw)), const(tab), const((N_HEADS * BLOCK, 1))],
        out_shape=[jax.ShapeDtypeStruct((S, D_MODEL), BF16),
                   jax.ShapeDtypeStruct((S + 2 * BLOCK, kvw), F32), jax.ShapeDtypeStruct((S + 2 * BLOCK, kvw), F32),
                   jax.ShapeDtypeStruct(tab, F32), jax.ShapeDtypeStruct((N_HEADS * BLOCK, 1), F32)],
        compiler_params=_params("arbitrary"),
    )(*([proj] * 7), datt, bias_tab, sink_col, dbias_in)


def _bucket_constants():
    half = N_BUCKETS // 2
    max_exact = half // 2
    qi = np.arange(BLOCK)[:, None]
    kj = np.arange(3 * BLOCK)[None, :]
    rel = kj - BLOCK - qi
    n = np.abs(rel)
    nf = np.maximum(n, 1).astype(np.float32)
    large = max_exact + (np.log(nf / np.float32(max_exact)) / np.float32(math.log(MAX_DISTANCE / max_exact))
                         * np.float32(half - max_exact)).astype(np.int32)
    large = np.minimum(large, half - 1)
    bucket = np.where(rel > 0, half, 0) + np.where(n < max_exact, n, large)
    onehot = (bucket.reshape(1, -1) == np.arange(N_BUCKETS)[:, None]).astype(np.float32)
    mask = np.where(n <= WINDOW, 0.0, NEG_INF).astype(np.float32).reshape(1, -1)
    return onehot, mask


def _bias_expand(rel_bias_t, onehot, mask):
    def body(r_ref, oh_ref, m_ref, o_ref):
        o_ref[...] = jnp.dot(r_ref[...], oh_ref[...], preferred_element_type=F32,
                             precision=lax.Precision.HIGHEST) + m_ref[...]

    return pl.pallas_call(
        body, name="bias_expand", out_shape=jax.ShapeDtypeStruct((N_HEADS, onehot.shape[1]), F32),
        compiler_params=_params(),
    )(rel_bias_t, onehot, mask)


def _bias_reduce(dtab, dsink_rows, onehot):
    def body(d_ref, s_ref, oh_ref, o_ref, so_ref):
        o_ref[...] = lax.dot_general(oh_ref[...], d_ref[...], (((1,), (1,)), ((), ())),
                                     preferred_element_type=F32, precision=lax.Precision.HIGHEST)
        so_ref[...] = jnp.sum(s_ref[...], axis=-1, keepdims=True)

    return pl.pallas_call(
        body, name="bias_reduce",
        out_shape=[jax.ShapeDtypeStruct((N_BUCKETS, N_HEADS), F32),
                   jax.ShapeDtypeStruct((dsink_rows.shape[0], 1), F32)],
        compiler_params=_params(),
    )(dtab, dsink_rows, onehot)


WEIGHT_NAMES = ("w_in", "conv_w", "w_a_out", "w_pool", "w_attn_out", "w_o", "w_gu", "w_down")


def _layer_fwd(x, W, ps, g_mix, g_ffn, bias_tab, sink_col):
    h, proj = _norm_matmul(x, g_mix, W["w_in"], "norm_proj")
    za = _conv_fwd(proj, W["conv_w"])
    p = _pool_fwd(proj)
    att = _attn_fwd(proj, bias_tab, sink_col)
    ya, ypr, yt, merged, x2 = _mix_fwd(za, p, att, proj, x, W["w_a_out"], W["w_pool"], ps, W["w_attn_out"], W["w_o"])
    h2, gu = _norm_matmul(x2, g_ffn, W["w_gu"], "norm_gu")
    act, x3 = _ffn_fwd(gu, x2, W["w_down"])
    saved = dict(x=x, h=h, proj=proj, za=za, p=p, att=att, ya=ya, ypr=ypr, yt=yt, merged=merged, x2=x2, h2=h2,
                 gu=gu, act=act)
    return x3, saved


def _layer_bwd(dx3, sv, W, ps, g_mix, g_ffn, bias_tab, sink_col, dbias):
    S = dx3.shape[0]
    dgu = _ffn_bwd(dx3, sv["gu"], W["w_down"])
    g_w_down = _wgrad(sv["act"], dx3, "wgrad_down", tk=_tile(D_FF, 1408), tn=512)
    g_w_gu = _wgrad(sv["h2"], dgu, "wgrad_gu", tk=1024, tn=512)
    dx2, dg_ffn = _dgrad_norm_bwd(dgu, W["w_gu"], sv["x2"], g_ffn, dx3, "dgrad_gu", tk=1408)
    dya, dyt, dyps, dza, datt, dp, dga, dgp, dgt, dps = _mix_bwd(
        dx2, sv["ya"], sv["ypr"], sv["yt"], sv["proj"], ps, W["w_a_out"], W["w_pool"], W["w_attn_out"], W["w_o"])
    g_w_o = _wgrad(sv["merged"], dx2, "wgrad_sq_f32", tk=1024, tn=512)
    g_w_a_out = _wgrad(sv["za"], dya, "wgrad_sq", tk=1024, tn=512)
    g_w_attn_out = _wgrad(sv["att"], dyt, "wgrad_sq", tk=1024, tn=512)
    g_w_pool = _wgrad_pool(sv["p"], dyps)
    db, dc, dxa, g_conv = _conv_bwd(dza, sv["proj"], W["conv_w"])
    dup = _pool_bwd(dp)
    dq, dkp, dvp, dbias, dsink = _attn_bwd(sv["proj"], datt, bias_tab, sink_col, dbias)
    dproj = jnp.concatenate([db, dc, dxa, dup, dq, dkp[BLOCK:BLOCK + S].astype(BF16),
                             dvp[BLOCK:BLOCK + S].astype(BF16), dga, dgp, dgt], axis=1)
    g_w_in = _wgrad(sv["h"], dproj, "wgrad_in", tk=1024, tn=512)
    dx, dg_mix = _dgrad_norm_bwd(dproj, W["w_in"], sv["x"], g_mix, dx2, "dgrad_in", tk=2176)
    grads = dict(w_in=g_w_in, conv_w=g_conv, w_a_out=g_w_a_out, w_pool=g_w_pool, w_attn_out=g_w_attn_out,
                 w_o=g_w_o, w_gu=g_w_gu, w_down=g_w_down)
    return dx, grads, dict(pool_scale=dps, g_mix=dg_mix, g_ffn=dg_ffn, attn_sink=dsink), dbias


def _local_step(x, tgt, weights, pool_scale, attn_sink, g_mix, g_ffn, rel_bias, g_final):
    onehot_np, mask_np = _bucket_constants()
    onehot, mask = jnp.asarray(onehot_np), jnp.asarray(mask_np)
    tab = (N_HEADS * BLOCK, 3 * BLOCK)
    bias_tab = _bias_expand(rel_bias.T, onehot, mask).reshape(tab)
    saved = []
    for l in range(DEPTH):
        sink_col = jnp.repeat(attn_sink[l], BLOCK).reshape(N_HEADS * BLOCK, 1)
        x, sv = _layer_fwd(x, weights[l], pool_scale[l:l + 1], g_mix[l:l + 1], g_ffn[l:l + 1], bias_tab, sink_col)
        sv["sink_col"] = sink_col
        saved.append(sv)
    loss, dx, dg_final = _loss_bwd(x, g_final.reshape(1, D_MODEL), tgt)
    dbias = jnp.zeros(tab, F32)
    wgrads, small = [None] * DEPTH, [None] * DEPTH
    for l in reversed(range(DEPTH)):
        dx, wgrads[l], small[l], dbias = _layer_bwd(
            dx, saved[l], weights[l], pool_scale[l:l + 1], g_mix[l:l + 1], g_ffn[l:l + 1], bias_tab,
            saved[l]["sink_col"], dbias)
    dsink_rows = jnp.concatenate([small[l]["attn_sink"].reshape(N_HEADS, BLOCK) for l in range(DEPTH)], axis=0)
    d_rel_bias, d_sink = _bias_reduce(dbias.reshape(N_HEADS, BLOCK * 3 * BLOCK), dsink_rows, onehot)
    cat = lambda k: jnp.concatenate([small[l][k] for l in range(DEPTH)], axis=0)
    smalls = dict(pool_scale=cat("pool_scale"), g_mix=cat("g_mix"), g_ffn=cat("g_ffn"),
                  attn_sink=d_sink.reshape(DEPTH, N_HEADS), rel_bias=d_rel_bias, g_final=dg_final)
    return loss[0, 0], dx, wgrads, smalls


SHARD_AXIS = dict(w_in=(1, IN_TOTAL // N_CHIPS), conv_w=(1, D_MODEL // N_CHIPS), w_a_out=(0, D_MODEL // N_CHIPS),
                  w_pool=(1, POOL_CG // N_CHIPS), w_attn_out=(0, D_MODEL // N_CHIPS), w_o=(0, D_MODEL // N_CHIPS),
                  w_gu=(1, 2 * D_FF // N_CHIPS), w_down=(0, D_FF // N_CHIPS))


def _shard_of(ref, name, chip):
    axis, n = SHARD_AXIS[name]
    idx = [slice(None)] * len(ref.shape)
    idx[axis] = pl.ds(chip * n, n)
    return ref.at[tuple(idx)]


def _full_shape(name, shard_shape):
    axis, n = SHARD_AXIS[name]
    s = list(shard_shape)
    s[axis] = n * N_CHIPS
    return tuple(s)


def _chip_peers(x, y):
    return [(1 - x, y), (x, 1 - y), (1 - x, 1 - y)]


def _all_gather_weights(shards):
    names = [n for n in WEIGHT_NAMES]
    nw = len(names)

    def body(*refs):
        ins, outs = refs[:nw], refs[nw:2 * nw]
        send_sems, recv_sems, local_sems = refs[2 * nw:]
        x, y, c = lax.axis_index("x"), lax.axis_index("y"), lax.axis_index("c")
        me = 2 * x + y
        for chip in range(N_CHIPS):
            @pl.when(me == chip)
            def _(chip=chip):
                copies = []
                for t, name in enumerate(names):
                    own = pltpu.make_async_copy(ins[t], _shard_of(outs[t], name, chip), local_sems.at[t])
                    own.start()
                    copies.append(own)
                    for j, (px, py) in enumerate(_chip_peers(x, y)):
                        cp = pltpu.make_async_remote_copy(
                            src_ref=ins[t], dst_ref=_shard_of(outs[t], name, chip),
                            send_sem=send_sems.at[3 * t + j], recv_sem=recv_sems.at[3 * t + j],
                            device_id=(px, py, c), device_id_type=MESH)
                        cp.start()
                        copies.append(cp)
                for cp in copies:
                    cp.wait()

    out_shape = [jax.ShapeDtypeStruct(_full_shape(n, shards[n].shape), shards[n].dtype) for n in names]
    outs = pl.pallas_call(
        body, name="all_gather_weights", in_specs=[ANY] * nw, out_specs=[ANY] * nw, out_shape=out_shape,
        scratch_shapes=[pltpu.SemaphoreType.DMA((3 * nw,)), pltpu.SemaphoreType.DMA((3 * nw,)),
                        pltpu.SemaphoreType.DMA((nw,))],
        compiler_params=pltpu.CompilerParams(has_side_effects=True),
    )(*[shards[n] for n in names])
    return dict(zip(names, outs))


def _scatter_grads(grads):
    names = [n for n in WEIGHT_NAMES]
    nw = len(names)

    def body(*refs):
        ins, outs = refs[:nw], refs[nw:2 * nw]
        send_sems, recv_sems, local_sems = refs[2 * nw:]
        x, y, c = lax.axis_index("x"), lax.axis_index("y"), lax.axis_index("c")
        me = 2 * x + y
        for chip in range(N_CHIPS):
            @pl.when(me == chip)
            def _(chip=chip):
                copies = []
                for t, name in enumerate(names):
                    own = pltpu.make_async_copy(_shard_of(ins[t], name, chip), outs[t].at[3], local_sems.at[t])
                    own.start()
                    copies.append(own)
                    for j, (px, py) in enumerate(_chip_peers(x, y)):
                        peer_chip = chip ^ (2, 1, 3)[j]
                        cp = pltpu.make_async_remote_copy(
                            src_ref=_shard_of(ins[t], name, peer_chip), dst_ref=outs[t].at[j],
                            send_sem=send_sems.at[3 * t + j], recv_sem=recv_sems.at[3 * t + j],
                            device_id=(px, py, c), device_id_type=MESH)
                        cp.start()
                        copies.append(cp)
                for cp in copies:
                    cp.wait()

    def slot_shape(n):
        axis, k = SHARD_AXIS[n]
        s = list(grads[n].shape)
        s[axis] = k
        return (N_CHIPS,) + tuple(s)

    out_shape = [jax.ShapeDtypeStruct(slot_shape(n), grads[n].dtype) for n in names]
    outs = pl.pallas_call(
        body, name="scatter_grads", in_specs=[ANY] * nw, out_specs=[ANY] * nw, out_shape=out_shape,
        scratch_shapes=[pltpu.SemaphoreType.DMA((3 * nw,)), pltpu.SemaphoreType.DMA((3 * nw,)),
                        pltpu.SemaphoreType.DMA((nw,))],
        compiler_params=pltpu.CompilerParams(has_side_effects=True),
    )(*[grads[n] for n in names])
    return dict(zip(names, outs))


def _sibling_exchange(parts):
    n = len(parts)

    def body(*refs):
        ins, outs = refs[:n], refs[n:2 * n]
        send_sems, recv_sems = refs[2 * n:]
        sibling = (lax.axis_index("x"), lax.axis_index("y"), 1 - lax.axis_index("c"))
        copies = [pltpu.make_async_remote_copy(src_ref=ins[t], dst_ref=outs[t], send_sem=send_sems.at[t],
                                               recv_sem=recv_sems.at[t], device_id=sibling, device_id_type=MESH)
                  for t in range(n)]
        for cp in copies:
            cp.start()
        for cp in copies:
            cp.wait()

    outs = pl.pallas_call(
        body, name="sibling_exchange", in_specs=[ANY] * n, out_specs=[ANY] * n,
        out_shape=[jax.ShapeDtypeStruct(p.shape, p.dtype) for p in parts],
        scratch_shapes=[pltpu.SemaphoreType.DMA((n,)), pltpu.SemaphoreType.DMA((n,))],
        compiler_params=pltpu.CompilerParams(has_side_effects=True),
    )(*parts)
    return list(outs)


N_DEV = 8


def _all_reduce_small(v):
    R, C = v.shape

    def body(v_ref, o_ref, slots, send_sems, recv_sems):
        x, y, c = lax.axis_index("x"), lax.axis_index("y"), lax.axis_index("c")
        me = 4 * x + 2 * y + c
        slots[me] = v_ref[...]
        copies = []
        for k in range(1, N_DEV):
            peer = me ^ k
            cp = pltpu.make_async_remote_copy(
                src_ref=v_ref, dst_ref=slots.at[me], send_sem=send_sems.at[k - 1], recv_sem=recv_sems.at[k - 1],
                device_id=(peer // 4, (peer // 2) % 2, peer % 2), device_id_type=MESH)
            cp.start()
            copies.append(cp)
        for cp in copies:
            cp.wait()
        acc = slots[0]
        for k in range(1, N_DEV):
            acc = acc + slots[k]
        o_ref[...] = acc

    return pl.pallas_call(
        body, name="all_reduce_small", out_shape=jax.ShapeDtypeStruct((R, C), F32),
        in_specs=[pl.BlockSpec(memory_space=pltpu.VMEM)], out_specs=pl.BlockSpec(memory_space=pltpu.VMEM),
        scratch_shapes=[pltpu.VMEM((N_DEV, R, C), F32), pltpu.SemaphoreType.DMA((N_DEV - 1,)),
                        pltpu.SemaphoreType.DMA((N_DEV - 1,))],
        compiler_params=pltpu.CompilerParams(has_side_effects=True),
    )(v)


def _as2d(shape):
    return (int(np.prod(shape[:-1])), shape[-1])


def _row_block(rows, cols, n_arrays):
    budget = V7X_VMEM_LIMIT // 2
    tr = rows
    while tr % 16 == 0 and 2 * n_arrays * tr * cols * 4 > budget:
        tr //= 2
    return tr


def _sum_slots(slots):
    _, R, C = slots.shape
    tr = _row_block(R, C, 5)

    def body(s_ref, o_ref):
        acc = s_ref[0].astype(F32)
        for k in range(1, N_CHIPS):
            acc = acc + s_ref[k].astype(F32)
        o_ref[...] = acc

    return pl.pallas_call(
        body, name="sum_slots", grid=(R // tr,),
        in_specs=[pl.BlockSpec((N_CHIPS, tr, C), lambda i: (0, i, 0))],
        out_specs=pl.BlockSpec((tr, C), lambda i: (i, 0)),
        out_shape=jax.ShapeDtypeStruct((R, C), F32),
        compiler_params=_params("parallel"),
    )(slots)


def _adamw(w, m, v, g_a, g_b):
    R, C = w.shape
    tr = _row_block(R, C, 9)
    c1 = 1.0 - ADAM_B1 ** ADAM_STEP
    c2 = 1.0 - ADAM_B2 ** ADAM_STEP

    def body(w_ref, m_ref, v_ref, a_ref, b_ref, g_ref, d_ref, nm_ref, nv_ref):
        g = a_ref[...] + b_ref[...]
        nm = ADAM_B1 * m_ref[...] + (1.0 - ADAM_B1) * g
        nv = ADAM_B2 * v_ref[...] + (1.0 - ADAM_B2) * (g * g)
        g_ref[...] = g
        nm_ref[...] = nm
        nv_ref[...] = nv
        d_ref[...] = -ADAM_LR * ((nm / c1) / (jnp.sqrt(nv / c2) + ADAM_EPS) + ADAM_WD * w_ref[...])

    blk = pl.BlockSpec((tr, C), lambda i: (i, 0))
    out = jax.ShapeDtypeStruct((R, C), F32)
    return pl.pallas_call(
        body, name="adamw", grid=(R // tr,), in_specs=[blk] * 5, out_specs=[blk] * 4, out_shape=[out] * 4,
        compiler_params=_params("parallel"),
    )(w, m, v, g_a, g_b)


SMALL_ROWS = 16


def _pack_small(pool_scale, g_mix, g_ffn, g_final, attn_sink, rel_bias):
    tail = jnp.concatenate([attn_sink.reshape(-1), rel_bias.reshape(-1)])
    tail = jnp.pad(tail, (0, D_MODEL - tail.shape[0])).reshape(1, D_MODEL)
    rows = jnp.concatenate([pool_scale, g_mix, g_ffn, g_final.reshape(1, D_MODEL), tail], axis=0)
    return jnp.pad(rows, ((0, SMALL_ROWS - rows.shape[0]), (0, 0)))


def _unpack_small(packed):
    n_sink = DEPTH * N_HEADS
    return dict(pool_scale=packed[0:4], g_mix=packed[4:8], g_ffn=packed[8:12], g_final=packed[12],
                attn_sink=packed[13, 0:n_sink].reshape(DEPTH, N_HEADS),
                rel_bias=packed[13, n_sink:n_sink + N_BUCKETS * N_HEADS].reshape(N_BUCKETS, N_HEADS))


def _layer_shards(l, w_in, conv_w, w_a_out, w_pool, w_attn_out, w_o, w_gu, w_down):
    return dict(
        w_in=w_in[l].astype(BF16),
        conv_w=jnp.pad(conv_w[l].reshape(3, -1), ((0, 5), (0, 0))),
        w_a_out=w_a_out[l].astype(BF16), w_pool=w_pool[l].astype(BF16), w_attn_out=w_attn_out[l].astype(BF16),
        w_o=w_o[l].astype(BF16), w_gu=w_gu[l].astype(BF16), w_down=w_down[l].astype(BF16))


def kernel(x, w_in, conv_w, w_a_out, w_pool, pool_scale, w_attn_out, attn_sink, w_o, g_mix, g_ffn, w_gu, w_down, rel_bias, g_final, loss_target, m_w_in, m_conv_w, m_w_a_out, m_w_pool, m_pool_scale, m_w_attn_out, m_attn_sink, m_w_o, m_g_mix, m_g_ffn, m_w_gu, m_w_down, m_rel_bias, m_g_final, v_w_in, v_conv_w, v_w_a_out, v_w_pool, v_pool_scale, v_w_attn_out, v_attn_sink, v_w_o, v_g_mix, v_g_ffn, v_w_gu, v_w_down, v_rel_bias, v_g_final):
    big = dict(w_in=(w_in, m_w_in, v_w_in), conv_w=(conv_w, m_conv_w, v_conv_w), w_a_out=(w_a_out, m_w_a_out, v_w_a_out),
               w_pool=(w_pool, m_w_pool, v_w_pool), w_attn_out=(w_attn_out, m_w_attn_out, v_w_attn_out),
               w_o=(w_o, m_w_o, v_w_o), w_gu=(w_gu, m_w_gu, v_w_gu), w_down=(w_down, m_w_down, v_w_down))

    weights = [_all_gather_weights(_layer_shards(l, w_in, conv_w, w_a_out, w_pool, w_attn_out, w_o, w_gu, w_down))
               for l in range(DEPTH)]
    loss, grad_x, wgrads, smalls = _local_step(x[0], loss_target[0], weights, pool_scale, attn_sink, g_mix, g_ffn,
                                               rel_bias, g_final)

    results = {n: [] for n in WEIGHT_NAMES}
    for l in range(DEPTH):
        slots = _scatter_grads(wgrads[l])
        parts = [_sum_slots(slots[n].reshape((N_CHIPS,) + _as2d(slots[n].shape[1:]))) for n in WEIGHT_NAMES]
        others = _sibling_exchange(parts)
        for n, mine, other in zip(WEIGHT_NAMES, parts, others):
            w, m, v = (a[l] for a in big[n])
            if n == "conv_w":
                w, m, v = (jnp.pad(a.reshape(3, -1), ((0, 5), (0, 0))) for a in (w, m, v))
            shape = w.shape
            out = _adamw(w.reshape(_as2d(shape)), m.reshape(_as2d(shape)), v.reshape(_as2d(shape)), mine, other)
            out = [o.reshape(shape) for o in out]
            if n == "conv_w":
                out = [o[0:3].reshape(3, 1, -1) for o in out]
            results[n].append(out)
    stacked = {n: [jnp.stack([results[n][l][k] for l in range(DEPTH)]) for k in range(4)] for n in WEIGHT_NAMES}

    g_small = _all_reduce_small(_pack_small(smalls["pool_scale"], smalls["g_mix"], smalls["g_ffn"], smalls["g_final"],
                                            smalls["attn_sink"], smalls["rel_bias"]))
    w_small = _pack_small(pool_scale, g_mix, g_ffn, g_final, attn_sink, rel_bias)
    m_small = _pack_small(m_pool_scale, m_g_mix, m_g_ffn, m_g_final, m_attn_sink, m_rel_bias)
    v_small = _pack_small(v_pool_scale, v_g_mix, v_g_ffn, v_g_final, v_attn_sink, v_rel_bias)
    small_out = [_unpack_small(o) for o in _adamw(w_small, m_small, v_small, g_small, jnp.zeros_like(g_small))]

    total_loss = lax.psum(loss, ("x", "y", "c"))

    order = ("w_in", "conv_w", "w_a_out", "w_pool", "pool_scale", "w_attn_out", "attn_sink", "w_o", "g_mix", "g_ffn",
             "w_gu", "w_down", "rel_bias", "g_final")
    outs = [total_loss, grad_x[None]]
    for k in range(4):
        for n in order:
            outs.append(stacked[n][k] if n in stacked else small_out[k][n])
    return tuple(outs)
```

```python
import functools
import math

import numpy as np
import jax
import jax.numpy as jnp
from jax import lax
from jax.experimental import pallas as pl
from jax.experimental.pallas import tpu as pltpu

F32 = jnp.float32
BF16 = jnp.bfloat16

D_MODEL = 1024
DEPTH = 4
N_HEADS = 16
N_KV_HEADS = 4
HEAD_DIM = 64
GQA = N_HEADS // N_KV_HEADS
WINDOW = 128
BLOCK = 128
N_BUCKETS = 32
MAX_DISTANCE = 128
POOL_GROUPS = 4
POOL_CG = D_MODEL // POOL_GROUPS
POOL_WINDOWS = (2, 4, 8, 16)
D_FF = 2816
IN_TOTAL = 8704
OFF_B, OFF_C, OFF_X, OFF_U, OFF_Q, OFF_K, OFF_V, OFF_GA, OFF_GP, OFF_GT = (
    0, 1024, 2048, 3072, 4096, 5120, 5376, 5632, 6656, 7680)
EPS = 1e-6
NEG_INF = -1e30
SM_SCALE = HEAD_DIM ** -0.5

ADAM_LR = 0.001
ADAM_B1 = 0.9
ADAM_B2 = 0.999
ADAM_EPS = 1e-08
ADAM_WD = 0.01
ADAM_STEP = 10

N_CHIPS = 4
HALO = 8
V7X_VMEM_LIMIT = 56 * 1024 * 1024
MESH = pl.DeviceIdType.MESH
ANY = pl.BlockSpec(memory_space=pl.ANY)


def _params(*sem):
    return pltpu.CompilerParams(dimension_semantics=tuple(sem) if sem else None,
                                vmem_limit_bytes=V7X_VMEM_LIMIT)


def _tile(n, pref):
    t = min(pref, n)
    while n % t or t % 128:
        t -= 128
    return t


def _nt(a, b):
    return lax.dot_general(a, b, (((1,), (1,)), ((), ())), preferred_element_type=F32)


def _tn(a, b):
    return lax.dot_general(a, b, (((0,), (0,)), ((), ())), preferred_element_type=F32)


def _nn(a, b):
    return jnp.dot(a, b, preferred_element_type=F32)


def _sigmoid(v):
    return 1.0 / (1.0 + jnp.exp(-v))


def _norm_matmul(x, g, w, name, token):
    S, Dm = x.shape
    N = w.shape[1]
    tm, tn = _tile(S, 1024), _tile(N, 512)

    def body(x_ref, g_ref, w_ref, token_ref, h_ref, o_ref):
        @pl.when(pl.program_id(1) == 0)
        def _():
            xv = x_ref[...]
            r = lax.rsqrt(jnp.mean(xv * xv, axis=-1, keepdims=True) + EPS)
            h_ref[...] = (xv * r * g_ref[...]).astype(BF16)
        o_ref[...] = _nn(h_ref[...], w_ref[...]).astype(BF16)

    return pl.pallas_call(
        body, name=name, grid=(S // tm, N // tn),
        in_specs=[pl.BlockSpec((tm, Dm), lambda i, j: (i, 0)),
                  pl.BlockSpec((1, Dm), lambda i, j: (0, 0)),
                  pl.BlockSpec((Dm, tn), lambda i, j: (0, j)), ANY],
        out_specs=[pl.BlockSpec((tm, Dm), lambda i, j: (i, 0)),
                   pl.BlockSpec((tm, tn), lambda i, j: (i, j))],
        out_shape=[jax.ShapeDtypeStruct((S, Dm), BF16), jax.ShapeDtypeStruct((S, N), BF16)],
        compiler_params=_params("parallel", "arbitrary"),
    )(x, g, w, token)


CB = 128


def _fill_padded(pad_ref, v, S):
    z = jnp.zeros((HALO, v.shape[1]), F32)
    pad_ref[pl.ds(0, HALO), :] = z
    pad_ref[pl.ds(S + HALO, HALO), :] = z
    pad_ref[pl.ds(HALO, S), :] = v


def _shifted(pad_ref, off, S):
    return pad_ref[pl.ds(HALO + off, S), :]


def _conv_fwd(proj, cw8):
    S = proj.shape[0]
    nblk = D_MODEL // CB

    def body(b_ref, c_ref, x_ref, w_ref, o_ref, pad):
        u = c_ref[...].astype(F32) * x_ref[...].astype(F32)
        _fill_padded(pad, u, S)
        cv = w_ref[0:1, :] * _shifted(pad, -1, S) + w_ref[1:2, :] * u + w_ref[2:3, :] * _shifted(pad, 1, S)
        o_ref[...] = (b_ref[...].astype(F32) * cv).astype(BF16)

    col = lambda base: pl.BlockSpec((S, CB), lambda j: (0, base // CB + j))
    return pl.pallas_call(
        body, name="conv_fwd", grid=(nblk,),
        in_specs=[col(OFF_B), col(OFF_C), col(OFF_X), pl.BlockSpec((8, CB), lambda j: (0, j))],
        out_specs=pl.BlockSpec((S, CB), lambda j: (0, j)),
        out_shape=jax.ShapeDtypeStruct((S, D_MODEL), BF16),
        scratch_shapes=[pltpu.VMEM((S + 2 * HALO, CB), F32)],
        compiler_params=_params("parallel"),
    )(proj, proj, proj, cw8)


def _pool_count(S, lo, hi):
    t = lax.broadcasted_iota(jnp.int32, (S, CB), 0)
    return (jnp.minimum(t + hi, S - 1) - jnp.maximum(t - lo, 0) + 1).astype(F32)


def _pool_fwd(proj):
    S = proj.shape[0]
    nblk = D_MODEL // CB
    per_group = POOL_CG // CB

    def body(u_ref, o_ref, pad):
        u = u_ref[...].astype(F32)
        _fill_padded(pad, u, S)
        grp = pl.program_id(0) // per_group
        for gi, w in enumerate(POOL_WINDOWS):
            @pl.when(grp == gi)
            def _(w=w):
                lo, hi = w // 2, w - 1 - w // 2
                acc = _shifted(pad, -lo, S)
                for off in range(-lo + 1, hi + 1):
                    acc = acc + _shifted(pad, off, S)
                o_ref[...] = (acc / _pool_count(S, lo, hi) - u).astype(BF16)

    return pl.pallas_call(
        body, name="pool_fwd", grid=(nblk,),
        in_specs=[pl.BlockSpec((S, CB), lambda j: (0, OFF_U // CB + j))],
        out_specs=pl.BlockSpec((S, CB), lambda j: (0, j)),
        out_shape=jax.ShapeDtypeStruct((S, D_MODEL), BF16),
        scratch_shapes=[pltpu.VMEM((S + 2 * HALO, CB), F32)],
        compiler_params=_params("parallel"),
    )(proj)


def _attn_specs(S):
    nb = S // BLOCK
    kcol, vcol = OFF_K // (N_KV_HEADS * HEAD_DIM), OFF_V // (N_KV_HEADS * HEAD_DIM)
    kvw = N_KV_HEADS * HEAD_DIM
    prev = lambda i: jnp.maximum(i - 1, 0)
    nxt = lambda i: jnp.minimum(i + 1, nb - 1)
    return [
        pl.BlockSpec((BLOCK, D_MODEL), lambda i: (i, OFF_Q // D_MODEL)),
        pl.BlockSpec((BLOCK, kvw), lambda i: (prev(i), kcol)),
        pl.BlockSpec((BLOCK, kvw), lambda i: (i, kcol)),
        pl.BlockSpec((BLOCK, kvw), lambda i: (nxt(i), kcol)),
        pl.BlockSpec((BLOCK, kvw), lambda i: (prev(i), vcol)),
        pl.BlockSpec((BLOCK, kvw), lambda i: (i, vcol)),
        pl.BlockSpec((BLOCK, kvw), lambda i: (nxt(i), vcol)),
    ]


def _heads_rows(ref_or_val, hk):
    return jnp.concatenate(
        [ref_or_val[:, (GQA * hk + g) * HEAD_DIM:(GQA * hk + g + 1) * HEAD_DIM] for g in range(GQA)], axis=0)


def _kv_rows(p_ref, c_ref, n_ref, hk):
    sl = slice(hk * HEAD_DIM, (hk + 1) * HEAD_DIM)
    return jnp.concatenate([p_ref[:, sl], c_ref[:, sl], n_ref[:, sl]], axis=0)


def _softmax_parts(q4s, kc, bias_blk, sink_blk):
    s = _nt(q4s, kc) + bias_blk
    m = jnp.maximum(jnp.max(s, axis=-1, keepdims=True), sink_blk)
    p = jnp.exp(s - m)
    e_sink = jnp.exp(sink_blk - m)
    inv = 1.0 / (jnp.sum(p, axis=-1, keepdims=True) + e_sink)
    return p * inv, e_sink * inv


TAB = (N_HEADS * BLOCK, 3 * BLOCK)


def _bias_spec(nb):
    return pl.BlockSpec((None,) + TAB, lambda i: (jnp.where(i == 0, 0, jnp.where(i == nb - 1, 2, 1)), 0, 0))


def _attn_fwd(proj, bias_tabs, sink_col):
    S = proj.shape[0]
    nb = S // BLOCK
    assert nb >= 2
    rows = GQA * BLOCK

    def body(q_ref, kp, kc_, kn, vp, vc_, vn, bias_ref, sink_ref, o_ref):
        outs = []
        for hk in range(N_KV_HEADS):
            q4s = _heads_rows(q_ref, hk) * SM_SCALE
            kc = _kv_rows(kp, kc_, kn, hk)
            vc = _kv_rows(vp, vc_, vn, hk)
            pn, _ = _softmax_parts(q4s, kc, bias_ref[pl.ds(hk * rows, rows), :], sink_ref[pl.ds(hk * rows, rows), :])
            o4 = _nn(pn.astype(BF16), vc)
            outs += [o4[g * BLOCK:(g + 1) * BLOCK, :] for g in range(GQA)]
        o_ref[...] = jnp.concatenate(outs, axis=1).astype(BF16)

    return pl.pallas_call(
        body, name="attn_fwd", grid=(nb,),
        in_specs=_attn_specs(S) + [_bias_spec(nb), pl.BlockSpec((N_HEADS * BLOCK, 1), lambda i: (0, 0))],
        out_specs=pl.BlockSpec((BLOCK, D_MODEL), lambda i: (i, 0)),
        out_shape=jax.ShapeDtypeStruct((S, D_MODEL), BF16),
        compiler_params=_params("parallel"),
    )(*([proj] * 7), bias_tabs, sink_col)


GATE_HALF = D_MODEL // 2


def _gate_specs(tm):
    return [pl.BlockSpec((tm, GATE_HALF), lambda i, c=off // GATE_HALF + k: (i, c))
            for off in (OFF_GA, OFF_GP, OFF_GT) for k in (0, 1)]


def _gate(lo_ref, hi_ref):
    return _sigmoid(jnp.concatenate([lo_ref[...], hi_ref[...]], axis=1).astype(F32))


def _pool_mix(p, wp):
    return jnp.concatenate(
        [_nn(p[:, g * POOL_CG:(g + 1) * POOL_CG], wp[g]) for g in range(POOL_GROUPS)], axis=1)


def _mix_fwd(za, p, att, proj, x, wa, wp, ps, wt, wo):
    S = x.shape[0]
    tm = _tile(S, 256)

    def body(za_ref, p_ref, att_ref, ga0, ga1, gp0, gp1, gt0, gt1, x_ref, wa_ref, wp_ref, ps_ref, wt_ref, wo_ref,
             ya_ref, yp_ref, yt_ref, mg_ref, x2_ref):
        ya = _nn(za_ref[...], wa_ref[...])
        ypr = _pool_mix(p_ref[...], wp_ref)
        yt = _nn(att_ref[...], wt_ref[...])
        merged = _gate(ga0, ga1) * ya + _gate(gp0, gp1) * (ypr * ps_ref[...]) + _gate(gt0, gt1) * yt
        mb = merged.astype(BF16)
        ya_ref[...] = ya.astype(BF16)
        yp_ref[...] = ypr.astype(BF16)
        yt_ref[...] = yt.astype(BF16)
        mg_ref[...] = mb
        x2_ref[...] = x_ref[...] + _nn(mb, wo_ref[...])

    row = lambda c=0: pl.BlockSpec((tm, D_MODEL), lambda i: (i, c))
    whole = lambda a: pl.BlockSpec(a.shape, lambda i: (0,) * a.ndim)
    act = jax.ShapeDtypeStruct((S, D_MODEL), BF16)
    return pl.pallas_call(
        body, name="mix_fwd", grid=(S // tm,),
        in_specs=[row(), row(), row()] + _gate_specs(tm) + [row(), whole(wa), whole(wp), whole(ps), whole(wt), whole(wo)],
        out_specs=[row(), row(), row(), row(), row()],
        out_shape=[act, act, act, act, jax.ShapeDtypeStruct((S, D_MODEL), F32)],
        compiler_params=_params("parallel"),
    )(za, p, att, *([proj] * 6), x, wa, wp, ps, wt, wo)


def _ffn_fwd(gu, x2, wd):
    S = x2.shape[0]
    tm = _tile(S, 256)

    def body(g_ref, u_ref, x_ref, w_ref, a_ref, o_ref):
        g = g_ref[...].astype(F32)
        a = (g * _sigmoid(g) * u_ref[...].astype(F32)).astype(BF16)
        a_ref[...] = a
        o_ref[...] = x_ref[...] + _nn(a, w_ref[...])

    return pl.pallas_call(
        body, name="ffn_fwd", grid=(S // tm,),
        in_specs=[pl.BlockSpec((tm, D_FF), lambda i: (i, 0)), pl.BlockSpec((tm, D_FF), lambda i: (i, 1)),
                  pl.BlockSpec((tm, D_MODEL), lambda i: (i, 0)), pl.BlockSpec((D_FF, D_MODEL), lambda i: (0, 0))],
        out_specs=[pl.BlockSpec((tm, D_FF), lambda i: (i, 0)), pl.BlockSpec((tm, D_MODEL), lambda i: (i, 0))],
        out_shape=[jax.ShapeDtypeStruct((S, D_FF), BF16), jax.ShapeDtypeStruct((S, D_MODEL), F32)],
        compiler_params=_params("parallel"),
    )(gu, gu, x2, wd)


def _loss_bwd(x, g, tgt):
    S, Dm = x.shape
    tm = _tile(S, 512)

    def body(x_ref, g_ref, t_ref, l_ref, dx_ref, dg_ref):
        @pl.when(pl.program_id(0) == 0)
        def _():
            l_ref[...] = jnp.zeros_like(l_ref)
            dg_ref[...] = jnp.zeros_like(dg_ref)
        xv, gv = x_ref[...], g_ref[...]
        r = lax.rsqrt(jnp.mean(xv * xv, axis=-1, keepdims=True) + EPS)
        n = xv * r
        err = n * gv - t_ref[...]
        l_ref[...] += 0.5 * jnp.sum(jnp.mean(err * err, axis=-1, keepdims=True), axis=0, keepdims=True)
        dy = err * (1.0 / Dm)
        dn = dy * gv
        dx_ref[...] = r * (dn - n * jnp.mean(dn * n, axis=-1, keepdims=True))
        dg_ref[...] += jnp.sum(dy * n, axis=0, keepdims=True)

    return pl.pallas_call(
        body, name="loss_bwd", grid=(S // tm,),
        in_specs=[pl.BlockSpec((tm, Dm), lambda i: (i, 0)), pl.BlockSpec((1, Dm), lambda i: (0, 0)),
                  pl.BlockSpec((tm, Dm), lambda i: (i, 0))],
        out_specs=[pl.BlockSpec((8, 128), lambda i: (0, 0)), pl.BlockSpec((tm, Dm), lambda i: (i, 0)),
                   pl.BlockSpec((1, Dm), lambda i: (0, 0))],
        out_shape=[jax.ShapeDtypeStruct((8, 128), F32), jax.ShapeDtypeStruct((S, Dm), F32),
                   jax.ShapeDtypeStruct((1, Dm), F32)],
        compiler_params=_params("arbitrary"),
    )(x, g, tgt)


def _ffn_bwd(dx3, gu, wd, token):
    S = dx3.shape[0]
    tm = _tile(S, 256)

    def body(d_ref, g_ref, u_ref, w_ref, token_ref, o_ref):
        dact = _nt(d_ref[...].astype(BF16), w_ref[...])
        g, u = g_ref[...].astype(F32), u_ref[...].astype(F32)
        sg = _sigmoid(g)
        o_ref[:, 0:D_FF] = (dact * u * (sg * (1.0 + g * (1.0 - sg)))).astype(BF16)
        o_ref[:, D_FF:2 * D_FF] = (dact * (g * sg)).astype(BF16)

    return pl.pallas_call(
        body, name="ffn_bwd", grid=(S // tm,),
        in_specs=[pl.BlockSpec((tm, D_MODEL), lambda i: (i, 0)),
                  pl.BlockSpec((tm, D_FF), lambda i: (i, 0)), pl.BlockSpec((tm, D_FF), lambda i: (i, 1)),
                  pl.BlockSpec((D_FF, D_MODEL), lambda i: (0, 0)), ANY],
        out_specs=pl.BlockSpec((tm, 2 * D_FF), lambda i: (i, 0)),
        out_shape=jax.ShapeDtypeStruct((S, 2 * D_FF), BF16),
        compiler_params=_params("parallel"),
    )(dx3, gu, gu, wd, token)


def _wgrad(a, b, name, tk=512, tn=512, out_dtype=BF16, token=None):
    S, K = a.shape
    N = b.shape[1]
    tk, tn, ts = _tile(K, tk), _tile(N, tn), _tile(S, 1024)
    n_s = S // ts
    extra = [] if token is None else [token]

    def body(a_ref, b_ref, *rest):
        o_ref, acc = rest[-2:]
        s = pl.program_id(2)

        @pl.when(s == 0)
        def _():
            acc[...] = jnp.zeros_like(acc)
        acc[...] += _tn(a_ref[...].astype(BF16), b_ref[...].astype(BF16))

        @pl.when(s == n_s - 1)
        def _():
            o_ref[...] = acc[...].astype(out_dtype)

    return pl.pallas_call(
        body, name=name, grid=(K // tk, N // tn, n_s),
        in_specs=[pl.BlockSpec((ts, tk), lambda k, n, s: (s, k)), pl.BlockSpec((ts, tn), lambda k, n, s: (s, n))]
        + [ANY] * len(extra),
        out_specs=pl.BlockSpec((tk, tn), lambda k, n, s: (k, n)),
        out_shape=jax.ShapeDtypeStruct((K, N), out_dtype),
        scratch_shapes=[pltpu.VMEM((tk, tn), F32)],
        compiler_params=_params("parallel", "parallel", "arbitrary"),
    )(a, b, *extra)


def _wgrad_pool(p, dyps):
    S = p.shape[0]
    ts = _tile(S, 512)
    n_s = S // ts

    def body(a_ref, b_ref, o_ref, acc):
        s = pl.program_id(1)

        @pl.when(s == 0)
        def _():
            acc[...] = jnp.zeros_like(acc)
        acc[...] += _tn(a_ref[...], b_ref[...])

        @pl.when(s == n_s - 1)
        def _():
            o_ref[...] = acc[...].astype(BF16)

    return pl.pallas_call(
        body, name="wgrad_pool", grid=(POOL_GROUPS, n_s),
        in_specs=[pl.BlockSpec((ts, POOL_CG), lambda g, s: (s, g)), pl.BlockSpec((ts, POOL_CG), lambda g, s: (s, g))],
        out_specs=pl.BlockSpec((None, POOL_CG, POOL_CG), lambda g, s: (g, 0, 0)),
        out_shape=jax.ShapeDtypeStruct((POOL_GROUPS, POOL_CG, POOL_CG), BF16),
        scratch_shapes=[pltpu.VMEM((POOL_CG, POOL_CG), F32)],
        compiler_params=_params("parallel", "arbitrary"),
    )(p, dyps)


def _dgrad_norm_bwd(dy, w, x, g, dres, name, tk):
    S, K = dy.shape
    Dm = x.shape[1]
    tm, tk = _tile(S, 512), _tile(K, tk)
    n_k = K // tk

    def body(dy_ref, w_ref, x_ref, g_ref, r_ref, dx_ref, dg_ref, acc):
        i, k = pl.program_id(0), pl.program_id(1)

        @pl.when((i == 0) & (k == 0))
        def _():
            dg_ref[...] = jnp.zeros_like(dg_ref)

        @pl.when(k == 0)
        def _():
            acc[...] = jnp.zeros_like(acc)
        acc[...] += _nt(dy_ref[...], w_ref[...])

        @pl.when(k == n_k - 1)
        def _():
            dh, xv = acc[...], x_ref[...]
            r = lax.rsqrt(jnp.mean(xv * xv, axis=-1, keepdims=True) + EPS)
            n = xv * r
            dn = dh * g_ref[...]
            dx_ref[...] = r_ref[...] + r * (dn - n * jnp.mean(dn * n, axis=-1, keepdims=True))
            dg_ref[...] += jnp.sum(dh * n, axis=0, keepdims=True)

    rowblk = pl.BlockSpec((tm, Dm), lambda i, k: (i, 0))
    vec = pl.BlockSpec((1, Dm), lambda i, k: (0, 0))
    return pl.pallas_call(
        body, name=name, grid=(S // tm, n_k),
        in_specs=[pl.BlockSpec((tm, tk), lambda i, k: (i, k)), pl.BlockSpec((Dm, tk), lambda i, k: (0, k)),
                  rowblk, vec, rowblk],
        out_specs=[rowblk, vec],
        out_shape=[jax.ShapeDtypeStruct((S, Dm), F32), jax.ShapeDtypeStruct((1, Dm), F32)],
        scratch_shapes=[pltpu.VMEM((tm, Dm), F32)],
        compiler_params=_params("arbitrary", "arbitrary"),
    )(dy, w, x, g, dres)


def _mix_bwd(dx2, ya, ypr, yt, proj, ps, wa, wp, wt, wo):
    S = dx2.shape[0]
    tm = _tile(S, 256)

    def body(dx_ref, ya_ref, yp_ref, yt_ref, ga0, ga1, gp0, gp1, gt0, gt1, ps_ref, wa_ref, wp_ref, wt_ref, wo_ref,
             dya_ref, dyt_ref, dyps_ref, dza_ref, datt_ref, dp_ref, dga_ref, dgp_ref, dgt_ref, dps_ref):
        @pl.when(pl.program_id(0) == 0)
        def _():
            dps_ref[...] = jnp.zeros_like(dps_ref)
        dm = _nt(dx_ref[...].astype(BF16), wo_ref[...])
        sa, sp, st = _gate(ga0, ga1), _gate(gp0, gp1), _gate(gt0, gt1)
        psv = ps_ref[...]
        ypr_v = yp_ref[...].astype(F32)
        dya = (sa * dm).astype(BF16)
        dyt = (st * dm).astype(BF16)
        dyp = sp * dm
        dyps = (dyp * psv).astype(BF16)
        dya_ref[...] = dya
        dyt_ref[...] = dyt
        dyps_ref[...] = dyps
        dga_ref[...] = (dm * ya_ref[...].astype(F32) * (sa * (1.0 - sa))).astype(BF16)
        dgp_ref[...] = (dm * (ypr_v * psv) * (sp * (1.0 - sp))).astype(BF16)
        dgt_ref[...] = (dm * yt_ref[...].astype(F32) * (st * (1.0 - st))).astype(BF16)
        dps_ref[...] += jnp.sum(dyp * ypr_v, axis=0, keepdims=True)
        dza_ref[...] = _nt(dya, wa_ref[...]).astype(BF16)
        datt_ref[...] = _nt(dyt, wt_ref[...]).astype(BF16)
        dp_ref[...] = jnp.concatenate(
            [_nt(dyps[:, g * POOL_CG:(g + 1) * POOL_CG], wp_ref[g]) for g in range(POOL_GROUPS)], axis=1).astype(BF16)

    row = lambda c=0: pl.BlockSpec((tm, D_MODEL), lambda i: (i, c))
    whole = lambda a: pl.BlockSpec(a.shape, lambda i: (0,) * a.ndim)
    act = jax.ShapeDtypeStruct((S, D_MODEL), BF16)
    return pl.pallas_call(
        body, name="mix_bwd", grid=(S // tm,),
        in_specs=[row(), row(), row(), row()] + _gate_specs(tm) + [whole(ps), whole(wa), whole(wp), whole(wt), whole(wo)],
        out_specs=[row()] * 9 + [pl.BlockSpec((1, D_MODEL), lambda i: (0, 0))],
        out_shape=[act] * 9 + [jax.ShapeDtypeStruct((1, D_MODEL), F32)],
        compiler_params=_params("arbitrary"),
    )(dx2, ya, ypr, yt, *([proj] * 6), ps, wa, wp, wt, wo)


def _conv_bwd(dza, proj, cw8):
    S = proj.shape[0]
    nblk = D_MODEL // CB

    def body(d_ref, b_ref, c_ref, x_ref, w_ref, db_ref, dc_ref, dxa_ref, dw_ref, pad_u, pad_d):
        c, xa = c_ref[...].astype(F32), x_ref[...].astype(F32)
        u = c * xa
        _fill_padded(pad_u, u, S)
        u_prev, u_next = _shifted(pad_u, -1, S), _shifted(pad_u, 1, S)
        cv = w_ref[0:1, :] * u_prev + w_ref[1:2, :] * u + w_ref[2:3, :] * u_next
        dza_v = d_ref[...].astype(F32)
        db_ref[...] = (dza_v * cv).astype(BF16)
        dcv = dza_v * b_ref[...].astype(F32)
        _fill_padded(pad_d, dcv, S)
        du = w_ref[0:1, :] * _shifted(pad_d, 1, S) + w_ref[1:2, :] * dcv + w_ref[2:3, :] * _shifted(pad_d, -1, S)
        dc_ref[...] = (du * xa).astype(BF16)
        dxa_ref[...] = (du * c).astype(BF16)
        dw_ref[...] = jnp.concatenate(
            [jnp.sum(dcv * u_prev, axis=0, keepdims=True), jnp.sum(dcv * u, axis=0, keepdims=True),
             jnp.sum(dcv * u_next, axis=0, keepdims=True), jnp.zeros((5, CB), F32)], axis=0)

    col = lambda base: pl.BlockSpec((S, CB), lambda j: (0, base // CB + j))
    act = jax.ShapeDtypeStruct((S, D_MODEL), BF16)
    return pl.pallas_call(
        body, name="conv_bwd", grid=(nblk,),
        in_specs=[col(0), col(OFF_B), col(OFF_C), col(OFF_X), pl.BlockSpec((8, CB), lambda j: (0, j))],
        out_specs=[col(0), col(0), col(0), pl.BlockSpec((8, CB), lambda j: (0, j))],
        out_shape=[act, act, act, jax.ShapeDtypeStruct((8, D_MODEL), F32)],
        scratch_shapes=[pltpu.VMEM((S + 2 * HALO, CB), F32), pltpu.VMEM((S + 2 * HALO, CB), F32)],
        compiler_params=_params("parallel"),
    )(dza, proj, proj, proj, cw8)


def _pool_bwd(dp):
    S = dp.shape[0]
    nblk = D_MODEL // CB
    per_group = POOL_CG // CB

    def body(d_ref, o_ref, pad):
        d = d_ref[...].astype(F32)
        grp = pl.program_id(0) // per_group
        for gi, w in enumerate(POOL_WINDOWS):
            @pl.when(grp == gi)
            def _(w=w):
                lo, hi = w // 2, w - 1 - w // 2
                _fill_padded(pad, d / _pool_count(S, lo, hi), S)
                acc = _shifted(pad, -hi, S)
                for off in range(-hi + 1, lo + 1):
                    acc = acc + _shifted(pad, off, S)
                o_ref[...] = (acc - d).astype(BF16)

    return pl.pallas_call(
        body, name="pool_bwd", grid=(nblk,),
        in_specs=[pl.BlockSpec((S, CB), lambda j: (0, j))],
        out_specs=pl.BlockSpec((S, CB), lambda j: (0, j)),
        out_shape=jax.ShapeDtypeStruct((S, D_MODEL), BF16),
        scratch_shapes=[pltpu.VMEM((S + 2 * HALO, CB), F32)],
        compiler_params=_params("parallel"),
    )(dp)


def _attn_bwd(proj, datt, bias_tabs, sink_col, dbias_in):
    S = proj.shape[0]
    nb = S // BLOCK
    rows = GQA * BLOCK
    kvw = N_KV_HEADS * HEAD_DIM

    def body(q_ref, kp, kc_, kn, vp, vc_, vn, do_ref, bias_ref, sink_ref, dbin_ref,
             dq_ref, dk_ref, dv_ref, db_ref, ds_ref):
        i = pl.program_id(0)

        @pl.when(i == 0)
        def _():
            dk_ref[...] = jnp.zeros_like(dk_ref)
            dv_ref[...] = jnp.zeros_like(dv_ref)
            db_ref[...] = dbin_ref[...]
            ds_ref[...] = jnp.zeros_like(ds_ref)
        dqs, dks, dvs = [], [], []
        for hk in range(N_KV_HEADS):
            q4s = _heads_rows(q_ref, hk) * SM_SCALE
            do4 = _heads_rows(do_ref, hk)
            kc = _kv_rows(kp, kc_, kn, hk)
            vc = _kv_rows(vp, vc_, vn, hk)
            blk = pl.ds(hk * rows, rows)
            pn, p_sink = _softmax_parts(q4s, kc, bias_ref[blk, :], sink_ref[blk, :])
            dpm = _nt(do4, vc)
            delta = jnp.sum(pn * dpm, axis=-1, keepdims=True)
            dsc = pn * (dpm - delta)
            db_ref[blk, :] += dsc
            ds_ref[blk, :] += -p_sink * delta
            dsb = dsc.astype(BF16)
            dq4 = _nn(dsb, kc) * SM_SCALE
            dqs += [dq4[g * BLOCK:(g + 1) * BLOCK, :] for g in range(GQA)]
            dks.append(_tn(dsb, q4s))
            dvs.append(_tn(pn.astype(BF16), do4))
        dq_ref[...] = jnp.concatenate(dqs, axis=1).astype(BF16)
        r0 = pl.multiple_of(i * BLOCK, BLOCK)
        dk_ref[pl.ds(r0, 3 * BLOCK), :] += jnp.concatenate(dks, axis=1)
        dv_ref[pl.ds(r0, 3 * BLOCK), :] += jnp.concatenate(dvs, axis=1)

    const = lambda shape: pl.BlockSpec(shape, lambda i: (0, 0))
    return pl.pallas_call(
        body, name="attn_bwd", grid=(nb,),
        in_specs=_attn_specs(S) + [pl.BlockSpec((BLOCK, D_MODEL), lambda i: (i, 0)),
                                   _bias_spec(nb), const((N_HEADS * BLOCK, 1)), const(TAB)],
        out_specs=[pl.BlockSpec((BLOCK, D_MODEL), lambda i: (i, 0)),
                   const((S + 2 * BLOCK, kvw)), const((S + 2 * BLOCK, kvw)), const(TAB), const((N_HEADS * BLOCK, 1))],
        out_shape=[jax.ShapeDtypeStruct((S, D_MODEL), BF16),
                   jax.ShapeDtypeStruct((S + 2 * BLOCK, kvw), F32), jax.ShapeDtypeStruct((S + 2 * BLOCK, kvw), F32),
                   jax.ShapeDtypeStruct(TAB, F32), jax.ShapeDtypeStruct((N_HEADS * BLOCK, 1), F32)],
        compiler_params=_params("arbitrary"),
    )(*([proj] * 7), datt, bias_tabs, sink_col, dbias_in)


def _bucket_constants():
    half = N_BUCKETS // 2
    max_exact = half // 2
    qi = np.arange(BLOCK)[:, None]
    kj = np.arange(3 * BLOCK)[None, :]
    rel = kj - BLOCK - qi
    n = np.abs(rel)
    nf = np.maximum(n, 1).astype(np.float32)
    large = max_exact + (np.log(nf / np.float32(max_exact)) / np.float32(math.log(MAX_DISTANCE / max_exact))
                         * np.float32(half - max_exact)).astype(np.int32)
    large = np.minimum(large, half - 1)
    bucket = np.where(rel > 0, half, 0) + np.where(n < max_exact, n, large)
    onehot = (bucket.reshape(1, -1) == np.arange(N_BUCKETS)[:, None]).astype(np.float32)
    window = n <= WINDOW
    first = window & (kj >= BLOCK)
    last = window & (kj < 2 * BLOCK)
    masks = np.stack([np.where(v, 0.0, NEG_INF).astype(np.float32).reshape(-1) for v in (first, window, last)])
    return onehot, masks


def _bias_expand(rel_bias_t, onehot, masks):
    def body(r_ref, oh_ref, m_ref, o_ref):
        tab = jnp.dot(r_ref[...], oh_ref[...], preferred_element_type=F32, precision=lax.Precision.HIGHEST)
        for v in range(3):
            o_ref[v] = tab + m_ref[v:v + 1, :]

    return pl.pallas_call(
        body, name="bias_expand", out_shape=jax.ShapeDtypeStruct((3, N_HEADS, onehot.shape[1]), F32),
        compiler_params=_params(),
    )(rel_bias_t, onehot, masks)


def _bias_reduce(dtab, dsink_rows, onehot):
    def body(d_ref, s_ref, oh_ref, o_ref, so_ref):
        o_ref[...] = lax.dot_general(oh_ref[...], d_ref[...], (((1,), (1,)), ((), ())),
                                     preferred_element_type=F32, precision=lax.Precision.HIGHEST)
        so_ref[...] = jnp.sum(s_ref[...], axis=-1, keepdims=True)

    return pl.pallas_call(
        body, name="bias_reduce",
        out_shape=[jax.ShapeDtypeStruct((N_BUCKETS, N_HEADS), F32),
                   jax.ShapeDtypeStruct((dsink_rows.shape[0], 1), F32)],
        compiler_params=_params(),
    )(dtab, dsink_rows, onehot)


WEIGHT_NAMES = ("w_in", "conv_w", "w_a_out", "w_pool", "w_attn_out", "w_o", "w_gu", "w_down")


def _layer_fwd(x, W, ps, g_mix, g_ffn, bias_tabs, sink_col, token):
    h, proj = _norm_matmul(x, g_mix, W["w_in"], "norm_proj", token)
    za = _conv_fwd(proj, W["conv_w"])
    p = _pool_fwd(proj)
    att = _attn_fwd(proj, bias_tabs, sink_col)
    ya, ypr, yt, merged, x2 = _mix_fwd(za, p, att, proj, x, W["w_a_out"], W["w_pool"], ps, W["w_attn_out"], W["w_o"])
    h2, gu = _norm_matmul(x2, g_ffn, W["w_gu"], "norm_gu", token)
    act, x3 = _ffn_fwd(gu, x2, W["w_down"])
    saved = dict(x=x, h=h, proj=proj, za=za, p=p, att=att, ya=ya, ypr=ypr, yt=yt, merged=merged, x2=x2, h2=h2,
                 gu=gu, act=act)
    return x3, saved


def _layer_bwd(dx3, sv, W, ps, g_mix, g_ffn, bias_tabs, sink_col, dbias, token):
    S = dx3.shape[0]
    dgu = _ffn_bwd(dx3, sv["gu"], W["w_down"], token)
    g_w_down = _wgrad(sv["act"], dx3, "wgrad_down", tk=_tile(D_FF, 1408), tn=512, token=token)
    g_w_gu = _wgrad(sv["h2"], dgu, "wgrad_gu", tk=1024, tn=512)
    dx2, dg_ffn = _dgrad_norm_bwd(dgu, W["w_gu"], sv["x2"], g_ffn, dx3, "dgrad_gu", tk=1408)
    dya, dyt, dyps, dza, datt, dp, dga, dgp, dgt, dps = _mix_bwd(
        dx2, sv["ya"], sv["ypr"], sv["yt"], sv["proj"], ps, W["w_a_out"], W["w_pool"], W["w_attn_out"], W["w_o"])
    g_w_o = _wgrad(sv["merged"], dx2, "wgrad_sq_f32", tk=1024, tn=512)
    g_w_a_out = _wgrad(sv["za"], dya, "wgrad_sq", tk=1024, tn=512)
    g_w_attn_out = _wgrad(sv["att"], dyt, "wgrad_sq", tk=1024, tn=512)
    g_w_pool = _wgrad_pool(sv["p"], dyps)
    db, dc, dxa, g_conv = _conv_bwd(dza, sv["proj"], W["conv_w"])
    dup = _pool_bwd(dp)
    dq, dkp, dvp, dbias, dsink = _attn_bwd(sv["proj"], datt, bias_tabs, sink_col, dbias)
    dproj = jnp.concatenate([db, dc, dxa, dup, dq, dkp[BLOCK:BLOCK + S].astype(BF16),
                             dvp[BLOCK:BLOCK + S].astype(BF16), dga, dgp, dgt], axis=1)
    g_w_in = _wgrad(sv["h"], dproj, "wgrad_in", tk=1024, tn=512)
    dx, dg_mix = _dgrad_norm_bwd(dproj, W["w_in"], sv["x"], g_mix, dx2, "dgrad_in", tk=2176)
    grads = dict(w_in=g_w_in, conv_w=g_conv, w_a_out=g_w_a_out, w_pool=g_w_pool, w_attn_out=g_w_attn_out,
                 w_o=g_w_o, w_gu=g_w_gu, w_down=g_w_down)
    return dx, grads, dict(pool_scale=dps, g_mix=dg_mix, g_ffn=dg_ffn, attn_sink=dsink), dbias


def _local_step(x, tgt, weights_of, grads_to, pool_scale, attn_sink, g_mix, g_ffn, rel_bias, g_final):
    onehot_np, masks_np = _bucket_constants()
    onehot, masks = jnp.asarray(onehot_np), jnp.asarray(masks_np)
    bias_tabs = _bias_expand(rel_bias.T, onehot, masks).reshape((3,) + TAB)
    saved, weights = [], []
    for l in range(DEPTH):
        W, token = weights_of(l, x)
        sink_col = jnp.repeat(attn_sink[l], BLOCK).reshape(N_HEADS * BLOCK, 1)
        x, sv = _layer_fwd(x, W, pool_scale[l:l + 1], g_mix[l:l + 1], g_ffn[l:l + 1], bias_tabs, sink_col, token)
        sv["sink_col"] = sink_col
        saved.append(sv)
        weights.append(W)
    loss, dx, dg_final = _loss_bwd(x, g_final.reshape(1, D_MODEL), tgt)
    dbias = jnp.zeros(TAB, F32)
    small = [None] * DEPTH
    for l in reversed(range(DEPTH)):
        dx, wgrads, small[l], dbias = _layer_bwd(
            dx, saved[l], weights[l], pool_scale[l:l + 1], g_mix[l:l + 1], g_ffn[l:l + 1], bias_tabs,
            saved[l]["sink_col"], dbias, token)
        token = grads_to(l, wgrads, dx)
    dsink_rows = jnp.concatenate([small[l]["attn_sink"].reshape(N_HEADS, BLOCK) for l in range(DEPTH)], axis=0)
    d_rel_bias, d_sink = _bias_reduce(dbias.reshape(N_HEADS, BLOCK * 3 * BLOCK), dsink_rows, onehot)
    cat = lambda k: jnp.concatenate([small[l][k] for l in range(DEPTH)], axis=0)
    smalls = dict(pool_scale=cat("pool_scale"), g_mix=cat("g_mix"), g_ffn=cat("g_ffn"),
                  attn_sink=d_sink.reshape(DEPTH, N_HEADS), rel_bias=d_rel_bias, g_final=dg_final)
    return loss[0, 0], dx, smalls


SHARD_AXIS = dict(w_in=(1, IN_TOTAL // N_CHIPS), conv_w=(1, D_MODEL // N_CHIPS), w_a_out=(0, D_MODEL // N_CHIPS),
                  w_pool=(1, POOL_CG // N_CHIPS), w_attn_out=(0, D_MODEL // N_CHIPS), w_o=(0, D_MODEL // N_CHIPS),
                  w_gu=(1, 2 * D_FF // N_CHIPS), w_down=(0, D_FF // N_CHIPS))
N_W = len(WEIGHT_NAMES)
HBM = pl.BlockSpec(memory_space=pltpu.HBM)
SEM = pl.BlockSpec(memory_space=pltpu.SEMAPHORE)
DATAFLOW = pltpu.SideEffectType.DATAFLOW_SIDE_EFFECTING
TOKEN = jax.ShapeDtypeStruct((8, 128), F32)


def _shard_of(ref, name, chip):
    axis, n = SHARD_AXIS[name]
    idx = [slice(None)] * len(ref.shape)
    idx[axis] = pl.ds(chip * n, n)
    return ref.at[tuple(idx)]


def _with_shard_axis(name, shape, size):
    axis, _ = SHARD_AXIS[name]
    s = list(shape)
    s[axis] = size
    return tuple(s)


def _chip_peers(x, y):
    return [(1 - x, y), (x, 1 - y), (1 - x, 1 - y)]


RELATION_XOR = (2, 1, 3)


def _layer_copies(kind, srcs, lands, send_sems, recv_sems, local_sems, chip):
    x, y, c = lax.axis_index("x"), lax.axis_index("y"), lax.axis_index("c")
    copies = []
    for t, name in enumerate(WEIGHT_NAMES):
        for j, (px, py) in enumerate(_chip_peers(x, y)):
            if kind == "gather":
                src, dst = srcs[t], _shard_of(lands[t], name, chip)
            else:
                src, dst = _shard_of(srcs[t], name, chip ^ RELATION_XOR[j]), lands[t].at[j]
            copies.append(pltpu.make_async_remote_copy(
                src_ref=src, dst_ref=dst, send_sem=send_sems.at[3 * t + j], recv_sem=recv_sems.at[3 * t + j],
                device_id=(px, py, c), device_id_type=MESH))
        if kind == "gather":
            src, dst = srcs[t], _shard_of(lands[t], name, chip)
        else:
            src, dst = _shard_of(srcs[t], name, chip), lands[t].at[N_CHIPS - 1]
        copies.append(pltpu.make_async_copy(src, dst, local_sems.at[t]))
    return copies


def _exchange_start(kind, srcs, land_shapes, after):
    def body(*refs):
        srcs_r, lands_r = refs[:N_W], refs[N_W:2 * N_W]
        send_sems, recv_sems, local_sems = refs[2 * N_W + 1:2 * N_W + 4]
        token = refs[-1]
        me = 2 * lax.axis_index("x") + lax.axis_index("y")
        for chip in range(N_CHIPS):
            @pl.when(me == chip)
            def _(chip=chip):
                for cp in _layer_copies(kind, srcs_r, lands_r, send_sems, recv_sems, local_sems, chip):
                    cp.start()
        token[...] = jnp.zeros_like(token)

    lands = [pltpu.with_memory_space_constraint(lax.empty(s.shape, s.dtype), pltpu.HBM) for s in land_shapes]
    srcs = [pltpu.with_memory_space_constraint(a, pltpu.HBM) for a in srcs]
    thru = [pltpu.HBM(a.shape, a.dtype) for a in srcs + lands]
    outs = pl.pallas_call(
        body, name=kind + "_start",
        in_specs=[HBM] * (2 * N_W) + [ANY],
        out_specs=[SEM, SEM, SEM] + [HBM] * (2 * N_W) + [pl.BlockSpec(memory_space=pltpu.VMEM)],
        out_shape=[pltpu.SemaphoreType.DMA((3 * N_W,)), pltpu.SemaphoreType.DMA((3 * N_W,)),
                   pltpu.SemaphoreType.DMA((N_W,))] + thru + [TOKEN],
        input_output_aliases={t: 3 + t for t in range(2 * N_W)},
        compiler_params=pltpu.CompilerParams(has_side_effects=DATAFLOW),
    )(*srcs, *lands, after)
    return dict(sems=outs[0:3], srcs=outs[3:3 + N_W], lands=outs[3 + N_W:3 + 2 * N_W], token=outs[-1])


def _exchange_wait(kind, started, after):
    def body(*refs):
        srcs_r, lands_r = refs[:N_W], refs[N_W:2 * N_W]
        send_sems, recv_sems, local_sems = refs[2 * N_W:2 * N_W + 3]
        for cp in _layer_copies(kind, srcs_r, lands_r, send_sems, recv_sems, local_sems, 0):
            cp.wait()

    srcs, lands = list(started["srcs"]), list(started["lands"])
    outs = pl.pallas_call(
        body, name=kind + "_wait",
        in_specs=[HBM] * (2 * N_W) + [SEM, SEM, SEM, ANY],
        out_specs=[HBM] * (2 * N_W),
        out_shape=[pltpu.HBM(a.shape, a.dtype) for a in srcs + lands],
        input_output_aliases={t: t for t in range(2 * N_W)},
        compiler_params=pltpu.CompilerParams(has_side_effects=DATAFLOW),
    )(*srcs, *lands, *started["sems"], after)
    return dict(zip(WEIGHT_NAMES, outs[N_W:]))


def _gather_start(shards, after):
    shapes = [jax.ShapeDtypeStruct(_with_shard_axis(n, shards[n].shape, SHARD_AXIS[n][1] * N_CHIPS), shards[n].dtype)
              for n in WEIGHT_NAMES]
    return _exchange_start("gather", [shards[n] for n in WEIGHT_NAMES], shapes, after)


def _scatter_start(grads, after):
    shapes = [jax.ShapeDtypeStruct((N_CHIPS,) + _with_shard_axis(n, grads[n].shape, SHARD_AXIS[n][1]), grads[n].dtype)
              for n in WEIGHT_NAMES]
    return _exchange_start("scatter", [grads[n] for n in WEIGHT_NAMES], shapes, after)


def _sibling_exchange(parts):
    n = len(parts)

    def body(*refs):
        ins, outs = refs[:n], refs[n:2 * n]
        send_sems, recv_sems = refs[2 * n:]
        sibling = (lax.axis_index("x"), lax.axis_index("y"), 1 - lax.axis_index("c"))
        copies = [pltpu.make_async_remote_copy(src_ref=ins[t], dst_ref=outs[t], send_sem=send_sems.at[t],
                                               recv_sem=recv_sems.at[t], device_id=sibling, device_id_type=MESH)
                  for t in range(n)]
        for cp in copies:
            cp.start()
        for cp in copies:
            cp.wait()

    outs = pl.pallas_call(
        body, name="sibling_exchange", in_specs=[ANY] * n, out_specs=[ANY] * n,
        out_shape=[jax.ShapeDtypeStruct(p.shape, p.dtype) for p in parts],
        scratch_shapes=[pltpu.SemaphoreType.DMA((n,)), pltpu.SemaphoreType.DMA((n,))],
        compiler_params=pltpu.CompilerParams(has_side_effects=True),
    )(*parts)
    return list(outs)


N_DEV = 8


def _all_reduce_small(v):
    R, C = v.shape

    def body(v_ref, o_ref, slots, send_sems, recv_sems):
        x, y, c = lax.axis_index("x"), lax.axis_index("y"), lax.axis_index("c")
        me = 4 * x + 2 * y + c
        slots[me] = v_ref[...]
        copies = []
        for k in range(1, N_DEV):
            peer = me ^ k
            cp = pltpu.make_async_remote_copy(
                src_ref=v_ref, dst_ref=slots.at[me], send_sem=send_sems.at[k - 1], recv_sem=recv_sems.at[k - 1],
                device_id=(peer // 4, (peer // 2) % 2, peer % 2), device_id_type=MESH)
            cp.start()
            copies.append(cp)
        for cp in copies:
            cp.wait()
        acc = slots[0]
        for k in range(1, N_DEV):
            acc = acc + slots[k]
        o_ref[...] = acc

    return pl.pallas_call(
        body, name="all_reduce_small", out_shape=jax.ShapeDtypeStruct((R, C), F32),
        in_specs=[pl.BlockSpec(memory_space=pltpu.VMEM)], out_specs=pl.BlockSpec(memory_space=pltpu.VMEM),
        scratch_shapes=[pltpu.VMEM((N_DEV, R, C), F32), pltpu.SemaphoreType.DMA((N_DEV - 1,)),
                        pltpu.SemaphoreType.DMA((N_DEV - 1,))],
        compiler_params=pltpu.CompilerParams(has_side_effects=True),
    )(v)


def _as2d(shape):
    return (int(np.prod(shape[:-1])), shape[-1])


def _row_block(rows, cols, n_arrays):
    budget = V7X_VMEM_LIMIT // 2
    tr = rows
    while tr % 16 == 0 and 2 * n_arrays * tr * cols * 4 > budget:
        tr //= 2
    return tr


def _sum_slots(slots):
    _, R, C = slots.shape
    tr = _row_block(R, C, 5)

    def body(s_ref, o_ref):
        acc = s_ref[0].astype(F32)
        for k in range(1, N_CHIPS):
            acc = acc + s_ref[k].astype(F32)
        o_ref[...] = acc

    return pl.pallas_call(
        body, name="sum_slots", grid=(R // tr,),
        in_specs=[pl.BlockSpec((N_CHIPS, tr, C), lambda i: (0, i, 0))],
        out_specs=pl.BlockSpec((tr, C), lambda i: (i, 0)),
        out_shape=jax.ShapeDtypeStruct((R, C), F32),
        compiler_params=_params("parallel"),
    )(slots)


def _adamw(l, w, m, v, g_a, g_b, prev):
    L, R, C = w.shape
    tr = _row_block(R, C, 9)
    c1 = 1.0 - ADAM_B1 ** ADAM_STEP
    c2 = 1.0 - ADAM_B2 ** ADAM_STEP

    def body(w_ref, m_ref, v_ref, a_ref, b_ref, *rest):
        g_ref, d_ref, nm_ref, nv_ref = rest[-4:]
        g = a_ref[...] + b_ref[...]
        nm = ADAM_B1 * m_ref[...] + (1.0 - ADAM_B1) * g
        nv = ADAM_B2 * v_ref[...] + (1.0 - ADAM_B2) * (g * g)
        g_ref[...] = g
        nm_ref[...] = nm
        nv_ref[...] = nv
        d_ref[...] = -ADAM_LR * ((nm / c1) / (jnp.sqrt(nv / c2) + ADAM_EPS) + ADAM_WD * w_ref[...])

    layer = pl.BlockSpec((None, tr, C), lambda i: (l, i, 0))
    blk = pl.BlockSpec((tr, C), lambda i: (i, 0))
    out = jax.ShapeDtypeStruct((L, R, C), F32)
    prev = [] if prev is None else list(prev)
    return pl.pallas_call(
        body, name="adamw", grid=(R // tr,), in_specs=[layer] * 3 + [blk] * 2 + [ANY] * len(prev),
        out_specs=[layer] * 4, out_shape=[out] * 4,
        input_output_aliases={5 + k: k for k in range(len(prev))},
        compiler_params=_params("parallel"),
    )(w, m, v, g_a, g_b, *prev)


SMALL_ROWS = 16


def _pack_small(pool_scale, g_mix, g_ffn, g_final, attn_sink, rel_bias):
    tail = jnp.concatenate([attn_sink.reshape(-1), rel_bias.reshape(-1)])
    tail = jnp.pad(tail, (0, D_MODEL - tail.shape[0])).reshape(1, D_MODEL)
    rows = jnp.concatenate([pool_scale, g_mix, g_ffn, g_final.reshape(1, D_MODEL), tail], axis=0)
    return jnp.pad(rows, ((0, SMALL_ROWS - rows.shape[0]), (0, 0)))


def _unpack_small(packed):
    n_sink = DEPTH * N_HEADS
    return dict(pool_scale=packed[0:4], g_mix=packed[4:8], g_ffn=packed[8:12], g_final=packed[12],
                attn_sink=packed[13, 0:n_sink].reshape(DEPTH, N_HEADS),
                rel_bias=packed[13, n_sink:n_sink + N_BUCKETS * N_HEADS].reshape(N_BUCKETS, N_HEADS))


def _layer_shards(l, w_in, conv_w, w_a_out, w_pool, w_attn_out, w_o, w_gu, w_down):
    return dict(
        w_in=w_in[l].astype(BF16),
        conv_w=jnp.pad(conv_w[l].reshape(3, -1), ((0, 5), (0, 0))),
        w_a_out=w_a_out[l].astype(BF16), w_pool=w_pool[l].astype(BF16), w_attn_out=w_attn_out[l].astype(BF16),
        w_o=w_o[l].astype(BF16), w_gu=w_gu[l].astype(BF16), w_down=w_down[l].astype(BF16))


def kernel(x, w_in, conv_w, w_a_out, w_pool, pool_scale, w_attn_out, attn_sink, w_o, g_mix, g_ffn, w_gu, w_down, rel_bias, g_final, loss_target, m_w_in, m_conv_w, m_w_a_out, m_w_pool, m_pool_scale, m_w_attn_out, m_attn_sink, m_w_o, m_g_mix, m_g_ffn, m_w_gu, m_w_down, m_rel_bias, m_g_final, v_w_in, v_conv_w, v_w_a_out, v_w_pool, v_pool_scale, v_w_attn_out, v_attn_sink, v_w_o, v_g_mix, v_g_ffn, v_w_gu, v_w_down, v_rel_bias, v_g_final):
    big = dict(w_in=(w_in, m_w_in, v_w_in), conv_w=(conv_w, m_conv_w, v_conv_w), w_a_out=(w_a_out, m_w_a_out, v_w_a_out),
               w_pool=(w_pool, m_w_pool, v_w_pool), w_attn_out=(w_attn_out, m_w_attn_out, v_w_attn_out),
               w_o=(w_o, m_w_o, v_w_o), w_gu=(w_gu, m_w_gu, v_w_gu), w_down=(w_down, m_w_down, v_w_down))

    big3 = {n: tuple(a.reshape((DEPTH,) + _as2d(a.shape[1:])) for a in big[n]) for n in WEIGHT_NAMES}
    shards = [_layer_shards(l, w_in, conv_w, w_a_out, w_pool, w_attn_out, w_o, w_gu, w_down) for l in range(DEPTH)]

    gathers = {0: _gather_start(shards[0], rel_bias)}

    def weights_of(l, x_l):
        W = _exchange_wait("gather", gathers[l], x_l)
        if l + 1 < DEPTH:
            gathers[l + 1] = _gather_start(shards[l + 1], W["w_in"])
        return W, gathers[min(l + 1, DEPTH - 1)]["token"]

    results = {n: None for n in WEIGHT_NAMES}
    scatters = {}

    def finish(l, slots):
        parts = [_sum_slots(slots[n].reshape((N_CHIPS,) + _as2d(slots[n].shape[1:]))) for n in WEIGHT_NAMES]
        others = _sibling_exchange(parts)
        for n, mine, other in zip(WEIGHT_NAMES, parts, others):
            if n == "conv_w":
                mine, other = mine[0:3], other[0:3]
            results[n] = _adamw(l, *big3[n], mine, other, results[n])

    def grads_to(l, wgrads, dx_l):
        after = dx_l
        if l + 1 < DEPTH:
            slots = _exchange_wait("scatter", scatters.pop(l + 1), dx_l)
            after = slots["w_in"]
        scatters[l] = _scatter_start(wgrads, after)
        if l + 1 < DEPTH:
            finish(l + 1, slots)
        return scatters[l]["token"]

    loss, grad_x, smalls = _local_step(x[0], loss_target[0], weights_of, grads_to, pool_scale, attn_sink, g_mix, g_ffn,
                                       rel_bias, g_final)
    finish(0, _exchange_wait("scatter", scatters.pop(0), grad_x))
    stacked = {n: [o.reshape(big[n][0].shape) for o in results[n]] for n in WEIGHT_NAMES}

    g_small = _all_reduce_small(_pack_small(smalls["pool_scale"], smalls["g_mix"], smalls["g_ffn"], smalls["g_final"],
                                            smalls["attn_sink"], smalls["rel_bias"]))
    w_small = _pack_small(pool_scale, g_mix, g_ffn, g_final, attn_sink, rel_bias)
    m_small = _pack_small(m_pool_scale, m_g_mix, m_g_ffn, m_g_final, m_attn_sink, m_rel_bias)
    v_small = _pack_small(v_pool_scale, v_g_mix, v_g_ffn, v_g_final, v_attn_sink, v_rel_bias)
    small_out = [_unpack_small(o[0]) for o in
                 _adamw(0, w_small[None], m_small[None], v_small[None], g_small, jnp.zeros_like(g_small), None)]

    total_loss = lax.psum(loss, ("x", "y", "c"))

    order = ("w_in", "conv_w", "w_a_out", "w_pool", "pool_scale", "w_attn_out", "attn_sink", "w_o", "g_mix", "g_ffn",
             "w_gu", "w_down", "rel_bias", "g_final")
    outs = [total_loss, grad_x[None]]
    for k in range(4):
        for n in order:
            outs.append(stacked[n][k] if n in stacked else small_out[k][n])
    return tuple(outs)
```

```python
import functools
import math

import numpy as np
import jax
import jax.numpy as jnp
from jax import lax
from jax.experimental import pallas as pl
from jax.experimental.pallas import tpu as pltpu

F32 = jnp.float32
BF16 = jnp.bfloat16

D_MODEL = 1024
DEPTH = 4
N_HEADS = 16
N_KV_HEADS = 4
HEAD_DIM = 64
GQA = N_HEADS // N_KV_HEADS
WINDOW = 128
BLOCK = 128
N_BUCKETS = 32
MAX_DISTANCE = 128
POOL_GROUPS = 4
POOL_CG = D_MODEL // POOL_GROUPS
POOL_WINDOWS = (2, 4, 8, 16)
D_FF = 2816
IN_TOTAL = 8704
OFF_B, OFF_C, OFF_X, OFF_U, OFF_Q, OFF_K, OFF_V, OFF_GA, OFF_GP, OFF_GT = (
    0, 1024, 2048, 3072, 4096, 5120, 5376, 5632, 6656, 7680)
EPS = 1e-6
NEG_INF = -1e30
SM_SCALE = HEAD_DIM ** -0.5

ADAM_LR = 0.001
ADAM_B1 = 0.9
ADAM_B2 = 0.999
ADAM_EPS = 1e-08
ADAM_WD = 0.01
ADAM_STEP = 10

N_CHIPS = 4
HALO = 8
V7X_VMEM_LIMIT = 56 * 1024 * 1024
MESH = pl.DeviceIdType.MESH
ANY = pl.BlockSpec(memory_space=pl.ANY)


def _params(*sem):
    return pltpu.CompilerParams(dimension_semantics=tuple(sem) if sem else None,
                                vmem_limit_bytes=V7X_VMEM_LIMIT)


def _tile(n, pref):
    t = min(pref, n)
    while n % t or t % 128:
        t -= 128
    return t


def _nt(a, b):
    return lax.dot_general(a, b, (((1,), (1,)), ((), ())), preferred_element_type=F32)


def _tn(a, b):
    return lax.dot_general(a, b, (((0,), (0,)), ((), ())), preferred_element_type=F32)


def _nn(a, b):
    return jnp.dot(a, b, preferred_element_type=F32)


def _sigmoid(v):
    return 1.0 / (1.0 + jnp.exp(-v))


def _norm_matmul(x, g, w, name, token):
    S, Dm = x.shape
    N = w.shape[1]
    tm, tn = _tile(S, 1024), _tile(N, 512)

    def body(x_ref, g_ref, w_ref, token_ref, h_ref, o_ref):
        @pl.when(pl.program_id(1) == 0)
        def _():
            xv = x_ref[...]
            r = lax.rsqrt(jnp.mean(xv * xv, axis=-1, keepdims=True) + EPS)
            h_ref[...] = (xv * r * g_ref[...]).astype(BF16)
        o_ref[...] = _nn(h_ref[...], w_ref[...]).astype(BF16)

    return pl.pallas_call(
        body, name=name, grid=(S // tm, N // tn),
        in_specs=[pl.BlockSpec((tm, Dm), lambda i, j: (i, 0)),
                  pl.BlockSpec((1, Dm), lambda i, j: (0, 0)),
                  pl.BlockSpec((Dm, tn), lambda i, j: (0, j)), ANY],
        out_specs=[pl.BlockSpec((tm, Dm), lambda i, j: (i, 0)),
                   pl.BlockSpec((tm, tn), lambda i, j: (i, j))],
        out_shape=[jax.ShapeDtypeStruct((S, Dm), BF16), jax.ShapeDtypeStruct((S, N), BF16)],
        compiler_params=_params("parallel", "arbitrary"),
    )(x, g, w, token)


CB = 128


def _fill_padded(pad_ref, v, S):
    z = jnp.zeros((HALO, v.shape[1]), F32)
    pad_ref[pl.ds(0, HALO), :] = z
    pad_ref[pl.ds(S + HALO, HALO), :] = z
    pad_ref[pl.ds(HALO, S), :] = v


def _shifted(pad_ref, off, S):
    return pad_ref[pl.ds(HALO + off, S), :]


def _conv_fwd(proj, cw8):
    S = proj.shape[0]
    nblk = D_MODEL // CB

    def body(b_ref, c_ref, x_ref, w_ref, o_ref, pad):
        u = c_ref[...].astype(F32) * x_ref[...].astype(F32)
        _fill_padded(pad, u, S)
        cv = w_ref[0:1, :] * _shifted(pad, -1, S) + w_ref[1:2, :] * u + w_ref[2:3, :] * _shifted(pad, 1, S)
        o_ref[...] = (b_ref[...].astype(F32) * cv).astype(BF16)

    col = lambda base: pl.BlockSpec((S, CB), lambda j: (0, base // CB + j))
    return pl.pallas_call(
        body, name="conv_fwd", grid=(nblk,),
        in_specs=[col(OFF_B), col(OFF_C), col(OFF_X), pl.BlockSpec((8, CB), lambda j: (0, j))],
        out_specs=pl.BlockSpec((S, CB), lambda j: (0, j)),
        out_shape=jax.ShapeDtypeStruct((S, D_MODEL), BF16),
        scratch_shapes=[pltpu.VMEM((S + 2 * HALO, CB), F32)],
        compiler_params=_params("parallel"),
    )(proj, proj, proj, cw8)


def _pool_count(S, lo, hi):
    t = lax.broadcasted_iota(jnp.int32, (S, CB), 0)
    return (jnp.minimum(t + hi, S - 1) - jnp.maximum(t - lo, 0) + 1).astype(F32)


def _pool_fwd(proj):
    S = proj.shape[0]
    nblk = D_MODEL // CB
    per_group = POOL_CG // CB

    def body(u_ref, o_ref, pad):
        u = u_ref[...].astype(F32)
        _fill_padded(pad, u, S)
        grp = pl.program_id(0) // per_group
        for gi, w in enumerate(POOL_WINDOWS):
            @pl.when(grp == gi)
            def _(w=w):
                lo, hi = w // 2, w - 1 - w // 2
                acc = _shifted(pad, -lo, S)
                for off in range(-lo + 1, hi + 1):
                    acc = acc + _shifted(pad, off, S)
                o_ref[...] = (acc / _pool_count(S, lo, hi) - u).astype(BF16)

    return pl.pallas_call(
        body, name="pool_fwd", grid=(nblk,),
        in_specs=[pl.BlockSpec((S, CB), lambda j: (0, OFF_U // CB + j))],
        out_specs=pl.BlockSpec((S, CB), lambda j: (0, j)),
        out_shape=jax.ShapeDtypeStruct((S, D_MODEL), BF16),
        scratch_shapes=[pltpu.VMEM((S + 2 * HALO, CB), F32)],
        compiler_params=_params("parallel"),
    )(proj)


def _attn_specs(S):
    nb = S // BLOCK
    kcol, vcol = OFF_K // (N_KV_HEADS * HEAD_DIM), OFF_V // (N_KV_HEADS * HEAD_DIM)
    kvw = N_KV_HEADS * HEAD_DIM
    prev = lambda i: jnp.maximum(i - 1, 0)
    nxt = lambda i: jnp.minimum(i + 1, nb - 1)
    return [
        pl.BlockSpec((BLOCK, D_MODEL), lambda i: (i, OFF_Q // D_MODEL)),
        pl.BlockSpec((BLOCK, kvw), lambda i: (prev(i), kcol)),
        pl.BlockSpec((BLOCK, kvw), lambda i: (i, kcol)),
        pl.BlockSpec((BLOCK, kvw), lambda i: (nxt(i), kcol)),
        pl.BlockSpec((BLOCK, kvw), lambda i: (prev(i), vcol)),
        pl.BlockSpec((BLOCK, kvw), lambda i: (i, vcol)),
        pl.BlockSpec((BLOCK, kvw), lambda i: (nxt(i), vcol)),
    ]


def _heads_rows(ref_or_val, hk):
    return jnp.concatenate(
        [ref_or_val[:, (GQA * hk + g) * HEAD_DIM:(GQA * hk + g + 1) * HEAD_DIM] for g in range(GQA)], axis=0)


def _kv_rows(p_ref, c_ref, n_ref, hk):
    sl = slice(hk * HEAD_DIM, (hk + 1) * HEAD_DIM)
    return jnp.concatenate([p_ref[:, sl], c_ref[:, sl], n_ref[:, sl]], axis=0)


def _bias_cols(bias_ref, hk):
    return jnp.concatenate([bias_ref[GQA * hk + g] for g in range(GQA)], axis=1)


def _softmax_keys_on_rows(q4s, kc, bias_blk, sink_row):
    s = _nt(kc, q4s) + bias_blk
    m = jnp.maximum(jnp.max(s, axis=0, keepdims=True), sink_row)
    p = jnp.exp(s - m)
    e_sink = jnp.exp(sink_row - m)
    inv = 1.0 / (jnp.sum(p, axis=0, keepdims=True) + e_sink)
    return p * inv, e_sink * inv


TAB = (N_HEADS, 3 * BLOCK, BLOCK)
TAB_FLAT = 3 * BLOCK * BLOCK


def _bias_spec(nb):
    return pl.BlockSpec((None,) + TAB, lambda i: (jnp.where(i == 0, 0, jnp.where(i == nb - 1, 2, 1)), 0, 0, 0))


def _attn_fwd(proj, bias_tabs, sink_rows):
    S = proj.shape[0]
    nb = S // BLOCK
    assert nb >= 2

    def body(q_ref, kp, kc_, kn, vp, vc_, vn, bias_ref, sink_ref, o_ref):
        outs = []
        for hk in range(N_KV_HEADS):
            q4s = _heads_rows(q_ref, hk) * SM_SCALE
            kc = _kv_rows(kp, kc_, kn, hk)
            vc = _kv_rows(vp, vc_, vn, hk)
            pn, _ = _softmax_keys_on_rows(q4s, kc, _bias_cols(bias_ref, hk), sink_ref[hk:hk + 1, :])
            o4 = _tn(pn.astype(BF16), vc)
            outs += [o4[g * BLOCK:(g + 1) * BLOCK, :] for g in range(GQA)]
        o_ref[...] = jnp.concatenate(outs, axis=1).astype(BF16)

    return pl.pallas_call(
        body, name="attn_fwd", grid=(nb,),
        in_specs=_attn_specs(S) + [_bias_spec(nb), pl.BlockSpec((N_KV_HEADS, GQA * BLOCK), lambda i: (0, 0))],
        out_specs=pl.BlockSpec((BLOCK, D_MODEL), lambda i: (i, 0)),
        out_shape=jax.ShapeDtypeStruct((S, D_MODEL), BF16),
        compiler_params=_params("parallel"),
    )(*([proj] * 7), bias_tabs, sink_rows)


GATE_HALF = D_MODEL // 2


def _gate_specs(tm):
    return [pl.BlockSpec((tm, GATE_HALF), lambda i, c=off // GATE_HALF + k: (i, c))
            for off in (OFF_GA, OFF_GP, OFF_GT) for k in (0, 1)]


def _gate(lo_ref, hi_ref):
    return _sigmoid(jnp.concatenate([lo_ref[...], hi_ref[...]], axis=1).astype(F32))


def _pool_mix(p, wp):
    return jnp.concatenate(
        [_nn(p[:, g * POOL_CG:(g + 1) * POOL_CG], wp[g]) for g in range(POOL_GROUPS)], axis=1)


def _mix_fwd(za, p, att, proj, x, wa, wp, ps, wt, wo):
    S = x.shape[0]
    tm = _tile(S, 256)

    def body(za_ref, p_ref, att_ref, ga0, ga1, gp0, gp1, gt0, gt1, x_ref, wa_ref, wp_ref, ps_ref, wt_ref, wo_ref,
             ya_ref, yp_ref, yt_ref, mg_ref, x2_ref):
        ya = _nn(za_ref[...], wa_ref[...])
        ypr = _pool_mix(p_ref[...], wp_ref)
        yt = _nn(att_ref[...], wt_ref[...])
        merged = _gate(ga0, ga1) * ya + _gate(gp0, gp1) * (ypr * ps_ref[...]) + _gate(gt0, gt1) * yt
        mb = merged.astype(BF16)
        ya_ref[...] = ya.astype(BF16)
        yp_ref[...] = ypr.astype(BF16)
        yt_ref[...] = yt.astype(BF16)
        mg_ref[...] = mb
        x2_ref[...] = x_ref[...] + _nn(mb, wo_ref[...])

    row = lambda c=0: pl.BlockSpec((tm, D_MODEL), lambda i: (i, c))
    whole = lambda a: pl.BlockSpec(a.shape, lambda i: (0,) * a.ndim)
    act = jax.ShapeDtypeStruct((S, D_MODEL), BF16)
    return pl.pallas_call(
        body, name="mix_fwd", grid=(S // tm,),
        in_specs=[row(), row(), row()] + _gate_specs(tm) + [row(), whole(wa), whole(wp), whole(ps), whole(wt), whole(wo)],
        out_specs=[row(), row(), row(), row(), row()],
        out_shape=[act, act, act, act, jax.ShapeDtypeStruct((S, D_MODEL), F32)],
        compiler_params=_params("parallel"),
    )(za, p, att, *([proj] * 6), x, wa, wp, ps, wt, wo)


def _ffn_fwd(gu, x2, wd):
    S = x2.shape[0]
    tm = _tile(S, 256)

    def body(g_ref, u_ref, x_ref, w_ref, a_ref, o_ref):
        g = g_ref[...].astype(F32)
        a = (g * _sigmoid(g) * u_ref[...].astype(F32)).astype(BF16)
        a_ref[...] = a
        o_ref[...] = x_ref[...] + _nn(a, w_ref[...])

    return pl.pallas_call(
        body, name="ffn_fwd", grid=(S // tm,),
        in_specs=[pl.BlockSpec((tm, D_FF), lambda i: (i, 0)), pl.BlockSpec((tm, D_FF), lambda i: (i, 1)),
                  pl.BlockSpec((tm, D_MODEL), lambda i: (i, 0)), pl.BlockSpec((D_FF, D_MODEL), lambda i: (0, 0))],
        out_specs=[pl.BlockSpec((tm, D_FF), lambda i: (i, 0)), pl.BlockSpec((tm, D_MODEL), lambda i: (i, 0))],
        out_shape=[jax.ShapeDtypeStruct((S, D_FF), BF16), jax.ShapeDtypeStruct((S, D_MODEL), F32)],
        compiler_params=_params("parallel"),
    )(gu, gu, x2, wd)


def _loss_bwd(x, g, tgt):
    S, Dm = x.shape
    tm = _tile(S, 512)

    def body(x_ref, g_ref, t_ref, l_ref, dx_ref, dg_ref):
        @pl.when(pl.program_id(0) == 0)
        def _():
            l_ref[...] = jnp.zeros_like(l_ref)
            dg_ref[...] = jnp.zeros_like(dg_ref)
        xv, gv = x_ref[...], g_ref[...]
        r = lax.rsqrt(jnp.mean(xv * xv, axis=-1, keepdims=True) + EPS)
        n = xv * r
        err = n * gv - t_ref[...]
        l_ref[...] += 0.5 * jnp.sum(jnp.mean(err * err, axis=-1, keepdims=True), axis=0, keepdims=True)
        dy = err * (1.0 / Dm)
        dn = dy * gv
        dx_ref[...] = r * (dn - n * jnp.mean(dn * n, axis=-1, keepdims=True))
        dg_ref[...] += jnp.sum(dy * n, axis=0, keepdims=True)

    return pl.pallas_call(
        body, name="loss_bwd", grid=(S // tm,),
        in_specs=[pl.BlockSpec((tm, Dm), lambda i: (i, 0)), pl.BlockSpec((1, Dm), lambda i: (0, 0)),
                  pl.BlockSpec((tm, Dm), lambda i: (i, 0))],
        out_specs=[pl.BlockSpec((8, 128), lambda i: (0, 0)), pl.BlockSpec((tm, Dm), lambda i: (i, 0)),
                   pl.BlockSpec((1, Dm), lambda i: (0, 0))],
        out_shape=[jax.ShapeDtypeStruct((8, 128), F32), jax.ShapeDtypeStruct((S, Dm), F32),
                   jax.ShapeDtypeStruct((1, Dm), F32)],
        compiler_params=_params("arbitrary"),
    )(x, g, tgt)


def _ffn_bwd(dx3, gu, wd, token):
    S = dx3.shape[0]
    tm = _tile(S, 256)

    def body(d_ref, g_ref, u_ref, w_ref, token_ref, o_ref):
        dact = _nt(d_ref[...].astype(BF16), w_ref[...])
        g, u = g_ref[...].astype(F32), u_ref[...].astype(F32)
        sg = _sigmoid(g)
        o_ref[:, 0:D_FF] = (dact * u * (sg * (1.0 + g * (1.0 - sg)))).astype(BF16)
        o_ref[:, D_FF:2 * D_FF] = (dact * (g * sg)).astype(BF16)

    return pl.pallas_call(
        body, name="ffn_bwd", grid=(S // tm,),
        in_specs=[pl.BlockSpec((tm, D_MODEL), lambda i: (i, 0)),
                  pl.BlockSpec((tm, D_FF), lambda i: (i, 0)), pl.BlockSpec((tm, D_FF), lambda i: (i, 1)),
                  pl.BlockSpec((D_FF, D_MODEL), lambda i: (0, 0)), ANY],
        out_specs=pl.BlockSpec((tm, 2 * D_FF), lambda i: (i, 0)),
        out_shape=jax.ShapeDtypeStruct((S, 2 * D_FF), BF16),
        compiler_params=_params("parallel"),
    )(dx3, gu, gu, wd, token)


def _wgrad(a, b, name, tk=512, tn=512, out_dtype=BF16, token=None):
    S, K = a.shape
    N = b.shape[1]
    tk, tn, ts = _tile(K, tk), _tile(N, tn), _tile(S, 1024)
    n_s = S // ts
    extra = [] if token is None else [token]

    def body(a_ref, b_ref, *rest):
        o_ref, acc = rest[-2:]
        s = pl.program_id(2)

        @pl.when(s == 0)
        def _():
            acc[...] = jnp.zeros_like(acc)
        acc[...] += _tn(a_ref[...].astype(BF16), b_ref[...].astype(BF16))

        @pl.when(s == n_s - 1)
        def _():
            o_ref[...] = acc[...].astype(out_dtype)

    return pl.pallas_call(
        body, name=name, grid=(K // tk, N // tn, n_s),
        in_specs=[pl.BlockSpec((ts, tk), lambda k, n, s: (s, k)), pl.BlockSpec((ts, tn), lambda k, n, s: (s, n))]
        + [ANY] * len(extra),
        out_specs=pl.BlockSpec((tk, tn), lambda k, n, s: (k, n)),
        out_shape=jax.ShapeDtypeStruct((K, N), out_dtype),
        scratch_shapes=[pltpu.VMEM((tk, tn), F32)],
        compiler_params=_params("parallel", "parallel", "arbitrary"),
    )(a, b, *extra)


def _wgrad_pool(p, dyps):
    S = p.shape[0]
    ts = _tile(S, 512)
    n_s = S // ts

    def body(a_ref, b_ref, o_ref, acc):
        s = pl.program_id(1)

        @pl.when(s == 0)
        def _():
            acc[...] = jnp.zeros_like(acc)
        acc[...] += _tn(a_ref[...], b_ref[...])

        @pl.when(s == n_s - 1)
        def _():
            o_ref[...] = acc[...].astype(BF16)

    return pl.pallas_call(
        body, name="wgrad_pool", grid=(POOL_GROUPS, n_s),
        in_specs=[pl.BlockSpec((ts, POOL_CG), lambda g, s: (s, g)), pl.BlockSpec((ts, POOL_CG), lambda g, s: (s, g))],
        out_specs=pl.BlockSpec((None, POOL_CG, POOL_CG), lambda g, s: (g, 0, 0)),
        out_shape=jax.ShapeDtypeStruct((POOL_GROUPS, POOL_CG, POOL_CG), BF16),
        scratch_shapes=[pltpu.VMEM((POOL_CG, POOL_CG), F32)],
        compiler_params=_params("parallel", "arbitrary"),
    )(p, dyps)


def _dgrad_norm_bwd(dy, w, x, g, dres, name, tk):
    S, K = dy.shape
    Dm = x.shape[1]
    tm, tk = _tile(S, 512), _tile(K, tk)
    n_k = K // tk

    def body(dy_ref, w_ref, x_ref, g_ref, r_ref, dx_ref, dg_ref, acc):
        i, k = pl.program_id(0), pl.program_id(1)

        @pl.when((i == 0) & (k == 0))
        def _():
            dg_ref[...] = jnp.zeros_like(dg_ref)

        @pl.when(k == 0)
        def _():
            acc[...] = jnp.zeros_like(acc)
        acc[...] += _nt(dy_ref[...], w_ref[...])

        @pl.when(k == n_k - 1)
        def _():
            dh, xv = acc[...], x_ref[...]
            r = lax.rsqrt(jnp.mean(xv * xv, axis=-1, keepdims=True) + EPS)
            n = xv * r
            dn = dh * g_ref[...]
            dx_ref[...] = r_ref[...] + r * (dn - n * jnp.mean(dn * n, axis=-1, keepdims=True))
            dg_ref[...] += jnp.sum(dh * n, axis=0, keepdims=True)

    rowblk = pl.BlockSpec((tm, Dm), lambda i, k: (i, 0))
    vec = pl.BlockSpec((1, Dm), lambda i, k: (0, 0))
    return pl.pallas_call(
        body, name=name, grid=(S // tm, n_k),
        in_specs=[pl.BlockSpec((tm, tk), lambda i, k: (i, k)), pl.BlockSpec((Dm, tk), lambda i, k: (0, k)),
                  rowblk, vec, rowblk],
        out_specs=[rowblk, vec],
        out_shape=[jax.ShapeDtypeStruct((S, Dm), F32), jax.ShapeDtypeStruct((1, Dm), F32)],
        scratch_shapes=[pltpu.VMEM((tm, Dm), F32)],
        compiler_params=_params("arbitrary", "arbitrary"),
    )(dy, w, x, g, dres)


def _mix_bwd(dx2, ya, ypr, yt, proj, ps, wa, wp, wt, wo, token):
    S = dx2.shape[0]
    tm = _tile(S, 256)

    def body(dx_ref, ya_ref, yp_ref, yt_ref, ga0, ga1, gp0, gp1, gt0, gt1, ps_ref, wa_ref, wp_ref, wt_ref, wo_ref,
             token_ref, dya_ref, dyt_ref, dyps_ref, dza_ref, datt_ref, dp_ref, dga_ref, dgp_ref, dgt_ref, dps_ref):
        @pl.when(pl.program_id(0) == 0)
        def _():
            dps_ref[...] = jnp.zeros_like(dps_ref)
        dm = _nt(dx_ref[...].astype(BF16), wo_ref[...])
        sa, sp, st = _gate(ga0, ga1), _gate(gp0, gp1), _gate(gt0, gt1)
        psv = ps_ref[...]
        ypr_v = yp_ref[...].astype(F32)
        dya = (sa * dm).astype(BF16)
        dyt = (st * dm).astype(BF16)
        dyp = sp * dm
        dyps = (dyp * psv).astype(BF16)
        dya_ref[...] = dya
        dyt_ref[...] = dyt
        dyps_ref[...] = dyps
        dga_ref[...] = (dm * ya_ref[...].astype(F32) * (sa * (1.0 - sa))).astype(BF16)
        dgp_ref[...] = (dm * (ypr_v * psv) * (sp * (1.0 - sp))).astype(BF16)
        dgt_ref[...] = (dm * yt_ref[...].astype(F32) * (st * (1.0 - st))).astype(BF16)
        dps_ref[...] += jnp.sum(dyp * ypr_v, axis=0, keepdims=True)
        dza_ref[...] = _nt(dya, wa_ref[...]).astype(BF16)
        datt_ref[...] = _nt(dyt, wt_ref[...]).astype(BF16)
        dp_ref[...] = jnp.concatenate(
            [_nt(dyps[:, g * POOL_CG:(g + 1) * POOL_CG], wp_ref[g]) for g in range(POOL_GROUPS)], axis=1).astype(BF16)

    row = lambda c=0: pl.BlockSpec((tm, D_MODEL), lambda i: (i, c))
    whole = lambda a: pl.BlockSpec(a.shape, lambda i: (0,) * a.ndim)
    act = jax.ShapeDtypeStruct((S, D_MODEL), BF16)
    return pl.pallas_call(
        body, name="mix_bwd", grid=(S // tm,),
        in_specs=[row(), row(), row(), row()] + _gate_specs(tm)
        + [whole(ps), whole(wa), whole(wp), whole(wt), whole(wo), ANY],
        out_specs=[row()] * 9 + [pl.BlockSpec((1, D_MODEL), lambda i: (0, 0))],
        out_shape=[act] * 9 + [jax.ShapeDtypeStruct((1, D_MODEL), F32)],
        compiler_params=_params("arbitrary"),
    )(dx2, ya, ypr, yt, *([proj] * 6), ps, wa, wp, wt, wo, token)


def _conv_bwd(dza, proj, cw8):
    S = proj.shape[0]
    nblk = D_MODEL // CB

    def body(d_ref, b_ref, c_ref, x_ref, w_ref, db_ref, dc_ref, dxa_ref, dw_ref, pad_u, pad_d):
        c, xa = c_ref[...].astype(F32), x_ref[...].astype(F32)
        u = c * xa
        _fill_padded(pad_u, u, S)
        u_prev, u_next = _shifted(pad_u, -1, S), _shifted(pad_u, 1, S)
        cv = w_ref[0:1, :] * u_prev + w_ref[1:2, :] * u + w_ref[2:3, :] * u_next
        dza_v = d_ref[...].astype(F32)
        db_ref[...] = (dza_v * cv).astype(BF16)
        dcv = dza_v * b_ref[...].astype(F32)
        _fill_padded(pad_d, dcv, S)
        du = w_ref[0:1, :] * _shifted(pad_d, 1, S) + w_ref[1:2, :] * dcv + w_ref[2:3, :] * _shifted(pad_d, -1, S)
        dc_ref[...] = (du * xa).astype(BF16)
        dxa_ref[...] = (du * c).astype(BF16)
        dw_ref[...] = jnp.concatenate(
            [jnp.sum(dcv * u_prev, axis=0, keepdims=True), jnp.sum(dcv * u, axis=0, keepdims=True),
             jnp.sum(dcv * u_next, axis=0, keepdims=True), jnp.zeros((5, CB), F32)], axis=0)

    col = lambda base: pl.BlockSpec((S, CB), lambda j: (0, base // CB + j))
    act = jax.ShapeDtypeStruct((S, D_MODEL), BF16)
    return pl.pallas_call(
        body, name="conv_bwd", grid=(nblk,),
        in_specs=[col(0), col(OFF_B), col(OFF_C), col(OFF_X), pl.BlockSpec((8, CB), lambda j: (0, j))],
        out_specs=[col(0), col(0), col(0), pl.BlockSpec((8, CB), lambda j: (0, j))],
        out_shape=[act, act, act, jax.ShapeDtypeStruct((8, D_MODEL), F32)],
        scratch_shapes=[pltpu.VMEM((S + 2 * HALO, CB), F32), pltpu.VMEM((S + 2 * HALO, CB), F32)],
        compiler_params=_params("parallel"),
    )(dza, proj, proj, proj, cw8)


def _pool_bwd(dp):
    S = dp.shape[0]
    nblk = D_MODEL // CB
    per_group = POOL_CG // CB

    def body(d_ref, o_ref, pad):
        d = d_ref[...].astype(F32)
        grp = pl.program_id(0) // per_group
        for gi, w in enumerate(POOL_WINDOWS):
            @pl.when(grp == gi)
            def _(w=w):
                lo, hi = w // 2, w - 1 - w // 2
                _fill_padded(pad, d / _pool_count(S, lo, hi), S)
                acc = _shifted(pad, -hi, S)
                for off in range(-hi + 1, lo + 1):
                    acc = acc + _shifted(pad, off, S)
                o_ref[...] = (acc - d).astype(BF16)

    return pl.pallas_call(
        body, name="pool_bwd", grid=(nblk,),
        in_specs=[pl.BlockSpec((S, CB), lambda j: (0, j))],
        out_specs=pl.BlockSpec((S, CB), lambda j: (0, j)),
        out_shape=jax.ShapeDtypeStruct((S, D_MODEL), BF16),
        scratch_shapes=[pltpu.VMEM((S + 2 * HALO, CB), F32)],
        compiler_params=_params("parallel"),
    )(dp)


def _attn_bwd(proj, datt, bias_tabs, sink_rows, dbias_in):
    S = proj.shape[0]
    nb = S // BLOCK
    kvw = N_KV_HEADS * HEAD_DIM

    def body(q_ref, kp, kc_, kn, vp, vc_, vn, do_ref, bias_ref, sink_ref, dbin_ref,
             dq_ref, dk_ref, dv_ref, db_ref, ds_ref):
        i = pl.program_id(0)

        @pl.when(i == 0)
        def _():
            dk_ref[...] = jnp.zeros_like(dk_ref)
            dv_ref[...] = jnp.zeros_like(dv_ref)
            db_ref[...] = dbin_ref[...]
            ds_ref[...] = jnp.zeros_like(ds_ref)
        dqs, dks, dvs = [], [], []
        for hk in range(N_KV_HEADS):
            q4s = _heads_rows(q_ref, hk) * SM_SCALE
            do4 = _heads_rows(do_ref, hk)
            kc = _kv_rows(kp, kc_, kn, hk)
            vc = _kv_rows(vp, vc_, vn, hk)
            pn, p_sink = _softmax_keys_on_rows(q4s, kc, _bias_cols(bias_ref, hk), sink_ref[hk:hk + 1, :])
            dpm = _nt(vc, do4)
            delta = jnp.sum(pn * dpm, axis=0, keepdims=True)
            dsc = pn * (dpm - delta)
            for g in range(GQA):
                db_ref[GQA * hk + g] += dsc[:, g * BLOCK:(g + 1) * BLOCK]
            ds_ref[hk:hk + 1, :] += -p_sink * delta
            dsb = dsc.astype(BF16)
            dq4 = _tn(dsb, kc) * SM_SCALE
            dqs += [dq4[g * BLOCK:(g + 1) * BLOCK, :] for g in range(GQA)]
            dks.append(_nn(dsb, q4s))
            dvs.append(_nn(pn.astype(BF16), do4))
        dq_ref[...] = jnp.concatenate(dqs, axis=1).astype(BF16)
        r0 = pl.multiple_of(i * BLOCK, BLOCK)
        dk_ref[pl.ds(r0, 3 * BLOCK), :] += jnp.concatenate(dks, axis=1)
        dv_ref[pl.ds(r0, 3 * BLOCK), :] += jnp.concatenate(dvs, axis=1)

    const = lambda shape: pl.BlockSpec(shape, lambda i: (0,) * len(shape))
    sink_shape = (N_KV_HEADS, GQA * BLOCK)
    return pl.pallas_call(
        body, name="attn_bwd", grid=(nb,),
        in_specs=_attn_specs(S) + [pl.BlockSpec((BLOCK, D_MODEL), lambda i: (i, 0)),
                                   _bias_spec(nb), const(sink_shape), const(TAB)],
        out_specs=[pl.BlockSpec((BLOCK, D_MODEL), lambda i: (i, 0)),
                   const((S + 2 * BLOCK, kvw)), const((S + 2 * BLOCK, kvw)), const(TAB), const(sink_shape)],
        out_shape=[jax.ShapeDtypeStruct((S, D_MODEL), BF16),
                   jax.ShapeDtypeStruct((S + 2 * BLOCK, kvw), F32), jax.ShapeDtypeStruct((S + 2 * BLOCK, kvw), F32),
                   jax.ShapeDtypeStruct(TAB, F32), jax.ShapeDtypeStruct(sink_shape, F32)],
        compiler_params=_params("arbitrary"),
    )(*([proj] * 7), datt, bias_tabs, sink_rows, dbias_in)


def _bucket_constants():
    half = N_BUCKETS // 2
    max_exact = half // 2
    qi = np.arange(BLOCK)[None, :]
    kj = np.arange(3 * BLOCK)[:, None]
    rel = kj - BLOCK - qi
    n = np.abs(rel)
    nf = np.maximum(n, 1).astype(np.float32)
    large = max_exact + (np.log(nf / np.float32(max_exact)) / np.float32(math.log(MAX_DISTANCE / max_exact))
                         * np.float32(half - max_exact)).astype(np.int32)
    large = np.minimum(large, half - 1)
    bucket = np.where(rel > 0, half, 0) + np.where(n < max_exact, n, large)
    onehot = (bucket.reshape(1, -1) == np.arange(N_BUCKETS)[:, None]).astype(np.float32)
    window = n <= WINDOW
    first = window & (kj >= BLOCK)
    last = window & (kj < 2 * BLOCK)
    masks = np.stack([np.where(v, 0.0, NEG_INF).astype(np.float32).reshape(-1) for v in (first, window, last)])
    return onehot, masks


def _bias_expand(rel_bias_t, onehot, masks):
    def body(r_ref, oh_ref, m_ref, o_ref):
        tab = jnp.dot(r_ref[...], oh_ref[...], preferred_element_type=F32, precision=lax.Precision.HIGHEST)
        for v in range(3):
            o_ref[v] = tab + m_ref[v:v + 1, :]

    return pl.pallas_call(
        body, name="bias_expand", out_shape=jax.ShapeDtypeStruct((3, N_HEADS, onehot.shape[1]), F32),
        compiler_params=_params(),
    )(rel_bias_t, onehot, masks)


def _bias_reduce(dtab, dsink_rows, onehot):
    def body(d_ref, s_ref, oh_ref, o_ref, so_ref):
        o_ref[...] = lax.dot_general(oh_ref[...], d_ref[...], (((1,), (1,)), ((), ())),
                                     preferred_element_type=F32, precision=lax.Precision.HIGHEST)
        so_ref[...] = jnp.sum(s_ref[...], axis=-1, keepdims=True)

    return pl.pallas_call(
        body, name="bias_reduce",
        out_shape=[jax.ShapeDtypeStruct((N_BUCKETS, N_HEADS), F32),
                   jax.ShapeDtypeStruct((dsink_rows.shape[0], 1), F32)],
        compiler_params=_params(),
    )(dtab, dsink_rows, onehot)


GROUPS = dict(mix=("w_in", "conv_w", "w_a_out", "w_pool", "w_attn_out", "w_o"), ffn=("w_gu", "w_down"))
WEIGHT_NAMES = GROUPS["mix"] + GROUPS["ffn"]


def _layer_fwd(l, x, weights_of, ps, g_mix, g_ffn, bias_tabs, sink_rows):
    W, token = weights_of(l, "mix", x)
    h, proj = _norm_matmul(x, g_mix, W["w_in"], "norm_proj", token)
    za = _conv_fwd(proj, W["conv_w"])
    p = _pool_fwd(proj)
    att = _attn_fwd(proj, bias_tabs, sink_rows)
    ya, ypr, yt, merged, x2 = _mix_fwd(za, p, att, proj, x, W["w_a_out"], W["w_pool"], ps, W["w_attn_out"], W["w_o"])
    Wf, token = weights_of(l, "ffn", x2)
    h2, gu = _norm_matmul(x2, g_ffn, Wf["w_gu"], "norm_gu", token)
    act, x3 = _ffn_fwd(gu, x2, Wf["w_down"])
    saved = dict(x=x, h=h, proj=proj, za=za, p=p, att=att, ya=ya, ypr=ypr, yt=yt, merged=merged, x2=x2, h2=h2,
                 gu=gu, act=act, W={**W, **Wf}, sink_rows=sink_rows)
    return x3, saved, token


def _layer_bwd(l, dx3, sv, grads_to, ps, g_mix, g_ffn, bias_tabs, dbias, token):
    S = dx3.shape[0]
    W, sink_rows = sv["W"], sv["sink_rows"]
    dgu = _ffn_bwd(dx3, sv["gu"], W["w_down"], token)
    g_w_down = _wgrad(sv["act"], dx3, "wgrad_down", tk=_tile(D_FF, 1408), tn=512, token=token)
    g_w_gu = _wgrad(sv["h2"], dgu, "wgrad_gu", tk=1024, tn=512)
    dx2, dg_ffn = _dgrad_norm_bwd(dgu, W["w_gu"], sv["x2"], g_ffn, dx3, "dgrad_gu", tk=1408)
    token = grads_to(l, "ffn", dict(w_gu=g_w_gu, w_down=g_w_down), dx2)
    dya, dyt, dyps, dza, datt, dp, dga, dgp, dgt, dps = _mix_bwd(
        dx2, sv["ya"], sv["ypr"], sv["yt"], sv["proj"], ps, W["w_a_out"], W["w_pool"], W["w_attn_out"], W["w_o"], token)
    g_w_o = _wgrad(sv["merged"], dx2, "wgrad_sq_f32", tk=1024, tn=512)
    g_w_a_out = _wgrad(sv["za"], dya, "wgrad_sq", tk=1024, tn=512)
    g_w_attn_out = _wgrad(sv["att"], dyt, "wgrad_sq", tk=1024, tn=512)
    g_w_pool = _wgrad_pool(sv["p"], dyps)
    db, dc, dxa, g_conv = _conv_bwd(dza, sv["proj"], W["conv_w"])
    dup = _pool_bwd(dp)
    dq, dkp, dvp, dbias, dsink = _attn_bwd(sv["proj"], datt, bias_tabs, sink_rows, dbias)
    dproj = jnp.concatenate([db, dc, dxa, dup, dq, dkp[BLOCK:BLOCK + S].astype(BF16),
                             dvp[BLOCK:BLOCK + S].astype(BF16), dga, dgp, dgt], axis=1)
    g_w_in = _wgrad(sv["h"], dproj, "wgrad_in", tk=1024, tn=512)
    dx, dg_mix = _dgrad_norm_bwd(dproj, W["w_in"], sv["x"], g_mix, dx2, "dgrad_in", tk=2176)
    token = grads_to(l, "mix", dict(w_in=g_w_in, conv_w=g_conv, w_a_out=g_w_a_out, w_pool=g_w_pool,
                                    w_attn_out=g_w_attn_out, w_o=g_w_o), dx)
    return dx, dict(pool_scale=dps, g_mix=dg_mix, g_ffn=dg_ffn, attn_sink=dsink), dbias, token


def _local_step(x, tgt, weights_of, grads_to, pool_scale, attn_sink, g_mix, g_ffn, rel_bias, g_final):
    onehot_np, masks_np = _bucket_constants()
    onehot, masks = jnp.asarray(onehot_np), jnp.asarray(masks_np)
    bias_tabs = _bias_expand(rel_bias.T, onehot, masks).reshape((3,) + TAB)
    saved = []
    for l in range(DEPTH):
        sink_rows = jnp.repeat(attn_sink[l], BLOCK).reshape(N_KV_HEADS, GQA * BLOCK)
        x, sv, token = _layer_fwd(l, x, weights_of, pool_scale[l:l + 1], g_mix[l:l + 1], g_ffn[l:l + 1], bias_tabs,
                                  sink_rows)
        saved.append(sv)
    loss, dx, dg_final = _loss_bwd(x, g_final.reshape(1, D_MODEL), tgt)
    dbias = jnp.zeros(TAB, F32)
    small = [None] * DEPTH
    for l in reversed(range(DEPTH)):
        dx, small[l], dbias, token = _layer_bwd(
            l, dx, saved[l], grads_to, pool_scale[l:l + 1], g_mix[l:l + 1], g_ffn[l:l + 1], bias_tabs, dbias, token)
    dsink_rows = jnp.concatenate([small[l]["attn_sink"].reshape(N_HEADS, BLOCK) for l in range(DEPTH)], axis=0)
    d_rel_bias, d_sink = _bias_reduce(dbias.reshape(N_HEADS, TAB_FLAT), dsink_rows, onehot)
    cat = lambda k: jnp.concatenate([small[l][k] for l in range(DEPTH)], axis=0)
    smalls = dict(pool_scale=cat("pool_scale"), g_mix=cat("g_mix"), g_ffn=cat("g_ffn"),
                  attn_sink=d_sink.reshape(DEPTH, N_HEADS), rel_bias=d_rel_bias, g_final=dg_final)
    return loss[0, 0], dx, smalls


SHARD_AXIS = dict(w_in=(1, IN_TOTAL // N_CHIPS), conv_w=(1, D_MODEL // N_CHIPS), w_a_out=(0, D_MODEL // N_CHIPS),
                  w_pool=(1, POOL_CG // N_CHIPS), w_attn_out=(0, D_MODEL // N_CHIPS), w_o=(0, D_MODEL // N_CHIPS),
                  w_gu=(1, 2 * D_FF // N_CHIPS), w_down=(0, D_FF // N_CHIPS))
HBM = pl.BlockSpec(memory_space=pltpu.HBM)
SEM = pl.BlockSpec(memory_space=pltpu.SEMAPHORE)
DATAFLOW = pltpu.SideEffectType.DATAFLOW_SIDE_EFFECTING
TOKEN = jax.ShapeDtypeStruct((8, 128), F32)


def _shard_of(ref, name, chip):
    axis, n = SHARD_AXIS[name]
    idx = [slice(None)] * len(ref.shape)
    idx[axis] = pl.ds(chip * n, n)
    return ref.at[tuple(idx)]


def _with_shard_axis(name, shape, size):
    axis, _ = SHARD_AXIS[name]
    s = list(shape)
    s[axis] = size
    return tuple(s)


def _chip_peers(x, y):
    return [(1 - x, y), (x, 1 - y), (1 - x, 1 - y)]


RELATION_XOR = (2, 1, 3)


def _group_copies(kind, group, srcs, lands, send_sems, recv_sems, local_sems, chip):
    x, y, c = lax.axis_index("x"), lax.axis_index("y"), lax.axis_index("c")
    copies = []
    for t, name in enumerate(GROUPS[group]):
        for j, (px, py) in enumerate(_chip_peers(x, y)):
            if kind == "gather":
                src, dst = srcs[t], _shard_of(lands[t], name, chip)
            else:
                src, dst = _shard_of(srcs[t], name, chip ^ RELATION_XOR[j]), lands[t].at[j]
            copies.append(pltpu.make_async_remote_copy(
                src_ref=src, dst_ref=dst, send_sem=send_sems.at[3 * t + j], recv_sem=recv_sems.at[3 * t + j],
                device_id=(px, py, c), device_id_type=MESH))
        if kind == "gather":
            src, dst = srcs[t], _shard_of(lands[t], name, chip)
        else:
            src, dst = _shard_of(srcs[t], name, chip), lands[t].at[N_CHIPS - 1]
        copies.append(pltpu.make_async_copy(src, dst, local_sems.at[t]))
    return copies


def _exchange_start(kind, group, srcs, land_shapes, after):
    nw = len(GROUPS[group])

    def body(*refs):
        srcs_r, lands_r = refs[:nw], refs[nw:2 * nw]
        send_sems, recv_sems, local_sems = refs[2 * nw + 1:2 * nw + 4]
        token = refs[-1]
        me = 2 * lax.axis_index("x") + lax.axis_index("y")
        for chip in range(N_CHIPS):
            @pl.when(me == chip)
            def _(chip=chip):
                for cp in _group_copies(kind, group, srcs_r, lands_r, send_sems, recv_sems, local_sems, chip):
                    cp.start()
        token[...] = jnp.zeros_like(token)

    lands = [pltpu.with_memory_space_constraint(lax.empty(s.shape, s.dtype), pltpu.HBM) for s in land_shapes]
    srcs = [pltpu.with_memory_space_constraint(a, pltpu.HBM) for a in srcs]
    thru = [pltpu.HBM(a.shape, a.dtype) for a in srcs + lands]
    outs = pl.pallas_call(
        body, name=f"{kind}_{group}_start",
        in_specs=[HBM] * (2 * nw) + [ANY],
        out_specs=[SEM, SEM, SEM] + [HBM] * (2 * nw) + [pl.BlockSpec(memory_space=pltpu.VMEM)],
        out_shape=[pltpu.SemaphoreType.DMA((3 * nw,)), pltpu.SemaphoreType.DMA((3 * nw,)),
                   pltpu.SemaphoreType.DMA((nw,))] + thru + [TOKEN],
        input_output_aliases={t: 3 + t for t in range(2 * nw)},
        compiler_params=pltpu.CompilerParams(has_side_effects=DATAFLOW),
    )(*srcs, *lands, after)
    return dict(sems=outs[0:3], srcs=outs[3:3 + nw], lands=outs[3 + nw:3 + 2 * nw], token=outs[-1])


def _exchange_wait(kind, group, started, after):
    nw = len(GROUPS[group])

    def body(*refs):
        srcs_r, lands_r = refs[:nw], refs[nw:2 * nw]
        send_sems, recv_sems, local_sems = refs[2 * nw:2 * nw + 3]
        for cp in _group_copies(kind, group, srcs_r, lands_r, send_sems, recv_sems, local_sems, 0):
            cp.wait()

    srcs, lands = list(started["srcs"]), list(started["lands"])
    outs = pl.pallas_call(
        body, name=f"{kind}_{group}_wait",
        in_specs=[HBM] * (2 * nw) + [SEM, SEM, SEM, ANY],
        out_specs=[HBM] * (2 * nw),
        out_shape=[pltpu.HBM(a.shape, a.dtype) for a in srcs + lands],
        input_output_aliases={t: t for t in range(2 * nw)},
        compiler_params=pltpu.CompilerParams(has_side_effects=DATAFLOW),
    )(*srcs, *lands, *started["sems"], after)
    return dict(zip(GROUPS[group], outs[nw:]))


def _gather_start(group, shards, after):
    names = GROUPS[group]
    shapes = [jax.ShapeDtypeStruct(_with_shard_axis(n, shards[n].shape, SHARD_AXIS[n][1] * N_CHIPS), shards[n].dtype)
              for n in names]
    return _exchange_start("gather", group, [shards[n] for n in names], shapes, after)


def _scatter_start(group, grads, after):
    names = GROUPS[group]
    shapes = [jax.ShapeDtypeStruct((N_CHIPS,) + _with_shard_axis(n, grads[n].shape, SHARD_AXIS[n][1]), grads[n].dtype)
              for n in names]
    return _exchange_start("scatter", group, [grads[n] for n in names], shapes, after)


def _sibling_exchange(parts):
    n = len(parts)

    def body(*refs):
        ins, outs = refs[:n], refs[n:2 * n]
        send_sems, recv_sems = refs[2 * n:]
        sibling = (lax.axis_index("x"), lax.axis_index("y"), 1 - lax.axis_index("c"))
        copies = [pltpu.make_async_remote_copy(src_ref=ins[t], dst_ref=outs[t], send_sem=send_sems.at[t],
                                               recv_sem=recv_sems.at[t], device_id=sibling, device_id_type=MESH)
                  for t in range(n)]
        for cp in copies:
            cp.start()
        for cp in copies:
            cp.wait()

    outs = pl.pallas_call(
        body, name="sibling_exchange", in_specs=[ANY] * n, out_specs=[ANY] * n,
        out_shape=[jax.ShapeDtypeStruct(p.shape, p.dtype) for p in parts],
        scratch_shapes=[pltpu.SemaphoreType.DMA((n,)), pltpu.SemaphoreType.DMA((n,))],
        compiler_params=pltpu.CompilerParams(has_side_effects=True),
    )(*parts)
    return list(outs)


N_DEV = 8


def _all_reduce_small(v):
    R, C = v.shape

    def body(v_ref, o_ref, slots, send_sems, recv_sems):
        x, y, c = lax.axis_index("x"), lax.axis_index("y"), lax.axis_index("c")
        me = 4 * x + 2 * y + c
        slots[me] = v_ref[...]
        copies = []
        for k in range(1, N_DEV):
            peer = me ^ k
            cp = pltpu.make_async_remote_copy(
                src_ref=v_ref, dst_ref=slots.at[me], send_sem=send_sems.at[k - 1], recv_sem=recv_sems.at[k - 1],
                device_id=(peer // 4, (peer // 2) % 2, peer % 2), device_id_type=MESH)
            cp.start()
            copies.append(cp)
        for cp in copies:
            cp.wait()
        acc = slots[0]
        for k in range(1, N_DEV):
            acc = acc + slots[k]
        o_ref[...] = acc

    return pl.pallas_call(
        body, name="all_reduce_small", out_shape=jax.ShapeDtypeStruct((R, C), F32),
        in_specs=[pl.BlockSpec(memory_space=pltpu.VMEM)], out_specs=pl.BlockSpec(memory_space=pltpu.VMEM),
        scratch_shapes=[pltpu.VMEM((N_DEV, R, C), F32), pltpu.SemaphoreType.DMA((N_DEV - 1,)),
                        pltpu.SemaphoreType.DMA((N_DEV - 1,))],
        compiler_params=pltpu.CompilerParams(has_side_effects=True),
    )(v)


def _as2d(shape):
    return (int(np.prod(shape[:-1])), shape[-1])


def _row_block(rows, cols, n_arrays):
    budget = V7X_VMEM_LIMIT // 2
    tr = rows
    while tr % 16 == 0 and 2 * n_arrays * tr * cols * 4 > budget:
        tr //= 2
    return tr


def _sum_slots(slots):
    _, R, C = slots.shape
    tr = _row_block(R, C, 5)

    def body(s_ref, o_ref):
        acc = s_ref[0].astype(F32)
        for k in range(1, N_CHIPS):
            acc = acc + s_ref[k].astype(F32)
        o_ref[...] = acc

    return pl.pallas_call(
        body, name="sum_slots", grid=(R // tr,),
        in_specs=[pl.BlockSpec((N_CHIPS, tr, C), lambda i: (0, i, 0))],
        out_specs=pl.BlockSpec((tr, C), lambda i: (i, 0)),
        out_shape=jax.ShapeDtypeStruct((R, C), F32),
        compiler_params=_params("parallel"),
    )(slots)


def _adamw(l, w, m, v, g_a, g_b, prev):
    L, R, C = w.shape
    tr = _row_block(R, C, 9)
    c1 = 1.0 - ADAM_B1 ** ADAM_STEP
    c2 = 1.0 - ADAM_B2 ** ADAM_STEP

    def body(w_ref, m_ref, v_ref, a_ref, b_ref, *rest):
        g_ref, d_ref, nm_ref, nv_ref = rest[-4:]
        g = a_ref[...] + b_ref[...]
        nm = ADAM_B1 * m_ref[...] + (1.0 - ADAM_B1) * g
        nv = ADAM_B2 * v_ref[...] + (1.0 - ADAM_B2) * (g * g)
        g_ref[...] = g
        nm_ref[...] = nm
        nv_ref[...] = nv
        d_ref[...] = -ADAM_LR * ((nm / c1) / (jnp.sqrt(nv / c2) + ADAM_EPS) + ADAM_WD * w_ref[...])

    layer = pl.BlockSpec((None, tr, C), lambda i: (l, i, 0))
    blk = pl.BlockSpec((tr, C), lambda i: (i, 0))
    out = jax.ShapeDtypeStruct((L, R, C), F32)
    prev = [] if prev is None else list(prev)
    return pl.pallas_call(
        body, name="adamw", grid=(R // tr,), in_specs=[layer] * 3 + [blk] * 2 + [ANY] * len(prev),
        out_specs=[layer] * 4, out_shape=[out] * 4,
        input_output_aliases={5 + k: k for k in range(len(prev))},
        compiler_params=_params("parallel"),
    )(w, m, v, g_a, g_b, *prev)


SMALL_ROWS = 16


def _pack_small(pool_scale, g_mix, g_ffn, g_final, attn_sink, rel_bias):
    tail = jnp.concatenate([attn_sink.reshape(-1), rel_bias.reshape(-1)])
    tail = jnp.pad(tail, (0, D_MODEL - tail.shape[0])).reshape(1, D_MODEL)
    rows = jnp.concatenate([pool_scale, g_mix, g_ffn, g_final.reshape(1, D_MODEL), tail], axis=0)
    return jnp.pad(rows, ((0, SMALL_ROWS - rows.shape[0]), (0, 0)))


def _unpack_small(packed):
    n_sink = DEPTH * N_HEADS
    return dict(pool_scale=packed[0:4], g_mix=packed[4:8], g_ffn=packed[8:12], g_final=packed[12],
                attn_sink=packed[13, 0:n_sink].reshape(DEPTH, N_HEADS),
                rel_bias=packed[13, n_sink:n_sink + N_BUCKETS * N_HEADS].reshape(N_BUCKETS, N_HEADS))


def _layer_shards(l, w_in, conv_w, w_a_out, w_pool, w_attn_out, w_o, w_gu, w_down):
    return dict(
        w_in=w_in[l].astype(BF16),
        conv_w=jnp.pad(conv_w[l].reshape(3, -1), ((0, 5), (0, 0))),
        w_a_out=w_a_out[l].astype(BF16), w_pool=w_pool[l].astype(BF16), w_attn_out=w_attn_out[l].astype(BF16),
        w_o=w_o[l].astype(BF16), w_gu=w_gu[l].astype(BF16), w_down=w_down[l].astype(BF16))


def kernel(x, w_in, conv_w, w_a_out, w_pool, pool_scale, w_attn_out, attn_sink, w_o, g_mix, g_ffn, w_gu, w_down, rel_bias, g_final, loss_target, m_w_in, m_conv_w, m_w_a_out, m_w_pool, m_pool_scale, m_w_attn_out, m_attn_sink, m_w_o, m_g_mix, m_g_ffn, m_w_gu, m_w_down, m_rel_bias, m_g_final, v_w_in, v_conv_w, v_w_a_out, v_w_pool, v_pool_scale, v_w_attn_out, v_attn_sink, v_w_o, v_g_mix, v_g_ffn, v_w_gu, v_w_down, v_rel_bias, v_g_final):
    big = dict(w_in=(w_in, m_w_in, v_w_in), conv_w=(conv_w, m_conv_w, v_conv_w), w_a_out=(w_a_out, m_w_a_out, v_w_a_out),
               w_pool=(w_pool, m_w_pool, v_w_pool), w_attn_out=(w_attn_out, m_w_attn_out, v_w_attn_out),
               w_o=(w_o, m_w_o, v_w_o), w_gu=(w_gu, m_w_gu, v_w_gu), w_down=(w_down, m_w_down, v_w_down))

    big3 = {n: tuple(a.reshape((DEPTH,) + _as2d(a.shape[1:])) for a in big[n]) for n in WEIGHT_NAMES}
    shards = [_layer_shards(l, w_in, conv_w, w_a_out, w_pool, w_attn_out, w_o, w_gu, w_down) for l in range(DEPTH)]

    gathers = {(0, "mix"): _gather_start("mix", shards[0], rel_bias)}
    gathers[0, "ffn"] = _gather_start("ffn", shards[0], gathers[0, "mix"]["token"])
    newest = {"token": gathers[0, "ffn"]["token"]}

    def weights_of(l, group, a):
        W = _exchange_wait("gather", group, gathers.pop((l, group)), a)
        if group == "mix" and l + 1 < DEPTH:
            gathers[l + 1, "mix"] = _gather_start("mix", shards[l + 1], W["w_in"])
            gathers[l + 1, "ffn"] = _gather_start("ffn", shards[l + 1], gathers[l + 1, "mix"]["token"])
            newest["token"] = gathers[l + 1, "ffn"]["token"]
        return W, newest["token"]

    results = {n: None for n in WEIGHT_NAMES}
    scatters = {}

    def finish(l, group, after):
        slots = _exchange_wait("scatter", group, scatters.pop((l, group)), after)
        names = GROUPS[group]
        parts = [_sum_slots(slots[n].reshape((N_CHIPS,) + _as2d(slots[n].shape[1:]))) for n in names]
        others = _sibling_exchange(parts)
        for n, mine, other in zip(names, parts, others):
            if n == "conv_w":
                mine, other = mine[0:3], other[0:3]
            results[n] = _adamw(l, *big3[n], mine, other, results[n])

    def grads_to(l, group, wgrads, a):
        scatters[l, group] = _scatter_start(group, wgrads, a)
        token = scatters[l, group]["token"]
        if group == "mix" and l + 1 < DEPTH:
            finish(l + 1, "ffn", token)
            finish(l + 1, "mix", token)
        return token

    loss, grad_x, smalls = _local_step(x[0], loss_target[0], weights_of, grads_to, pool_scale, attn_sink, g_mix, g_ffn,
                                       rel_bias, g_final)
    finish(0, "ffn", grad_x)
    finish(0, "mix", results["w_down"][0])
    stacked = {n: [o.reshape(big[n][0].shape) for o in results[n]] for n in WEIGHT_NAMES}

    g_small = _all_reduce_small(_pack_small(smalls["pool_scale"], smalls["g_mix"], smalls["g_ffn"], smalls["g_final"],
                                            smalls["attn_sink"], smalls["rel_bias"]))
    w_small = _pack_small(pool_scale, g_mix, g_ffn, g_final, attn_sink, rel_bias)
    m_small = _pack_small(m_pool_scale, m_g_mix, m_g_ffn, m_g_final, m_attn_sink, m_rel_bias)
    v_small = _pack_small(v_pool_scale, v_g_mix, v_g_ffn, v_g_final, v_attn_sink, v_rel_bias)
    small_out = [_unpack_small(o[0]) for o in
                 _adamw(0, w_small[None], m_small[None], v_small[None], g_small, jnp.zeros_like(g_small), None)]

    total_loss = lax.psum(loss, ("x", "y", "c"))

    order = ("w_in", "conv_w", "w_a_out", "w_pool", "pool_scale", "w_attn_out", "attn_sink", "w_o", "g_mix", "g_ffn",
             "w_gu", "w_down", "rel_bias", "g_final")
    outs = [total_loss, grad_x[None]]
    for k in range(4):
        for n in order:
            outs.append(stacked[n][k] if n in stacked else small_out[k][n])
    return tuple(outs)
```

```python
import functools
import math

import numpy as np
import jax
import jax.numpy as jnp
from jax import lax
from jax.experimental import pallas as pl
from jax.experimental.pallas import tpu as pltpu

F32 = jnp.float32
BF16 = jnp.bfloat16

D_MODEL = 1024
DEPTH = 4
N_HEADS = 16
N_KV_HEADS = 4
HEAD_DIM = 64
GQA = N_HEADS // N_KV_HEADS
WINDOW = 128
BLOCK = 128
N_BUCKETS = 32
MAX_DISTANCE = 128
POOL_GROUPS = 4
POOL_CG = D_MODEL // POOL_GROUPS
POOL_WINDOWS = (2, 4, 8, 16)
D_FF = 2816
IN_TOTAL = 8704
OFF_B, OFF_C, OFF_X, OFF_U, OFF_Q, OFF_K, OFF_V, OFF_GA, OFF_GP, OFF_GT = (
    0, 1024, 2048, 3072, 4096, 5120, 5376, 5632, 6656, 7680)
EPS = 1e-6
NEG_INF = -1e30
SM_SCALE = HEAD_DIM ** -0.5

ADAM_LR = 0.001
ADAM_B1 = 0.9
ADAM_B2 = 0.999
ADAM_EPS = 1e-08
ADAM_WD = 0.01
ADAM_STEP = 10

N_CHIPS = 4
HALO = 8
V7X_VMEM_LIMIT = 56 * 1024 * 1024
MESH = pl.DeviceIdType.MESH
ANY = pl.BlockSpec(memory_space=pl.ANY)


def _params(*sem):
    return pltpu.CompilerParams(dimension_semantics=tuple(sem) if sem else None,
                                vmem_limit_bytes=V7X_VMEM_LIMIT)


def _tile(n, pref):
    t = min(pref, n)
    while n % t or t % 128:
        t -= 128
    return t


def _nt(a, b):
    return lax.dot_general(a, b, (((1,), (1,)), ((), ())), preferred_element_type=F32)


def _tn(a, b):
    return lax.dot_general(a, b, (((0,), (0,)), ((), ())), preferred_element_type=F32)


def _nn(a, b):
    return jnp.dot(a, b, preferred_element_type=F32)


def _sigmoid(v):
    return 1.0 / (1.0 + jnp.exp(-v))


def _norm_matmul(x, g, w, name, token):
    S, Dm = x.shape
    N = w.shape[1]
    tm, tn = _tile(S, 1024), _tile(N, 512)

    def body(x_ref, g_ref, w_ref, token_ref, h_ref, o_ref):
        @pl.when(pl.program_id(1) == 0)
        def _():
            xv = x_ref[...]
            r = lax.rsqrt(jnp.mean(xv * xv, axis=-1, keepdims=True) + EPS)
            h_ref[...] = (xv * r * g_ref[...]).astype(BF16)
        o_ref[...] = _nn(h_ref[...], w_ref[...]).astype(BF16)

    return pl.pallas_call(
        body, name=name, grid=(S // tm, N // tn),
        in_specs=[pl.BlockSpec((tm, Dm), lambda i, j: (i, 0)),
                  pl.BlockSpec((1, Dm), lambda i, j: (0, 0)),
                  pl.BlockSpec((Dm, tn), lambda i, j: (0, j)), ANY],
        out_specs=[pl.BlockSpec((tm, Dm), lambda i, j: (i, 0)),
                   pl.BlockSpec((tm, tn), lambda i, j: (i, j))],
        out_shape=[jax.ShapeDtypeStruct((S, Dm), BF16), jax.ShapeDtypeStruct((S, N), BF16)],
        compiler_params=_params("parallel", "arbitrary"),
    )(x, g, w, token)


CB = 128


def _fill_padded(pad_ref, v, S):
    z = jnp.zeros((HALO, v.shape[1]), F32)
    pad_ref[pl.ds(0, HALO), :] = z
    pad_ref[pl.ds(S + HALO, HALO), :] = z
    pad_ref[pl.ds(HALO, S), :] = v


def _shifted(pad_ref, off, S):
    return pad_ref[pl.ds(HALO + off, S), :]


def _conv_fwd(proj, cw8):
    S = proj.shape[0]
    nblk = D_MODEL // CB

    def body(b_ref, c_ref, x_ref, w_ref, o_ref, pad):
        u = c_ref[...].astype(F32) * x_ref[...].astype(F32)
        _fill_padded(pad, u, S)
        cv = w_ref[0:1, :] * _shifted(pad, -1, S) + w_ref[1:2, :] * u + w_ref[2:3, :] * _shifted(pad, 1, S)
        o_ref[...] = (b_ref[...].astype(F32) * cv).astype(BF16)

    col = lambda base: pl.BlockSpec((S, CB), lambda j: (0, base // CB + j))
    return pl.pallas_call(
        body, name="conv_fwd", grid=(nblk,),
        in_specs=[col(OFF_B), col(OFF_C), col(OFF_X), pl.BlockSpec((8, CB), lambda j: (0, j))],
        out_specs=pl.BlockSpec((S, CB), lambda j: (0, j)),
        out_shape=jax.ShapeDtypeStruct((S, D_MODEL), BF16),
        scratch_shapes=[pltpu.VMEM((S + 2 * HALO, CB), F32)],
        compiler_params=_params("parallel"),
    )(proj, proj, proj, cw8)


def _pool_count(S, lo, hi):
    t = lax.broadcasted_iota(jnp.int32, (S, CB), 0)
    return (jnp.minimum(t + hi, S - 1) - jnp.maximum(t - lo, 0) + 1).astype(F32)


def _pool_fwd(proj):
    S = proj.shape[0]
    nblk = D_MODEL // CB
    per_group = POOL_CG // CB

    def body(u_ref, o_ref, pad):
        u = u_ref[...].astype(F32)
        _fill_padded(pad, u, S)
        grp = pl.program_id(0) // per_group
        for gi, w in enumerate(POOL_WINDOWS):
            @pl.when(grp == gi)
            def _(w=w):
                lo, hi = w // 2, w - 1 - w // 2
                acc = _shifted(pad, -lo, S)
                for off in range(-lo + 1, hi + 1):
                    acc = acc + _shifted(pad, off, S)
                o_ref[...] = (acc / _pool_count(S, lo, hi) - u).astype(BF16)

    return pl.pallas_call(
        body, name="pool_fwd", grid=(nblk,),
        in_specs=[pl.BlockSpec((S, CB), lambda j: (0, OFF_U // CB + j))],
        out_specs=pl.BlockSpec((S, CB), lambda j: (0, j)),
        out_shape=jax.ShapeDtypeStruct((S, D_MODEL), BF16),
        scratch_shapes=[pltpu.VMEM((S + 2 * HALO, CB), F32)],
        compiler_params=_params("parallel"),
    )(proj)


def _attn_specs(S):
    nb = S // BLOCK
    kcol, vcol = OFF_K // (N_KV_HEADS * HEAD_DIM), OFF_V // (N_KV_HEADS * HEAD_DIM)
    kvw = N_KV_HEADS * HEAD_DIM
    prev = lambda i: jnp.maximum(i - 1, 0)
    nxt = lambda i: jnp.minimum(i + 1, nb - 1)
    return [
        pl.BlockSpec((BLOCK, D_MODEL), lambda i: (i, OFF_Q // D_MODEL)),
        pl.BlockSpec((BLOCK, kvw), lambda i: (prev(i), kcol)),
        pl.BlockSpec((BLOCK, kvw), lambda i: (i, kcol)),
        pl.BlockSpec((BLOCK, kvw), lambda i: (nxt(i), kcol)),
        pl.BlockSpec((BLOCK, kvw), lambda i: (prev(i), vcol)),
        pl.BlockSpec((BLOCK, kvw), lambda i: (i, vcol)),
        pl.BlockSpec((BLOCK, kvw), lambda i: (nxt(i), vcol)),
    ]


def _heads_rows(ref_or_val, hk):
    return jnp.concatenate(
        [ref_or_val[:, (GQA * hk + g) * HEAD_DIM:(GQA * hk + g + 1) * HEAD_DIM] for g in range(GQA)], axis=0)


def _kv_rows(p_ref, c_ref, n_ref, hk):
    sl = slice(hk * HEAD_DIM, (hk + 1) * HEAD_DIM)
    return jnp.concatenate([p_ref[:, sl], c_ref[:, sl], n_ref[:, sl]], axis=0)


def _bias_cols(bias_ref, hk):
    return jnp.concatenate([bias_ref[GQA * hk + g] for g in range(GQA)], axis=1)


def _softmax_keys_on_rows(q4s, kc, bias_blk, sink_row):
    s = _nt(kc, q4s) + bias_blk
    m = jnp.maximum(jnp.max(s, axis=0, keepdims=True), sink_row)
    p = jnp.exp(s - m)
    e_sink = jnp.exp(sink_row - m)
    inv = 1.0 / (jnp.sum(p, axis=0, keepdims=True) + e_sink)
    return p * inv, e_sink * inv


TAB = (N_HEADS, 3 * BLOCK, BLOCK)
TAB_FLAT = 3 * BLOCK * BLOCK


def _bias_spec(nb):
    return pl.BlockSpec((None,) + TAB, lambda i: (jnp.where(i == 0, 0, jnp.where(i == nb - 1, 2, 1)), 0, 0, 0))


def _attn_fwd(proj, bias_tabs, sink_rows):
    S = proj.shape[0]
    nb = S // BLOCK
    assert nb >= 2

    def body(q_ref, kp, kc_, kn, vp, vc_, vn, bias_ref, sink_ref, o_ref):
        outs = []
        for hk in range(N_KV_HEADS):
            q4s = _heads_rows(q_ref, hk) * SM_SCALE
            kc = _kv_rows(kp, kc_, kn, hk)
            vc = _kv_rows(vp, vc_, vn, hk)
            pn, _ = _softmax_keys_on_rows(q4s, kc, _bias_cols(bias_ref, hk), sink_ref[hk:hk + 1, :])
            o4 = _tn(pn.astype(BF16), vc)
            outs += [o4[g * BLOCK:(g + 1) * BLOCK, :] for g in range(GQA)]
        o_ref[...] = jnp.concatenate(outs, axis=1).astype(BF16)

    return pl.pallas_call(
        body, name="attn_fwd", grid=(nb,),
        in_specs=_attn_specs(S) + [_bias_spec(nb), pl.BlockSpec((N_KV_HEADS, GQA * BLOCK), lambda i: (0, 0))],
        out_specs=pl.BlockSpec((BLOCK, D_MODEL), lambda i: (i, 0)),
        out_shape=jax.ShapeDtypeStruct((S, D_MODEL), BF16),
        compiler_params=_params("parallel"),
    )(*([proj] * 7), bias_tabs, sink_rows)


GATE_HALF = D_MODEL // 2


def _gate_specs(tm):
    return [pl.BlockSpec((tm, GATE_HALF), lambda i, c=off // GATE_HALF + k: (i, c))
            for off in (OFF_GA, OFF_GP, OFF_GT) for k in (0, 1)]


def _gate(lo_ref, hi_ref):
    return _sigmoid(jnp.concatenate([lo_ref[...], hi_ref[...]], axis=1).astype(F32))


def _pool_mix(p, wp):
    return jnp.concatenate(
        [_nn(p[:, g * POOL_CG:(g + 1) * POOL_CG], wp[g]) for g in range(POOL_GROUPS)], axis=1)


def _mix_fwd(za, p, att, proj, x, wa, wp, ps, wt, wo):
    S = x.shape[0]
    tm = _tile(S, 256)

    def body(za_ref, p_ref, att_ref, ga0, ga1, gp0, gp1, gt0, gt1, x_ref, wa_ref, wp_ref, ps_ref, wt_ref, wo_ref,
             ya_ref, yp_ref, yt_ref, mg_ref, x2_ref):
        ya = _nn(za_ref[...], wa_ref[...])
        ypr = _pool_mix(p_ref[...], wp_ref)
        yt = _nn(att_ref[...], wt_ref[...])
        merged = _gate(ga0, ga1) * ya + _gate(gp0, gp1) * (ypr * ps_ref[...]) + _gate(gt0, gt1) * yt
        mb = merged.astype(BF16)
        ya_ref[...] = ya.astype(BF16)
        yp_ref[...] = ypr.astype(BF16)
        yt_ref[...] = yt.astype(BF16)
        mg_ref[...] = mb
        x2_ref[...] = x_ref[...] + _nn(mb, wo_ref[...])

    row = lambda c=0: pl.BlockSpec((tm, D_MODEL), lambda i: (i, c))
    whole = lambda a: pl.BlockSpec(a.shape, lambda i: (0,) * a.ndim)
    act = jax.ShapeDtypeStruct((S, D_MODEL), BF16)
    return pl.pallas_call(
        body, name="mix_fwd", grid=(S // tm,),
        in_specs=[row(), row(), row()] + _gate_specs(tm) + [row(), whole(wa), whole(wp), whole(ps), whole(wt), whole(wo)],
        out_specs=[row(), row(), row(), row(), row()],
        out_shape=[act, act, act, act, jax.ShapeDtypeStruct((S, D_MODEL), F32)],
        compiler_params=_params("parallel"),
    )(za, p, att, *([proj] * 6), x, wa, wp, ps, wt, wo)


def _ffn_fwd(gu, x2, wd):
    S = x2.shape[0]
    tm = _tile(S, 256)

    def body(g_ref, u_ref, x_ref, w_ref, a_ref, o_ref):
        g = g_ref[...].astype(F32)
        a = (g * _sigmoid(g) * u_ref[...].astype(F32)).astype(BF16)
        a_ref[...] = a
        o_ref[...] = x_ref[...] + _nn(a, w_ref[...])

    return pl.pallas_call(
        body, name="ffn_fwd", grid=(S // tm,),
        in_specs=[pl.BlockSpec((tm, D_FF), lambda i: (i, 0)), pl.BlockSpec((tm, D_FF), lambda i: (i, 1)),
                  pl.BlockSpec((tm, D_MODEL), lambda i: (i, 0)), pl.BlockSpec((D_FF, D_MODEL), lambda i: (0, 0))],
        out_specs=[pl.BlockSpec((tm, D_FF), lambda i: (i, 0)), pl.BlockSpec((tm, D_MODEL), lambda i: (i, 0))],
        out_shape=[jax.ShapeDtypeStruct((S, D_FF), BF16), jax.ShapeDtypeStruct((S, D_MODEL), F32)],
        compiler_params=_params("parallel"),
    )(gu, gu, x2, wd)


def _loss_bwd(x, g, tgt):
    S, Dm = x.shape
    tm = _tile(S, 512)

    def body(x_ref, g_ref, t_ref, l_ref, dx_ref, dg_ref):
        @pl.when(pl.program_id(0) == 0)
        def _():
            l_ref[...] = jnp.zeros_like(l_ref)
            dg_ref[...] = jnp.zeros_like(dg_ref)
        xv, gv = x_ref[...], g_ref[...]
        r = lax.rsqrt(jnp.mean(xv * xv, axis=-1, keepdims=True) + EPS)
        n = xv * r
        err = n * gv - t_ref[...]
        l_ref[...] += 0.5 * jnp.sum(jnp.mean(err * err, axis=-1, keepdims=True), axis=0, keepdims=True)
        dy = err * (1.0 / Dm)
        dn = dy * gv
        dx_ref[...] = r * (dn - n * jnp.mean(dn * n, axis=-1, keepdims=True))
        dg_ref[...] += jnp.sum(dy * n, axis=0, keepdims=True)

    return pl.pallas_call(
        body, name="loss_bwd", grid=(S // tm,),
        in_specs=[pl.BlockSpec((tm, Dm), lambda i: (i, 0)), pl.BlockSpec((1, Dm), lambda i: (0, 0)),
                  pl.BlockSpec((tm, Dm), lambda i: (i, 0))],
        out_specs=[pl.BlockSpec((8, 128), lambda i: (0, 0)), pl.BlockSpec((tm, Dm), lambda i: (i, 0)),
                   pl.BlockSpec((1, Dm), lambda i: (0, 0))],
        out_shape=[jax.ShapeDtypeStruct((8, 128), F32), jax.ShapeDtypeStruct((S, Dm), F32),
                   jax.ShapeDtypeStruct((1, Dm), F32)],
        compiler_params=_params("arbitrary"),
    )(x, g, tgt)


def _ffn_bwd(dx3, gu, wd, token):
    S = dx3.shape[0]
    tm = _tile(S, 256)

    def body(d_ref, g_ref, u_ref, w_ref, token_ref, o_ref):
        dact = _nt(d_ref[...].astype(BF16), w_ref[...])
        g, u = g_ref[...].astype(F32), u_ref[...].astype(F32)
        sg = _sigmoid(g)
        o_ref[:, 0:D_FF] = (dact * u * (sg * (1.0 + g * (1.0 - sg)))).astype(BF16)
        o_ref[:, D_FF:2 * D_FF] = (dact * (g * sg)).astype(BF16)

    return pl.pallas_call(
        body, name="ffn_bwd", grid=(S // tm,),
        in_specs=[pl.BlockSpec((tm, D_MODEL), lambda i: (i, 0)),
                  pl.BlockSpec((tm, D_FF), lambda i: (i, 0)), pl.BlockSpec((tm, D_FF), lambda i: (i, 1)),
                  pl.BlockSpec((D_FF, D_MODEL), lambda i: (0, 0)), ANY],
        out_specs=pl.BlockSpec((tm, 2 * D_FF), lambda i: (i, 0)),
        out_shape=jax.ShapeDtypeStruct((S, 2 * D_FF), BF16),
        compiler_params=_params("parallel"),
    )(dx3, gu, gu, wd, token)


def _wgrad(a, b, name, tk=512, tn=512, out_dtype=BF16, token=None):
    S, K = a.shape
    N = b.shape[1]
    tk, tn, ts = _tile(K, tk), _tile(N, tn), _tile(S, 1024)
    n_s = S // ts
    extra = [] if token is None else [token]

    def body(a_ref, b_ref, *rest):
        o_ref, acc = rest[-2:]
        s = pl.program_id(2)

        @pl.when(s == 0)
        def _():
            acc[...] = jnp.zeros_like(acc)
        acc[...] += _tn(a_ref[...].astype(BF16), b_ref[...].astype(BF16))

        @pl.when(s == n_s - 1)
        def _():
            o_ref[...] = acc[...].astype(out_dtype)

    return pl.pallas_call(
        body, name=name, grid=(K // tk, N // tn, n_s),
        in_specs=[pl.BlockSpec((ts, tk), lambda k, n, s: (s, k)), pl.BlockSpec((ts, tn), lambda k, n, s: (s, n))]
        + [ANY] * len(extra),
        out_specs=pl.BlockSpec((tk, tn), lambda k, n, s: (k, n)),
        out_shape=jax.ShapeDtypeStruct((K, N), out_dtype),
        scratch_shapes=[pltpu.VMEM((tk, tn), F32)],
        compiler_params=_params("parallel", "parallel", "arbitrary"),
    )(a, b, *extra)


def _wgrad_pool(p, dyps):
    S = p.shape[0]
    ts = _tile(S, 512)
    n_s = S // ts

    def body(a_ref, b_ref, o_ref, acc):
        s = pl.program_id(1)

        @pl.when(s == 0)
        def _():
            acc[...] = jnp.zeros_like(acc)
        acc[...] += _tn(a_ref[...], b_ref[...])

        @pl.when(s == n_s - 1)
        def _():
            o_ref[...] = acc[...].astype(BF16)

    return pl.pallas_call(
        body, name="wgrad_pool", grid=(POOL_GROUPS, n_s),
        in_specs=[pl.BlockSpec((ts, POOL_CG), lambda g, s: (s, g)), pl.BlockSpec((ts, POOL_CG), lambda g, s: (s, g))],
        out_specs=pl.BlockSpec((None, POOL_CG, POOL_CG), lambda g, s: (g, 0, 0)),
        out_shape=jax.ShapeDtypeStruct((POOL_GROUPS, POOL_CG, POOL_CG), BF16),
        scratch_shapes=[pltpu.VMEM((POOL_CG, POOL_CG), F32)],
        compiler_params=_params("parallel", "arbitrary"),
    )(p, dyps)


def _dgrad_norm_bwd(dy, w, x, g, dres, name, tk):
    S, K = dy.shape
    Dm = x.shape[1]
    tm, tk = _tile(S, 512), _tile(K, tk)
    n_k = K // tk

    def body(dy_ref, w_ref, x_ref, g_ref, r_ref, dx_ref, dg_ref, acc):
        i, k = pl.program_id(0), pl.program_id(1)

        @pl.when((i == 0) & (k == 0))
        def _():
            dg_ref[...] = jnp.zeros_like(dg_ref)

        @pl.when(k == 0)
        def _():
            acc[...] = jnp.zeros_like(acc)
        acc[...] += _nt(dy_ref[...], w_ref[...])

        @pl.when(k == n_k - 1)
        def _():
            dh, xv = acc[...], x_ref[...]
            r = lax.rsqrt(jnp.mean(xv * xv, axis=-1, keepdims=True) + EPS)
            n = xv * r
            dn = dh * g_ref[...]
            dx_ref[...] = r_ref[...] + r * (dn - n * jnp.mean(dn * n, axis=-1, keepdims=True))
            dg_ref[...] += jnp.sum(dh * n, axis=0, keepdims=True)

    rowblk = pl.BlockSpec((tm, Dm), lambda i, k: (i, 0))
    vec = pl.BlockSpec((1, Dm), lambda i, k: (0, 0))
    return pl.pallas_call(
        body, name=name, grid=(S // tm, n_k),
        in_specs=[pl.BlockSpec((tm, tk), lambda i, k: (i, k)), pl.BlockSpec((Dm, tk), lambda i, k: (0, k)),
                  rowblk, vec, rowblk],
        out_specs=[rowblk, vec],
        out_shape=[jax.ShapeDtypeStruct((S, Dm), F32), jax.ShapeDtypeStruct((1, Dm), F32)],
        scratch_shapes=[pltpu.VMEM((tm, Dm), F32)],
        compiler_params=_params("arbitrary", "arbitrary"),
    )(dy, w, x, g, dres)


def _mix_bwd(dx2, ya, ypr, yt, proj, ps, wa, wp, wt, wo, token):
    S = dx2.shape[0]
    tm = _tile(S, 256)

    def body(dx_ref, ya_ref, yp_ref, yt_ref, ga0, ga1, gp0, gp1, gt0, gt1, ps_ref, wa_ref, wp_ref, wt_ref, wo_ref,
             token_ref, dya_ref, dyt_ref, dyps_ref, dza_ref, datt_ref, dp_ref, dga_ref, dgp_ref, dgt_ref, dps_ref):
        @pl.when(pl.program_id(0) == 0)
        def _():
            dps_ref[...] = jnp.zeros_like(dps_ref)
        dm = _nt(dx_ref[...].astype(BF16), wo_ref[...])
        sa, sp, st = _gate(ga0, ga1), _gate(gp0, gp1), _gate(gt0, gt1)
        psv = ps_ref[...]
        ypr_v = yp_ref[...].astype(F32)
        dya = (sa * dm).astype(BF16)
        dyt = (st * dm).astype(BF16)
        dyp = sp * dm
        dyps = (dyp * psv).astype(BF16)
        dya_ref[...] = dya
        dyt_ref[...] = dyt
        dyps_ref[...] = dyps
        dga_ref[...] = (dm * ya_ref[...].astype(F32) * (sa * (1.0 - sa))).astype(BF16)
        dgp_ref[...] = (dm * (ypr_v * psv) * (sp * (1.0 - sp))).astype(BF16)
        dgt_ref[...] = (dm * yt_ref[...].astype(F32) * (st * (1.0 - st))).astype(BF16)
        dps_ref[...] += jnp.sum(dyp * ypr_v, axis=0, keepdims=True)
        dza_ref[...] = _nt(dya, wa_ref[...]).astype(BF16)
        datt_ref[...] = _nt(dyt, wt_ref[...]).astype(BF16)
        dp_ref[...] = jnp.concatenate(
            [_nt(dyps[:, g * POOL_CG:(g + 1) * POOL_CG], wp_ref[g]) for g in range(POOL_GROUPS)], axis=1).astype(BF16)

    row = lambda c=0: pl.BlockSpec((tm, D_MODEL), lambda i: (i, c))
    whole = lambda a: pl.BlockSpec(a.shape, lambda i: (0,) * a.ndim)
    act = jax.ShapeDtypeStruct((S, D_MODEL), BF16)
    return pl.pallas_call(
        body, name="mix_bwd", grid=(S // tm,),
        in_specs=[row(), row(), row(), row()] + _gate_specs(tm)
        + [whole(ps), whole(wa), whole(wp), whole(wt), whole(wo), ANY],
        out_specs=[row()] * 9 + [pl.BlockSpec((1, D_MODEL), lambda i: (0, 0))],
        out_shape=[act] * 9 + [jax.ShapeDtypeStruct((1, D_MODEL), F32)],
        compiler_params=_params("arbitrary"),
    )(dx2, ya, ypr, yt, *([proj] * 6), ps, wa, wp, wt, wo, token)


def _conv_bwd(dza, proj, cw8):
    S = proj.shape[0]
    nblk = D_MODEL // CB

    def body(d_ref, b_ref, c_ref, x_ref, w_ref, db_ref, dc_ref, dxa_ref, dw_ref, pad_u, pad_d):
        c, xa = c_ref[...].astype(F32), x_ref[...].astype(F32)
        u = c * xa
        _fill_padded(pad_u, u, S)
        u_prev, u_next = _shifted(pad_u, -1, S), _shifted(pad_u, 1, S)
        cv = w_ref[0:1, :] * u_prev + w_ref[1:2, :] * u + w_ref[2:3, :] * u_next
        dza_v = d_ref[...].astype(F32)
        db_ref[...] = (dza_v * cv).astype(BF16)
        dcv = dza_v * b_ref[...].astype(F32)
        _fill_padded(pad_d, dcv, S)
        du = w_ref[0:1, :] * _shifted(pad_d, 1, S) + w_ref[1:2, :] * dcv + w_ref[2:3, :] * _shifted(pad_d, -1, S)
        dc_ref[...] = (du * xa).astype(BF16)
        dxa_ref[...] = (du * c).astype(BF16)
        dw_ref[...] = jnp.concatenate(
            [jnp.sum(dcv * u_prev, axis=0, keepdims=True), jnp.sum(dcv * u, axis=0, keepdims=True),
             jnp.sum(dcv * u_next, axis=0, keepdims=True), jnp.zeros((5, CB), F32)], axis=0)

    col = lambda base: pl.BlockSpec((S, CB), lambda j: (0, base // CB + j))
    act = jax.ShapeDtypeStruct((S, D_MODEL), BF16)
    return pl.pallas_call(
        body, name="conv_bwd", grid=(nblk,),
        in_specs=[col(0), col(OFF_B), col(OFF_C), col(OFF_X), pl.BlockSpec((8, CB), lambda j: (0, j))],
        out_specs=[col(0), col(0), col(0), pl.BlockSpec((8, CB), lambda j: (0, j))],
        out_shape=[act, act, act, jax.ShapeDtypeStruct((8, D_MODEL), F32)],
        scratch_shapes=[pltpu.VMEM((S + 2 * HALO, CB), F32), pltpu.VMEM((S + 2 * HALO, CB), F32)],
        compiler_params=_params("parallel"),
    )(dza, proj, proj, proj, cw8)


def _pool_bwd(dp):
    S = dp.shape[0]
    nblk = D_MODEL // CB
    per_group = POOL_CG // CB

    def body(d_ref, o_ref, pad):
        d = d_ref[...].astype(F32)
        grp = pl.program_id(0) // per_group
        for gi, w in enumerate(POOL_WINDOWS):
            @pl.when(grp == gi)
            def _(w=w):
                lo, hi = w // 2, w - 1 - w // 2
                _fill_padded(pad, d / _pool_count(S, lo, hi), S)
                acc = _shifted(pad, -hi, S)
                for off in range(-hi + 1, lo + 1):
                    acc = acc + _shifted(pad, off, S)
                o_ref[...] = (acc - d).astype(BF16)

    return pl.pallas_call(
        body, name="pool_bwd", grid=(nblk,),
        in_specs=[pl.BlockSpec((S, CB), lambda j: (0, j))],
        out_specs=pl.BlockSpec((S, CB), lambda j: (0, j)),
        out_shape=jax.ShapeDtypeStruct((S, D_MODEL), BF16),
        scratch_shapes=[pltpu.VMEM((S + 2 * HALO, CB), F32)],
        compiler_params=_params("parallel"),
    )(dp)


def _attn_bwd(proj, datt, bias_tabs, sink_rows, dbias_in):
    S = proj.shape[0]
    nb = S // BLOCK
    kvw = N_KV_HEADS * HEAD_DIM

    def body(q_ref, kp, kc_, kn, vp, vc_, vn, do_ref, bias_ref, sink_ref, dbin_ref,
             dq_ref, dk_ref, dv_ref, db_ref, ds_ref):
        i = pl.program_id(0)

        @pl.when(i == 0)
        def _():
            dk_ref[...] = jnp.zeros_like(dk_ref)
            dv_ref[...] = jnp.zeros_like(dv_ref)
            db_ref[...] = dbin_ref[...]
            ds_ref[...] = jnp.zeros_like(ds_ref)
        dqs, dks, dvs = [], [], []
        for hk in range(N_KV_HEADS):
            q4s = _heads_rows(q_ref, hk) * SM_SCALE
            do4 = _heads_rows(do_ref, hk)
            kc = _kv_rows(kp, kc_, kn, hk)
            vc = _kv_rows(vp, vc_, vn, hk)
            pn, p_sink = _softmax_keys_on_rows(q4s, kc, _bias_cols(bias_ref, hk), sink_ref[hk:hk + 1, :])
            dpm = _nt(vc, do4)
            delta = jnp.sum(pn * dpm, axis=0, keepdims=True)
            dsc = pn * (dpm - delta)
            for g in range(GQA):
                db_ref[GQA * hk + g] += dsc[:, g * BLOCK:(g + 1) * BLOCK]
            ds_ref[hk:hk + 1, :] += -p_sink * delta
            dsb = dsc.astype(BF16)
            dq4 = _tn(dsb, kc) * SM_SCALE
            dqs += [dq4[g * BLOCK:(g + 1) * BLOCK, :] for g in range(GQA)]
            dks.append(_nn(dsb, q4s))
            dvs.append(_nn(pn.astype(BF16), do4))
        dq_ref[...] = jnp.concatenate(dqs, axis=1).astype(BF16)
        r0 = pl.multiple_of(i * BLOCK, BLOCK)
        dk_ref[pl.ds(r0, 3 * BLOCK), :] += jnp.concatenate(dks, axis=1)
        dv_ref[pl.ds(r0, 3 * BLOCK), :] += jnp.concatenate(dvs, axis=1)

    const = lambda shape: pl.BlockSpec(shape, lambda i: (0,) * len(shape))
    sink_shape = (N_KV_HEADS, GQA * BLOCK)
    return pl.pallas_call(
        body, name="attn_bwd", grid=(nb,),
        in_specs=_attn_specs(S) + [pl.BlockSpec((BLOCK, D_MODEL), lambda i: (i, 0)),
                                   _bias_spec(nb), const(sink_shape), const(TAB)],
        out_specs=[pl.BlockSpec((BLOCK, D_MODEL), lambda i: (i, 0)),
                   const((S + 2 * BLOCK, kvw)), const((S + 2 * BLOCK, kvw)), const(TAB), const(sink_shape)],
        out_shape=[jax.ShapeDtypeStruct((S, D_MODEL), BF16),
                   jax.ShapeDtypeStruct((S + 2 * BLOCK, kvw), F32), jax.ShapeDtypeStruct((S + 2 * BLOCK, kvw), F32),
                   jax.ShapeDtypeStruct(TAB, F32), jax.ShapeDtypeStruct(sink_shape, F32)],
        compiler_params=_params("arbitrary"),
    )(*([proj] * 7), datt, bias_tabs, sink_rows, dbias_in)


def _bucket_constants():
    half = N_BUCKETS // 2
    max_exact = half // 2
    qi = np.arange(BLOCK)[None, :]
    kj = np.arange(3 * BLOCK)[:, None]
    rel = kj - BLOCK - qi
    n = np.abs(rel)
    nf = np.maximum(n, 1).astype(np.float32)
    large = max_exact + (np.log(nf / np.float32(max_exact)) / np.float32(math.log(MAX_DISTANCE / max_exact))
                         * np.float32(half - max_exact)).astype(np.int32)
    large = np.minimum(large, half - 1)
    bucket = np.where(rel > 0, half, 0) + np.where(n < max_exact, n, large)
    onehot = (bucket.reshape(1, -1) == np.arange(N_BUCKETS)[:, None]).astype(np.float32)
    window = n <= WINDOW
    first = window & (kj >= BLOCK)
    last = window & (kj < 2 * BLOCK)
    masks = np.stack([np.where(v, 0.0, NEG_INF).astype(np.float32).reshape(-1) for v in (first, window, last)])
    return onehot, masks


def _bias_expand(rel_bias_t, onehot, masks):
    def body(r_ref, oh_ref, m_ref, o_ref):
        tab = jnp.dot(r_ref[...], oh_ref[...], preferred_element_type=F32, precision=lax.Precision.HIGHEST)
        for v in range(3):
            o_ref[v] = tab + m_ref[v:v + 1, :]

    return pl.pallas_call(
        body, name="bias_expand", out_shape=jax.ShapeDtypeStruct((3, N_HEADS, onehot.shape[1]), F32),
        compiler_params=_params(),
    )(rel_bias_t, onehot, masks)


def _bias_reduce(dtab, dsink_rows, onehot):
    def body(d_ref, s_ref, oh_ref, o_ref, so_ref):
        o_ref[...] = lax.dot_general(oh_ref[...], d_ref[...], (((1,), (1,)), ((), ())),
                                     preferred_element_type=F32, precision=lax.Precision.HIGHEST)
        so_ref[...] = jnp.sum(s_ref[...], axis=-1, keepdims=True)

    return pl.pallas_call(
        body, name="bias_reduce",
        out_shape=[jax.ShapeDtypeStruct((N_BUCKETS, N_HEADS), F32),
                   jax.ShapeDtypeStruct((dsink_rows.shape[0], 1), F32)],
        compiler_params=_params(),
    )(dtab, dsink_rows, onehot)


GROUPS = dict(mix=("w_in", "conv_w", "w_a_out", "w_pool", "w_attn_out", "w_o"), ffn=("w_gu", "w_down"))
WEIGHT_NAMES = GROUPS["mix"] + GROUPS["ffn"]


def _layer_fwd(l, x, weights_of, ps, g_mix, g_ffn, bias_tabs, sink_rows):
    W, token = weights_of(l, "mix", x)
    h, proj = _norm_matmul(x, g_mix, W["w_in"], "norm_proj", token)
    za = _conv_fwd(proj, W["conv_w"])
    p = _pool_fwd(proj)
    att = _attn_fwd(proj, bias_tabs, sink_rows)
    ya, ypr, yt, merged, x2 = _mix_fwd(za, p, att, proj, x, W["w_a_out"], W["w_pool"], ps, W["w_attn_out"], W["w_o"])
    Wf, token = weights_of(l, "ffn", x2)
    h2, gu = _norm_matmul(x2, g_ffn, Wf["w_gu"], "norm_gu", token)
    act, x3 = _ffn_fwd(gu, x2, Wf["w_down"])
    saved = dict(x=x, h=h, proj=proj, za=za, p=p, att=att, ya=ya, ypr=ypr, yt=yt, merged=merged, x2=x2, h2=h2,
                 gu=gu, act=act, W={**W, **Wf}, sink_rows=sink_rows)
    return x3, saved, token


def _layer_bwd(l, dx3, sv, grads_to, ps, g_mix, g_ffn, bias_tabs, dbias, token):
    S = dx3.shape[0]
    W, sink_rows = sv["W"], sv["sink_rows"]
    dgu = _ffn_bwd(dx3, sv["gu"], W["w_down"], token)
    g_w_down = _wgrad(sv["act"], dx3, "wgrad_down", tk=_tile(D_FF, 1408), tn=512, token=token)
    g_w_gu = _wgrad(sv["h2"], dgu, "wgrad_gu", tk=1024, tn=512)
    dx2, dg_ffn = _dgrad_norm_bwd(dgu, W["w_gu"], sv["x2"], g_ffn, dx3, "dgrad_gu", tk=1408)
    token = grads_to(l, "ffn", dict(w_gu=g_w_gu, w_down=g_w_down), dx2)
    dya, dyt, dyps, dza, datt, dp, dga, dgp, dgt, dps = _mix_bwd(
        dx2, sv["ya"], sv["ypr"], sv["yt"], sv["proj"], ps, W["w_a_out"], W["w_pool"], W["w_attn_out"], W["w_o"], token)
    g_w_o = _wgrad(sv["merged"], dx2, "wgrad_sq_f32", tk=1024, tn=512)
    g_w_a_out = _wgrad(sv["za"], dya, "wgrad_sq", tk=1024, tn=512)
    g_w_attn_out = _wgrad(sv["att"], dyt, "wgrad_sq", tk=1024, tn=512)
    g_w_pool = _wgrad_pool(sv["p"], dyps)
    db, dc, dxa, g_conv = _conv_bwd(dza, sv["proj"], W["conv_w"])
    dup = _pool_bwd(dp)
    dq, dkp, dvp, dbias, dsink = _attn_bwd(sv["proj"], datt, bias_tabs, sink_rows, dbias)
    dproj = jnp.concatenate([db, dc, dxa, dup, dq, dkp[BLOCK:BLOCK + S].astype(BF16),
                             dvp[BLOCK:BLOCK + S].astype(BF16), dga, dgp, dgt], axis=1)
    g_w_in = _wgrad(sv["h"], dproj, "wgrad_in", tk=1024, tn=512)
    dx, dg_mix = _dgrad_norm_bwd(dproj, W["w_in"], sv["x"], g_mix, dx2, "dgrad_in", tk=2176)
    token = grads_to(l, "mix", dict(w_in=g_w_in, conv_w=g_conv, w_a_out=g_w_a_out, w_pool=g_w_pool,
                                    w_attn_out=g_w_attn_out, w_o=g_w_o), dx)
    return dx, dict(pool_scale=dps, g_mix=dg_mix, g_ffn=dg_ffn, attn_sink=dsink), dbias, token


def _local_step(x, tgt, weights_of, grads_to, pool_scale, attn_sink, g_mix, g_ffn, rel_bias, g_final):
    onehot_np, masks_np = _bucket_constants()
    onehot, masks = jnp.asarray(onehot_np), jnp.asarray(masks_np)
    bias_tabs = _bias_expand(rel_bias.T, onehot, masks).reshape((3,) + TAB)
    saved = []
    for l in range(DEPTH):
        sink_rows = jnp.repeat(attn_sink[l], BLOCK).reshape(N_KV_HEADS, GQA * BLOCK)
        x, sv, token = _layer_fwd(l, x, weights_of, pool_scale[l:l + 1], g_mix[l:l + 1], g_ffn[l:l + 1], bias_tabs,
                                  sink_rows)
        saved.append(sv)
    loss, dx, dg_final = _loss_bwd(x, g_final.reshape(1, D_MODEL), tgt)
    dbias = jnp.zeros(TAB, F32)
    small = [None] * DEPTH
    for l in reversed(range(DEPTH)):
        dx, small[l], dbias, token = _layer_bwd(
            l, dx, saved[l], grads_to, pool_scale[l:l + 1], g_mix[l:l + 1], g_ffn[l:l + 1], bias_tabs, dbias, token)
    dsink_rows = jnp.concatenate([small[l]["attn_sink"].reshape(N_HEADS, BLOCK) for l in range(DEPTH)], axis=0)
    d_rel_bias, d_sink = _bias_reduce(dbias.reshape(N_HEADS, TAB_FLAT), dsink_rows, onehot)
    cat = lambda k: jnp.concatenate([small[l][k] for l in range(DEPTH)], axis=0)
    smalls = dict(pool_scale=cat("pool_scale"), g_mix=cat("g_mix"), g_ffn=cat("g_ffn"),
                  attn_sink=d_sink.reshape(DEPTH, N_HEADS), rel_bias=d_rel_bias, g_final=dg_final)
    return loss[0, 0], dx, smalls


SHARD_AXIS = dict(w_in=(1, IN_TOTAL // N_CHIPS), conv_w=(1, D_MODEL // N_CHIPS), w_a_out=(0, D_MODEL // N_CHIPS),
                  w_pool=(1, POOL_CG // N_CHIPS), w_attn_out=(0, D_MODEL // N_CHIPS), w_o=(0, D_MODEL // N_CHIPS),
                  w_gu=(1, 2 * D_FF // N_CHIPS), w_down=(0, D_FF // N_CHIPS))
HBM = pl.BlockSpec(memory_space=pltpu.HBM)
SEM = pl.BlockSpec(memory_space=pltpu.SEMAPHORE)
DATAFLOW = pltpu.SideEffectType.DATAFLOW_SIDE_EFFECTING
TOKEN = jax.ShapeDtypeStruct((8, 128), F32)


def _shard_of(ref, name, chip):
    axis, n = SHARD_AXIS[name]
    idx = [slice(None)] * len(ref.shape)
    idx[axis] = pl.ds(chip * n, n)
    return ref.at[tuple(idx)]


def _with_shard_axis(name, shape, size):
    axis, _ = SHARD_AXIS[name]
    s = list(shape)
    s[axis] = size
    return tuple(s)


HALF_AXIS = dict(w_in=0, conv_w=1, w_a_out=0, w_pool=1, w_attn_out=0, w_o=0, w_gu=0, w_down=0)


def _half_of_shard(ref, name, core):
    axis = HALF_AXIS[name]
    n = ref.shape[axis] // 2
    idx = [slice(None)] * len(ref.shape)
    idx[axis] = pl.ds(core * n, n)
    return ref.at[tuple(idx)]


def _half_in_full(ref, name, chip, core):
    saxis, n = SHARD_AXIS[name]
    haxis = HALF_AXIS[name]
    idx = [slice(None)] * len(ref.shape)
    if haxis == saxis:
        idx[saxis] = pl.ds(chip * n + core * (n // 2), n // 2)
    else:
        h = ref.shape[haxis] // 2
        idx[saxis] = pl.ds(chip * n, n)
        idx[haxis] = pl.ds(core * h, h)
    return ref.at[tuple(idx)]


def _on_each_device(fn):
    me = 2 * lax.axis_index("x") + lax.axis_index("y")
    c = lax.axis_index("c")
    for chip in range(N_CHIPS):
        for core in range(2):
            pl.when((me == chip) & (c == core))(functools.partial(fn, chip, core))


def _chip_peers(x, y):
    return [(1 - x, y), (x, 1 - y), (1 - x, 1 - y)]


RELATION_XOR = (2, 1, 3)


def _group_copies(kind, group, srcs, lands, send_sems, recv_sems, local_sems, chip, core):
    x, y, c = lax.axis_index("x"), lax.axis_index("y"), lax.axis_index("c")
    copies = []
    for t, name in enumerate(GROUPS[group]):
        for j, (px, py) in enumerate(_chip_peers(x, y)):
            if kind == "gather":
                src, dst = _half_of_shard(srcs[t], name, core), _half_in_full(lands[t], name, chip, core)
            else:
                src, dst = _shard_of(srcs[t], name, chip ^ RELATION_XOR[j]), lands[t].at[j]
            copies.append(pltpu.make_async_remote_copy(
                src_ref=src, dst_ref=dst, send_sem=send_sems.at[3 * t + j], recv_sem=recv_sems.at[3 * t + j],
                device_id=(px, py, c), device_id_type=MESH))
        if kind == "gather":
            src, dst = srcs[t], _shard_of(lands[t], name, chip)
        else:
            src, dst = _shard_of(srcs[t], name, chip), lands[t].at[N_CHIPS - 1]
        copies.append(pltpu.make_async_copy(src, dst, local_sems.at[t]))
    return copies


def _exchange_start(kind, group, srcs, land_shapes, after):
    nw = len(GROUPS[group])

    def body(*refs):
        srcs_r, lands_r = refs[:nw], refs[nw:2 * nw]
        send_sems, recv_sems, local_sems = refs[2 * nw + 1:2 * nw + 4]
        token = refs[-1]

        def issue(chip, core):
            for cp in _group_copies(kind, group, srcs_r, lands_r, send_sems, recv_sems, local_sems, chip, core):
                cp.start()
        _on_each_device(issue)
        token[...] = jnp.zeros_like(token)

    lands = [pltpu.with_memory_space_constraint(lax.empty(s.shape, s.dtype), pltpu.HBM) for s in land_shapes]
    srcs = [pltpu.with_memory_space_constraint(a, pltpu.HBM) for a in srcs]
    thru = [pltpu.HBM(a.shape, a.dtype) for a in srcs + lands]
    outs = pl.pallas_call(
        body, name=f"{kind}_{group}_start",
        in_specs=[HBM] * (2 * nw) + [ANY],
        out_specs=[SEM, SEM, SEM] + [HBM] * (2 * nw) + [pl.BlockSpec(memory_space=pltpu.VMEM)],
        out_shape=[pltpu.SemaphoreType.DMA((3 * nw,)), pltpu.SemaphoreType.DMA((3 * nw,)),
                   pltpu.SemaphoreType.DMA((nw,))] + thru + [TOKEN],
        input_output_aliases={t: 3 + t for t in range(2 * nw)},
        compiler_params=pltpu.CompilerParams(has_side_effects=DATAFLOW),
    )(*srcs, *lands, after)
    return dict(sems=outs[0:3], srcs=outs[3:3 + nw], lands=outs[3 + nw:3 + 2 * nw], token=outs[-1])


def _exchange_wait(kind, group, started, after):
    nw = len(GROUPS[group])

    def body(*refs):
        srcs_r, lands_r = refs[:nw], refs[nw:2 * nw]
        send_sems, recv_sems, local_sems = refs[2 * nw:2 * nw + 3]
        for cp in _group_copies(kind, group, srcs_r, lands_r, send_sems, recv_sems, local_sems, 0, 0):
            cp.wait()

    srcs, lands = list(started["srcs"]), list(started["lands"])
    outs = pl.pallas_call(
        body, name=f"{kind}_{group}_wait",
        in_specs=[HBM] * (2 * nw) + [SEM, SEM, SEM, ANY],
        out_specs=[HBM] * (2 * nw),
        out_shape=[pltpu.HBM(a.shape, a.dtype) for a in srcs + lands],
        input_output_aliases={t: t for t in range(2 * nw)},
        compiler_params=pltpu.CompilerParams(has_side_effects=DATAFLOW),
    )(*srcs, *lands, *started["sems"], after)
    return dict(zip(GROUPS[group], outs[nw:]))


def _gather_start(group, shards, after):
    names = GROUPS[group]
    shapes = [jax.ShapeDtypeStruct(_with_shard_axis(n, shards[n].shape, SHARD_AXIS[n][1] * N_CHIPS), shards[n].dtype)
              for n in names]
    return _exchange_start("gather", group, [shards[n] for n in names], shapes, after)


def _scatter_start(group, grads, after):
    names = GROUPS[group]
    shapes = [jax.ShapeDtypeStruct((N_CHIPS,) + _with_shard_axis(n, grads[n].shape, SHARD_AXIS[n][1]), grads[n].dtype)
              for n in names]
    return _exchange_start("scatter", group, [grads[n] for n in names], shapes, after)


def _sibling_exchange(parts):
    n = len(parts)

    def body(*refs):
        ins, outs = refs[:n], refs[n:2 * n]
        send_sems, recv_sems = refs[2 * n:]
        sibling = (lax.axis_index("x"), lax.axis_index("y"), 1 - lax.axis_index("c"))
        copies = [pltpu.make_async_remote_copy(src_ref=ins[t], dst_ref=outs[t], send_sem=send_sems.at[t],
                                               recv_sem=recv_sems.at[t], device_id=sibling, device_id_type=MESH)
                  for t in range(n)]
        for cp in copies:
            cp.start()
        for cp in copies:
            cp.wait()

    outs = pl.pallas_call(
        body, name="sibling_exchange", in_specs=[ANY] * n, out_specs=[ANY] * n,
        out_shape=[jax.ShapeDtypeStruct(p.shape, p.dtype) for p in parts],
        scratch_shapes=[pltpu.SemaphoreType.DMA((n,)), pltpu.SemaphoreType.DMA((n,))],
        compiler_params=pltpu.CompilerParams(has_side_effects=True),
    )(*parts)
    return list(outs)


def _sibling_fill(group, fulls):
    names = GROUPS[group]
    nw = len(names)

    def body(*refs):
        ins, outs = refs[:nw], refs[nw:2 * nw]
        send_sems, recv_sems = refs[2 * nw:]
        sibling = (lax.axis_index("x"), lax.axis_index("y"), 1 - lax.axis_index("c"))

        def forward(chip, core):
            copies = []
            for t, name in enumerate(names):
                for j in range(3):
                    other = chip ^ RELATION_XOR[j]
                    copies.append(pltpu.make_async_remote_copy(
                        src_ref=_half_in_full(ins[t], name, other, core),
                        dst_ref=_half_in_full(outs[t], name, other, core),
                        send_sem=send_sems.at[3 * t + j], recv_sem=recv_sems.at[3 * t + j],
                        device_id=sibling, device_id_type=MESH))
            for cp in copies:
                cp.start()
            for cp in copies:
                cp.wait()
        _on_each_device(forward)

    arrays = [fulls[n] for n in names]
    outs = pl.pallas_call(
        body, name=f"sibling_fill_{group}", in_specs=[ANY] * nw, out_specs=[ANY] * nw,
        out_shape=[jax.ShapeDtypeStruct(a.shape, a.dtype) for a in arrays],
        scratch_shapes=[pltpu.SemaphoreType.DMA((3 * nw,)), pltpu.SemaphoreType.DMA((3 * nw,))],
        input_output_aliases={t: t for t in range(nw)},
        compiler_params=pltpu.CompilerParams(has_side_effects=True),
    )(*arrays)
    return dict(zip(names, outs))


N_DEV = 8


def _all_reduce_small(v):
    R, C = v.shape

    def body(v_ref, o_ref, slots, send_sems, recv_sems):
        x, y, c = lax.axis_index("x"), lax.axis_index("y"), lax.axis_index("c")
        me = 4 * x + 2 * y + c
        slots[me] = v_ref[...]
        copies = []
        for k in range(1, N_DEV):
            peer = me ^ k
            cp = pltpu.make_async_remote_copy(
                src_ref=v_ref, dst_ref=slots.at[me], send_sem=send_sems.at[k - 1], recv_sem=recv_sems.at[k - 1],
                device_id=(peer // 4, (peer // 2) % 2, peer % 2), device_id_type=MESH)
            cp.start()
            copies.append(cp)
        for cp in copies:
            cp.wait()
        acc = slots[0]
        for k in range(1, N_DEV):
            acc = acc + slots[k]
        o_ref[...] = acc

    return pl.pallas_call(
        body, name="all_reduce_small", out_shape=jax.ShapeDtypeStruct((R, C), F32),
        in_specs=[pl.BlockSpec(memory_space=pltpu.VMEM)], out_specs=pl.BlockSpec(memory_space=pltpu.VMEM),
        scratch_shapes=[pltpu.VMEM((N_DEV, R, C), F32), pltpu.SemaphoreType.DMA((N_DEV - 1,)),
                        pltpu.SemaphoreType.DMA((N_DEV - 1,))],
        compiler_params=pltpu.CompilerParams(has_side_effects=True),
    )(v)


def _as2d(shape):
    return (int(np.prod(shape[:-1])), shape[-1])


def _row_block(rows, cols, n_arrays):
    budget = V7X_VMEM_LIMIT // 2
    tr = rows
    while tr % 16 == 0 and 2 * n_arrays * tr * cols * 4 > budget:
        tr //= 2
    return tr


def _sum_slots(slots):
    _, R, C = slots.shape
    tr = _row_block(R, C, 5)

    def body(s_ref, o_ref):
        acc = s_ref[0].astype(F32)
        for k in range(1, N_CHIPS):
            acc = acc + s_ref[k].astype(F32)
        o_ref[...] = acc.astype(BF16)

    return pl.pallas_call(
        body, name="sum_slots", grid=(R // tr,),
        in_specs=[pl.BlockSpec((N_CHIPS, tr, C), lambda i: (0, i, 0))],
        out_specs=pl.BlockSpec((tr, C), lambda i: (i, 0)),
        out_shape=jax.ShapeDtypeStruct((R, C), BF16),
        compiler_params=_params("parallel"),
    )(slots)


def _adamw(l, w, m, v, g_a, g_b, prev):
    L, R, C = w.shape
    tr = _row_block(R, C, 9)
    c1 = 1.0 - ADAM_B1 ** ADAM_STEP
    c2 = 1.0 - ADAM_B2 ** ADAM_STEP

    def body(w_ref, m_ref, v_ref, a_ref, b_ref, *rest):
        g_ref, d_ref, nm_ref, nv_ref = rest[-4:]
        g = a_ref[...].astype(F32) + b_ref[...].astype(F32)
        nm = ADAM_B1 * m_ref[...] + (1.0 - ADAM_B1) * g
        nv = ADAM_B2 * v_ref[...] + (1.0 - ADAM_B2) * (g * g)
        g_ref[...] = g
        nm_ref[...] = nm
        nv_ref[...] = nv
        d_ref[...] = -ADAM_LR * ((nm / c1) / (jnp.sqrt(nv / c2) + ADAM_EPS) + ADAM_WD * w_ref[...])

    layer = pl.BlockSpec((None, tr, C), lambda i: (l, i, 0))
    blk = pl.BlockSpec((tr, C), lambda i: (i, 0))
    out = jax.ShapeDtypeStruct((L, R, C), F32)
    prev = [] if prev is None else list(prev)
    return pl.pallas_call(
        body, name="adamw", grid=(R // tr,), in_specs=[layer] * 3 + [blk] * 2 + [ANY] * len(prev),
        out_specs=[layer] * 4, out_shape=[out] * 4,
        input_output_aliases={5 + k: k for k in range(len(prev))},
        compiler_params=_params("parallel"),
    )(w, m, v, g_a, g_b, *prev)


SMALL_ROWS = 16


def _pack_small(pool_scale, g_mix, g_ffn, g_final, attn_sink, rel_bias):
    tail = jnp.concatenate([attn_sink.reshape(-1), rel_bias.reshape(-1)])
    tail = jnp.pad(tail, (0, D_MODEL - tail.shape[0])).reshape(1, D_MODEL)
    rows = jnp.concatenate([pool_scale, g_mix, g_ffn, g_final.reshape(1, D_MODEL), tail], axis=0)
    return jnp.pad(rows, ((0, SMALL_ROWS - rows.shape[0]), (0, 0)))


def _unpack_small(packed):
    n_sink = DEPTH * N_HEADS
    return dict(pool_scale=packed[0:4], g_mix=packed[4:8], g_ffn=packed[8:12], g_final=packed[12],
                attn_sink=packed[13, 0:n_sink].reshape(DEPTH, N_HEADS),
                rel_bias=packed[13, n_sink:n_sink + N_BUCKETS * N_HEADS].reshape(N_BUCKETS, N_HEADS))


def _group_shards(l, group, masters):
    out = {}
    for n in GROUPS[group]:
        w = masters[n][l]
        out[n] = jnp.pad(w.reshape(3, -1), ((0, 5), (0, 0))) if n == "conv_w" else w.astype(BF16)
    return out


def kernel(x, w_in, conv_w, w_a_out, w_pool, pool_scale, w_attn_out, attn_sink, w_o, g_mix, g_ffn, w_gu, w_down, rel_bias, g_final, loss_target, m_w_in, m_conv_w, m_w_a_out, m_w_pool, m_pool_scale, m_w_attn_out, m_attn_sink, m_w_o, m_g_mix, m_g_ffn, m_w_gu, m_w_down, m_rel_bias, m_g_final, v_w_in, v_conv_w, v_w_a_out, v_w_pool, v_pool_scale, v_w_attn_out, v_attn_sink, v_w_o, v_g_mix, v_g_ffn, v_w_gu, v_w_down, v_rel_bias, v_g_final):
    big = dict(w_in=(w_in, m_w_in, v_w_in), conv_w=(conv_w, m_conv_w, v_conv_w), w_a_out=(w_a_out, m_w_a_out, v_w_a_out),
               w_pool=(w_pool, m_w_pool, v_w_pool), w_attn_out=(w_attn_out, m_w_attn_out, v_w_attn_out),
               w_o=(w_o, m_w_o, v_w_o), w_gu=(w_gu, m_w_gu, v_w_gu), w_down=(w_down, m_w_down, v_w_down))

    big3 = {n: tuple(a.reshape((DEPTH,) + _as2d(a.shape[1:])) for a in big[n]) for n in WEIGHT_NAMES}
    masters = {n: big[n][0] for n in WEIGHT_NAMES}

    gathers = {(0, "mix"): _gather_start("mix", _group_shards(0, "mix", masters), rel_bias)}
    newest = {"token": gathers[0, "mix"]["token"]}
    masters = dict(zip(WEIGHT_NAMES, lax.optimization_barrier(
        (tuple(masters[n] for n in WEIGHT_NAMES), newest["token"]))[0]))

    def weights_of(l, group, a):
        W = _sibling_fill(group, _exchange_wait("gather", group, gathers.pop((l, group)), a))
        if group == "mix":
            gathers[l, "ffn"] = _gather_start("ffn", _group_shards(l, "ffn", masters), W["w_in"])
            newest["token"] = gathers[l, "ffn"]["token"]
            if l + 1 < DEPTH:
                gathers[l + 1, "mix"] = _gather_start("mix", _group_shards(l + 1, "mix", masters), newest["token"])
                newest["token"] = gathers[l + 1, "mix"]["token"]
        return W, newest["token"]

    results = {n: None for n in WEIGHT_NAMES}
    scatters = {}

    def finish(l, group, after):
        slots = _exchange_wait("scatter", group, scatters.pop((l, group)), after)
        names = GROUPS[group]
        parts = [_sum_slots(slots[n].reshape((N_CHIPS,) + _as2d(slots[n].shape[1:]))) for n in names]
        others = _sibling_exchange(parts)
        for n, mine, other in zip(names, parts, others):
            if n == "conv_w":
                mine, other = mine[0:3], other[0:3]
            results[n] = _adamw(l, *big3[n], mine, other, results[n])

    def grads_to(l, group, wgrads, a):
        scatters[l, group] = _scatter_start(group, wgrads, a)
        token = scatters[l, group]["token"]
        if group == "mix" and l + 1 < DEPTH:
            finish(l + 1, "ffn", token)
            finish(l + 1, "mix", token)
        return token

    loss, grad_x, smalls = _local_step(x[0], loss_target[0], weights_of, grads_to, pool_scale, attn_sink, g_mix, g_ffn,
                                       rel_bias, g_final)
    finish(0, "ffn", grad_x)
    finish(0, "mix", results["w_down"][0])
    stacked = {n: [o.reshape(big[n][0].shape) for o in results[n]] for n in WEIGHT_NAMES}

    g_small = _all_reduce_small(_pack_small(smalls["pool_scale"], smalls["g_mix"], smalls["g_ffn"], smalls["g_final"],
                                            smalls["attn_sink"], smalls["rel_bias"]))
    w_small = _pack_small(pool_scale, g_mix, g_ffn, g_final, attn_sink, rel_bias)
    m_small = _pack_small(m_pool_scale, m_g_mix, m_g_ffn, m_g_final, m_attn_sink, m_rel_bias)
    v_small = _pack_small(v_pool_scale, v_g_mix, v_g_ffn, v_g_final, v_attn_sink, v_rel_bias)
    small_out = [_unpack_small(o[0]) for o in
                 _adamw(0, w_small[None], m_small[None], v_small[None], g_small, jnp.zeros_like(g_small), None)]

    total_loss = lax.psum(loss, ("x", "y", "c"))

    order = ("w_in", "conv_w", "w_a_out", "w_pool", "pool_scale", "w_attn_out", "attn_sink", "w_o", "g_mix", "g_ffn",
             "w_gu", "w_down", "rel_bias", "g_final")
    outs = [total_loss, grad_x[None]]
    for k in range(4):
        for n in order:
            outs.append(stacked[n][k] if n in stacked else small_out[k][n])
    return tuple(outs)
```

```python
import functools
import math

import numpy as np
import jax
import jax.numpy as jnp
from jax import lax
from jax.experimental import pallas as pl
from jax.experimental.pallas import tpu as pltpu

F32 = jnp.float32
BF16 = jnp.bfloat16

D_MODEL = 1024
DEPTH = 4
N_HEADS = 16
N_KV_HEADS = 4
HEAD_DIM = 64
GQA = N_HEADS // N_KV_HEADS
WINDOW = 128
BLOCK = 128
N_BUCKETS = 32
MAX_DISTANCE = 128
POOL_GROUPS = 4
POOL_CG = D_MODEL // POOL_GROUPS
POOL_WINDOWS = (2, 4, 8, 16)
D_FF = 2816
IN_TOTAL = 8704
OFF_B, OFF_C, OFF_X, OFF_U, OFF_Q, OFF_K, OFF_V, OFF_GA, OFF_GP, OFF_GT = (
    0, 1024, 2048, 3072, 4096, 5120, 5376, 5632, 6656, 7680)
EPS = 1e-6
NEG_INF = -1e30
SM_SCALE = HEAD_DIM ** -0.5

ADAM_LR = 0.001
ADAM_B1 = 0.9
ADAM_B2 = 0.999
ADAM_EPS = 1e-08
ADAM_WD = 0.01
ADAM_STEP = 10

N_CHIPS = 4
HALO = 8
V7X_VMEM_LIMIT = 56 * 1024 * 1024
MESH = pl.DeviceIdType.MESH
ANY = pl.BlockSpec(memory_space=pl.ANY)


def _params(*sem):
    return pltpu.CompilerParams(dimension_semantics=tuple(sem) if sem else None,
                                vmem_limit_bytes=V7X_VMEM_LIMIT)


def _tile(n, pref):
    t = min(pref, n)
    while n % t or t % 128:
        t -= 128
    return t


def _nt(a, b):
    return lax.dot_general(a, b, (((1,), (1,)), ((), ())), preferred_element_type=F32)


def _tn(a, b):
    return lax.dot_general(a, b, (((0,), (0,)), ((), ())), preferred_element_type=F32)


def _nn(a, b):
    return jnp.dot(a, b, preferred_element_type=F32)


def _sigmoid(v):
    return 1.0 / (1.0 + jnp.exp(-v))


def _norm_matmul(x, g, w, name, token):
    S, Dm = x.shape
    N = w.shape[1]
    tm, tn = _tile(S, 1024), _tile(N, N // 4)

    def body(x_ref, g_ref, w_ref, token_ref, h_ref, o_ref):
        @pl.when(pl.program_id(1) == 0)
        def _():
            xv = x_ref[...]
            r = lax.rsqrt(jnp.mean(xv * xv, axis=-1, keepdims=True) + EPS)
            h_ref[...] = (xv * r * g_ref[...]).astype(BF16)
        o_ref[...] = _nn(h_ref[...], w_ref[...]).astype(BF16)

    return pl.pallas_call(
        body, name=name, grid=(S // tm, N // tn),
        in_specs=[pl.BlockSpec((tm, Dm), lambda i, j: (i, 0)),
                  pl.BlockSpec((1, Dm), lambda i, j: (0, 0)),
                  pl.BlockSpec((Dm, tn), lambda i, j: (0, j)), ANY],
        out_specs=[pl.BlockSpec((tm, Dm), lambda i, j: (i, 0)),
                   pl.BlockSpec((tm, tn), lambda i, j: (i, j))],
        out_shape=[jax.ShapeDtypeStruct((S, Dm), BF16), jax.ShapeDtypeStruct((S, N), BF16)],
        compiler_params=_params("parallel", "arbitrary"),
    )(x, g, w, token)


CB = 128


def _fill_padded(pad_ref, v, S):
    z = jnp.zeros((HALO, v.shape[1]), F32)
    pad_ref[pl.ds(0, HALO), :] = z
    pad_ref[pl.ds(S + HALO, HALO), :] = z
    pad_ref[pl.ds(HALO, S), :] = v


def _shifted(pad_ref, off, S):
    return pad_ref[pl.ds(HALO + off, S), :]


def _conv_fwd(proj, cw8):
    S = proj.shape[0]
    nblk = D_MODEL // CB

    def body(b_ref, c_ref, x_ref, w_ref, o_ref, pad):
        u = c_ref[...].astype(F32) * x_ref[...].astype(F32)
        _fill_padded(pad, u, S)
        cv = w_ref[0:1, :] * _shifted(pad, -1, S) + w_ref[1:2, :] * u + w_ref[2:3, :] * _shifted(pad, 1, S)
        o_ref[...] = (b_ref[...].astype(F32) * cv).astype(BF16)

    col = lambda base: pl.BlockSpec((S, CB), lambda j: (0, base // CB + j))
    return pl.pallas_call(
        body, name="conv_fwd", grid=(nblk,),
        in_specs=[col(OFF_B), col(OFF_C), col(OFF_X), pl.BlockSpec((8, CB), lambda j: (0, j))],
        out_specs=pl.BlockSpec((S, CB), lambda j: (0, j)),
        out_shape=jax.ShapeDtypeStruct((S, D_MODEL), BF16),
        scratch_shapes=[pltpu.VMEM((S + 2 * HALO, CB), F32)],
        compiler_params=_params("parallel"),
    )(proj, proj, proj, cw8)


def _pool_count(S, lo, hi):
    t = lax.broadcasted_iota(jnp.int32, (S, CB), 0)
    return (jnp.minimum(t + hi, S - 1) - jnp.maximum(t - lo, 0) + 1).astype(F32)


def _pool_fwd(proj):
    S = proj.shape[0]
    nblk = D_MODEL // CB
    per_group = POOL_CG // CB

    def body(u_ref, o_ref, pad):
        u = u_ref[...].astype(F32)
        _fill_padded(pad, u, S)
        grp = pl.program_id(0) // per_group
        for gi, w in enumerate(POOL_WINDOWS):
            @pl.when(grp == gi)
            def _(w=w):
                lo, hi = w // 2, w - 1 - w // 2
                acc = _shifted(pad, -lo, S)
                for off in range(-lo + 1, hi + 1):
                    acc = acc + _shifted(pad, off, S)
                o_ref[...] = (acc / _pool_count(S, lo, hi) - u).astype(BF16)

    return pl.pallas_call(
        body, name="pool_fwd", grid=(nblk,),
        in_specs=[pl.BlockSpec((S, CB), lambda j: (0, OFF_U // CB + j))],
        out_specs=pl.BlockSpec((S, CB), lambda j: (0, j)),
        out_shape=jax.ShapeDtypeStruct((S, D_MODEL), BF16),
        scratch_shapes=[pltpu.VMEM((S + 2 * HALO, CB), F32)],
        compiler_params=_params("parallel"),
    )(proj)


def _attn_specs(S):
    nb = S // BLOCK
    kcol, vcol = OFF_K // (N_KV_HEADS * HEAD_DIM), OFF_V // (N_KV_HEADS * HEAD_DIM)
    kvw = N_KV_HEADS * HEAD_DIM
    prev = lambda i: jnp.maximum(i - 1, 0)
    nxt = lambda i: jnp.minimum(i + 1, nb - 1)
    return [
        pl.BlockSpec((BLOCK, D_MODEL), lambda i: (i, OFF_Q // D_MODEL)),
        pl.BlockSpec((BLOCK, kvw), lambda i: (prev(i), kcol)),
        pl.BlockSpec((BLOCK, kvw), lambda i: (i, kcol)),
        pl.BlockSpec((BLOCK, kvw), lambda i: (nxt(i), kcol)),
        pl.BlockSpec((BLOCK, kvw), lambda i: (prev(i), vcol)),
        pl.BlockSpec((BLOCK, kvw), lambda i: (i, vcol)),
        pl.BlockSpec((BLOCK, kvw), lambda i: (nxt(i), vcol)),
    ]


def _heads_rows(ref_or_val, hk):
    return jnp.concatenate(
        [ref_or_val[:, (GQA * hk + g) * HEAD_DIM:(GQA * hk + g + 1) * HEAD_DIM] for g in range(GQA)], axis=0)


def _kv_rows(p_ref, c_ref, n_ref, hk):
    sl = slice(hk * HEAD_DIM, (hk + 1) * HEAD_DIM)
    return jnp.concatenate([p_ref[:, sl], c_ref[:, sl], n_ref[:, sl]], axis=0)


def _bias_cols(bias_ref, hk):
    return jnp.concatenate([bias_ref[GQA * hk + g] for g in range(GQA)], axis=1)


def _softmax_keys_on_rows(q4s, kc, bias_blk, sink_row):
    s = _nt(kc, q4s) + bias_blk
    m = jnp.maximum(jnp.max(s, axis=0, keepdims=True), sink_row)
    p = jnp.exp(s - m)
    e_sink = jnp.exp(sink_row - m)
    inv = 1.0 / (jnp.sum(p, axis=0, keepdims=True) + e_sink)
    return p * inv, e_sink * inv


TAB = (N_HEADS, 3 * BLOCK, BLOCK)
TAB_FLAT = 3 * BLOCK * BLOCK


def _bias_spec(nb):
    return pl.BlockSpec((None,) + TAB, lambda i: (jnp.where(i == 0, 0, jnp.where(i == nb - 1, 2, 1)), 0, 0, 0))


def _attn_fwd(proj, bias_tabs, sink_rows):
    S = proj.shape[0]
    nb = S // BLOCK
    assert nb >= 2

    def body(q_ref, kp, kc_, kn, vp, vc_, vn, bias_ref, sink_ref, o_ref):
        outs = []
        for hk in range(N_KV_HEADS):
            q4s = _heads_rows(q_ref, hk) * SM_SCALE
            kc = _kv_rows(kp, kc_, kn, hk)
            vc = _kv_rows(vp, vc_, vn, hk)
            pn, _ = _softmax_keys_on_rows(q4s, kc, _bias_cols(bias_ref, hk), sink_ref[hk:hk + 1, :])
            o4 = _tn(pn.astype(BF16), vc)
            outs += [o4[g * BLOCK:(g + 1) * BLOCK, :] for g in range(GQA)]
        o_ref[...] = jnp.concatenate(outs, axis=1).astype(BF16)

    return pl.pallas_call(
        body, name="attn_fwd", grid=(nb,),
        in_specs=_attn_specs(S) + [_bias_spec(nb), pl.BlockSpec((N_KV_HEADS, GQA * BLOCK), lambda i: (0, 0))],
        out_specs=pl.BlockSpec((BLOCK, D_MODEL), lambda i: (i, 0)),
        out_shape=jax.ShapeDtypeStruct((S, D_MODEL), BF16),
        compiler_params=_params("parallel"),
    )(*([proj] * 7), bias_tabs, sink_rows)


GATE_HALF = D_MODEL // 2


def _gate_specs(tm):
    return [pl.BlockSpec((tm, GATE_HALF), lambda i, c=off // GATE_HALF + k: (i, c))
            for off in (OFF_GA, OFF_GP, OFF_GT) for k in (0, 1)]


def _gate(lo_ref, hi_ref):
    return _sigmoid(jnp.concatenate([lo_ref[...], hi_ref[...]], axis=1).astype(F32))


def _pool_mix(p, wp):
    return jnp.concatenate(
        [_nn(p[:, g * POOL_CG:(g + 1) * POOL_CG], wp[g]) for g in range(POOL_GROUPS)], axis=1)


def _mix_fwd(za, p, att, proj, x, wa, wp, ps, wt, wo):
    S = x.shape[0]
    tm = _tile(S, 256)

    def body(za_ref, p_ref, att_ref, ga0, ga1, gp0, gp1, gt0, gt1, x_ref, wa_ref, wp_ref, ps_ref, wt_ref, wo_ref,
             ya_ref, yp_ref, yt_ref, mg_ref, x2_ref):
        ya = _nn(za_ref[...], wa_ref[...])
        ypr = _pool_mix(p_ref[...], wp_ref)
        yt = _nn(att_ref[...], wt_ref[...])
        merged = _gate(ga0, ga1) * ya + _gate(gp0, gp1) * (ypr * ps_ref[...]) + _gate(gt0, gt1) * yt
        mb = merged.astype(BF16)
        ya_ref[...] = ya.astype(BF16)
        yp_ref[...] = ypr.astype(BF16)
        yt_ref[...] = yt.astype(BF16)
        mg_ref[...] = mb
        x2_ref[...] = x_ref[...] + _nn(mb, wo_ref[...])

    row = lambda c=0: pl.BlockSpec((tm, D_MODEL), lambda i: (i, c))
    whole = lambda a: pl.BlockSpec(a.shape, lambda i: (0,) * a.ndim)
    act = jax.ShapeDtypeStruct((S, D_MODEL), BF16)
    return pl.pallas_call(
        body, name="mix_fwd", grid=(S // tm,),
        in_specs=[row(), row(), row()] + _gate_specs(tm) + [row(), whole(wa), whole(wp), whole(ps), whole(wt), whole(wo)],
        out_specs=[row(), row(), row(), row(), row()],
        out_shape=[act, act, act, act, jax.ShapeDtypeStruct((S, D_MODEL), F32)],
        compiler_params=_params("parallel"),
    )(za, p, att, *([proj] * 6), x, wa, wp, ps, wt, wo)


def _ffn_fwd(gu, x2, wd):
    S = x2.shape[0]
    tm = _tile(S, 256)

    def body(g_ref, u_ref, x_ref, w_ref, a_ref, o_ref):
        g = g_ref[...].astype(F32)
        a = (g * _sigmoid(g) * u_ref[...].astype(F32)).astype(BF16)
        a_ref[...] = a
        o_ref[...] = x_ref[...] + _nn(a, w_ref[...])

    return pl.pallas_call(
        body, name="ffn_fwd", grid=(S // tm,),
        in_specs=[pl.BlockSpec((tm, D_FF), lambda i: (i, 0)), pl.BlockSpec((tm, D_FF), lambda i: (i, 1)),
                  pl.BlockSpec((tm, D_MODEL), lambda i: (i, 0)), pl.BlockSpec((D_FF, D_MODEL), lambda i: (0, 0))],
        out_specs=[pl.BlockSpec((tm, D_FF), lambda i: (i, 0)), pl.BlockSpec((tm, D_MODEL), lambda i: (i, 0))],
        out_shape=[jax.ShapeDtypeStruct((S, D_FF), BF16), jax.ShapeDtypeStruct((S, D_MODEL), F32)],
        compiler_params=_params("parallel"),
    )(gu, gu, x2, wd)


def _loss_bwd(x, g, tgt):
    S, Dm = x.shape
    tm = _tile(S, 512)

    def body(x_ref, g_ref, t_ref, l_ref, dx_ref, dg_ref):
        @pl.when(pl.program_id(0) == 0)
        def _():
            l_ref[...] = jnp.zeros_like(l_ref)
            dg_ref[...] = jnp.zeros_like(dg_ref)
        xv, gv = x_ref[...], g_ref[...]
        r = lax.rsqrt(jnp.mean(xv * xv, axis=-1, keepdims=True) + EPS)
        n = xv * r
        err = n * gv - t_ref[...]
        l_ref[...] += 0.5 * jnp.sum(jnp.mean(err * err, axis=-1, keepdims=True), axis=0, keepdims=True)
        dy = err * (1.0 / Dm)
        dn = dy * gv
        dx_ref[...] = r * (dn - n * jnp.mean(dn * n, axis=-1, keepdims=True))
        dg_ref[...] += jnp.sum(dy * n, axis=0, keepdims=True)

    return pl.pallas_call(
        body, name="loss_bwd", grid=(S // tm,),
        in_specs=[pl.BlockSpec((tm, Dm), lambda i: (i, 0)), pl.BlockSpec((1, Dm), lambda i: (0, 0)),
                  pl.BlockSpec((tm, Dm), lambda i: (i, 0))],
        out_specs=[pl.BlockSpec((8, 128), lambda i: (0, 0)), pl.BlockSpec((tm, Dm), lambda i: (i, 0)),
                   pl.BlockSpec((1, Dm), lambda i: (0, 0))],
        out_shape=[jax.ShapeDtypeStruct((8, 128), F32), jax.ShapeDtypeStruct((S, Dm), F32),
                   jax.ShapeDtypeStruct((1, Dm), F32)],
        compiler_params=_params("arbitrary"),
    )(x, g, tgt)


def _ffn_bwd(dx3, gu, wd, token):
    S = dx3.shape[0]
    tm = _tile(S, 256)

    def body(d_ref, g_ref, u_ref, w_ref, token_ref, o_ref):
        dact = _nt(d_ref[...].astype(BF16), w_ref[...])
        g, u = g_ref[...].astype(F32), u_ref[...].astype(F32)
        sg = _sigmoid(g)
        o_ref[:, 0:D_FF] = (dact * u * (sg * (1.0 + g * (1.0 - sg)))).astype(BF16)
        o_ref[:, D_FF:2 * D_FF] = (dact * (g * sg)).astype(BF16)

    return pl.pallas_call(
        body, name="ffn_bwd", grid=(S // tm,),
        in_specs=[pl.BlockSpec((tm, D_MODEL), lambda i: (i, 0)),
                  pl.BlockSpec((tm, D_FF), lambda i: (i, 0)), pl.BlockSpec((tm, D_FF), lambda i: (i, 1)),
                  pl.BlockSpec((D_FF, D_MODEL), lambda i: (0, 0)), ANY],
        out_specs=pl.BlockSpec((tm, 2 * D_FF), lambda i: (i, 0)),
        out_shape=jax.ShapeDtypeStruct((S, 2 * D_FF), BF16),
        compiler_params=_params("parallel"),
    )(dx3, gu, gu, wd, token)


def _wgrad(a, b, name, tk=512, tn=512, out_dtype=BF16, token=None):
    S, K = a.shape
    N = b.shape[1]
    tk, tn, ts = _tile(K, tk), _tile(N, tn), _tile(S, 1024)
    n_s = S // ts
    extra = [] if token is None else [token]

    def body(a_ref, b_ref, *rest):
        o_ref, acc = rest[-2:]
        s = pl.program_id(2)

        @pl.when(s == 0)
        def _():
            acc[...] = jnp.zeros_like(acc)
        acc[...] += _tn(a_ref[...].astype(BF16), b_ref[...].astype(BF16))

        @pl.when(s == n_s - 1)
        def _():
            o_ref[...] = acc[...].astype(out_dtype)

    return pl.pallas_call(
        body, name=name, grid=(K // tk, N // tn, n_s),
        in_specs=[pl.BlockSpec((ts, tk), lambda k, n, s: (s, k)), pl.BlockSpec((ts, tn), lambda k, n, s: (s, n))]
        + [ANY] * len(extra),
        out_specs=pl.BlockSpec((tk, tn), lambda k, n, s: (k, n)),
        out_shape=jax.ShapeDtypeStruct((K, N), out_dtype),
        scratch_shapes=[pltpu.VMEM((tk, tn), F32)],
        compiler_params=_params("parallel", "parallel", "arbitrary"),
    )(a, b, *extra)


def _wgrad_pool(p, dyps):
    S = p.shape[0]
    ts = _tile(S, 512)
    n_s = S // ts

    def body(a_ref, b_ref, o_ref, acc):
        s = pl.program_id(1)

        @pl.when(s == 0)
        def _():
            acc[...] = jnp.zeros_like(acc)
        acc[...] += _tn(a_ref[...], b_ref[...])

        @pl.when(s == n_s - 1)
        def _():
            o_ref[...] = acc[...].astype(BF16)

    return pl.pallas_call(
        body, name="wgrad_pool", grid=(POOL_GROUPS, n_s),
        in_specs=[pl.BlockSpec((ts, POOL_CG), lambda g, s: (s, g)), pl.BlockSpec((ts, POOL_CG), lambda g, s: (s, g))],
        out_specs=pl.BlockSpec((None, POOL_CG, POOL_CG), lambda g, s: (g, 0, 0)),
        out_shape=jax.ShapeDtypeStruct((POOL_GROUPS, POOL_CG, POOL_CG), BF16),
        scratch_shapes=[pltpu.VMEM((POOL_CG, POOL_CG), F32)],
        compiler_params=_params("parallel", "arbitrary"),
    )(p, dyps)


def _dgrad_norm_bwd(dy, w, x, g, dres, name, tk):
    S, K = dy.shape
    Dm = x.shape[1]
    tm, tk = _tile(S, 1024), _tile(K, tk)
    n_k = K // tk

    def body(dy_ref, w_ref, x_ref, g_ref, r_ref, dx_ref, dg_ref, acc):
        i, k = pl.program_id(0), pl.program_id(1)

        @pl.when((i == 0) & (k == 0))
        def _():
            dg_ref[...] = jnp.zeros_like(dg_ref)

        @pl.when(k == 0)
        def _():
            acc[...] = jnp.zeros_like(acc)
        acc[...] += _nt(dy_ref[...], w_ref[...])

        @pl.when(k == n_k - 1)
        def _():
            dh, xv = acc[...], x_ref[...]
            r = lax.rsqrt(jnp.mean(xv * xv, axis=-1, keepdims=True) + EPS)
            n = xv * r
            dn = dh * g_ref[...]
            dx_ref[...] = r_ref[...] + r * (dn - n * jnp.mean(dn * n, axis=-1, keepdims=True))
            dg_ref[...] += jnp.sum(dh * n, axis=0, keepdims=True)

    rowblk = pl.BlockSpec((tm, Dm), lambda i, k: (i, 0))
    vec = pl.BlockSpec((1, Dm), lambda i, k: (0, 0))
    return pl.pallas_call(
        body, name=name, grid=(S // tm, n_k),
        in_specs=[pl.BlockSpec((tm, tk), lambda i, k: (i, k)), pl.BlockSpec((Dm, tk), lambda i, k: (0, k)),
                  rowblk, vec, rowblk],
        out_specs=[rowblk, vec],
        out_shape=[jax.ShapeDtypeStruct((S, Dm), F32), jax.ShapeDtypeStruct((1, Dm), F32)],
        scratch_shapes=[pltpu.VMEM((tm, Dm), F32)],
        compiler_params=_params("arbitrary", "arbitrary"),
    )(dy, w, x, g, dres)


def _mix_bwd(dx2, ya, ypr, yt, proj, ps, wa, wp, wt, wo, token):
    S = dx2.shape[0]
    tm = _tile(S, 256)

    n_half = 3 * D_MODEL // GATE_HALF

    def body(dx_ref, ya_ref, yp_ref, yt_ref, ga0, ga1, gp0, gp1, gt0, gt1, ps_ref, wa_ref, wp_ref, wt_ref, wo_ref,
             token_ref, dya_ref, dyt_ref, dyps_ref, dza_ref, datt_ref, dp_ref, dproj_ref, dps_ref, dgates):
        i, j = pl.program_id(0), pl.program_id(1)

        @pl.when((i == 0) & (j == 0))
        def _():
            dps_ref[...] = jnp.zeros_like(dps_ref)

        @pl.when(j == 0)
        def _():
            dm = _nt(dx_ref[...].astype(BF16), wo_ref[...])
            sa, sp, st = _gate(ga0, ga1), _gate(gp0, gp1), _gate(gt0, gt1)
            psv = ps_ref[...]
            ypr_v = yp_ref[...].astype(F32)
            dya = (sa * dm).astype(BF16)
            dyt = (st * dm).astype(BF16)
            dyp = sp * dm
            dyps = (dyp * psv).astype(BF16)
            dya_ref[...] = dya
            dyt_ref[...] = dyt
            dyps_ref[...] = dyps
            dg = [dm * ya_ref[...].astype(F32) * (sa * (1.0 - sa)), dm * (ypr_v * psv) * (sp * (1.0 - sp)),
                  dm * yt_ref[...].astype(F32) * (st * (1.0 - st))]
            for k in range(n_half):
                dgates[k] = dg[k // 2][:, (k % 2) * GATE_HALF:(k % 2 + 1) * GATE_HALF].astype(BF16)
            dps_ref[...] += jnp.sum(dyp * ypr_v, axis=0, keepdims=True)
            dza_ref[...] = _nt(dya, wa_ref[...]).astype(BF16)
            datt_ref[...] = _nt(dyt, wt_ref[...]).astype(BF16)
            dp_ref[...] = jnp.concatenate(
                [_nt(dyps[:, g * POOL_CG:(g + 1) * POOL_CG], wp_ref[g]) for g in range(POOL_GROUPS)],
                axis=1).astype(BF16)
        dproj_ref[...] = dgates[j]

    row = lambda c=0: pl.BlockSpec((tm, D_MODEL), lambda i, j: (i, c))
    whole = lambda a: pl.BlockSpec(a.shape, lambda i, j: (0,) * a.ndim)
    gates = [pl.BlockSpec((tm, GATE_HALF), lambda i, j, c=off // GATE_HALF + k: (i, c))
             for off in (OFF_GA, OFF_GP, OFF_GT) for k in (0, 1)]
    act = jax.ShapeDtypeStruct((S, D_MODEL), BF16)
    return pl.pallas_call(
        body, name="mix_bwd", grid=(S // tm, n_half),
        in_specs=[row(), row(), row(), row()] + gates + [whole(ps), whole(wa), whole(wp), whole(wt), whole(wo), ANY],
        out_specs=[row()] * 6 + [pl.BlockSpec((tm, GATE_HALF), lambda i, j: (i, OFF_GA // GATE_HALF + j)),
                                 pl.BlockSpec((1, D_MODEL), lambda i, j: (0, 0))],
        out_shape=[act] * 6 + [jax.ShapeDtypeStruct((S, IN_TOTAL), BF16), jax.ShapeDtypeStruct((1, D_MODEL), F32)],
        scratch_shapes=[pltpu.VMEM((n_half, tm, GATE_HALF), BF16)],
        compiler_params=_params("arbitrary", "arbitrary"),
    )(dx2, ya, ypr, yt, *([proj] * 6), ps, wa, wp, wt, wo, token)


def _conv_bwd(dza, proj, cw8, dproj):
    S = proj.shape[0]
    nblk = D_MODEL // CB

    def body(d_ref, b_ref, c_ref, x_ref, w_ref, dproj_in, dproj_ref, dw_ref, pad_u, pad_d, parts):
        j = pl.program_id(1)

        @pl.when(j == 0)
        def _():
            c, xa = c_ref[...].astype(F32), x_ref[...].astype(F32)
            u = c * xa
            _fill_padded(pad_u, u, S)
            u_prev, u_next = _shifted(pad_u, -1, S), _shifted(pad_u, 1, S)
            cv = w_ref[0:1, :] * u_prev + w_ref[1:2, :] * u + w_ref[2:3, :] * u_next
            dza_v = d_ref[...].astype(F32)
            parts[0] = (dza_v * cv).astype(BF16)
            dcv = dza_v * b_ref[...].astype(F32)
            _fill_padded(pad_d, dcv, S)
            du = (w_ref[0:1, :] * _shifted(pad_d, 1, S) + w_ref[1:2, :] * dcv
                  + w_ref[2:3, :] * _shifted(pad_d, -1, S))
            parts[1] = (du * xa).astype(BF16)
            parts[2] = (du * c).astype(BF16)
            dw_ref[...] = jnp.concatenate(
                [jnp.sum(dcv * u_prev, axis=0, keepdims=True), jnp.sum(dcv * u, axis=0, keepdims=True),
                 jnp.sum(dcv * u_next, axis=0, keepdims=True), jnp.zeros((5, CB), F32)], axis=0)
        dproj_ref[...] = parts[j]

    col = lambda base: pl.BlockSpec((S, CB), lambda cb, j: (0, base // CB + cb))
    taps = pl.BlockSpec((8, CB), lambda cb, j: (0, cb))
    return pl.pallas_call(
        body, name="conv_bwd", grid=(nblk, 3),
        in_specs=[col(0), col(OFF_B), col(OFF_C), col(OFF_X), taps, ANY],
        out_specs=[pl.BlockSpec((S, CB), lambda cb, j: (0, j * nblk + cb)), taps],
        out_shape=[jax.ShapeDtypeStruct(dproj.shape, dproj.dtype), jax.ShapeDtypeStruct((8, D_MODEL), F32)],
        scratch_shapes=[pltpu.VMEM((S + 2 * HALO, CB), F32), pltpu.VMEM((S + 2 * HALO, CB), F32),
                        pltpu.VMEM((3, S, CB), BF16)],
        input_output_aliases={5: 0},
        compiler_params=_params("arbitrary", "arbitrary"),
    )(dza, proj, proj, proj, cw8, dproj)


def _pool_bwd(dp, dproj):
    S = dp.shape[0]
    nblk = D_MODEL // CB
    per_group = POOL_CG // CB

    def body(d_ref, dproj_in, o_ref, pad):
        d = d_ref[...].astype(F32)
        grp = pl.program_id(0) // per_group
        for gi, w in enumerate(POOL_WINDOWS):
            @pl.when(grp == gi)
            def _(w=w):
                lo, hi = w // 2, w - 1 - w // 2
                _fill_padded(pad, d / _pool_count(S, lo, hi), S)
                acc = _shifted(pad, -hi, S)
                for off in range(-hi + 1, lo + 1):
                    acc = acc + _shifted(pad, off, S)
                o_ref[...] = (acc - d).astype(BF16)

    return pl.pallas_call(
        body, name="pool_bwd", grid=(nblk,),
        in_specs=[pl.BlockSpec((S, CB), lambda j: (0, j)), ANY],
        out_specs=pl.BlockSpec((S, CB), lambda j: (0, OFF_U // CB + j)),
        out_shape=jax.ShapeDtypeStruct(dproj.shape, dproj.dtype),
        scratch_shapes=[pltpu.VMEM((S + 2 * HALO, CB), F32)],
        input_output_aliases={1: 0},
        compiler_params=_params("parallel"),
    )(dp, dproj)


def _attn_bwd(proj, datt, bias_tabs, sink_rows, dbias_in, dproj):
    S = proj.shape[0]
    nb = S // BLOCK
    kvw = N_KV_HEADS * HEAD_DIM

    def body(q_ref, kp, kc_, kn, vp, vc_, vn, do_ref, bias_ref, sink_ref, dbin_ref, dproj_in,
             dq_ref, dk_ref, dv_ref, db_ref, ds_ref):
        i = pl.program_id(0)

        @pl.when(i == 0)
        def _():
            dk_ref[...] = jnp.zeros_like(dk_ref)
            dv_ref[...] = jnp.zeros_like(dv_ref)
            db_ref[...] = dbin_ref[...]
            ds_ref[...] = jnp.zeros_like(ds_ref)
        dqs, dks, dvs = [], [], []
        for hk in range(N_KV_HEADS):
            q4s = _heads_rows(q_ref, hk) * SM_SCALE
            do4 = _heads_rows(do_ref, hk)
            kc = _kv_rows(kp, kc_, kn, hk)
            vc = _kv_rows(vp, vc_, vn, hk)
            pn, p_sink = _softmax_keys_on_rows(q4s, kc, _bias_cols(bias_ref, hk), sink_ref[hk:hk + 1, :])
            dpm = _nt(vc, do4)
            delta = jnp.sum(pn * dpm, axis=0, keepdims=True)
            dsc = pn * (dpm - delta)
            for g in range(GQA):
                db_ref[GQA * hk + g] += dsc[:, g * BLOCK:(g + 1) * BLOCK]
            ds_ref[hk:hk + 1, :] += -p_sink * delta
            dsb = dsc.astype(BF16)
            dq4 = _tn(dsb, kc) * SM_SCALE
            dqs += [dq4[g * BLOCK:(g + 1) * BLOCK, :] for g in range(GQA)]
            dks.append(_nn(dsb, q4s))
            dvs.append(_nn(pn.astype(BF16), do4))
        dq_ref[...] = jnp.concatenate(dqs, axis=1).astype(BF16)
        r0 = pl.multiple_of(i * BLOCK, BLOCK)
        dk_ref[pl.ds(r0, 3 * BLOCK), :] += jnp.concatenate(dks, axis=1)
        dv_ref[pl.ds(r0, 3 * BLOCK), :] += jnp.concatenate(dvs, axis=1)

    const = lambda shape: pl.BlockSpec(shape, lambda i: (0,) * len(shape))
    sink_shape = (N_KV_HEADS, GQA * BLOCK)
    return pl.pallas_call(
        body, name="attn_bwd", grid=(nb,),
        in_specs=_attn_specs(S) + [pl.BlockSpec((BLOCK, D_MODEL), lambda i: (i, 0)),
                                   _bias_spec(nb), const(sink_shape), const(TAB), ANY],
        out_specs=[pl.BlockSpec((BLOCK, D_MODEL), lambda i: (i, OFF_Q // D_MODEL)),
                   const((S + 2 * BLOCK, kvw)), const((S + 2 * BLOCK, kvw)), const(TAB), const(sink_shape)],
        out_shape=[jax.ShapeDtypeStruct(dproj.shape, dproj.dtype),
                   jax.ShapeDtypeStruct((S + 2 * BLOCK, kvw), F32), jax.ShapeDtypeStruct((S + 2 * BLOCK, kvw), F32),
                   jax.ShapeDtypeStruct(TAB, F32), jax.ShapeDtypeStruct(sink_shape, F32)],
        input_output_aliases={11: 0},
        compiler_params=_params("arbitrary"),
    )(*([proj] * 7), datt, bias_tabs, sink_rows, dbias_in, dproj)


def _kv_finish(dkp, dvp, dproj):
    S = dproj.shape[0]
    kvw = N_KV_HEADS * HEAD_DIM

    def body(dk_ref, dv_ref, dproj_in, o_ref):
        o_ref[:, 0:kvw] = dk_ref[pl.ds(BLOCK, S), :].astype(BF16)
        o_ref[:, kvw:2 * kvw] = dv_ref[pl.ds(BLOCK, S), :].astype(BF16)

    whole = pl.BlockSpec((S + 2 * BLOCK, kvw), lambda i: (0, 0))
    return pl.pallas_call(
        body, name="kv_finish", grid=(1,), in_specs=[whole, whole, ANY],
        out_specs=pl.BlockSpec((S, 2 * kvw), lambda i: (0, OFF_K // (2 * kvw))),
        out_shape=jax.ShapeDtypeStruct(dproj.shape, dproj.dtype),
        input_output_aliases={2: 0},
        compiler_params=_params("arbitrary"),
    )(dkp, dvp, dproj)


def _bucket_constants():
    half = N_BUCKETS // 2
    max_exact = half // 2
    qi = np.arange(BLOCK)[None, :]
    kj = np.arange(3 * BLOCK)[:, None]
    rel = kj - BLOCK - qi
    n = np.abs(rel)
    nf = np.maximum(n, 1).astype(np.float32)
    large = max_exact + (np.log(nf / np.float32(max_exact)) / np.float32(math.log(MAX_DISTANCE / max_exact))
                         * np.float32(half - max_exact)).astype(np.int32)
    large = np.minimum(large, half - 1)
    bucket = np.where(rel > 0, half, 0) + np.where(n < max_exact, n, large)
    onehot = (bucket.reshape(1, -1) == np.arange(N_BUCKETS)[:, None]).astype(np.float32)
    window = n <= WINDOW
    first = window & (kj >= BLOCK)
    last = window & (kj < 2 * BLOCK)
    masks = np.stack([np.where(v, 0.0, NEG_INF).astype(np.float32).reshape(-1) for v in (first, window, last)])
    return onehot, masks


def _bias_expand(rel_bias_t, onehot, masks):
    def body(r_ref, oh_ref, m_ref, o_ref):
        tab = jnp.dot(r_ref[...], oh_ref[...], preferred_element_type=F32, precision=lax.Precision.HIGHEST)
        for v in range(3):
            o_ref[v] = tab + m_ref[v:v + 1, :]

    return pl.pallas_call(
        body, name="bias_expand", out_shape=jax.ShapeDtypeStruct((3, N_HEADS, onehot.shape[1]), F32),
        compiler_params=_params(),
    )(rel_bias_t, onehot, masks)


def _bias_reduce(dtab, dsink_rows, onehot):
    def body(d_ref, s_ref, oh_ref, o_ref, so_ref):
        o_ref[...] = lax.dot_general(oh_ref[...], d_ref[...], (((1,), (1,)), ((), ())),
                                     preferred_element_type=F32, precision=lax.Precision.HIGHEST)
        so_ref[...] = jnp.sum(s_ref[...], axis=-1, keepdims=True)

    return pl.pallas_call(
        body, name="bias_reduce",
        out_shape=[jax.ShapeDtypeStruct((N_BUCKETS, N_HEADS), F32),
                   jax.ShapeDtypeStruct((dsink_rows.shape[0], 1), F32)],
        compiler_params=_params(),
    )(dtab, dsink_rows, onehot)


GROUPS = dict(mix=("w_in", "conv_w", "w_a_out", "w_pool", "w_attn_out", "w_o"), ffn=("w_gu", "w_down"))
WEIGHT_NAMES = GROUPS["mix"] + GROUPS["ffn"]


def _layer_fwd(l, x, weights_of, ps, g_mix, g_ffn, bias_tabs, sink_rows):
    W, token = weights_of(l, "mix", x)
    h, proj = _norm_matmul(x, g_mix, W["w_in"], "norm_proj", token)
    za = _conv_fwd(proj, W["conv_w"])
    p = _pool_fwd(proj)
    att = _attn_fwd(proj, bias_tabs, sink_rows)
    ya, ypr, yt, merged, x2 = _mix_fwd(za, p, att, proj, x, W["w_a_out"], W["w_pool"], ps, W["w_attn_out"], W["w_o"])
    Wf, token = weights_of(l, "ffn", x2)
    h2, gu = _norm_matmul(x2, g_ffn, Wf["w_gu"], "norm_gu", token)
    act, x3 = _ffn_fwd(gu, x2, Wf["w_down"])
    saved = dict(x=x, h=h, proj=proj, za=za, p=p, att=att, ya=ya, ypr=ypr, yt=yt, merged=merged, x2=x2, h2=h2,
                 gu=gu, act=act, W={**W, **Wf}, sink_rows=sink_rows)
    return x3, saved, token


def _layer_bwd(l, dx3, sv, grads_to, ps, g_mix, g_ffn, bias_tabs, dbias, token):
    W, sink_rows = sv["W"], sv["sink_rows"]
    dgu = _ffn_bwd(dx3, sv["gu"], W["w_down"], token)
    g_w_down = _wgrad(sv["act"], dx3, "wgrad_down", tk=1408, tn=1024, token=token)
    g_w_gu = _wgrad(sv["h2"], dgu, "wgrad_gu", tk=1024, tn=1408)
    dx2, dg_ffn = _dgrad_norm_bwd(dgu, W["w_gu"], sv["x2"], g_ffn, dx3, "dgrad_gu", tk=1408)
    token = grads_to(l, "ffn", dict(w_gu=g_w_gu, w_down=g_w_down), dx2)
    dya, dyt, dyps, dza, datt, dp, dproj, dps = _mix_bwd(
        dx2, sv["ya"], sv["ypr"], sv["yt"], sv["proj"], ps, W["w_a_out"], W["w_pool"], W["w_attn_out"], W["w_o"], token)
    g_w_o = _wgrad(sv["merged"], dx2, "wgrad_sq_f32", tk=1024, tn=1024)
    g_w_a_out = _wgrad(sv["za"], dya, "wgrad_sq", tk=1024, tn=1024)
    g_w_attn_out = _wgrad(sv["att"], dyt, "wgrad_sq", tk=1024, tn=1024)
    g_w_pool = _wgrad_pool(sv["p"], dyps)
    dproj, g_conv = _conv_bwd(dza, sv["proj"], W["conv_w"], dproj)
    dproj = _pool_bwd(dp, dproj)
    dproj, dkp, dvp, dbias, dsink = _attn_bwd(sv["proj"], datt, bias_tabs, sink_rows, dbias, dproj)
    dproj = _kv_finish(dkp, dvp, dproj)
    g_w_in = _wgrad(sv["h"], dproj, "wgrad_in", tk=1024, tn=2176)
    dx, dg_mix = _dgrad_norm_bwd(dproj, W["w_in"], sv["x"], g_mix, dx2, "dgrad_in", tk=2176)
    token = grads_to(l, "mix", dict(w_in=g_w_in, conv_w=g_conv, w_a_out=g_w_a_out, w_pool=g_w_pool,
                                    w_attn_out=g_w_attn_out, w_o=g_w_o), dx)
    return dx, dict(pool_scale=dps, g_mix=dg_mix, g_ffn=dg_ffn, attn_sink=dsink), dbias, token


def _local_step(x, tgt, weights_of, grads_to, pool_scale, attn_sink, g_mix, g_ffn, rel_bias, g_final):
    onehot_np, masks_np = _bucket_constants()
    onehot, masks = jnp.asarray(onehot_np), jnp.asarray(masks_np)
    bias_tabs = _bias_expand(rel_bias.T, onehot, masks).reshape((3,) + TAB)
    saved = []
    for l in range(DEPTH):
        sink_rows = jnp.repeat(attn_sink[l], BLOCK).reshape(N_KV_HEADS, GQA * BLOCK)
        x, sv, token = _layer_fwd(l, x, weights_of, pool_scale[l:l + 1], g_mix[l:l + 1], g_ffn[l:l + 1], bias_tabs,
                                  sink_rows)
        saved.append(sv)
    loss, dx, dg_final = _loss_bwd(x, g_final.reshape(1, D_MODEL), tgt)
    dbias = jnp.zeros(TAB, F32)
    small = [None] * DEPTH
    for l in reversed(range(DEPTH)):
        dx, small[l], dbias, token = _layer_bwd(
            l, dx, saved[l], grads_to, pool_scale[l:l + 1], g_mix[l:l + 1], g_ffn[l:l + 1], bias_tabs, dbias, token)
    dsink_rows = jnp.concatenate([small[l]["attn_sink"].reshape(N_HEADS, BLOCK) for l in range(DEPTH)], axis=0)
    d_rel_bias, d_sink = _bias_reduce(dbias.reshape(N_HEADS, TAB_FLAT), dsink_rows, onehot)
    cat = lambda k: jnp.concatenate([small[l][k] for l in range(DEPTH)], axis=0)
    smalls = dict(pool_scale=cat("pool_scale"), g_mix=cat("g_mix"), g_ffn=cat("g_ffn"),
                  attn_sink=d_sink.reshape(DEPTH, N_HEADS), rel_bias=d_rel_bias, g_final=dg_final)
    return loss[0, 0], dx, smalls


SHARD_AXIS = dict(w_in=(1, IN_TOTAL // N_CHIPS), conv_w=(1, D_MODEL // N_CHIPS), w_a_out=(0, D_MODEL // N_CHIPS),
                  w_pool=(1, POOL_CG // N_CHIPS), w_attn_out=(0, D_MODEL // N_CHIPS), w_o=(0, D_MODEL // N_CHIPS),
                  w_gu=(1, 2 * D_FF // N_CHIPS), w_down=(0, D_FF // N_CHIPS))
HBM = pl.BlockSpec(memory_space=pltpu.HBM)
SEM = pl.BlockSpec(memory_space=pltpu.SEMAPHORE)
DATAFLOW = pltpu.SideEffectType.DATAFLOW_SIDE_EFFECTING
TOKEN = jax.ShapeDtypeStruct((8, 128), F32)


def _shard_of(ref, name, chip):
    axis, n = SHARD_AXIS[name]
    idx = [slice(None)] * len(ref.shape)
    idx[axis] = pl.ds(chip * n, n)
    return ref.at[tuple(idx)]


def _with_shard_axis(name, shape, size):
    axis, _ = SHARD_AXIS[name]
    s = list(shape)
    s[axis] = size
    return tuple(s)


HALF_AXIS = dict(w_in=0, conv_w=1, w_a_out=0, w_pool=1, w_attn_out=0, w_o=0, w_gu=0, w_down=0)


def _half_of_shard(ref, name, core):
    axis = HALF_AXIS[name]
    n = ref.shape[axis] // 2
    idx = [slice(None)] * len(ref.shape)
    idx[axis] = pl.ds(core * n, n)
    return ref.at[tuple(idx)]


def _half_in_full(ref, name, chip, core):
    saxis, n = SHARD_AXIS[name]
    haxis = HALF_AXIS[name]
    idx = [slice(None)] * len(ref.shape)
    if haxis == saxis:
        idx[saxis] = pl.ds(chip * n + core * (n // 2), n // 2)
    else:
        h = ref.shape[haxis] // 2
        idx[saxis] = pl.ds(chip * n, n)
        idx[haxis] = pl.ds(core * h, h)
    return ref.at[tuple(idx)]


def _on_each_device(fn):
    me = 2 * lax.axis_index("x") + lax.axis_index("y")
    c = lax.axis_index("c")
    for chip in range(N_CHIPS):
        for core in range(2):
            pl.when((me == chip) & (c == core))(functools.partial(fn, chip, core))


def _chip_peers(x, y):
    return [(1 - x, y), (x, 1 - y), (1 - x, 1 - y)]


RELATION_XOR = (2, 1, 3)


def _group_copies(kind, group, srcs, lands, send_sems, recv_sems, local_sems, chip, core):
    x, y, c = lax.axis_index("x"), lax.axis_index("y"), lax.axis_index("c")
    copies = []
    for t, name in enumerate(GROUPS[group]):
        for j, (px, py) in enumerate(_chip_peers(x, y)):
            if kind == "gather":
                src, dst = _half_of_shard(srcs[t], name, core), _half_in_full(lands[t], name, chip, core)
            else:
                src, dst = _shard_of(srcs[t], name, chip ^ RELATION_XOR[j]), lands[t].at[j]
            copies.append(pltpu.make_async_remote_copy(
                src_ref=src, dst_ref=dst, send_sem=send_sems.at[3 * t + j], recv_sem=recv_sems.at[3 * t + j],
                device_id=(px, py, c), device_id_type=MESH))
        if kind == "gather":
            src, dst = srcs[t], _shard_of(lands[t], name, chip)
        else:
            src, dst = _shard_of(srcs[t], name, chip), lands[t].at[N_CHIPS - 1]
        copies.append(pltpu.make_async_copy(src, dst, local_sems.at[t]))
    return copies


def _exchange_start(kind, group, srcs, land_shapes, after):
    nw = len(GROUPS[group])

    def body(*refs):
        srcs_r, lands_r = refs[:nw], refs[nw:2 * nw]
        send_sems, recv_sems, local_sems = refs[2 * nw + 1:2 * nw + 4]
        token = refs[-1]

        def issue(chip, core):
            for cp in _group_copies(kind, group, srcs_r, lands_r, send_sems, recv_sems, local_sems, chip, core):
                cp.start()
        _on_each_device(issue)
        token[...] = jnp.zeros_like(token)

    lands = [pltpu.with_memory_space_constraint(lax.empty(s.shape, s.dtype), pltpu.HBM) for s in land_shapes]
    srcs = [pltpu.with_memory_space_constraint(a, pltpu.HBM) for a in srcs]
    thru = [pltpu.HBM(a.shape, a.dtype) for a in srcs + lands]
    outs = pl.pallas_call(
        body, name=f"{kind}_{group}_start",
        in_specs=[HBM] * (2 * nw) + [ANY],
        out_specs=[SEM, SEM, SEM] + [HBM] * (2 * nw) + [pl.BlockSpec(memory_space=pltpu.VMEM)],
        out_shape=[pltpu.SemaphoreType.DMA((3 * nw,)), pltpu.SemaphoreType.DMA((3 * nw,)),
                   pltpu.SemaphoreType.DMA((nw,))] + thru + [TOKEN],
        input_output_aliases={t: 3 + t for t in range(2 * nw)},
        compiler_params=pltpu.CompilerParams(has_side_effects=DATAFLOW),
    )(*srcs, *lands, after)
    return dict(sems=outs[0:3], srcs=outs[3:3 + nw], lands=outs[3 + nw:3 + 2 * nw], token=outs[-1])


def _exchange_wait(kind, group, started, after):
    nw = len(GROUPS[group])

    def body(*refs):
        srcs_r, lands_r = refs[:nw], refs[nw:2 * nw]
        send_sems, recv_sems, local_sems = refs[2 * nw:2 * nw + 3]
        for cp in _group_copies(kind, group, srcs_r, lands_r, send_sems, recv_sems, local_sems, 0, 0):
            cp.wait()

    srcs, lands = list(started["srcs"]), list(started["lands"])
    outs = pl.pallas_call(
        body, name=f"{kind}_{group}_wait",
        in_specs=[HBM] * (2 * nw) + [SEM, SEM, SEM, ANY],
        out_specs=[HBM] * (2 * nw),
        out_shape=[pltpu.HBM(a.shape, a.dtype) for a in srcs + lands],
        input_output_aliases={t: t for t in range(2 * nw)},
        compiler_params=pltpu.CompilerParams(has_side_effects=DATAFLOW),
    )(*srcs, *lands, *started["sems"], after)
    return dict(zip(GROUPS[group], outs[nw:]))


def _gather_start(group, shards, after):
    names = GROUPS[group]
    shapes = [jax.ShapeDtypeStruct(_with_shard_axis(n, shards[n].shape, SHARD_AXIS[n][1] * N_CHIPS), shards[n].dtype)
              for n in names]
    return _exchange_start("gather", group, [shards[n] for n in names], shapes, after)


def _scatter_start(group, grads, after):
    names = GROUPS[group]
    shapes = [jax.ShapeDtypeStruct((N_CHIPS,) + _with_shard_axis(n, grads[n].shape, SHARD_AXIS[n][1]), grads[n].dtype)
              for n in names]
    return _exchange_start("scatter", group, [grads[n] for n in names], shapes, after)


def _sibling_exchange(parts):
    n = len(parts)

    def body(*refs):
        ins, outs = refs[:n], refs[n:2 * n]
        send_sems, recv_sems = refs[2 * n:]
        sibling = (lax.axis_index("x"), lax.axis_index("y"), 1 - lax.axis_index("c"))
        copies = [pltpu.make_async_remote_copy(src_ref=ins[t], dst_ref=outs[t], send_sem=send_sems.at[t],
                                               recv_sem=recv_sems.at[t], device_id=sibling, device_id_type=MESH)
                  for t in range(n)]
        for cp in copies:
            cp.start()
        for cp in copies:
            cp.wait()

    outs = pl.pallas_call(
        body, name="sibling_exchange", in_specs=[ANY] * n, out_specs=[ANY] * n,
        out_shape=[jax.ShapeDtypeStruct(p.shape, p.dtype) for p in parts],
        scratch_shapes=[pltpu.SemaphoreType.DMA((n,)), pltpu.SemaphoreType.DMA((n,))],
        compiler_params=pltpu.CompilerParams(has_side_effects=True),
    )(*parts)
    return list(outs)


def _sibling_fill(group, fulls):
    names = GROUPS[group]
    nw = len(names)

    def body(*refs):
        ins, outs = refs[:nw], refs[nw:2 * nw]
        send_sems, recv_sems = refs[2 * nw:]
        sibling = (lax.axis_index("x"), lax.axis_index("y"), 1 - lax.axis_index("c"))

        def forward(chip, core):
            copies = []
            for t, name in enumerate(names):
                for j in range(3):
                    other = chip ^ RELATION_XOR[j]
                    copies.append(pltpu.make_async_remote_copy(
                        src_ref=_half_in_full(ins[t], name, other, core),
                        dst_ref=_half_in_full(outs[t], name, other, core),
                        send_sem=send_sems.at[3 * t + j], recv_sem=recv_sems.at[3 * t + j],
                        device_id=sibling, device_id_type=MESH))
            for cp in copies:
                cp.start()
            for cp in copies:
                cp.wait()
        _on_each_device(forward)

    arrays = [fulls[n] for n in names]
    outs = pl.pallas_call(
        body, name=f"sibling_fill_{group}", in_specs=[ANY] * nw, out_specs=[ANY] * nw,
        out_shape=[jax.ShapeDtypeStruct(a.shape, a.dtype) for a in arrays],
        scratch_shapes=[pltpu.SemaphoreType.DMA((3 * nw,)), pltpu.SemaphoreType.DMA((3 * nw,))],
        input_output_aliases={t: t for t in range(nw)},
        compiler_params=pltpu.CompilerParams(has_side_effects=True),
    )(*arrays)
    return dict(zip(names, outs))


N_DEV = 8


def _all_reduce_small(v):
    R, C = v.shape

    def body(v_ref, o_ref, slots, send_sems, recv_sems):
        x, y, c = lax.axis_index("x"), lax.axis_index("y"), lax.axis_index("c")
        me = 4 * x + 2 * y + c
        slots[me] = v_ref[...]
        copies = []
        for k in range(1, N_DEV):
            peer = me ^ k
            cp = pltpu.make_async_remote_copy(
                src_ref=v_ref, dst_ref=slots.at[me], send_sem=send_sems.at[k - 1], recv_sem=recv_sems.at[k - 1],
                device_id=(peer // 4, (peer // 2) % 2, peer % 2), device_id_type=MESH)
            cp.start()
            copies.append(cp)
        for cp in copies:
            cp.wait()
        acc = slots[0]
        for k in range(1, N_DEV):
            acc = acc + slots[k]
        o_ref[...] = acc

    return pl.pallas_call(
        body, name="all_reduce_small", out_shape=jax.ShapeDtypeStruct((R, C), F32),
        in_specs=[pl.BlockSpec(memory_space=pltpu.VMEM)], out_specs=pl.BlockSpec(memory_space=pltpu.VMEM),
        scratch_shapes=[pltpu.VMEM((N_DEV, R, C), F32), pltpu.SemaphoreType.DMA((N_DEV - 1,)),
                        pltpu.SemaphoreType.DMA((N_DEV - 1,))],
        compiler_params=pltpu.CompilerParams(has_side_effects=True),
    )(v)


def _as2d(shape):
    return (int(np.prod(shape[:-1])), shape[-1])


def _row_block(rows, cols, n_arrays):
    budget = V7X_VMEM_LIMIT // 2
    tr = rows
    while tr % 16 == 0 and 2 * n_arrays * tr * cols * 4 > budget:
        tr //= 2
    return tr


def _sum_slots(slots):
    _, R, C = slots.shape
    tr = _row_block(R, C, 5)

    def body(s_ref, o_ref):
        acc = s_ref[0].astype(F32)
        for k in range(1, N_CHIPS):
            acc = acc + s_ref[k].astype(F32)
        o_ref[...] = acc.astype(BF16)

    return pl.pallas_call(
        body, name="sum_slots", grid=(R // tr,),
        in_specs=[pl.BlockSpec((N_CHIPS, tr, C), lambda i: (0, i, 0))],
        out_specs=pl.BlockSpec((tr, C), lambda i: (i, 0)),
        out_shape=jax.ShapeDtypeStruct((R, C), BF16),
        compiler_params=_params("parallel"),
    )(slots)


def _adamw(l, w, m, v, g_a, g_b, prev):
    L, R, C = w.shape
    tr = _row_block(R, C, 9)
    c1 = 1.0 - ADAM_B1 ** ADAM_STEP
    c2 = 1.0 - ADAM_B2 ** ADAM_STEP

    def body(w_ref, m_ref, v_ref, a_ref, b_ref, *rest):
        g_ref, d_ref, nm_ref, nv_ref = rest[-4:]
        g = a_ref[...].astype(F32) + b_ref[...].astype(F32)
        nm = ADAM_B1 * m_ref[...] + (1.0 - ADAM_B1) * g
        nv = ADAM_B2 * v_ref[...] + (1.0 - ADAM_B2) * (g * g)
        g_ref[...] = g
        nm_ref[...] = nm
        nv_ref[...] = nv
        d_ref[...] = -ADAM_LR * ((nm / c1) / (jnp.sqrt(nv / c2) + ADAM_EPS) + ADAM_WD * w_ref[...])

    layer = pl.BlockSpec((None, tr, C), lambda i: (l, i, 0))
    blk = pl.BlockSpec((tr, C), lambda i: (i, 0))
    out = jax.ShapeDtypeStruct((L, R, C), F32)
    prev = [] if prev is None else list(prev)
    return pl.pallas_call(
        body, name="adamw", grid=(R // tr,), in_specs=[layer] * 3 + [blk] * 2 + [ANY] * len(prev),
        out_specs=[layer] * 4, out_shape=[out] * 4,
        input_output_aliases={5 + k: k for k in range(len(prev))},
        compiler_params=_params("parallel"),
    )(w, m, v, g_a, g_b, *prev)


SMALL_ROWS = 16


def _pack_small(pool_scale, g_mix, g_ffn, g_final, attn_sink, rel_bias):
    tail = jnp.concatenate([attn_sink.reshape(-1), rel_bias.reshape(-1)])
    tail = jnp.pad(tail, (0, D_MODEL - tail.shape[0])).reshape(1, D_MODEL)
    rows = jnp.concatenate([pool_scale, g_mix, g_ffn, g_final.reshape(1, D_MODEL), tail], axis=0)
    return jnp.pad(rows, ((0, SMALL_ROWS - rows.shape[0]), (0, 0)))


def _unpack_small(packed):
    n_sink = DEPTH * N_HEADS
    return dict(pool_scale=packed[0:4], g_mix=packed[4:8], g_ffn=packed[8:12], g_final=packed[12],
                attn_sink=packed[13, 0:n_sink].reshape(DEPTH, N_HEADS),
                rel_bias=packed[13, n_sink:n_sink + N_BUCKETS * N_HEADS].reshape(N_BUCKETS, N_HEADS))


def _group_shards(l, group, masters):
    out = {}
    for n in GROUPS[group]:
        w = masters[n][l]
        out[n] = jnp.pad(w.reshape(3, -1), ((0, 5), (0, 0))) if n == "conv_w" else w.astype(BF16)
    return out


def kernel(x, w_in, conv_w, w_a_out, w_pool, pool_scale, w_attn_out, attn_sink, w_o, g_mix, g_ffn, w_gu, w_down, rel_bias, g_final, loss_target, m_w_in, m_conv_w, m_w_a_out, m_w_pool, m_pool_scale, m_w_attn_out, m_attn_sink, m_w_o, m_g_mix, m_g_ffn, m_w_gu, m_w_down, m_rel_bias, m_g_final, v_w_in, v_conv_w, v_w_a_out, v_w_pool, v_pool_scale, v_w_attn_out, v_attn_sink, v_w_o, v_g_mix, v_g_ffn, v_w_gu, v_w_down, v_rel_bias, v_g_final):
    big = dict(w_in=(w_in, m_w_in, v_w_in), conv_w=(conv_w, m_conv_w, v_conv_w), w_a_out=(w_a_out, m_w_a_out, v_w_a_out),
               w_pool=(w_pool, m_w_pool, v_w_pool), w_attn_out=(w_attn_out, m_w_attn_out, v_w_attn_out),
               w_o=(w_o, m_w_o, v_w_o), w_gu=(w_gu, m_w_gu, v_w_gu), w_down=(w_down, m_w_down, v_w_down))

    big3 = {n: tuple(a.reshape((DEPTH,) + _as2d(a.shape[1:])) for a in big[n]) for n in WEIGHT_NAMES}
    masters = {n: big[n][0] for n in WEIGHT_NAMES}

    gathers = {(0, "mix"): _gather_start("mix", _group_shards(0, "mix", masters), rel_bias)}
    newest = {"token": gathers[0, "mix"]["token"]}
    masters = dict(zip(WEIGHT_NAMES, lax.optimization_barrier(
        (tuple(masters[n] for n in WEIGHT_NAMES), newest["token"]))[0]))

    def weights_of(l, group, a):
        W = _sibling_fill(group, _exchange_wait("gather", group, gathers.pop((l, group)), a))
        if group == "mix":
            gathers[l, "ffn"] = _gather_start("ffn", _group_shards(l, "ffn", masters), W["w_in"])
            newest["token"] = gathers[l, "ffn"]["token"]
            if l + 1 < DEPTH:
                gathers[l + 1, "mix"] = _gather_start("mix", _group_shards(l + 1, "mix", masters), newest["token"])
                newest["token"] = gathers[l + 1, "mix"]["token"]
        return W, newest["token"]

    results = {n: None for n in WEIGHT_NAMES}
    scatters = {}

    def finish(l, group, after):
        slots = _exchange_wait("scatter", group, scatters.pop((l, group)), after)
        names = GROUPS[group]
        parts = [_sum_slots(slots[n].reshape((N_CHIPS,) + _as2d(slots[n].shape[1:]))) for n in names]
        others = _sibling_exchange(parts)
        for n, mine, other in zip(names, parts, others):
            if n == "conv_w":
                mine, other = mine[0:3], other[0:3]
            results[n] = _adamw(l, *big3[n], mine, other, results[n])

    def grads_to(l, group, wgrads, a):
        scatters[l, group] = _scatter_start(group, wgrads, a)
        token = scatters[l, group]["token"]
        if group == "mix" and l + 1 < DEPTH:
            finish(l + 1, "ffn", token)
            finish(l + 1, "mix", token)
        return token

    loss, grad_x, smalls = _local_step(x[0], loss_target[0], weights_of, grads_to, pool_scale, attn_sink, g_mix, g_ffn,
                                       rel_bias, g_final)
    finish(0, "ffn", grad_x)
    finish(0, "mix", results["w_down"][0])
    stacked = {n: [o.reshape(big[n][0].shape) for o in results[n]] for n in WEIGHT_NAMES}

    g_small = _all_reduce_small(_pack_small(smalls["pool_scale"], smalls["g_mix"], smalls["g_ffn"], smalls["g_final"],
                                            smalls["attn_sink"], smalls["rel_bias"]))
    w_small = _pack_small(pool_scale, g_mix, g_ffn, g_final, attn_sink, rel_bias)
    m_small = _pack_small(m_pool_scale, m_g_mix, m_g_ffn, m_g_final, m_attn_sink, m_rel_bias)
    v_small = _pack_small(v_pool_scale, v_g_mix, v_g_ffn, v_g_final, v_attn_sink, v_rel_bias)
    small_out = [_unpack_small(o[0]) for o in
                 _adamw(0, w_small[None], m_small[None], v_small[None], g_small, jnp.zeros_like(g_small), None)]

    total_loss = lax.psum(loss, ("x", "y", "c"))

    order = ("w_in", "conv_w", "w_a_out", "w_pool", "pool_scale", "w_attn_out", "attn_sink", "w_o", "g_mix", "g_ffn",
             "w_gu", "w_down", "rel_bias", "g_final")
    outs = [total_loss, grad_x[None]]
    for k in range(4):
        for n in order:
            outs.append(stacked[n][k] if n in stacked else small_out[k][n])
    return tuple(outs)
```

```python
import functools
import math

import numpy as np
import jax
import jax.numpy as jnp
from jax import lax
from jax.experimental import pallas as pl
from jax.experimental.pallas import tpu as pltpu

F32 = jnp.float32
BF16 = jnp.bfloat16

D_MODEL = 1024
DEPTH = 4
N_HEADS = 16
N_KV_HEADS = 4
HEAD_DIM = 64
GQA = N_HEADS // N_KV_HEADS
WINDOW = 128
BLOCK = 128
N_BUCKETS = 32
MAX_DISTANCE = 128
POOL_GROUPS = 4
POOL_CG = D_MODEL // POOL_GROUPS
POOL_WINDOWS = (2, 4, 8, 16)
D_FF = 2816
IN_TOTAL = 8704
OFF_B, OFF_C, OFF_X, OFF_U, OFF_Q, OFF_K, OFF_V, OFF_GA, OFF_GP, OFF_GT = (
    0, 1024, 2048, 3072, 4096, 5120, 5376, 5632, 6656, 7680)
EPS = 1e-6
NEG_INF = -1e30
SM_SCALE = HEAD_DIM ** -0.5

ADAM_LR = 0.001
ADAM_B1 = 0.9
ADAM_B2 = 0.999
ADAM_EPS = 1e-08
ADAM_WD = 0.01
ADAM_STEP = 10

N_CHIPS = 4
HALO = 8
V7X_VMEM_LIMIT = 56 * 1024 * 1024
MESH = pl.DeviceIdType.MESH
ANY = pl.BlockSpec(memory_space=pl.ANY)


def _params(*sem):
    return pltpu.CompilerParams(dimension_semantics=tuple(sem) if sem else None,
                                vmem_limit_bytes=V7X_VMEM_LIMIT)


def _tile(n, pref):
    t = min(pref, n)
    while n % t or t % 128:
        t -= 128
    return t


def _nt(a, b):
    return lax.dot_general(a, b, (((1,), (1,)), ((), ())), preferred_element_type=F32)


def _tn(a, b):
    return lax.dot_general(a, b, (((0,), (0,)), ((), ())), preferred_element_type=F32)


def _nn(a, b):
    return jnp.dot(a, b, preferred_element_type=F32)


def _sigmoid(v):
    return 1.0 / (1.0 + jnp.exp(-v))


def _norm_matmul(x, g, w, name, token):
    S, Dm = x.shape
    N = w.shape[1]
    tm, tn = _tile(S, 1024), _tile(N, N // 4)

    def body(x_ref, g_ref, w_ref, token_ref, h_ref, o_ref):
        @pl.when(pl.program_id(1) == 0)
        def _():
            xv = x_ref[...]
            r = lax.rsqrt(jnp.mean(xv * xv, axis=-1, keepdims=True) + EPS)
            h_ref[...] = (xv * r * g_ref[...]).astype(BF16)
        o_ref[...] = _nn(h_ref[...], w_ref[...]).astype(BF16)

    return pl.pallas_call(
        body, name=name, grid=(S // tm, N // tn),
        in_specs=[pl.BlockSpec((tm, Dm), lambda i, j: (i, 0)),
                  pl.BlockSpec((1, Dm), lambda i, j: (0, 0)),
                  pl.BlockSpec((Dm, tn), lambda i, j: (0, j)), ANY],
        out_specs=[pl.BlockSpec((tm, Dm), lambda i, j: (i, 0)),
                   pl.BlockSpec((tm, tn), lambda i, j: (i, j))],
        out_shape=[jax.ShapeDtypeStruct((S, Dm), BF16), jax.ShapeDtypeStruct((S, N), BF16)],
        compiler_params=_params("parallel", "arbitrary"),
    )(x, g, w, token)


CB = 128


def _fill_padded(pad_ref, v, S):
    z = jnp.zeros((HALO, v.shape[1]), F32)
    pad_ref[pl.ds(0, HALO), :] = z
    pad_ref[pl.ds(S + HALO, HALO), :] = z
    pad_ref[pl.ds(HALO, S), :] = v


def _shifted(pad_ref, off, S):
    return pad_ref[pl.ds(HALO + off, S), :]


def _conv_fwd(proj, cw8):
    S = proj.shape[0]
    nblk = D_MODEL // CB

    def body(b_ref, c_ref, x_ref, w_ref, o_ref, pad):
        u = c_ref[...].astype(F32) * x_ref[...].astype(F32)
        _fill_padded(pad, u, S)
        cv = w_ref[0:1, :] * _shifted(pad, -1, S) + w_ref[1:2, :] * u + w_ref[2:3, :] * _shifted(pad, 1, S)
        o_ref[...] = (b_ref[...].astype(F32) * cv).astype(BF16)

    col = lambda base: pl.BlockSpec((S, CB), lambda j: (0, base // CB + j))
    return pl.pallas_call(
        body, name="conv_fwd", grid=(nblk,),
        in_specs=[col(OFF_B), col(OFF_C), col(OFF_X), pl.BlockSpec((8, CB), lambda j: (0, j))],
        out_specs=pl.BlockSpec((S, CB), lambda j: (0, j)),
        out_shape=jax.ShapeDtypeStruct((S, D_MODEL), BF16),
        scratch_shapes=[pltpu.VMEM((S + 2 * HALO, CB), F32)],
        compiler_params=_params("parallel"),
    )(proj, proj, proj, cw8)


def _pool_count(S, lo, hi):
    t = lax.broadcasted_iota(jnp.int32, (S, CB), 0)
    return (jnp.minimum(t + hi, S - 1) - jnp.maximum(t - lo, 0) + 1).astype(F32)


def _pool_fwd(proj):
    S = proj.shape[0]
    nblk = D_MODEL // CB
    per_group = POOL_CG // CB

    def body(u_ref, o_ref, pad):
        u = u_ref[...].astype(F32)
        _fill_padded(pad, u, S)
        grp = pl.program_id(0) // per_group
        for gi, w in enumerate(POOL_WINDOWS):
            @pl.when(grp == gi)
            def _(w=w):
                lo, hi = w // 2, w - 1 - w // 2
                acc = _shifted(pad, -lo, S)
                for off in range(-lo + 1, hi + 1):
                    acc = acc + _shifted(pad, off, S)
                o_ref[...] = (acc / _pool_count(S, lo, hi) - u).astype(BF16)

    return pl.pallas_call(
        body, name="pool_fwd", grid=(nblk,),
        in_specs=[pl.BlockSpec((S, CB), lambda j: (0, OFF_U // CB + j))],
        out_specs=pl.BlockSpec((S, CB), lambda j: (0, j)),
        out_shape=jax.ShapeDtypeStruct((S, D_MODEL), BF16),
        scratch_shapes=[pltpu.VMEM((S + 2 * HALO, CB), F32)],
        compiler_params=_params("parallel"),
    )(proj)


def _attn_specs(S):
    nb = S // BLOCK
    kcol, vcol = OFF_K // (N_KV_HEADS * HEAD_DIM), OFF_V // (N_KV_HEADS * HEAD_DIM)
    kvw = N_KV_HEADS * HEAD_DIM
    prev = lambda i: jnp.maximum(i - 1, 0)
    nxt = lambda i: jnp.minimum(i + 1, nb - 1)
    return [
        pl.BlockSpec((BLOCK, D_MODEL), lambda i: (i, OFF_Q // D_MODEL)),
        pl.BlockSpec((BLOCK, kvw), lambda i: (prev(i), kcol)),
        pl.BlockSpec((BLOCK, kvw), lambda i: (i, kcol)),
        pl.BlockSpec((BLOCK, kvw), lambda i: (nxt(i), kcol)),
        pl.BlockSpec((BLOCK, kvw), lambda i: (prev(i), vcol)),
        pl.BlockSpec((BLOCK, kvw), lambda i: (i, vcol)),
        pl.BlockSpec((BLOCK, kvw), lambda i: (nxt(i), vcol)),
    ]


def _heads_rows(ref_or_val, hk):
    return jnp.concatenate(
        [ref_or_val[:, (GQA * hk + g) * HEAD_DIM:(GQA * hk + g + 1) * HEAD_DIM] for g in range(GQA)], axis=0)


def _kv_rows(p_ref, c_ref, n_ref, hk):
    sl = slice(hk * HEAD_DIM, (hk + 1) * HEAD_DIM)
    return jnp.concatenate([p_ref[:, sl], c_ref[:, sl], n_ref[:, sl]], axis=0)


def _bias_cols(bias_ref, hk):
    return jnp.concatenate([bias_ref[GQA * hk + g] for g in range(GQA)], axis=1)


def _softmax_keys_on_rows(q4s, kc, bias_blk, sink_row):
    s = _nt(kc, q4s) + bias_blk
    m = jnp.maximum(jnp.max(s, axis=0, keepdims=True), sink_row)
    p = jnp.exp(s - m)
    e_sink = jnp.exp(sink_row - m)
    inv = 1.0 / (jnp.sum(p, axis=0, keepdims=True) + e_sink)
    return p * inv, e_sink * inv


TAB = (N_HEADS, 3 * BLOCK, BLOCK)
TAB_FLAT = 3 * BLOCK * BLOCK


def _bias_spec(nb):
    return pl.BlockSpec((None,) + TAB, lambda i: (jnp.where(i == 0, 0, jnp.where(i == nb - 1, 2, 1)), 0, 0, 0))


def _attn_fwd(proj, bias_tabs, sink_rows):
    S = proj.shape[0]
    nb = S // BLOCK
    assert nb >= 2

    def body(q_ref, kp, kc_, kn, vp, vc_, vn, bias_ref, sink_ref, o_ref):
        outs = []
        for hk in range(N_KV_HEADS):
            q4s = _heads_rows(q_ref, hk) * SM_SCALE
            kc = _kv_rows(kp, kc_, kn, hk)
            vc = _kv_rows(vp, vc_, vn, hk)
            pn, _ = _softmax_keys_on_rows(q4s, kc, _bias_cols(bias_ref, hk), sink_ref[hk:hk + 1, :])
            o4 = _tn(pn.astype(BF16), vc)
            outs += [o4[g * BLOCK:(g + 1) * BLOCK, :] for g in range(GQA)]
        o_ref[...] = jnp.concatenate(outs, axis=1).astype(BF16)

    return pl.pallas_call(
        body, name="attn_fwd", grid=(nb,),
        in_specs=_attn_specs(S) + [_bias_spec(nb), pl.BlockSpec((N_KV_HEADS, GQA * BLOCK), lambda i: (0, 0))],
        out_specs=pl.BlockSpec((BLOCK, D_MODEL), lambda i: (i, 0)),
        out_shape=jax.ShapeDtypeStruct((S, D_MODEL), BF16),
        compiler_params=_params("parallel"),
    )(*([proj] * 7), bias_tabs, sink_rows)


GATE_HALF = D_MODEL // 2


def _gate_specs(tm):
    return [pl.BlockSpec((tm, GATE_HALF), lambda i, c=off // GATE_HALF + k: (i, c))
            for off in (OFF_GA, OFF_GP, OFF_GT) for k in (0, 1)]


def _gate(lo_ref, hi_ref):
    return _sigmoid(jnp.concatenate([lo_ref[...], hi_ref[...]], axis=1).astype(F32))


def _pool_mix(p, wp):
    return jnp.concatenate(
        [_nn(p[:, g * POOL_CG:(g + 1) * POOL_CG], wp[g]) for g in range(POOL_GROUPS)], axis=1)


def _mix_fwd(za, p, att, proj, x, wa, wp, ps, wt, wo):
    S = x.shape[0]
    tm = _tile(S, 256)

    def body(za_ref, p_ref, att_ref, ga0, ga1, gp0, gp1, gt0, gt1, x_ref, wa_ref, wp_ref, ps_ref, wt_ref, wo_ref,
             ya_ref, yp_ref, yt_ref, mg_ref, x2_ref):
        ya = _nn(za_ref[...], wa_ref[...])
        ypr = _pool_mix(p_ref[...], wp_ref)
        yt = _nn(att_ref[...], wt_ref[...])
        merged = _gate(ga0, ga1) * ya + _gate(gp0, gp1) * (ypr * ps_ref[...]) + _gate(gt0, gt1) * yt
        mb = merged.astype(BF16)
        ya_ref[...] = ya.astype(BF16)
        yp_ref[...] = ypr.astype(BF16)
        yt_ref[...] = yt.astype(BF16)
        mg_ref[...] = mb
        x2_ref[...] = x_ref[...] + _nn(mb, wo_ref[...])

    row = lambda c=0: pl.BlockSpec((tm, D_MODEL), lambda i: (i, c))
    whole = lambda a: pl.BlockSpec(a.shape, lambda i: (0,) * a.ndim)
    act = jax.ShapeDtypeStruct((S, D_MODEL), BF16)
    return pl.pallas_call(
        body, name="mix_fwd", grid=(S // tm,),
        in_specs=[row(), row(), row()] + _gate_specs(tm) + [row(), whole(wa), whole(wp), whole(ps), whole(wt), whole(wo)],
        out_specs=[row(), row(), row(), row(), row()],
        out_shape=[act, act, act, act, jax.ShapeDtypeStruct((S, D_MODEL), F32)],
        compiler_params=_params("parallel"),
    )(za, p, att, *([proj] * 6), x, wa, wp, ps, wt, wo)


def _ffn_fwd(gu, x2, wd):
    S = x2.shape[0]
    tm = _tile(S, 256)

    def body(g_ref, u_ref, x_ref, w_ref, a_ref, o_ref):
        g = g_ref[...].astype(F32)
        a = (g * _sigmoid(g) * u_ref[...].astype(F32)).astype(BF16)
        a_ref[...] = a
        o_ref[...] = x_ref[...] + _nn(a, w_ref[...])

    return pl.pallas_call(
        body, name="ffn_fwd", grid=(S // tm,),
        in_specs=[pl.BlockSpec((tm, D_FF), lambda i: (i, 0)), pl.BlockSpec((tm, D_FF), lambda i: (i, 1)),
                  pl.BlockSpec((tm, D_MODEL), lambda i: (i, 0)), pl.BlockSpec((D_FF, D_MODEL), lambda i: (0, 0))],
        out_specs=[pl.BlockSpec((tm, D_FF), lambda i: (i, 0)), pl.BlockSpec((tm, D_MODEL), lambda i: (i, 0))],
        out_shape=[jax.ShapeDtypeStruct((S, D_FF), BF16), jax.ShapeDtypeStruct((S, D_MODEL), F32)],
        compiler_params=_params("parallel"),
    )(gu, gu, x2, wd)


def _loss_bwd(x, g, tgt):
    S, Dm = x.shape
    tm = _tile(S, 512)

    def body(x_ref, g_ref, t_ref, l_ref, dx_ref, dg_ref):
        @pl.when(pl.program_id(0) == 0)
        def _():
            l_ref[...] = jnp.zeros_like(l_ref)
            dg_ref[...] = jnp.zeros_like(dg_ref)
        xv, gv = x_ref[...], g_ref[...]
        r = lax.rsqrt(jnp.mean(xv * xv, axis=-1, keepdims=True) + EPS)
        n = xv * r
        err = n * gv - t_ref[...]
        l_ref[...] += 0.5 * jnp.sum(jnp.mean(err * err, axis=-1, keepdims=True), axis=0, keepdims=True)
        dy = err * (1.0 / Dm)
        dn = dy * gv
        dx_ref[...] = r * (dn - n * jnp.mean(dn * n, axis=-1, keepdims=True))
        dg_ref[...] += jnp.sum(dy * n, axis=0, keepdims=True)

    return pl.pallas_call(
        body, name="loss_bwd", grid=(S // tm,),
        in_specs=[pl.BlockSpec((tm, Dm), lambda i: (i, 0)), pl.BlockSpec((1, Dm), lambda i: (0, 0)),
                  pl.BlockSpec((tm, Dm), lambda i: (i, 0))],
        out_specs=[pl.BlockSpec((8, 128), lambda i: (0, 0)), pl.BlockSpec((tm, Dm), lambda i: (i, 0)),
                   pl.BlockSpec((1, Dm), lambda i: (0, 0))],
        out_shape=[jax.ShapeDtypeStruct((8, 128), F32), jax.ShapeDtypeStruct((S, Dm), F32),
                   jax.ShapeDtypeStruct((1, Dm), F32)],
        compiler_params=_params("arbitrary"),
    )(x, g, tgt)


def _ffn_bwd(dx3, gu, wd, token):
    S = dx3.shape[0]
    tm = _tile(S, 256)

    def body(d_ref, g_ref, u_ref, w_ref, token_ref, o_ref):
        dact = _nt(d_ref[...].astype(BF16), w_ref[...])
        g, u = g_ref[...].astype(F32), u_ref[...].astype(F32)
        sg = _sigmoid(g)
        o_ref[:, 0:D_FF] = (dact * u * (sg * (1.0 + g * (1.0 - sg)))).astype(BF16)
        o_ref[:, D_FF:2 * D_FF] = (dact * (g * sg)).astype(BF16)

    return pl.pallas_call(
        body, name="ffn_bwd", grid=(S // tm,),
        in_specs=[pl.BlockSpec((tm, D_MODEL), lambda i: (i, 0)),
                  pl.BlockSpec((tm, D_FF), lambda i: (i, 0)), pl.BlockSpec((tm, D_FF), lambda i: (i, 1)),
                  pl.BlockSpec((D_FF, D_MODEL), lambda i: (0, 0)), ANY],
        out_specs=pl.BlockSpec((tm, 2 * D_FF), lambda i: (i, 0)),
        out_shape=jax.ShapeDtypeStruct((S, 2 * D_FF), BF16),
        compiler_params=_params("parallel"),
    )(dx3, gu, gu, wd, token)


def _wgrad(a, b, name, tk=512, tn=512, out_dtype=BF16, token=None):
    S, K = a.shape
    N = b.shape[1]
    tk, tn, ts = _tile(K, tk), _tile(N, tn), _tile(S, 1024)
    n_s = S // ts
    extra = [] if token is None else [token]

    def body(a_ref, b_ref, *rest):
        o_ref, acc = rest[-2:]
        s = pl.program_id(2)

        @pl.when(s == 0)
        def _():
            acc[...] = jnp.zeros_like(acc)
        acc[...] += _tn(a_ref[...].astype(BF16), b_ref[...].astype(BF16))

        @pl.when(s == n_s - 1)
        def _():
            o_ref[...] = acc[...].astype(out_dtype)

    return pl.pallas_call(
        body, name=name, grid=(K // tk, N // tn, n_s),
        in_specs=[pl.BlockSpec((ts, tk), lambda k, n, s: (s, k)), pl.BlockSpec((ts, tn), lambda k, n, s: (s, n))]
        + [ANY] * len(extra),
        out_specs=pl.BlockSpec((tk, tn), lambda k, n, s: (k, n)),
        out_shape=jax.ShapeDtypeStruct((K, N), out_dtype),
        scratch_shapes=[pltpu.VMEM((tk, tn), F32)],
        compiler_params=_params("parallel", "parallel", "arbitrary"),
    )(a, b, *extra)


def _wgrad_pool(p, dyps):
    S = p.shape[0]
    ts = _tile(S, 512)
    n_s = S // ts

    def body(a_ref, b_ref, o_ref, acc):
        s = pl.program_id(1)

        @pl.when(s == 0)
        def _():
            acc[...] = jnp.zeros_like(acc)
        acc[...] += _tn(a_ref[...], b_ref[...])

        @pl.when(s == n_s - 1)
        def _():
            o_ref[...] = acc[...].astype(BF16)

    return pl.pallas_call(
        body, name="wgrad_pool", grid=(POOL_GROUPS, n_s),
        in_specs=[pl.BlockSpec((ts, POOL_CG), lambda g, s: (s, g)), pl.BlockSpec((ts, POOL_CG), lambda g, s: (s, g))],
        out_specs=pl.BlockSpec((None, POOL_CG, POOL_CG), lambda g, s: (g, 0, 0)),
        out_shape=jax.ShapeDtypeStruct((POOL_GROUPS, POOL_CG, POOL_CG), BF16),
        scratch_shapes=[pltpu.VMEM((POOL_CG, POOL_CG), F32)],
        compiler_params=_params("parallel", "arbitrary"),
    )(p, dyps)


def _dgrad_norm_bwd(dy, w, x, g, dres, name, tk):
    S, K = dy.shape
    Dm = x.shape[1]
    tm, tk = _tile(S, 1024), _tile(K, tk)
    n_k = K // tk

    def body(dy_ref, w_ref, x_ref, g_ref, r_ref, dx_ref, dg_ref, acc):
        i, k = pl.program_id(0), pl.program_id(1)

        @pl.when((i == 0) & (k == 0))
        def _():
            dg_ref[...] = jnp.zeros_like(dg_ref)

        @pl.when(k == 0)
        def _():
            acc[...] = jnp.zeros_like(acc)
        acc[...] += _nt(dy_ref[...], w_ref[...])

        @pl.when(k == n_k - 1)
        def _():
            dh, xv = acc[...], x_ref[...]
            r = lax.rsqrt(jnp.mean(xv * xv, axis=-1, keepdims=True) + EPS)
            n = xv * r
            dn = dh * g_ref[...]
            dx_ref[...] = r_ref[...] + r * (dn - n * jnp.mean(dn * n, axis=-1, keepdims=True))
            dg_ref[...] += jnp.sum(dh * n, axis=0, keepdims=True)

    rowblk = pl.BlockSpec((tm, Dm), lambda i, k: (i, 0))
    vec = pl.BlockSpec((1, Dm), lambda i, k: (0, 0))
    return pl.pallas_call(
        body, name=name, grid=(S // tm, n_k),
        in_specs=[pl.BlockSpec((tm, tk), lambda i, k: (i, k)), pl.BlockSpec((Dm, tk), lambda i, k: (0, k)),
                  rowblk, vec, rowblk],
        out_specs=[rowblk, vec],
        out_shape=[jax.ShapeDtypeStruct((S, Dm), F32), jax.ShapeDtypeStruct((1, Dm), F32)],
        scratch_shapes=[pltpu.VMEM((tm, Dm), F32)],
        compiler_params=_params("arbitrary", "arbitrary"),
    )(dy, w, x, g, dres)


def _mix_bwd(dx2, ya, ypr, yt, proj, ps, wa, wp, wt, wo, token):
    S = dx2.shape[0]
    tm = _tile(S, 256)

    n_tiles = S // tm

    def body(dx_ref, ya_ref, yp_ref, yt_ref, ga0, ga1, gp0, gp1, gt0, gt1, ps_ref, wa_ref, wp_ref, wt_ref, wo_ref,
             token_ref, dya_ref, dyt_ref, dyps_ref, dza_ref, datt_ref, dp_ref, dproj_hbm, dps_ref, dgates, sem):
        i = pl.program_id(0)
        to_dproj = pltpu.make_async_copy(
            dgates, dproj_hbm.at[pl.ds(pl.multiple_of(i * tm, tm), tm), pl.ds(OFF_GA, 3 * D_MODEL)], sem)

        @pl.when(i == 0)
        def _():
            dps_ref[...] = jnp.zeros_like(dps_ref)
        dm = _nt(dx_ref[...].astype(BF16), wo_ref[...])
        sa, sp, st = _gate(ga0, ga1), _gate(gp0, gp1), _gate(gt0, gt1)
        psv = ps_ref[...]
        ypr_v = yp_ref[...].astype(F32)
        dya = (sa * dm).astype(BF16)
        dyt = (st * dm).astype(BF16)
        dyp = sp * dm
        dyps = (dyp * psv).astype(BF16)
        dya_ref[...] = dya
        dyt_ref[...] = dyt
        dyps_ref[...] = dyps
        dg = jnp.concatenate(
            [dm * ya_ref[...].astype(F32) * (sa * (1.0 - sa)), dm * (ypr_v * psv) * (sp * (1.0 - sp)),
             dm * yt_ref[...].astype(F32) * (st * (1.0 - st))], axis=1).astype(BF16)

        @pl.when(i > 0)
        def _():
            to_dproj.wait()
        dgates[...] = dg
        to_dproj.start()
        dps_ref[...] += jnp.sum(dyp * ypr_v, axis=0, keepdims=True)
        dza_ref[...] = _nt(dya, wa_ref[...]).astype(BF16)
        datt_ref[...] = _nt(dyt, wt_ref[...]).astype(BF16)
        dp_ref[...] = jnp.concatenate(
            [_nt(dyps[:, g * POOL_CG:(g + 1) * POOL_CG], wp_ref[g]) for g in range(POOL_GROUPS)], axis=1).astype(BF16)

        @pl.when(i == n_tiles - 1)
        def _():
            to_dproj.wait()

    row = lambda c=0: pl.BlockSpec((tm, D_MODEL), lambda i: (i, c))
    whole = lambda a: pl.BlockSpec(a.shape, lambda i: (0,) * a.ndim)
    act = jax.ShapeDtypeStruct((S, D_MODEL), BF16)
    return pl.pallas_call(
        body, name="mix_bwd", grid=(n_tiles,),
        in_specs=[row(), row(), row(), row()] + _gate_specs(tm)
        + [whole(ps), whole(wa), whole(wp), whole(wt), whole(wo), ANY],
        out_specs=[row()] * 6 + [ANY, pl.BlockSpec((1, D_MODEL), lambda i: (0, 0))],
        out_shape=[act] * 6 + [jax.ShapeDtypeStruct((S, IN_TOTAL), BF16), jax.ShapeDtypeStruct((1, D_MODEL), F32)],
        scratch_shapes=[pltpu.VMEM((tm, 3 * D_MODEL), BF16), pltpu.SemaphoreType.DMA],
        compiler_params=_params("arbitrary"),
    )(dx2, ya, ypr, yt, *([proj] * 6), ps, wa, wp, wt, wo, token)


def _conv_bwd(dza, proj, cw8, dproj):
    S = proj.shape[0]
    nblk = D_MODEL // CB

    def body(d_ref, b_ref, c_ref, x_ref, w_ref, dproj_in, dproj_hbm, dw_ref, pad_u, pad_d, parts, sems):
        cb = pl.program_id(0)
        to_dproj = [pltpu.make_async_copy(
            parts.at[k], dproj_hbm.at[:, pl.ds(pl.multiple_of(off + cb * CB, CB), CB)], sems.at[k])
            for k, off in enumerate((OFF_B, OFF_C, OFF_X))]
        c, xa = c_ref[...].astype(F32), x_ref[...].astype(F32)
        u = c * xa
        _fill_padded(pad_u, u, S)
        u_prev, u_next = _shifted(pad_u, -1, S), _shifted(pad_u, 1, S)
        cv = w_ref[0:1, :] * u_prev + w_ref[1:2, :] * u + w_ref[2:3, :] * u_next
        dza_v = d_ref[...].astype(F32)
        dcv = dza_v * b_ref[...].astype(F32)
        _fill_padded(pad_d, dcv, S)
        du = w_ref[0:1, :] * _shifted(pad_d, 1, S) + w_ref[1:2, :] * dcv + w_ref[2:3, :] * _shifted(pad_d, -1, S)

        @pl.when(cb > 0)
        def _():
            for cp in to_dproj:
                cp.wait()
        parts[0] = (dza_v * cv).astype(BF16)
        parts[1] = (du * xa).astype(BF16)
        parts[2] = (du * c).astype(BF16)
        for cp in to_dproj:
            cp.start()
        dw_ref[...] = jnp.concatenate(
            [jnp.sum(dcv * u_prev, axis=0, keepdims=True), jnp.sum(dcv * u, axis=0, keepdims=True),
             jnp.sum(dcv * u_next, axis=0, keepdims=True), jnp.zeros((5, CB), F32)], axis=0)

        @pl.when(cb == nblk - 1)
        def _():
            for cp in to_dproj:
                cp.wait()

    col = lambda base: pl.BlockSpec((S, CB), lambda cb: (0, base // CB + cb))
    taps = pl.BlockSpec((8, CB), lambda cb: (0, cb))
    return pl.pallas_call(
        body, name="conv_bwd", grid=(nblk,),
        in_specs=[col(0), col(OFF_B), col(OFF_C), col(OFF_X), taps, ANY],
        out_specs=[ANY, taps],
        out_shape=[jax.ShapeDtypeStruct(dproj.shape, dproj.dtype), jax.ShapeDtypeStruct((8, D_MODEL), F32)],
        scratch_shapes=[pltpu.VMEM((S + 2 * HALO, CB), F32), pltpu.VMEM((S + 2 * HALO, CB), F32),
                        pltpu.VMEM((3, S, CB), BF16), pltpu.SemaphoreType.DMA((3,))],
        input_output_aliases={5: 0},
        compiler_params=_params("arbitrary"),
    )(dza, proj, proj, proj, cw8, dproj)


def _pool_bwd(dp, dproj):
    S = dp.shape[0]
    nblk = D_MODEL // CB
    per_group = POOL_CG // CB

    def body(d_ref, dproj_in, o_ref, pad):
        d = d_ref[...].astype(F32)
        grp = pl.program_id(0) // per_group
        for gi, w in enumerate(POOL_WINDOWS):
            @pl.when(grp == gi)
            def _(w=w):
                lo, hi = w // 2, w - 1 - w // 2
                _fill_padded(pad, d / _pool_count(S, lo, hi), S)
                acc = _shifted(pad, -hi, S)
                for off in range(-hi + 1, lo + 1):
                    acc = acc + _shifted(pad, off, S)
                o_ref[...] = (acc - d).astype(BF16)

    return pl.pallas_call(
        body, name="pool_bwd", grid=(nblk,),
        in_specs=[pl.BlockSpec((S, CB), lambda j: (0, j)), ANY],
        out_specs=pl.BlockSpec((S, CB), lambda j: (0, OFF_U // CB + j)),
        out_shape=jax.ShapeDtypeStruct(dproj.shape, dproj.dtype),
        scratch_shapes=[pltpu.VMEM((S + 2 * HALO, CB), F32)],
        input_output_aliases={1: 0},
        compiler_params=_params("parallel"),
    )(dp, dproj)


def _attn_bwd(proj, datt, bias_tabs, sink_rows, dbias_in, dproj):
    S = proj.shape[0]
    nb = S // BLOCK
    kvw = N_KV_HEADS * HEAD_DIM

    def body(q_ref, kp, kc_, kn, vp, vc_, vn, do_ref, bias_ref, sink_ref, dbin_ref, dproj_in,
             dq_ref, dk_ref, dv_ref, db_ref, ds_ref):
        i = pl.program_id(0)

        @pl.when(i == 0)
        def _():
            dk_ref[...] = jnp.zeros_like(dk_ref)
            dv_ref[...] = jnp.zeros_like(dv_ref)
            db_ref[...] = dbin_ref[...]
            ds_ref[...] = jnp.zeros_like(ds_ref)
        dqs, dks, dvs = [], [], []
        for hk in range(N_KV_HEADS):
            q4s = _heads_rows(q_ref, hk) * SM_SCALE
            do4 = _heads_rows(do_ref, hk)
            kc = _kv_rows(kp, kc_, kn, hk)
            vc = _kv_rows(vp, vc_, vn, hk)
            pn, p_sink = _softmax_keys_on_rows(q4s, kc, _bias_cols(bias_ref, hk), sink_ref[hk:hk + 1, :])
            dpm = _nt(vc, do4)
            delta = jnp.sum(pn * dpm, axis=0, keepdims=True)
            dsc = pn * (dpm - delta)
            for g in range(GQA):
                db_ref[GQA * hk + g] += dsc[:, g * BLOCK:(g + 1) * BLOCK]
            ds_ref[hk:hk + 1, :] += -p_sink * delta
            dsb = dsc.astype(BF16)
            dq4 = _tn(dsb, kc) * SM_SCALE
            dqs += [dq4[g * BLOCK:(g + 1) * BLOCK, :] for g in range(GQA)]
            dks.append(_nn(dsb, q4s))
            dvs.append(_nn(pn.astype(BF16), do4))
        dq_ref[...] = jnp.concatenate(dqs, axis=1).astype(BF16)
        r0 = pl.multiple_of(i * BLOCK, BLOCK)
        dk_ref[pl.ds(r0, 3 * BLOCK), :] += jnp.concatenate(dks, axis=1)
        dv_ref[pl.ds(r0, 3 * BLOCK), :] += jnp.concatenate(dvs, axis=1)

    const = lambda shape: pl.BlockSpec(shape, lambda i: (0,) * len(shape))
    sink_shape = (N_KV_HEADS, GQA * BLOCK)
    return pl.pallas_call(
        body, name="attn_bwd", grid=(nb,),
        in_specs=_attn_specs(S) + [pl.BlockSpec((BLOCK, D_MODEL), lambda i: (i, 0)),
                                   _bias_spec(nb), const(sink_shape), const(TAB), ANY],
        out_specs=[pl.BlockSpec((BLOCK, D_MODEL), lambda i: (i, OFF_Q // D_MODEL)),
                   const((S + 2 * BLOCK, kvw)), const((S + 2 * BLOCK, kvw)), const(TAB), const(sink_shape)],
        out_shape=[jax.ShapeDtypeStruct(dproj.shape, dproj.dtype),
                   jax.ShapeDtypeStruct((S + 2 * BLOCK, kvw), F32), jax.ShapeDtypeStruct((S + 2 * BLOCK, kvw), F32),
                   jax.ShapeDtypeStruct(TAB, F32), jax.ShapeDtypeStruct(sink_shape, F32)],
        input_output_aliases={11: 0},
        compiler_params=_params("arbitrary"),
    )(*([proj] * 7), datt, bias_tabs, sink_rows, dbias_in, dproj)


def _kv_finish(dkp, dvp, dproj):
    S = dproj.shape[0]
    kvw = N_KV_HEADS * HEAD_DIM

    def body(dk_ref, dv_ref, dproj_in, o_ref):
        o_ref[:, 0:kvw] = dk_ref[pl.ds(BLOCK, S), :].astype(BF16)
        o_ref[:, kvw:2 * kvw] = dv_ref[pl.ds(BLOCK, S), :].astype(BF16)

    whole = pl.BlockSpec((S + 2 * BLOCK, kvw), lambda i: (0, 0))
    return pl.pallas_call(
        body, name="kv_finish", grid=(1,), in_specs=[whole, whole, ANY],
        out_specs=pl.BlockSpec((S, 2 * kvw), lambda i: (0, OFF_K // (2 * kvw))),
        out_shape=jax.ShapeDtypeStruct(dproj.shape, dproj.dtype),
        input_output_aliases={2: 0},
        compiler_params=_params("arbitrary"),
    )(dkp, dvp, dproj)


def _bucket_constants():
    half = N_BUCKETS // 2
    max_exact = half // 2
    qi = np.arange(BLOCK)[None, :]
    kj = np.arange(3 * BLOCK)[:, None]
    rel = kj - BLOCK - qi
    n = np.abs(rel)
    nf = np.maximum(n, 1).astype(np.float32)
    large = max_exact + (np.log(nf / np.float32(max_exact)) / np.float32(math.log(MAX_DISTANCE / max_exact))
                         * np.float32(half - max_exact)).astype(np.int32)
    large = np.minimum(large, half - 1)
    bucket = np.where(rel > 0, half, 0) + np.where(n < max_exact, n, large)
    onehot = (bucket.reshape(1, -1) == np.arange(N_BUCKETS)[:, None]).astype(np.float32)
    window = n <= WINDOW
    first = window & (kj >= BLOCK)
    last = window & (kj < 2 * BLOCK)
    masks = np.stack([np.where(v, 0.0, NEG_INF).astype(np.float32).reshape(-1) for v in (first, window, last)])
    return onehot, masks


def _bias_expand(rel_bias_t, onehot, masks):
    def body(r_ref, oh_ref, m_ref, o_ref):
        tab = jnp.dot(r_ref[...], oh_ref[...], preferred_element_type=F32, precision=lax.Precision.HIGHEST)
        for v in range(3):
            o_ref[v] = tab + m_ref[v:v + 1, :]

    return pl.pallas_call(
        body, name="bias_expand", out_shape=jax.ShapeDtypeStruct((3, N_HEADS, onehot.shape[1]), F32),
        compiler_params=_params(),
    )(rel_bias_t, onehot, masks)


def _bias_reduce(dtab, dsink_rows, onehot):
    def body(d_ref, s_ref, oh_ref, o_ref, so_ref):
        o_ref[...] = lax.dot_general(oh_ref[...], d_ref[...], (((1,), (1,)), ((), ())),
                                     preferred_element_type=F32, precision=lax.Precision.HIGHEST)
        so_ref[...] = jnp.sum(s_ref[...], axis=-1, keepdims=True)

    return pl.pallas_call(
        body, name="bias_reduce",
        out_shape=[jax.ShapeDtypeStruct((N_BUCKETS, N_HEADS), F32),
                   jax.ShapeDtypeStruct((dsink_rows.shape[0], 1), F32)],
        compiler_params=_params(),
    )(dtab, dsink_rows, onehot)


GROUPS = dict(mix=("w_in", "conv_w", "w_a_out", "w_pool", "w_attn_out", "w_o"), ffn=("w_gu", "w_down"))
WEIGHT_NAMES = GROUPS["mix"] + GROUPS["ffn"]


def _layer_fwd(l, x, weights_of, ps, g_mix, g_ffn, bias_tabs, sink_rows):
    W, token = weights_of(l, "mix", x)
    h, proj = _norm_matmul(x, g_mix, W["w_in"], "norm_proj", token)
    za = _conv_fwd(proj, W["conv_w"])
    p = _pool_fwd(proj)
    att = _attn_fwd(proj, bias_tabs, sink_rows)
    ya, ypr, yt, merged, x2 = _mix_fwd(za, p, att, proj, x, W["w_a_out"], W["w_pool"], ps, W["w_attn_out"], W["w_o"])
    Wf, token = weights_of(l, "ffn", x2)
    h2, gu = _norm_matmul(x2, g_ffn, Wf["w_gu"], "norm_gu", token)
    act, x3 = _ffn_fwd(gu, x2, Wf["w_down"])
    saved = dict(x=x, h=h, proj=proj, za=za, p=p, att=att, ya=ya, ypr=ypr, yt=yt, merged=merged, x2=x2, h2=h2,
                 gu=gu, act=act, W={**W, **Wf}, sink_rows=sink_rows)
    return x3, saved, token


def _layer_bwd(l, dx3, sv, grads_to, ps, g_mix, g_ffn, bias_tabs, dbias, token):
    W, sink_rows = sv["W"], sv["sink_rows"]
    dgu = _ffn_bwd(dx3, sv["gu"], W["w_down"], token)
    g_w_down = _wgrad(sv["act"], dx3, "wgrad_down", tk=1408, tn=1024, token=token)
    g_w_gu = _wgrad(sv["h2"], dgu, "wgrad_gu", tk=1024, tn=1408)
    dx2, dg_ffn = _dgrad_norm_bwd(dgu, W["w_gu"], sv["x2"], g_ffn, dx3, "dgrad_gu", tk=1408)
    token = grads_to(l, "ffn", dict(w_gu=g_w_gu, w_down=g_w_down), dx2)
    dya, dyt, dyps, dza, datt, dp, dproj, dps = _mix_bwd(
        dx2, sv["ya"], sv["ypr"], sv["yt"], sv["proj"], ps, W["w_a_out"], W["w_pool"], W["w_attn_out"], W["w_o"], token)
    g_w_o = _wgrad(sv["merged"], dx2, "wgrad_sq_f32", tk=1024, tn=1024)
    g_w_a_out = _wgrad(sv["za"], dya, "wgrad_sq", tk=1024, tn=1024)
    g_w_attn_out = _wgrad(sv["att"], dyt, "wgrad_sq", tk=1024, tn=1024)
    g_w_pool = _wgrad_pool(sv["p"], dyps)
    dproj, g_conv = _conv_bwd(dza, sv["proj"], W["conv_w"], dproj)
    dproj = _pool_bwd(dp, dproj)
    dproj, dkp, dvp, dbias, dsink = _attn_bwd(sv["proj"], datt, bias_tabs, sink_rows, dbias, dproj)
    dproj = _kv_finish(dkp, dvp, dproj)
    g_w_in = _wgrad(sv["h"], dproj, "wgrad_in", tk=1024, tn=2176)
    dx, dg_mix = _dgrad_norm_bwd(dproj, W["w_in"], sv["x"], g_mix, dx2, "dgrad_in", tk=2176)
    token = grads_to(l, "mix", dict(w_in=g_w_in, conv_w=g_conv, w_a_out=g_w_a_out, w_pool=g_w_pool,
                                    w_attn_out=g_w_attn_out, w_o=g_w_o), dx)
    return dx, dict(pool_scale=dps, g_mix=dg_mix, g_ffn=dg_ffn, attn_sink=dsink), dbias, token


def _local_step(x, tgt, weights_of, grads_to, pool_scale, attn_sink, g_mix, g_ffn, rel_bias, g_final):
    onehot_np, masks_np = _bucket_constants()
    onehot, masks = jnp.asarray(onehot_np), jnp.asarray(masks_np)
    bias_tabs = _bias_expand(rel_bias.T, onehot, masks).reshape((3,) + TAB)
    saved = []
    for l in range(DEPTH):
        sink_rows = jnp.repeat(attn_sink[l], BLOCK).reshape(N_KV_HEADS, GQA * BLOCK)
        x, sv, token = _layer_fwd(l, x, weights_of, pool_scale[l:l + 1], g_mix[l:l + 1], g_ffn[l:l + 1], bias_tabs,
                                  sink_rows)
        saved.append(sv)
    loss, dx, dg_final = _loss_bwd(x, g_final.reshape(1, D_MODEL), tgt)
    dbias = jnp.zeros(TAB, F32)
    small = [None] * DEPTH
    for l in reversed(range(DEPTH)):
        dx, small[l], dbias, token = _layer_bwd(
            l, dx, saved[l], grads_to, pool_scale[l:l + 1], g_mix[l:l + 1], g_ffn[l:l + 1], bias_tabs, dbias, token)
    dsink_rows = jnp.concatenate([small[l]["attn_sink"].reshape(N_HEADS, BLOCK) for l in range(DEPTH)], axis=0)
    d_rel_bias, d_sink = _bias_reduce(dbias.reshape(N_HEADS, TAB_FLAT), dsink_rows, onehot)
    cat = lambda k: jnp.concatenate([small[l][k] for l in range(DEPTH)], axis=0)
    smalls = dict(pool_scale=cat("pool_scale"), g_mix=cat("g_mix"), g_ffn=cat("g_ffn"),
                  attn_sink=d_sink.reshape(DEPTH, N_HEADS), rel_bias=d_rel_bias, g_final=dg_final)
    return loss[0, 0], dx, smalls


SHARD_AXIS = dict(w_in=(1, IN_TOTAL // N_CHIPS), conv_w=(1, D_MODEL // N_CHIPS), w_a_out=(0, D_MODEL // N_CHIPS),
                  w_pool=(1, POOL_CG // N_CHIPS), w_attn_out=(0, D_MODEL // N_CHIPS), w_o=(0, D_MODEL // N_CHIPS),
                  w_gu=(1, 2 * D_FF // N_CHIPS), w_down=(0, D_FF // N_CHIPS))
HBM = pl.BlockSpec(memory_space=pltpu.HBM)
SEM = pl.BlockSpec(memory_space=pltpu.SEMAPHORE)
DATAFLOW = pltpu.SideEffectType.DATAFLOW_SIDE_EFFECTING
TOKEN = jax.ShapeDtypeStruct((8, 128), F32)


def _shard_of(ref, name, chip):
    axis, n = SHARD_AXIS[name]
    idx = [slice(None)] * len(ref.shape)
    idx[axis] = pl.ds(chip * n, n)
    return ref.at[tuple(idx)]


def _with_shard_axis(name, shape, size):
    axis, _ = SHARD_AXIS[name]
    s = list(shape)
    s[axis] = size
    return tuple(s)


HALF_AXIS = dict(w_in=0, conv_w=1, w_a_out=0, w_pool=1, w_attn_out=0, w_o=0, w_gu=0, w_down=0)


def _half_of_shard(ref, name, core):
    axis = HALF_AXIS[name]
    n = ref.shape[axis] // 2
    idx = [slice(None)] * len(ref.shape)
    idx[axis] = pl.ds(core * n, n)
    return ref.at[tuple(idx)]


def _half_in_full(ref, name, chip, core):
    saxis, n = SHARD_AXIS[name]
    haxis = HALF_AXIS[name]
    idx = [slice(None)] * len(ref.shape)
    if haxis == saxis:
        idx[saxis] = pl.ds(chip * n + core * (n // 2), n // 2)
    else:
        h = ref.shape[haxis] // 2
        idx[saxis] = pl.ds(chip * n, n)
        idx[haxis] = pl.ds(core * h, h)
    return ref.at[tuple(idx)]


def _on_each_device(fn):
    me = 2 * lax.axis_index("x") + lax.axis_index("y")
    c = lax.axis_index("c")
    for chip in range(N_CHIPS):
        for core in range(2):
            pl.when((me == chip) & (c == core))(functools.partial(fn, chip, core))


def _chip_peers(x, y):
    return [(1 - x, y), (x, 1 - y), (1 - x, 1 - y)]


RELATION_XOR = (2, 1, 3)


def _group_copies(kind, group, srcs, lands, send_sems, recv_sems, local_sems, chip, core):
    x, y, c = lax.axis_index("x"), lax.axis_index("y"), lax.axis_index("c")
    copies = []
    for t, name in enumerate(GROUPS[group]):
        for j, (px, py) in enumerate(_chip_peers(x, y)):
            if kind == "gather":
                src, dst = _half_of_shard(srcs[t], name, core), _half_in_full(lands[t], name, chip, core)
            else:
                src, dst = _shard_of(srcs[t], name, chip ^ RELATION_XOR[j]), lands[t].at[j]
            copies.append(pltpu.make_async_remote_copy(
                src_ref=src, dst_ref=dst, send_sem=send_sems.at[3 * t + j], recv_sem=recv_sems.at[3 * t + j],
                device_id=(px, py, c), device_id_type=MESH))
        if kind == "gather":
            src, dst = srcs[t], _shard_of(lands[t], name, chip)
        else:
            src, dst = _shard_of(srcs[t], name, chip), lands[t].at[N_CHIPS - 1]
        copies.append(pltpu.make_async_copy(src, dst, local_sems.at[t]))
    return copies


def _exchange_start(kind, group, srcs, land_shapes, after):
    nw = len(GROUPS[group])

    def body(*refs):
        srcs_r, lands_r = refs[:nw], refs[nw:2 * nw]
        send_sems, recv_sems, local_sems = refs[2 * nw + 1:2 * nw + 4]
        token = refs[-1]

        def issue(chip, core):
            for cp in _group_copies(kind, group, srcs_r, lands_r, send_sems, recv_sems, local_sems, chip, core):
                cp.start()
        _on_each_device(issue)
        token[...] = jnp.zeros_like(token)

    lands = [pltpu.with_memory_space_constraint(lax.empty(s.shape, s.dtype), pltpu.HBM) for s in land_shapes]
    srcs = [pltpu.with_memory_space_constraint(a, pltpu.HBM) for a in srcs]
    thru = [pltpu.HBM(a.shape, a.dtype) for a in srcs + lands]
    outs = pl.pallas_call(
        body, name=f"{kind}_{group}_start",
        in_specs=[HBM] * (2 * nw) + [ANY],
        out_specs=[SEM, SEM, SEM] + [HBM] * (2 * nw) + [pl.BlockSpec(memory_space=pltpu.VMEM)],
        out_shape=[pltpu.SemaphoreType.DMA((3 * nw,)), pltpu.SemaphoreType.DMA((3 * nw,)),
                   pltpu.SemaphoreType.DMA((nw,))] + thru + [TOKEN],
        input_output_aliases={t: 3 + t for t in range(2 * nw)},
        compiler_params=pltpu.CompilerParams(has_side_effects=DATAFLOW),
    )(*srcs, *lands, after)
    return dict(sems=outs[0:3], srcs=outs[3:3 + nw], lands=outs[3 + nw:3 + 2 * nw], token=outs[-1])


def _exchange_wait(kind, group, started, after):
    nw = len(GROUPS[group])

    def body(*refs):
        srcs_r, lands_r = refs[:nw], refs[nw:2 * nw]
        send_sems, recv_sems, local_sems = refs[2 * nw:2 * nw + 3]
        for cp in _group_copies(kind, group, srcs_r, lands_r, send_sems, recv_sems, local_sems, 0, 0):
            cp.wait()

    srcs, lands = list(started["srcs"]), list(started["lands"])
    outs = pl.pallas_call(
        body, name=f"{kind}_{group}_wait",
        in_specs=[HBM] * (2 * nw) + [SEM, SEM, SEM, ANY],
        out_specs=[HBM] * (2 * nw),
        out_shape=[pltpu.HBM(a.shape, a.dtype) for a in srcs + lands],
        input_output_aliases={t: t for t in range(2 * nw)},
        compiler_params=pltpu.CompilerParams(has_side_effects=DATAFLOW),
    )(*srcs, *lands, *started["sems"], after)
    return dict(zip(GROUPS[group], outs[nw:]))


def _gather_start(group, shards, after):
    names = GROUPS[group]
    shapes = [jax.ShapeDtypeStruct(_with_shard_axis(n, shards[n].shape, SHARD_AXIS[n][1] * N_CHIPS), shards[n].dtype)
              for n in names]
    return _exchange_start("gather", group, [shards[n] for n in names], shapes, after)


def _scatter_start(group, grads, after):
    names = GROUPS[group]
    shapes = [jax.ShapeDtypeStruct((N_CHIPS,) + _with_shard_axis(n, grads[n].shape, SHARD_AXIS[n][1]), grads[n].dtype)
              for n in names]
    return _exchange_start("scatter", group, [grads[n] for n in names], shapes, after)


def _sibling_exchange(parts):
    n = len(parts)

    def body(*refs):
        ins, outs = refs[:n], refs[n:2 * n]
        send_sems, recv_sems = refs[2 * n:]
        sibling = (lax.axis_index("x"), lax.axis_index("y"), 1 - lax.axis_index("c"))
        copies = [pltpu.make_async_remote_copy(src_ref=ins[t], dst_ref=outs[t], send_sem=send_sems.at[t],
                                               recv_sem=recv_sems.at[t], device_id=sibling, device_id_type=MESH)
                  for t in range(n)]
        for cp in copies:
            cp.start()
        for cp in copies:
            cp.wait()

    outs = pl.pallas_call(
        body, name="sibling_exchange", in_specs=[ANY] * n, out_specs=[ANY] * n,
        out_shape=[jax.ShapeDtypeStruct(p.shape, p.dtype) for p in parts],
        scratch_shapes=[pltpu.SemaphoreType.DMA((n,)), pltpu.SemaphoreType.DMA((n,))],
        compiler_params=pltpu.CompilerParams(has_side_effects=True),
    )(*parts)
    return list(outs)


def _sibling_fill(group, fulls):
    names = GROUPS[group]
    nw = len(names)

    def body(*refs):
        ins, outs = refs[:nw], refs[nw:2 * nw]
        send_sems, recv_sems = refs[2 * nw:]
        sibling = (lax.axis_index("x"), lax.axis_index("y"), 1 - lax.axis_index("c"))

        def forward(chip, core):
            copies = []
            for t, name in enumerate(names):
                for j in range(3):
                    other = chip ^ RELATION_XOR[j]
                    copies.append(pltpu.make_async_remote_copy(
                        src_ref=_half_in_full(ins[t], name, other, core),
                        dst_ref=_half_in_full(outs[t], name, other, core),
                        send_sem=send_sems.at[3 * t + j], recv_sem=recv_sems.at[3 * t + j],
                        device_id=sibling, device_id_type=MESH))
            for cp in copies:
                cp.start()
            for cp in copies:
                cp.wait()
        _on_each_device(forward)

    arrays = [fulls[n] for n in names]
    outs = pl.pallas_call(
        body, name=f"sibling_fill_{group}", in_specs=[ANY] * nw, out_specs=[ANY] * nw,
        out_shape=[jax.ShapeDtypeStruct(a.shape, a.dtype) for a in arrays],
        scratch_shapes=[pltpu.SemaphoreType.DMA((3 * nw,)), pltpu.SemaphoreType.DMA((3 * nw,))],
        input_output_aliases={t: t for t in range(nw)},
        compiler_params=pltpu.CompilerParams(has_side_effects=True),
    )(*arrays)
    return dict(zip(names, outs))


N_DEV = 8


def _all_reduce_small(v):
    R, C = v.shape

    def body(v_ref, o_ref, slots, send_sems, recv_sems):
        x, y, c = lax.axis_index("x"), lax.axis_index("y"), lax.axis_index("c")
        me = 4 * x + 2 * y + c
        slots[me] = v_ref[...]
        copies = []
        for k in range(1, N_DEV):
            peer = me ^ k
            cp = pltpu.make_async_remote_copy(
                src_ref=v_ref, dst_ref=slots.at[me], send_sem=send_sems.at[k - 1], recv_sem=recv_sems.at[k - 1],
                device_id=(peer // 4, (peer // 2) % 2, peer % 2), device_id_type=MESH)
            cp.start()
            copies.append(cp)
        for cp in copies:
            cp.wait()
        acc = slots[0]
        for k in range(1, N_DEV):
            acc = acc + slots[k]
        o_ref[...] = acc

    return pl.pallas_call(
        body, name="all_reduce_small", out_shape=jax.ShapeDtypeStruct((R, C), F32),
        in_specs=[pl.BlockSpec(memory_space=pltpu.VMEM)], out_specs=pl.BlockSpec(memory_space=pltpu.VMEM),
        scratch_shapes=[pltpu.VMEM((N_DEV, R, C), F32), pltpu.SemaphoreType.DMA((N_DEV - 1,)),
                        pltpu.SemaphoreType.DMA((N_DEV - 1,))],
        compiler_params=pltpu.CompilerParams(has_side_effects=True),
    )(v)


def _as2d(shape):
    return (int(np.prod(shape[:-1])), shape[-1])


def _row_block(rows, cols, n_arrays):
    budget = V7X_VMEM_LIMIT // 2
    tr = rows
    while tr % 16 == 0 and 2 * n_arrays * tr * cols * 4 > budget:
        tr //= 2
    return tr


def _sum_slots(slots):
    _, R, C = slots.shape
    tr = _row_block(R, C, 5)

    def body(s_ref, o_ref):
        acc = s_ref[0].astype(F32)
        for k in range(1, N_CHIPS):
            acc = acc + s_ref[k].astype(F32)
        o_ref[...] = acc.astype(BF16)

    return pl.pallas_call(
        body, name="sum_slots", grid=(R // tr,),
        in_specs=[pl.BlockSpec((N_CHIPS, tr, C), lambda i: (0, i, 0))],
        out_specs=pl.BlockSpec((tr, C), lambda i: (i, 0)),
        out_shape=jax.ShapeDtypeStruct((R, C), BF16),
        compiler_params=_params("parallel"),
    )(slots)


def _adamw(l, w, m, v, g_a, g_b, prev):
    L, R, C = w.shape
    tr = _row_block(R, C, 9)
    c1 = 1.0 - ADAM_B1 ** ADAM_STEP
    c2 = 1.0 - ADAM_B2 ** ADAM_STEP

    def body(w_ref, m_ref, v_ref, a_ref, b_ref, *rest):
        g_ref, d_ref, nm_ref, nv_ref = rest[-4:]
        g = a_ref[...].astype(F32) + b_ref[...].astype(F32)
        nm = ADAM_B1 * m_ref[...] + (1.0 - ADAM_B1) * g
        nv = ADAM_B2 * v_ref[...] + (1.0 - ADAM_B2) * (g * g)
        g_ref[...] = g
        nm_ref[...] = nm
        nv_ref[...] = nv
        d_ref[...] = -ADAM_LR * ((nm / c1) / (jnp.sqrt(nv / c2) + ADAM_EPS) + ADAM_WD * w_ref[...])

    layer = pl.BlockSpec((None, tr, C), lambda i: (l, i, 0))
    blk = pl.BlockSpec((tr, C), lambda i: (i, 0))
    out = jax.ShapeDtypeStruct((L, R, C), F32)
    prev = [] if prev is None else list(prev)
    return pl.pallas_call(
        body, name="adamw", grid=(R // tr,), in_specs=[layer] * 3 + [blk] * 2 + [ANY] * len(prev),
        out_specs=[layer] * 4, out_shape=[out] * 4,
        input_output_aliases={5 + k: k for k in range(len(prev))},
        compiler_params=_params("parallel"),
    )(w, m, v, g_a, g_b, *prev)


SMALL_ROWS = 16


def _pack_small(pool_scale, g_mix, g_ffn, g_final, attn_sink, rel_bias):
    tail = jnp.concatenate([attn_sink.reshape(-1), rel_bias.reshape(-1)])
    tail = jnp.pad(tail, (0, D_MODEL - tail.shape[0])).reshape(1, D_MODEL)
    rows = jnp.concatenate([pool_scale, g_mix, g_ffn, g_final.reshape(1, D_MODEL), tail], axis=0)
    return jnp.pad(rows, ((0, SMALL_ROWS - rows.shape[0]), (0, 0)))


def _unpack_small(packed):
    n_sink = DEPTH * N_HEADS
    return dict(pool_scale=packed[0:4], g_mix=packed[4:8], g_ffn=packed[8:12], g_final=packed[12],
                attn_sink=packed[13, 0:n_sink].reshape(DEPTH, N_HEADS),
                rel_bias=packed[13, n_sink:n_sink + N_BUCKETS * N_HEADS].reshape(N_BUCKETS, N_HEADS))


def _group_shards(l, group, masters):
    out = {}
    for n in GROUPS[group]:
        w = masters[n][l]
        out[n] = jnp.pad(w.reshape(3, -1), ((0, 5), (0, 0))) if n == "conv_w" else w.astype(BF16)
    return out


def kernel(x, w_in, conv_w, w_a_out, w_pool, pool_scale, w_attn_out, attn_sink, w_o, g_mix, g_ffn, w_gu, w_down, rel_bias, g_final, loss_target, m_w_in, m_conv_w, m_w_a_out, m_w_pool, m_pool_scale, m_w_attn_out, m_attn_sink, m_w_o, m_g_mix, m_g_ffn, m_w_gu, m_w_down, m_rel_bias, m_g_final, v_w_in, v_conv_w, v_w_a_out, v_w_pool, v_pool_scale, v_w_attn_out, v_attn_sink, v_w_o, v_g_mix, v_g_ffn, v_w_gu, v_w_down, v_rel_bias, v_g_final):
    big = dict(w_in=(w_in, m_w_in, v_w_in), conv_w=(conv_w, m_conv_w, v_conv_w), w_a_out=(w_a_out, m_w_a_out, v_w_a_out),
               w_pool=(w_pool, m_w_pool, v_w_pool), w_attn_out=(w_attn_out, m_w_attn_out, v_w_attn_out),
               w_o=(w_o, m_w_o, v_w_o), w_gu=(w_gu, m_w_gu, v_w_gu), w_down=(w_down, m_w_down, v_w_down))

    big3 = {n: tuple(a.reshape((DEPTH,) + _as2d(a.shape[1:])) for a in big[n]) for n in WEIGHT_NAMES}
    masters = {n: big[n][0] for n in WEIGHT_NAMES}

    gathers = {(0, "mix"): _gather_start("mix", _group_shards(0, "mix", masters), rel_bias)}
    newest = {"token": gathers[0, "mix"]["token"]}
    masters = dict(zip(WEIGHT_NAMES, lax.optimization_barrier(
        (tuple(masters[n] for n in WEIGHT_NAMES), newest["token"]))[0]))

    def weights_of(l, group, a):
        W = _sibling_fill(group, _exchange_wait("gather", group, gathers.pop((l, group)), a))
        if group == "mix":
            gathers[l, "ffn"] = _gather_start("ffn", _group_shards(l, "ffn", masters), W["w_in"])
            newest["token"] = gathers[l, "ffn"]["token"]
            if l + 1 < DEPTH:
                gathers[l + 1, "mix"] = _gather_start("mix", _group_shards(l + 1, "mix", masters), newest["token"])
                newest["token"] = gathers[l + 1, "mix"]["token"]
        return W, newest["token"]

    results = {n: None for n in WEIGHT_NAMES}
    scatters = {}

    def finish(l, group, after):
        slots = _exchange_wait("scatter", group, scatters.pop((l, group)), after)
        names = GROUPS[group]
        parts = [_sum_slots(slots[n].reshape((N_CHIPS,) + _as2d(slots[n].shape[1:]))) for n in names]
        others = _sibling_exchange(parts)
        for n, mine, other in zip(names, parts, others):
            if n == "conv_w":
                mine, other = mine[0:3], other[0:3]
            results[n] = _adamw(l, *big3[n], mine, other, results[n])

    def grads_to(l, group, wgrads, a):
        scatters[l, group] = _scatter_start(group, wgrads, a)
        token = scatters[l, group]["token"]
        if group == "mix" and l + 1 < DEPTH:
            finish(l + 1, "ffn", token)
            finish(l + 1, "mix", token)
        return token

    loss, grad_x, smalls = _local_step(x[0], loss_target[0], weights_of, grads_to, pool_scale, attn_sink, g_mix, g_ffn,
                                       rel_bias, g_final)
    finish(0, "ffn", grad_x)
    finish(0, "mix", results["w_down"][0])
    stacked = {n: [o.reshape(big[n][0].shape) for o in results[n]] for n in WEIGHT_NAMES}

    g_small = _all_reduce_small(_pack_small(smalls["pool_scale"], smalls["g_mix"], smalls["g_ffn"], smalls["g_final"],
                                            smalls["attn_sink"], smalls["rel_bias"]))
    w_small = _pack_small(pool_scale, g_mix, g_ffn, g_final, attn_sink, rel_bias)
    m_small = _pack_small(m_pool_scale, m_g_mix, m_g_ffn, m_g_final, m_attn_sink, m_rel_bias)
    v_small = _pack_small(v_pool_scale, v_g_mix, v_g_ffn, v_g_final, v_attn_sink, v_rel_bias)
    small_out = [_unpack_small(o[0]) for o in
                 _adamw(0, w_small[None], m_small[None], v_small[None], g_small, jnp.zeros_like(g_small), None)]

    total_loss = lax.psum(loss, ("x", "y", "c"))

    order = ("w_in", "conv_w", "w_a_out", "w_pool", "pool_scale", "w_attn_out", "attn_sink", "w_o", "g_mix", "g_ffn",
             "w_gu", "w_down", "rel_bias", "g_final")
    outs = [total_loss, grad_x[None]]
    for k in range(4):
        for n in order:
            outs.append(stacked[n][k] if n in stacked else small_out[k][n])
    return tuple(outs)
```

```python
import functools
import math

import numpy as np
import jax
import jax.numpy as jnp
from jax import lax
from jax.experimental import pallas as pl
from jax.experimental.pallas import tpu as pltpu

F32 = jnp.float32
BF16 = jnp.bfloat16

D_MODEL = 1024
DEPTH = 4
N_HEADS = 16
N_KV_HEADS = 4
HEAD_DIM = 64
GQA = N_HEADS // N_KV_HEADS
WINDOW = 128
BLOCK = 128
N_BUCKETS = 32
MAX_DISTANCE = 128
POOL_GROUPS = 4
POOL_CG = D_MODEL // POOL_GROUPS
POOL_WINDOWS = (2, 4, 8, 16)
D_FF = 2816
IN_TOTAL = 8704
OFF_B, OFF_C, OFF_X, OFF_U, OFF_Q, OFF_K, OFF_V, OFF_GA, OFF_GP, OFF_GT = (
    0, 1024, 2048, 3072, 4096, 5120, 5376, 5632, 6656, 7680)
EPS = 1e-6
NEG_INF = -1e30
SM_SCALE = HEAD_DIM ** -0.5

ADAM_LR = 0.001
ADAM_B1 = 0.9
ADAM_B2 = 0.999
ADAM_EPS = 1e-08
ADAM_WD = 0.01
ADAM_STEP = 10

N_CHIPS = 4
HALO = 8
V7X_VMEM_LIMIT = 56 * 1024 * 1024
MESH = pl.DeviceIdType.MESH
ANY = pl.BlockSpec(memory_space=pl.ANY)


def _params(*sem):
    return pltpu.CompilerParams(dimension_semantics=tuple(sem) if sem else None,
                                vmem_limit_bytes=V7X_VMEM_LIMIT)


def _tile(n, pref):
    t = min(pref, n)
    while n % t or t % 128:
        t -= 128
    return t


def _nt(a, b):
    return lax.dot_general(a, b, (((1,), (1,)), ((), ())), preferred_element_type=F32)


def _tn(a, b):
    return lax.dot_general(a, b, (((0,), (0,)), ((), ())), preferred_element_type=F32)


def _nn(a, b):
    return jnp.dot(a, b, preferred_element_type=F32)


def _sigmoid(v):
    return 1.0 / (1.0 + jnp.exp(-v))


def _norm_matmul(x, g, w, name, token):
    S, Dm = x.shape
    N = w.shape[1]
    tm, tn = _tile(S, 1024), _tile(N, N // 4)

    def body(x_ref, g_ref, w_ref, token_ref, h_ref, o_ref):
        @pl.when(pl.program_id(1) == 0)
        def _():
            xv = x_ref[...]
            r = lax.rsqrt(jnp.mean(xv * xv, axis=-1, keepdims=True) + EPS)
            h_ref[...] = (xv * r * g_ref[...]).astype(BF16)
        o_ref[...] = _nn(h_ref[...], w_ref[...]).astype(BF16)

    return pl.pallas_call(
        body, name=name, grid=(S // tm, N // tn),
        in_specs=[pl.BlockSpec((tm, Dm), lambda i, j: (i, 0)),
                  pl.BlockSpec((1, Dm), lambda i, j: (0, 0)),
                  pl.BlockSpec((Dm, tn), lambda i, j: (0, j)), ANY],
        out_specs=[pl.BlockSpec((tm, Dm), lambda i, j: (i, 0)),
                   pl.BlockSpec((tm, tn), lambda i, j: (i, j))],
        out_shape=[jax.ShapeDtypeStruct((S, Dm), BF16), jax.ShapeDtypeStruct((S, N), BF16)],
        compiler_params=_params("parallel", "arbitrary"),
    )(x, g, w, token)


CB = 128
CBW = 256


def _fill_padded(pad_ref, v, S):
    z = jnp.zeros((HALO, v.shape[1]), F32)
    pad_ref[pl.ds(0, HALO), :] = z
    pad_ref[pl.ds(S + HALO, HALO), :] = z
    pad_ref[pl.ds(HALO, S), :] = v


def _shifted(pad_ref, off, S):
    return pad_ref[pl.ds(HALO + off, S), :]


def _conv_fwd(proj, cw8):
    S = proj.shape[0]
    nblk = D_MODEL // CBW

    def body(b_ref, c_ref, x_ref, w_ref, o_ref, pad):
        u = c_ref[...].astype(F32) * x_ref[...].astype(F32)
        _fill_padded(pad, u, S)
        cv = w_ref[0:1, :] * _shifted(pad, -1, S) + w_ref[1:2, :] * u + w_ref[2:3, :] * _shifted(pad, 1, S)
        o_ref[...] = (b_ref[...].astype(F32) * cv).astype(BF16)

    col = lambda base: pl.BlockSpec((S, CBW), lambda j: (0, base // CBW + j))
    return pl.pallas_call(
        body, name="conv_fwd", grid=(nblk,),
        in_specs=[col(OFF_B), col(OFF_C), col(OFF_X), pl.BlockSpec((8, CBW), lambda j: (0, j))],
        out_specs=pl.BlockSpec((S, CBW), lambda j: (0, j)),
        out_shape=jax.ShapeDtypeStruct((S, D_MODEL), BF16),
        scratch_shapes=[pltpu.VMEM((S + 2 * HALO, CBW), F32)],
        compiler_params=_params("parallel"),
    )(proj, proj, proj, cw8)


def _pool_count(S, lo, hi):
    t = lax.broadcasted_iota(jnp.int32, (S, CBW), 0)
    return (jnp.minimum(t + hi, S - 1) - jnp.maximum(t - lo, 0) + 1).astype(F32)


def _pool_fwd(proj):
    S = proj.shape[0]
    nblk = D_MODEL // CBW
    per_group = POOL_CG // CBW

    def body(u_ref, o_ref, pad):
        u = u_ref[...].astype(F32)
        _fill_padded(pad, u, S)
        grp = pl.program_id(0) // per_group
        for gi, w in enumerate(POOL_WINDOWS):
            @pl.when(grp == gi)
            def _(w=w):
                lo, hi = w // 2, w - 1 - w // 2
                acc = _shifted(pad, -lo, S)
                for off in range(-lo + 1, hi + 1):
                    acc = acc + _shifted(pad, off, S)
                o_ref[...] = (acc / _pool_count(S, lo, hi) - u).astype(BF16)

    return pl.pallas_call(
        body, name="pool_fwd", grid=(nblk,),
        in_specs=[pl.BlockSpec((S, CBW), lambda j: (0, OFF_U // CBW + j))],
        out_specs=pl.BlockSpec((S, CBW), lambda j: (0, j)),
        out_shape=jax.ShapeDtypeStruct((S, D_MODEL), BF16),
        scratch_shapes=[pltpu.VMEM((S + 2 * HALO, CBW), F32)],
        compiler_params=_params("parallel"),
    )(proj)


def _attn_specs(S):
    nb = S // BLOCK
    kcol, vcol = OFF_K // (N_KV_HEADS * HEAD_DIM), OFF_V // (N_KV_HEADS * HEAD_DIM)
    kvw = N_KV_HEADS * HEAD_DIM
    prev = lambda i: jnp.maximum(i - 1, 0)
    nxt = lambda i: jnp.minimum(i + 1, nb - 1)
    return [
        pl.BlockSpec((BLOCK, D_MODEL), lambda i: (i, OFF_Q // D_MODEL)),
        pl.BlockSpec((BLOCK, kvw), lambda i: (prev(i), kcol)),
        pl.BlockSpec((BLOCK, kvw), lambda i: (i, kcol)),
        pl.BlockSpec((BLOCK, kvw), lambda i: (nxt(i), kcol)),
        pl.BlockSpec((BLOCK, kvw), lambda i: (prev(i), vcol)),
        pl.BlockSpec((BLOCK, kvw), lambda i: (i, vcol)),
        pl.BlockSpec((BLOCK, kvw), lambda i: (nxt(i), vcol)),
    ]


def _heads_rows(ref_or_val, hk):
    return jnp.concatenate(
        [ref_or_val[:, (GQA * hk + g) * HEAD_DIM:(GQA * hk + g + 1) * HEAD_DIM] for g in range(GQA)], axis=0)


def _kv_rows(p_ref, c_ref, n_ref, hk):
    sl = slice(hk * HEAD_DIM, (hk + 1) * HEAD_DIM)
    return jnp.concatenate([p_ref[:, sl], c_ref[:, sl], n_ref[:, sl]], axis=0)


def _bias_cols(bias_ref, hk):
    return jnp.concatenate([bias_ref[GQA * hk + g] for g in range(GQA)], axis=1)


def _softmax_keys_on_rows(q4s, kc, bias_blk, sink_row):
    s = _nt(kc, q4s) + bias_blk
    m = jnp.maximum(jnp.max(s, axis=0, keepdims=True), sink_row)
    p = jnp.exp(s - m)
    e_sink = jnp.exp(sink_row - m)
    inv = 1.0 / (jnp.sum(p, axis=0, keepdims=True) + e_sink)
    return p * inv, e_sink * inv


TAB = (N_HEADS, 3 * BLOCK, BLOCK)
TAB_FLAT = 3 * BLOCK * BLOCK


def _bias_spec(nb):
    return pl.BlockSpec((None,) + TAB, lambda i: (jnp.where(i == 0, 0, jnp.where(i == nb - 1, 2, 1)), 0, 0, 0))


def _attn_fwd(proj, bias_tabs, sink_rows):
    S = proj.shape[0]
    nb = S // BLOCK
    assert nb >= 2

    def body(q_ref, kp, kc_, kn, vp, vc_, vn, bias_ref, sink_ref, o_ref):
        outs = []
        for hk in range(N_KV_HEADS):
            q4s = _heads_rows(q_ref, hk) * SM_SCALE
            kc = _kv_rows(kp, kc_, kn, hk)
            vc = _kv_rows(vp, vc_, vn, hk)
            pn, _ = _softmax_keys_on_rows(q4s, kc, _bias_cols(bias_ref, hk), sink_ref[hk:hk + 1, :])
            o4 = _tn(pn.astype(BF16), vc)
            outs += [o4[g * BLOCK:(g + 1) * BLOCK, :] for g in range(GQA)]
        o_ref[...] = jnp.concatenate(outs, axis=1).astype(BF16)

    return pl.pallas_call(
        body, name="attn_fwd", grid=(nb,),
        in_specs=_attn_specs(S) + [_bias_spec(nb), pl.BlockSpec((N_KV_HEADS, GQA * BLOCK), lambda i: (0, 0))],
        out_specs=pl.BlockSpec((BLOCK, D_MODEL), lambda i: (i, 0)),
        out_shape=jax.ShapeDtypeStruct((S, D_MODEL), BF16),
        compiler_params=_params("parallel"),
    )(*([proj] * 7), bias_tabs, sink_rows)


GATE_HALF = D_MODEL // 2


def _gate_specs(tm):
    return [pl.BlockSpec((tm, GATE_HALF), lambda i, c=off // GATE_HALF + k: (i, c))
            for off in (OFF_GA, OFF_GP, OFF_GT) for k in (0, 1)]


def _gate(lo_ref, hi_ref):
    return _sigmoid(jnp.concatenate([lo_ref[...], hi_ref[...]], axis=1).astype(F32))


def _pool_mix(p, wp):
    return jnp.concatenate(
        [_nn(p[:, g * POOL_CG:(g + 1) * POOL_CG], wp[g]) for g in range(POOL_GROUPS)], axis=1)


def _mix_fwd(za, p, att, proj, x, wa, wp, ps, wt, wo):
    S = x.shape[0]
    tm = _tile(S, 256)

    def body(za_ref, p_ref, att_ref, ga0, ga1, gp0, gp1, gt0, gt1, x_ref, wa_ref, wp_ref, ps_ref, wt_ref, wo_ref,
             ya_ref, yp_ref, yt_ref, mg_ref, x2_ref):
        ya = _nn(za_ref[...], wa_ref[...])
        ypr = _pool_mix(p_ref[...], wp_ref)
        yt = _nn(att_ref[...], wt_ref[...])
        merged = _gate(ga0, ga1) * ya + _gate(gp0, gp1) * (ypr * ps_ref[...]) + _gate(gt0, gt1) * yt
        mb = merged.astype(BF16)
        ya_ref[...] = ya.astype(BF16)
        yp_ref[...] = ypr.astype(BF16)
        yt_ref[...] = yt.astype(BF16)
        mg_ref[...] = mb
        x2_ref[...] = x_ref[...] + _nn(mb, wo_ref[...])

    row = lambda c=0: pl.BlockSpec((tm, D_MODEL), lambda i: (i, c))
    whole = lambda a: pl.BlockSpec(a.shape, lambda i: (0,) * a.ndim)
    act = jax.ShapeDtypeStruct((S, D_MODEL), BF16)
    return pl.pallas_call(
        body, name="mix_fwd", grid=(S // tm,),
        in_specs=[row(), row(), row()] + _gate_specs(tm) + [row(), whole(wa), whole(wp), whole(ps), whole(wt), whole(wo)],
        out_specs=[row(), row(), row(), row(), row()],
        out_shape=[act, act, act, act, jax.ShapeDtypeStruct((S, D_MODEL), F32)],
        compiler_params=_params("parallel"),
    )(za, p, att, *([proj] * 6), x, wa, wp, ps, wt, wo)


def _ffn_fwd(gu, x2, wd):
    S = x2.shape[0]
    tm = _tile(S, 256)

    def body(g_ref, u_ref, x_ref, w_ref, a_ref, o_ref):
        g = g_ref[...].astype(F32)
        a = (g * _sigmoid(g) * u_ref[...].astype(F32)).astype(BF16)
        a_ref[...] = a
        o_ref[...] = x_ref[...] + _nn(a, w_ref[...])

    return pl.pallas_call(
        body, name="ffn_fwd", grid=(S // tm,),
        in_specs=[pl.BlockSpec((tm, D_FF), lambda i: (i, 0)), pl.BlockSpec((tm, D_FF), lambda i: (i, 1)),
                  pl.BlockSpec((tm, D_MODEL), lambda i: (i, 0)), pl.BlockSpec((D_FF, D_MODEL), lambda i: (0, 0))],
        out_specs=[pl.BlockSpec((tm, D_FF), lambda i: (i, 0)), pl.BlockSpec((tm, D_MODEL), lambda i: (i, 0))],
        out_shape=[jax.ShapeDtypeStruct((S, D_FF), BF16), jax.ShapeDtypeStruct((S, D_MODEL), F32)],
        compiler_params=_params("parallel"),
    )(gu, gu, x2, wd)


def _loss_bwd(x, g, tgt):
    S, Dm = x.shape
    tm = _tile(S, 512)

    def body(x_ref, g_ref, t_ref, l_ref, dx_ref, dg_ref):
        @pl.when(pl.program_id(0) == 0)
        def _():
            l_ref[...] = jnp.zeros_like(l_ref)
            dg_ref[...] = jnp.zeros_like(dg_ref)
        xv, gv = x_ref[...], g_ref[...]
        r = lax.rsqrt(jnp.mean(xv * xv, axis=-1, keepdims=True) + EPS)
        n = xv * r
        err = n * gv - t_ref[...]
        l_ref[...] += 0.5 * jnp.sum(jnp.mean(err * err, axis=-1, keepdims=True), axis=0, keepdims=True)
        dy = err * (1.0 / Dm)
        dn = dy * gv
        dx_ref[...] = r * (dn - n * jnp.mean(dn * n, axis=-1, keepdims=True))
        dg_ref[...] += jnp.sum(dy * n, axis=0, keepdims=True)

    return pl.pallas_call(
        body, name="loss_bwd", grid=(S // tm,),
        in_specs=[pl.BlockSpec((tm, Dm), lambda i: (i, 0)), pl.BlockSpec((1, Dm), lambda i: (0, 0)),
                  pl.BlockSpec((tm, Dm), lambda i: (i, 0))],
        out_specs=[pl.BlockSpec((8, 128), lambda i: (0, 0)), pl.BlockSpec((tm, Dm), lambda i: (i, 0)),
                   pl.BlockSpec((1, Dm), lambda i: (0, 0))],
        out_shape=[jax.ShapeDtypeStruct((8, 128), F32), jax.ShapeDtypeStruct((S, Dm), F32),
                   jax.ShapeDtypeStruct((1, Dm), F32)],
        compiler_params=_params("arbitrary"),
    )(x, g, tgt)


def _ffn_bwd(dx3, gu, wd, token):
    S = dx3.shape[0]
    tm = _tile(S, 256)

    def body(d_ref, g_ref, u_ref, w_ref, token_ref, o_ref):
        dact = _nt(d_ref[...].astype(BF16), w_ref[...])
        g, u = g_ref[...].astype(F32), u_ref[...].astype(F32)
        sg = _sigmoid(g)
        o_ref[:, 0:D_FF] = (dact * u * (sg * (1.0 + g * (1.0 - sg)))).astype(BF16)
        o_ref[:, D_FF:2 * D_FF] = (dact * (g * sg)).astype(BF16)

    return pl.pallas_call(
        body, name="ffn_bwd", grid=(S // tm,),
        in_specs=[pl.BlockSpec((tm, D_MODEL), lambda i: (i, 0)),
                  pl.BlockSpec((tm, D_FF), lambda i: (i, 0)), pl.BlockSpec((tm, D_FF), lambda i: (i, 1)),
                  pl.BlockSpec((D_FF, D_MODEL), lambda i: (0, 0)), ANY],
        out_specs=pl.BlockSpec((tm, 2 * D_FF), lambda i: (i, 0)),
        out_shape=jax.ShapeDtypeStruct((S, 2 * D_FF), BF16),
        compiler_params=_params("parallel"),
    )(dx3, gu, gu, wd, token)


def _wgrad(a, b, name, tk=512, tn=512, out_dtype=BF16, token=None):
    S, K = a.shape
    N = b.shape[1]
    tk, tn, ts = _tile(K, tk), _tile(N, tn), _tile(S, 1024)
    n_s = S // ts
    extra = [] if token is None else [token]

    def body(a_ref, b_ref, *rest):
        o_ref, acc = rest[-2:]
        s = pl.program_id(2)

        @pl.when(s == 0)
        def _():
            acc[...] = jnp.zeros_like(acc)
        acc[...] += _tn(a_ref[...].astype(BF16), b_ref[...].astype(BF16))

        @pl.when(s == n_s - 1)
        def _():
            o_ref[...] = acc[...].astype(out_dtype)

    return pl.pallas_call(
        body, name=name, grid=(K // tk, N // tn, n_s),
        in_specs=[pl.BlockSpec((ts, tk), lambda k, n, s: (s, k)), pl.BlockSpec((ts, tn), lambda k, n, s: (s, n))]
        + [ANY] * len(extra),
        out_specs=pl.BlockSpec((tk, tn), lambda k, n, s: (k, n)),
        out_shape=jax.ShapeDtypeStruct((K, N), out_dtype),
        scratch_shapes=[pltpu.VMEM((tk, tn), F32)],
        compiler_params=_params("parallel", "parallel", "arbitrary"),
    )(a, b, *extra)


def _wgrad_pool(p, dyps):
    S = p.shape[0]
    ts = _tile(S, 4096)
    n_s = S // ts

    def body(a_ref, b_ref, o_ref, acc):
        s = pl.program_id(1)

        @pl.when(s == 0)
        def _():
            acc[...] = jnp.zeros_like(acc)
        acc[...] += _tn(a_ref[...], b_ref[...])

        @pl.when(s == n_s - 1)
        def _():
            o_ref[...] = acc[...].astype(BF16)

    return pl.pallas_call(
        body, name="wgrad_pool", grid=(POOL_GROUPS, n_s),
        in_specs=[pl.BlockSpec((ts, POOL_CG), lambda g, s: (s, g)), pl.BlockSpec((ts, POOL_CG), lambda g, s: (s, g))],
        out_specs=pl.BlockSpec((None, POOL_CG, POOL_CG), lambda g, s: (g, 0, 0)),
        out_shape=jax.ShapeDtypeStruct((POOL_GROUPS, POOL_CG, POOL_CG), BF16),
        scratch_shapes=[pltpu.VMEM((POOL_CG, POOL_CG), F32)],
        compiler_params=_params("parallel", "arbitrary"),
    )(p, dyps)


def _dgrad_norm_bwd(dy, w, x, g, dres, name, tk, token=None):
    S, K = dy.shape
    Dm = x.shape[1]
    tm, tk = _tile(S, 1024), _tile(K, tk)
    n_k = K // tk
    extra = [] if token is None else [token]

    def body(dy_ref, w_ref, x_ref, g_ref, r_ref, *rest):
        dx_ref, dg_ref, acc = rest[-3:]
        i, k = pl.program_id(0), pl.program_id(1)

        @pl.when((i == 0) & (k == 0))
        def _():
            dg_ref[...] = jnp.zeros_like(dg_ref)

        @pl.when(k == 0)
        def _():
            acc[...] = jnp.zeros_like(acc)
        acc[...] += _nt(dy_ref[...], w_ref[...])

        @pl.when(k == n_k - 1)
        def _():
            dh, xv = acc[...], x_ref[...]
            r = lax.rsqrt(jnp.mean(xv * xv, axis=-1, keepdims=True) + EPS)
            n = xv * r
            dn = dh * g_ref[...]
            dx_ref[...] = r_ref[...] + r * (dn - n * jnp.mean(dn * n, axis=-1, keepdims=True))
            dg_ref[...] += jnp.sum(dh * n, axis=0, keepdims=True)

    rowblk = pl.BlockSpec((tm, Dm), lambda i, k: (i, 0))
    vec = pl.BlockSpec((1, Dm), lambda i, k: (0, 0))
    return pl.pallas_call(
        body, name=name, grid=(S // tm, n_k),
        in_specs=[pl.BlockSpec((tm, tk), lambda i, k: (i, k)), pl.BlockSpec((Dm, tk), lambda i, k: (0, k)),
                  rowblk, vec, rowblk] + [ANY] * len(extra),
        out_specs=[rowblk, vec],
        out_shape=[jax.ShapeDtypeStruct((S, Dm), F32), jax.ShapeDtypeStruct((1, Dm), F32)],
        scratch_shapes=[pltpu.VMEM((tm, Dm), F32)],
        compiler_params=_params("arbitrary", "arbitrary"),
    )(dy, w, x, g, dres, *extra)


def _mix_bwd(dx2, ya, ypr, yt, proj, ps, wa, wp, wt, wo, token):
    S = dx2.shape[0]
    tm = _tile(S, 256)

    n_tiles = S // tm

    def body(dx_ref, ya_ref, yp_ref, yt_ref, ga0, ga1, gp0, gp1, gt0, gt1, ps_ref, wa_ref, wp_ref, wt_ref, wo_ref,
             token_ref, dya_ref, dyt_ref, dyps_ref, dza_ref, datt_ref, dp_ref, dproj_hbm, dps_ref, dgates, sem):
        i = pl.program_id(0)
        to_dproj = pltpu.make_async_copy(
            dgates, dproj_hbm.at[pl.ds(pl.multiple_of(i * tm, tm), tm), pl.ds(OFF_GA, 3 * D_MODEL)], sem)

        @pl.when(i == 0)
        def _():
            dps_ref[...] = jnp.zeros_like(dps_ref)
        dm = _nt(dx_ref[...].astype(BF16), wo_ref[...])
        sa, sp, st = _gate(ga0, ga1), _gate(gp0, gp1), _gate(gt0, gt1)
        psv = ps_ref[...]
        ypr_v = yp_ref[...].astype(F32)
        dya = (sa * dm).astype(BF16)
        dyt = (st * dm).astype(BF16)
        dyp = sp * dm
        dyps = (dyp * psv).astype(BF16)
        dya_ref[...] = dya
        dyt_ref[...] = dyt
        dyps_ref[...] = dyps
        dg = jnp.concatenate(
            [dm * ya_ref[...].astype(F32) * (sa * (1.0 - sa)), dm * (ypr_v * psv) * (sp * (1.0 - sp)),
             dm * yt_ref[...].astype(F32) * (st * (1.0 - st))], axis=1).astype(BF16)

        @pl.when(i > 0)
        def _():
            to_dproj.wait()
        dgates[...] = dg
        to_dproj.start()
        dps_ref[...] += jnp.sum(dyp * ypr_v, axis=0, keepdims=True)
        dza_ref[...] = _nt(dya, wa_ref[...]).astype(BF16)
        datt_ref[...] = _nt(dyt, wt_ref[...]).astype(BF16)
        dp_ref[...] = jnp.concatenate(
            [_nt(dyps[:, g * POOL_CG:(g + 1) * POOL_CG], wp_ref[g]) for g in range(POOL_GROUPS)], axis=1).astype(BF16)

        @pl.when(i == n_tiles - 1)
        def _():
            to_dproj.wait()

    row = lambda c=0: pl.BlockSpec((tm, D_MODEL), lambda i: (i, c))
    whole = lambda a: pl.BlockSpec(a.shape, lambda i: (0,) * a.ndim)
    act = jax.ShapeDtypeStruct((S, D_MODEL), BF16)
    return pl.pallas_call(
        body, name="mix_bwd", grid=(n_tiles,),
        in_specs=[row(), row(), row(), row()] + _gate_specs(tm)
        + [whole(ps), whole(wa), whole(wp), whole(wt), whole(wo), ANY],
        out_specs=[row()] * 6 + [ANY, pl.BlockSpec((1, D_MODEL), lambda i: (0, 0))],
        out_shape=[act] * 6 + [jax.ShapeDtypeStruct((S, IN_TOTAL), BF16), jax.ShapeDtypeStruct((1, D_MODEL), F32)],
        scratch_shapes=[pltpu.VMEM((tm, 3 * D_MODEL), BF16), pltpu.SemaphoreType.DMA],
        compiler_params=_params("arbitrary"),
    )(dx2, ya, ypr, yt, *([proj] * 6), ps, wa, wp, wt, wo, token)


def _conv_bwd(dza, proj, cw8, dproj):
    S = proj.shape[0]
    nblk = D_MODEL // CB

    def body(d_ref, b_ref, c_ref, x_ref, w_ref, dproj_in, dproj_hbm, dw_ref, pad_u, pad_d, parts, sems):
        cb = pl.program_id(0)
        to_dproj = [pltpu.make_async_copy(
            parts.at[k], dproj_hbm.at[:, pl.ds(pl.multiple_of(off + cb * CB, CB), CB)], sems.at[k])
            for k, off in enumerate((OFF_B, OFF_C, OFF_X))]
        c, xa = c_ref[...].astype(F32), x_ref[...].astype(F32)
        u = c * xa
        _fill_padded(pad_u, u, S)
        u_prev, u_next = _shifted(pad_u, -1, S), _shifted(pad_u, 1, S)
        cv = w_ref[0:1, :] * u_prev + w_ref[1:2, :] * u + w_ref[2:3, :] * u_next
        dza_v = d_ref[...].astype(F32)
        dcv = dza_v * b_ref[...].astype(F32)
        _fill_padded(pad_d, dcv, S)
        du = w_ref[0:1, :] * _shifted(pad_d, 1, S) + w_ref[1:2, :] * dcv + w_ref[2:3, :] * _shifted(pad_d, -1, S)

        @pl.when(cb > 0)
        def _():
            for cp in to_dproj:
                cp.wait()
        parts[0] = (dza_v * cv).astype(BF16)
        parts[1] = (du * xa).astype(BF16)
        parts[2] = (du * c).astype(BF16)
        for cp in to_dproj:
            cp.start()
        dw_ref[...] = jnp.concatenate(
            [jnp.sum(dcv * u_prev, axis=0, keepdims=True), jnp.sum(dcv * u, axis=0, keepdims=True),
             jnp.sum(dcv * u_next, axis=0, keepdims=True), jnp.zeros((5, CB), F32)], axis=0)

        @pl.when(cb == nblk - 1)
        def _():
            for cp in to_dproj:
                cp.wait()

    col = lambda base: pl.BlockSpec((S, CB), lambda cb: (0, base // CB + cb))
    taps = pl.BlockSpec((8, CB), lambda cb: (0, cb))
    return pl.pallas_call(
        body, name="conv_bwd", grid=(nblk,),
        in_specs=[col(0), col(OFF_B), col(OFF_C), col(OFF_X), taps, ANY],
        out_specs=[ANY, taps],
        out_shape=[jax.ShapeDtypeStruct(dproj.shape, dproj.dtype), jax.ShapeDtypeStruct((8, D_MODEL), F32)],
        scratch_shapes=[pltpu.VMEM((S + 2 * HALO, CB), F32), pltpu.VMEM((S + 2 * HALO, CB), F32),
                        pltpu.VMEM((3, S, CB), BF16), pltpu.SemaphoreType.DMA((3,))],
        input_output_aliases={5: 0},
        compiler_params=_params("arbitrary"),
    )(dza, proj, proj, proj, cw8, dproj)


def _pool_bwd(dp, dproj):
    S = dp.shape[0]
    nblk = D_MODEL // CBW
    per_group = POOL_CG // CBW

    def body(d_ref, dproj_in, o_ref, pad):
        d = d_ref[...].astype(F32)
        grp = pl.program_id(0) // per_group
        for gi, w in enumerate(POOL_WINDOWS):
            @pl.when(grp == gi)
            def _(w=w):
                lo, hi = w // 2, w - 1 - w // 2
                _fill_padded(pad, d / _pool_count(S, lo, hi), S)
                acc = _shifted(pad, -hi, S)
                for off in range(-hi + 1, lo + 1):
                    acc = acc + _shifted(pad, off, S)
                o_ref[...] = (acc - d).astype(BF16)

    return pl.pallas_call(
        body, name="pool_bwd", grid=(nblk,),
        in_specs=[pl.BlockSpec((S, CBW), lambda j: (0, j)), ANY],
        out_specs=pl.BlockSpec((S, CBW), lambda j: (0, OFF_U // CBW + j)),
        out_shape=jax.ShapeDtypeStruct(dproj.shape, dproj.dtype),
        scratch_shapes=[pltpu.VMEM((S + 2 * HALO, CBW), F32)],
        input_output_aliases={1: 0},
        compiler_params=_params("parallel"),
    )(dp, dproj)


def _attn_bwd(proj, datt, bias_tabs, sink_rows, dbias_in, dproj):
    S = proj.shape[0]
    nb = S // BLOCK
    kvw = N_KV_HEADS * HEAD_DIM

    def body(q_ref, kp, kc_, kn, vp, vc_, vn, do_ref, bias_ref, sink_ref, dbin_ref, dproj_in,
             dq_ref, dk_ref, dv_ref, db_ref, ds_ref):
        i = pl.program_id(0)

        @pl.when(i == 0)
        def _():
            dk_ref[...] = jnp.zeros_like(dk_ref)
            dv_ref[...] = jnp.zeros_like(dv_ref)
            db_ref[...] = dbin_ref[...]
            ds_ref[...] = jnp.zeros_like(ds_ref)
        dqs, dks, dvs = [], [], []
        for hk in range(N_KV_HEADS):
            q4s = _heads_rows(q_ref, hk) * SM_SCALE
            do4 = _heads_rows(do_ref, hk)
            kc = _kv_rows(kp, kc_, kn, hk)
            vc = _kv_rows(vp, vc_, vn, hk)
            pn, p_sink = _softmax_keys_on_rows(q4s, kc, _bias_cols(bias_ref, hk), sink_ref[hk:hk + 1, :])
            dpm = _nt(vc, do4)
            delta = jnp.sum(pn * dpm, axis=0, keepdims=True)
            dsc = pn * (dpm - delta)
            for g in range(GQA):
                db_ref[GQA * hk + g] += dsc[:, g * BLOCK:(g + 1) * BLOCK]
            ds_ref[hk:hk + 1, :] += -p_sink * delta
            dsb = dsc.astype(BF16)
            dq4 = _tn(dsb, kc) * SM_SCALE
            dqs += [dq4[g * BLOCK:(g + 1) * BLOCK, :] for g in range(GQA)]
            dks.append(_nn(dsb, q4s))
            dvs.append(_nn(pn.astype(BF16), do4))
        dq_ref[...] = jnp.concatenate(dqs, axis=1).astype(BF16)
        r0 = pl.multiple_of(i * BLOCK, BLOCK)
        dk_ref[pl.ds(r0, 3 * BLOCK), :] += jnp.concatenate(dks, axis=1)
        dv_ref[pl.ds(r0, 3 * BLOCK), :] += jnp.concatenate(dvs, axis=1)

    const = lambda shape: pl.BlockSpec(shape, lambda i: (0,) * len(shape))
    sink_shape = (N_KV_HEADS, GQA * BLOCK)
    return pl.pallas_call(
        body, name="attn_bwd", grid=(nb,),
        in_specs=_attn_specs(S) + [pl.BlockSpec((BLOCK, D_MODEL), lambda i: (i, 0)),
                                   _bias_spec(nb), const(sink_shape), const(TAB), ANY],
        out_specs=[pl.BlockSpec((BLOCK, D_MODEL), lambda i: (i, OFF_Q // D_MODEL)),
                   const((S + 2 * BLOCK, kvw)), const((S + 2 * BLOCK, kvw)), const(TAB), const(sink_shape)],
        out_shape=[jax.ShapeDtypeStruct(dproj.shape, dproj.dtype),
                   jax.ShapeDtypeStruct((S + 2 * BLOCK, kvw), F32), jax.ShapeDtypeStruct((S + 2 * BLOCK, kvw), F32),
                   jax.ShapeDtypeStruct(TAB, F32), jax.ShapeDtypeStruct(sink_shape, F32)],
        input_output_aliases={11: 0},
        compiler_params=_params("arbitrary"),
    )(*([proj] * 7), datt, bias_tabs, sink_rows, dbias_in, dproj)


def _kv_finish(dkp, dvp, dproj):
    S = dproj.shape[0]
    kvw = N_KV_HEADS * HEAD_DIM

    def body(dk_ref, dv_ref, dproj_in, o_ref):
        o_ref[:, 0:kvw] = dk_ref[pl.ds(BLOCK, S), :].astype(BF16)
        o_ref[:, kvw:2 * kvw] = dv_ref[pl.ds(BLOCK, S), :].astype(BF16)

    whole = pl.BlockSpec((S + 2 * BLOCK, kvw), lambda i: (0, 0))
    return pl.pallas_call(
        body, name="kv_finish", grid=(1,), in_specs=[whole, whole, ANY],
        out_specs=pl.BlockSpec((S, 2 * kvw), lambda i: (0, OFF_K // (2 * kvw))),
        out_shape=jax.ShapeDtypeStruct(dproj.shape, dproj.dtype),
        input_output_aliases={2: 0},
        compiler_params=_params("arbitrary"),
    )(dkp, dvp, dproj)


def _bucket_constants():
    half = N_BUCKETS // 2
    max_exact = half // 2
    qi = np.arange(BLOCK)[None, :]
    kj = np.arange(3 * BLOCK)[:, None]
    rel = kj - BLOCK - qi
    n = np.abs(rel)
    nf = np.maximum(n, 1).astype(np.float32)
    large = max_exact + (np.log(nf / np.float32(max_exact)) / np.float32(math.log(MAX_DISTANCE / max_exact))
                         * np.float32(half - max_exact)).astype(np.int32)
    large = np.minimum(large, half - 1)
    bucket = np.where(rel > 0, half, 0) + np.where(n < max_exact, n, large)
    onehot = (bucket.reshape(1, -1) == np.arange(N_BUCKETS)[:, None]).astype(np.float32)
    window = n <= WINDOW
    first = window & (kj >= BLOCK)
    last = window & (kj < 2 * BLOCK)
    masks = np.stack([np.where(v, 0.0, NEG_INF).astype(np.float32).reshape(-1) for v in (first, window, last)])
    return onehot, masks


def _bias_expand(rel_bias_t, onehot, masks):
    def body(r_ref, oh_ref, m_ref, o_ref):
        tab = jnp.dot(r_ref[...], oh_ref[...], preferred_element_type=F32, precision=lax.Precision.HIGHEST)
        for v in range(3):
            o_ref[v] = tab + m_ref[v:v + 1, :]

    return pl.pallas_call(
        body, name="bias_expand", out_shape=jax.ShapeDtypeStruct((3, N_HEADS, onehot.shape[1]), F32),
        compiler_params=_params(),
    )(rel_bias_t, onehot, masks)


def _bias_reduce(dtab, dsink_rows, onehot):
    def body(d_ref, s_ref, oh_ref, o_ref, so_ref):
        o_ref[...] = lax.dot_general(oh_ref[...], d_ref[...], (((1,), (1,)), ((), ())),
                                     preferred_element_type=F32, precision=lax.Precision.HIGHEST)
        so_ref[...] = jnp.sum(s_ref[...], axis=-1, keepdims=True)

    return pl.pallas_call(
        body, name="bias_reduce",
        out_shape=[jax.ShapeDtypeStruct((N_BUCKETS, N_HEADS), F32),
                   jax.ShapeDtypeStruct((dsink_rows.shape[0], 1), F32)],
        compiler_params=_params(),
    )(dtab, dsink_rows, onehot)


GROUPS = dict(mix=("w_in", "conv_w", "w_a_out", "w_pool", "w_attn_out", "w_o"), ffn=("w_gu", "w_down"))
WEIGHT_NAMES = GROUPS["mix"] + GROUPS["ffn"]


def _layer_fwd(l, x, weights_of, ps, g_mix, g_ffn, bias_tabs, sink_rows):
    W, token = weights_of(l, "mix", x)
    h, proj = _norm_matmul(x, g_mix, W["w_in"], "norm_proj", token)
    za = _conv_fwd(proj, W["conv_w"])
    p = _pool_fwd(proj)
    att = _attn_fwd(proj, bias_tabs, sink_rows)
    ya, ypr, yt, merged, x2 = _mix_fwd(za, p, att, proj, x, W["w_a_out"], W["w_pool"], ps, W["w_attn_out"], W["w_o"])
    Wf, token = weights_of(l, "ffn", x2)
    h2, gu = _norm_matmul(x2, g_ffn, Wf["w_gu"], "norm_gu", token)
    act, x3 = _ffn_fwd(gu, x2, Wf["w_down"])
    saved = dict(x=x, h=h, proj=proj, za=za, p=p, att=att, ya=ya, ypr=ypr, yt=yt, merged=merged, x2=x2, h2=h2,
                 gu=gu, act=act, W={**W, **Wf}, sink_rows=sink_rows)
    return x3, saved, token


def _layer_bwd(l, dx3, sv, grads_to, ps, g_mix, g_ffn, bias_tabs, dbias, token):
    W, sink_rows = sv["W"], sv["sink_rows"]
    dgu = _ffn_bwd(dx3, sv["gu"], W["w_down"], token)
    g_w_down = _wgrad(sv["act"], dx3, "wgrad_down", tk=1408, tn=1024, token=token)
    g_w_gu = _wgrad(sv["h2"], dgu, "wgrad_gu", tk=1024, tn=1408)
    dx2, dg_ffn = _dgrad_norm_bwd(dgu, W["w_gu"], sv["x2"], g_ffn, dx3, "dgrad_gu", tk=1408)
    token = grads_to(l, "ffn", dict(w_gu=g_w_gu, w_down=g_w_down), dx2)
    dya, dyt, dyps, dza, datt, dp, dproj, dps = _mix_bwd(
        dx2, sv["ya"], sv["ypr"], sv["yt"], sv["proj"], ps, W["w_a_out"], W["w_pool"], W["w_attn_out"], W["w_o"], token)
    g_w_o = _wgrad(sv["merged"], dx2, "wgrad_sq_f32", tk=1024, tn=1024)
    g_w_a_out = _wgrad(sv["za"], dya, "wgrad_sq", tk=1024, tn=1024)
    g_w_attn_out = _wgrad(sv["att"], dyt, "wgrad_sq", tk=1024, tn=1024)
    g_w_pool = _wgrad_pool(sv["p"], dyps)
    dproj, g_conv = _conv_bwd(dza, sv["proj"], W["conv_w"], dproj)
    dproj = _pool_bwd(dp, dproj)
    dproj, dkp, dvp, dbias, dsink = _attn_bwd(sv["proj"], datt, bias_tabs, sink_rows, dbias, dproj)
    dproj = _kv_finish(dkp, dvp, dproj)
    g_w_in = _wgrad(sv["h"], dproj, "wgrad_in", tk=1024, tn=2176)
    token = grads_to(l, "mix", dict(w_in=g_w_in, conv_w=g_conv, w_a_out=g_w_a_out, w_pool=g_w_pool,
                                    w_attn_out=g_w_attn_out, w_o=g_w_o), dproj)
    dx, dg_mix = _dgrad_norm_bwd(dproj, W["w_in"], sv["x"], g_mix, dx2, "dgrad_in", tk=2176, token=token)
    return dx, dict(pool_scale=dps, g_mix=dg_mix, g_ffn=dg_ffn, attn_sink=dsink), dbias, token


def _local_step(x, tgt, weights_of, grads_to, pool_scale, attn_sink, g_mix, g_ffn, rel_bias, g_final):
    onehot_np, masks_np = _bucket_constants()
    onehot, masks = jnp.asarray(onehot_np), jnp.asarray(masks_np)
    bias_tabs = _bias_expand(rel_bias.T, onehot, masks).reshape((3,) + TAB)
    saved = []
    for l in range(DEPTH):
        sink_rows = jnp.repeat(attn_sink[l], BLOCK).reshape(N_KV_HEADS, GQA * BLOCK)
        x, sv, token = _layer_fwd(l, x, weights_of, pool_scale[l:l + 1], g_mix[l:l + 1], g_ffn[l:l + 1], bias_tabs,
                                  sink_rows)
        saved.append(sv)
    loss, dx, dg_final = _loss_bwd(x, g_final.reshape(1, D_MODEL), tgt)
    dbias = jnp.zeros(TAB, F32)
    small = [None] * DEPTH
    for l in reversed(range(DEPTH)):
        dx, small[l], dbias, token = _layer_bwd(
            l, dx, saved[l], grads_to, pool_scale[l:l + 1], g_mix[l:l + 1], g_ffn[l:l + 1], bias_tabs, dbias, token)
    dsink_rows = jnp.concatenate([small[l]["attn_sink"].reshape(N_HEADS, BLOCK) for l in range(DEPTH)], axis=0)
    d_rel_bias, d_sink = _bias_reduce(dbias.reshape(N_HEADS, TAB_FLAT), dsink_rows, onehot)
    cat = lambda k: jnp.concatenate([small[l][k] for l in range(DEPTH)], axis=0)
    smalls = dict(pool_scale=cat("pool_scale"), g_mix=cat("g_mix"), g_ffn=cat("g_ffn"),
                  attn_sink=d_sink.reshape(DEPTH, N_HEADS), rel_bias=d_rel_bias, g_final=dg_final)
    return loss[0, 0], dx, smalls


SHARD_AXIS = dict(w_in=(1, IN_TOTAL // N_CHIPS), conv_w=(1, D_MODEL // N_CHIPS), w_a_out=(0, D_MODEL // N_CHIPS),
                  w_pool=(1, POOL_CG // N_CHIPS), w_attn_out=(0, D_MODEL // N_CHIPS), w_o=(0, D_MODEL // N_CHIPS),
                  w_gu=(1, 2 * D_FF // N_CHIPS), w_down=(0, D_FF // N_CHIPS))
HBM = pl.BlockSpec(memory_space=pltpu.HBM)
SEM = pl.BlockSpec(memory_space=pltpu.SEMAPHORE)
DATAFLOW = pltpu.SideEffectType.DATAFLOW_SIDE_EFFECTING
TOKEN = jax.ShapeDtypeStruct((8, 128), F32)


def _shard_of(ref, name, chip):
    axis, n = SHARD_AXIS[name]
    idx = [slice(None)] * len(ref.shape)
    idx[axis] = pl.ds(chip * n, n)
    return ref.at[tuple(idx)]


def _with_shard_axis(name, shape, size):
    axis, _ = SHARD_AXIS[name]
    s = list(shape)
    s[axis] = size
    return tuple(s)


HALF_AXIS = dict(w_in=0, conv_w=1, w_a_out=0, w_pool=1, w_attn_out=0, w_o=0, w_gu=0, w_down=0)


def _half_of_shard(ref, name, core):
    axis = HALF_AXIS[name]
    n = ref.shape[axis] // 2
    idx = [slice(None)] * len(ref.shape)
    idx[axis] = pl.ds(core * n, n)
    return ref.at[tuple(idx)]


def _half_in_full(ref, name, chip, core):
    saxis, n = SHARD_AXIS[name]
    haxis = HALF_AXIS[name]
    idx = [slice(None)] * len(ref.shape)
    if haxis == saxis:
        idx[saxis] = pl.ds(chip * n + core * (n // 2), n // 2)
    else:
        h = ref.shape[haxis] // 2
        idx[saxis] = pl.ds(chip * n, n)
        idx[haxis] = pl.ds(core * h, h)
    return ref.at[tuple(idx)]


def _on_each_device(fn):
    me = 2 * lax.axis_index("x") + lax.axis_index("y")
    c = lax.axis_index("c")
    for chip in range(N_CHIPS):
        for core in range(2):
            pl.when((me == chip) & (c == core))(functools.partial(fn, chip, core))


def _chip_peers(x, y):
    return [(1 - x, y), (x, 1 - y), (1 - x, 1 - y)]


RELATION_XOR = (2, 1, 3)


def _group_copies(kind, group, srcs, lands, send_sems, recv_sems, local_sems, chip, core):
    x, y, c = lax.axis_index("x"), lax.axis_index("y"), lax.axis_index("c")
    copies = []
    for t, name in enumerate(GROUPS[group]):
        for j, (px, py) in enumerate(_chip_peers(x, y)):
            if kind == "gather":
                src, dst = _half_of_shard(srcs[t], name, core), _half_in_full(lands[t], name, chip, core)
            else:
                src, dst = _shard_of(srcs[t], name, chip ^ RELATION_XOR[j]), lands[t].at[j]
            copies.append(pltpu.make_async_remote_copy(
                src_ref=src, dst_ref=dst, send_sem=send_sems.at[3 * t + j], recv_sem=recv_sems.at[3 * t + j],
                device_id=(px, py, c), device_id_type=MESH))
        if kind == "gather":
            src, dst = srcs[t], _shard_of(lands[t], name, chip)
        else:
            src, dst = _shard_of(srcs[t], name, chip), lands[t].at[N_CHIPS - 1]
        copies.append(pltpu.make_async_copy(src, dst, local_sems.at[t]))
    return copies


def _exchange_start(kind, group, srcs, land_shapes, after):
    nw = len(GROUPS[group])

    def body(*refs):
        srcs_r, lands_r = refs[:nw], refs[nw:2 * nw]
        send_sems, recv_sems, local_sems = refs[2 * nw + 1:2 * nw + 4]
        token = refs[-1]

        def issue(chip, core):
            for cp in _group_copies(kind, group, srcs_r, lands_r, send_sems, recv_sems, local_sems, chip, core):
                cp.start()
        _on_each_device(issue)
        token[...] = jnp.zeros_like(token)

    lands = [pltpu.with_memory_space_constraint(lax.empty(s.shape, s.dtype), pltpu.HBM) for s in land_shapes]
    srcs = [pltpu.with_memory_space_constraint(a, pltpu.HBM) for a in srcs]
    thru = [pltpu.HBM(a.shape, a.dtype) for a in srcs + lands]
    outs = pl.pallas_call(
        body, name=f"{kind}_{group}_start",
        in_specs=[HBM] * (2 * nw) + [ANY],
        out_specs=[SEM, SEM, SEM] + [HBM] * (2 * nw) + [pl.BlockSpec(memory_space=pltpu.VMEM)],
        out_shape=[pltpu.SemaphoreType.DMA((3 * nw,)), pltpu.SemaphoreType.DMA((3 * nw,)),
                   pltpu.SemaphoreType.DMA((nw,))] + thru + [TOKEN],
        input_output_aliases={t: 3 + t for t in range(2 * nw)},
        compiler_params=pltpu.CompilerParams(has_side_effects=DATAFLOW),
    )(*srcs, *lands, after)
    return dict(sems=outs[0:3], srcs=outs[3:3 + nw], lands=outs[3 + nw:3 + 2 * nw], token=outs[-1])


def _exchange_wait(kind, group, started, after):
    nw = len(GROUPS[group])

    def body(*refs):
        srcs_r, lands_r = refs[:nw], refs[nw:2 * nw]
        send_sems, recv_sems, local_sems = refs[2 * nw:2 * nw + 3]
        for cp in _group_copies(kind, group, srcs_r, lands_r, send_sems, recv_sems, local_sems, 0, 0):
            cp.wait()

    srcs, lands = list(started["srcs"]), list(started["lands"])
    outs = pl.pallas_call(
        body, name=f"{kind}_{group}_wait",
        in_specs=[HBM] * (2 * nw) + [SEM, SEM, SEM, ANY],
        out_specs=[HBM] * (2 * nw),
        out_shape=[pltpu.HBM(a.shape, a.dtype) for a in srcs + lands],
        input_output_aliases={t: t for t in range(2 * nw)},
        compiler_params=pltpu.CompilerParams(has_side_effects=DATAFLOW),
    )(*srcs, *lands, *started["sems"], after)
    return dict(zip(GROUPS[group], outs[nw:]))


def _gather_start(group, shards, after):
    names = GROUPS[group]
    shapes = [jax.ShapeDtypeStruct(_with_shard_axis(n, shards[n].shape, SHARD_AXIS[n][1] * N_CHIPS), shards[n].dtype)
              for n in names]
    return _exchange_start("gather", group, [shards[n] for n in names], shapes, after)


def _scatter_start(group, grads, after):
    names = GROUPS[group]
    shapes = [jax.ShapeDtypeStruct((N_CHIPS,) + _with_shard_axis(n, grads[n].shape, SHARD_AXIS[n][1]), grads[n].dtype)
              for n in names]
    return _exchange_start("scatter", group, [grads[n] for n in names], shapes, after)


def _sibling_exchange(parts):
    n = len(parts)

    def body(*refs):
        ins, outs = refs[:n], refs[n:2 * n]
        send_sems, recv_sems = refs[2 * n:]
        sibling = (lax.axis_index("x"), lax.axis_index("y"), 1 - lax.axis_index("c"))
        copies = [pltpu.make_async_remote_copy(src_ref=ins[t], dst_ref=outs[t], send_sem=send_sems.at[t],
                                               recv_sem=recv_sems.at[t], device_id=sibling, device_id_type=MESH)
                  for t in range(n)]
        for cp in copies:
            cp.start()
        for cp in copies:
            cp.wait()

    outs = pl.pallas_call(
        body, name="sibling_exchange", in_specs=[ANY] * n, out_specs=[ANY] * n,
        out_shape=[jax.ShapeDtypeStruct(p.shape, p.dtype) for p in parts],
        scratch_shapes=[pltpu.SemaphoreType.DMA((n,)), pltpu.SemaphoreType.DMA((n,))],
        compiler_params=pltpu.CompilerParams(has_side_effects=True),
    )(*parts)
    return list(outs)


def _sibling_fill(group, fulls):
    names = GROUPS[group]
    nw = len(names)

    def body(*refs):
        ins, outs = refs[:nw], refs[nw:2 * nw]
        send_sems, recv_sems = refs[2 * nw:]
        sibling = (lax.axis_index("x"), lax.axis_index("y"), 1 - lax.axis_index("c"))

        def forward(chip, core):
            copies = []
            for t, name in enumerate(names):
                for j in range(3):
                    other = chip ^ RELATION_XOR[j]
                    copies.append(pltpu.make_async_remote_copy(
                        src_ref=_half_in_full(ins[t], name, other, core),
                        dst_ref=_half_in_full(outs[t], name, other, core),
                        send_sem=send_sems.at[3 * t + j], recv_sem=recv_sems.at[3 * t + j],
                        device_id=sibling, device_id_type=MESH))
            for cp in copies:
                cp.start()
            for cp in copies:
                cp.wait()
        _on_each_device(forward)

    arrays = [fulls[n] for n in names]
    outs = pl.pallas_call(
        body, name=f"sibling_fill_{group}", in_specs=[ANY] * nw, out_specs=[ANY] * nw,
        out_shape=[jax.ShapeDtypeStruct(a.shape, a.dtype) for a in arrays],
        scratch_shapes=[pltpu.SemaphoreType.DMA((3 * nw,)), pltpu.SemaphoreType.DMA((3 * nw,))],
        input_output_aliases={t: t for t in range(nw)},
        compiler_params=pltpu.CompilerParams(has_side_effects=True),
    )(*arrays)
    return dict(zip(names, outs))


N_DEV = 8


def _all_reduce_small(v):
    R, C = v.shape

    def body(v_ref, o_ref, slots, send_sems, recv_sems):
        x, y, c = lax.axis_index("x"), lax.axis_index("y"), lax.axis_index("c")
        me = 4 * x + 2 * y + c
        slots[me] = v_ref[...]
        copies = []
        for k in range(1, N_DEV):
            peer = me ^ k
            cp = pltpu.make_async_remote_copy(
                src_ref=v_ref, dst_ref=slots.at[me], send_sem=send_sems.at[k - 1], recv_sem=recv_sems.at[k - 1],
                device_id=(peer // 4, (peer // 2) % 2, peer % 2), device_id_type=MESH)
            cp.start()
            copies.append(cp)
        for cp in copies:
            cp.wait()
        acc = slots[0]
        for k in range(1, N_DEV):
            acc = acc + slots[k]
        o_ref[...] = acc

    return pl.pallas_call(
        body, name="all_reduce_small", out_shape=jax.ShapeDtypeStruct((R, C), F32),
        in_specs=[pl.BlockSpec(memory_space=pltpu.VMEM)], out_specs=pl.BlockSpec(memory_space=pltpu.VMEM),
        scratch_shapes=[pltpu.VMEM((N_DEV, R, C), F32), pltpu.SemaphoreType.DMA((N_DEV - 1,)),
                        pltpu.SemaphoreType.DMA((N_DEV - 1,))],
        compiler_params=pltpu.CompilerParams(has_side_effects=True),
    )(v)


def _as2d(shape):
    return (int(np.prod(shape[:-1])), shape[-1])


def _row_block(rows, cols, n_arrays):
    budget = V7X_VMEM_LIMIT // 2
    tr = rows
    while tr % 16 == 0 and 2 * n_arrays * tr * cols * 4 > budget:
        tr //= 2
    return tr


def _sum_slots(slots):
    _, R, C = slots.shape
    tr = _row_block(R, C, 5)

    def body(s_ref, o_ref):
        acc = s_ref[0].astype(F32)
        for k in range(1, N_CHIPS):
            acc = acc + s_ref[k].astype(F32)
        o_ref[...] = acc.astype(BF16)

    return pl.pallas_call(
        body, name="sum_slots", grid=(R // tr,),
        in_specs=[pl.BlockSpec((N_CHIPS, tr, C), lambda i: (0, i, 0))],
        out_specs=pl.BlockSpec((tr, C), lambda i: (i, 0)),
        out_shape=jax.ShapeDtypeStruct((R, C), BF16),
        compiler_params=_params("parallel"),
    )(slots)


def _adamw(l, w, m, v, g_a, g_b, prev):
    L, R, C = w.shape
    tr = _row_block(R, C, 9)
    c1 = 1.0 - ADAM_B1 ** ADAM_STEP
    c2 = 1.0 - ADAM_B2 ** ADAM_STEP

    def body(w_ref, m_ref, v_ref, a_ref, b_ref, *rest):
        g_ref, d_ref, nm_ref, nv_ref = rest[-4:]
        g = a_ref[...].astype(F32) + b_ref[...].astype(F32)
        nm = ADAM_B1 * m_ref[...] + (1.0 - ADAM_B1) * g
        nv = ADAM_B2 * v_ref[...] + (1.0 - ADAM_B2) * (g * g)
        g_ref[...] = g
        nm_ref[...] = nm
        nv_ref[...] = nv
        d_ref[...] = -ADAM_LR * ((nm / c1) / (jnp.sqrt(nv / c2) + ADAM_EPS) + ADAM_WD * w_ref[...])

    layer = pl.BlockSpec((None, tr, C), lambda i: (l, i, 0))
    blk = pl.BlockSpec((tr, C), lambda i: (i, 0))
    out = jax.ShapeDtypeStruct((L, R, C), F32)
    prev = [] if prev is None else list(prev)
    return pl.pallas_call(
        body, name="adamw", grid=(R // tr,), in_specs=[layer] * 3 + [blk] * 2 + [ANY] * len(prev),
        out_specs=[layer] * 4, out_shape=[out] * 4,
        input_output_aliases={5 + k: k for k in range(len(prev))},
        compiler_params=_params("parallel"),
    )(w, m, v, g_a, g_b, *prev)


SMALL_ROWS = 16


def _pack_small(pool_scale, g_mix, g_ffn, g_final, attn_sink, rel_bias):
    tail = jnp.concatenate([attn_sink.reshape(-1), rel_bias.reshape(-1)])
    tail = jnp.pad(tail, (0, D_MODEL - tail.shape[0])).reshape(1, D_MODEL)
    rows = jnp.concatenate([pool_scale, g_mix, g_ffn, g_final.reshape(1, D_MODEL), tail], axis=0)
    return jnp.pad(rows, ((0, SMALL_ROWS - rows.shape[0]), (0, 0)))


def _unpack_small(packed):
    n_sink = DEPTH * N_HEADS
    return dict(pool_scale=packed[0:4], g_mix=packed[4:8], g_ffn=packed[8:12], g_final=packed[12],
                attn_sink=packed[13, 0:n_sink].reshape(DEPTH, N_HEADS),
                rel_bias=packed[13, n_sink:n_sink + N_BUCKETS * N_HEADS].reshape(N_BUCKETS, N_HEADS))


def _group_shards(l, group, masters):
    out = {}
    for n in GROUPS[group]:
        w = masters[n][l]
        out[n] = jnp.pad(w.reshape(3, -1), ((0, 5), (0, 0))) if n == "conv_w" else w.astype(BF16)
    return out


def kernel(x, w_in, conv_w, w_a_out, w_pool, pool_scale, w_attn_out, attn_sink, w_o, g_mix, g_ffn, w_gu, w_down, rel_bias, g_final, loss_target, m_w_in, m_conv_w, m_w_a_out, m_w_pool, m_pool_scale, m_w_attn_out, m_attn_sink, m_w_o, m_g_mix, m_g_ffn, m_w_gu, m_w_down, m_rel_bias, m_g_final, v_w_in, v_conv_w, v_w_a_out, v_w_pool, v_pool_scale, v_w_attn_out, v_attn_sink, v_w_o, v_g_mix, v_g_ffn, v_w_gu, v_w_down, v_rel_bias, v_g_final):
    big = dict(w_in=(w_in, m_w_in, v_w_in), conv_w=(conv_w, m_conv_w, v_conv_w), w_a_out=(w_a_out, m_w_a_out, v_w_a_out),
               w_pool=(w_pool, m_w_pool, v_w_pool), w_attn_out=(w_attn_out, m_w_attn_out, v_w_attn_out),
               w_o=(w_o, m_w_o, v_w_o), w_gu=(w_gu, m_w_gu, v_w_gu), w_down=(w_down, m_w_down, v_w_down))

    big3 = {n: tuple(a.reshape((DEPTH,) + _as2d(a.shape[1:])) for a in big[n]) for n in WEIGHT_NAMES}
    masters = {n: big[n][0] for n in WEIGHT_NAMES}

    gathers = {(0, "mix"): _gather_start("mix", _group_shards(0, "mix", masters), rel_bias)}
    newest = {"token": gathers[0, "mix"]["token"]}
    masters = dict(zip(WEIGHT_NAMES, lax.optimization_barrier(
        (tuple(masters[n] for n in WEIGHT_NAMES), newest["token"]))[0]))

    def weights_of(l, group, a):
        W = _sibling_fill(group, _exchange_wait("gather", group, gathers.pop((l, group)), a))
        if group == "mix":
            gathers[l, "ffn"] = _gather_start("ffn", _group_shards(l, "ffn", masters), W["w_in"])
            newest["token"] = gathers[l, "ffn"]["token"]
            if l + 1 < DEPTH:
                gathers[l + 1, "mix"] = _gather_start("mix", _group_shards(l + 1, "mix", masters), newest["token"])
                newest["token"] = gathers[l + 1, "mix"]["token"]
        return W, newest["token"]

    results = {n: None for n in WEIGHT_NAMES}
    scatters = {}

    def finish(l, group, after):
        slots = _exchange_wait("scatter", group, scatters.pop((l, group)), after)
        names = GROUPS[group]
        parts = [_sum_slots(slots[n].reshape((N_CHIPS,) + _as2d(slots[n].shape[1:]))) for n in names]
        others = _sibling_exchange(parts)
        for n, mine, other in zip(names, parts, others):
            if n == "conv_w":
                mine, other = mine[0:3], other[0:3]
            results[n] = _adamw(l, *big3[n], mine, other, results[n])

    def grads_to(l, group, wgrads, a):
        scatters[l, group] = _scatter_start(group, wgrads, a)
        token = scatters[l, group]["token"]
        if group == "mix" and l + 1 < DEPTH:
            finish(l + 1, "ffn", token)
            finish(l + 1, "mix", token)
        return token

    loss, grad_x, smalls = _local_step(x[0], loss_target[0], weights_of, grads_to, pool_scale, attn_sink, g_mix, g_ffn,
                                       rel_bias, g_final)
    finish(0, "ffn", grad_x)
    finish(0, "mix", results["w_down"][0])
    stacked = {n: [o.reshape(big[n][0].shape) for o in results[n]] for n in WEIGHT_NAMES}

    g_small = _all_reduce_small(_pack_small(smalls["pool_scale"], smalls["g_mix"], smalls["g_ffn"], smalls["g_final"],
                                            smalls["attn_sink"], smalls["rel_bias"]))
    w_small = _pack_small(pool_scale, g_mix, g_ffn, g_final, attn_sink, rel_bias)
    m_small = _pack_small(m_pool_scale, m_g_mix, m_g_ffn, m_g_final, m_attn_sink, m_rel_bias)
    v_small = _pack_small(v_pool_scale, v_g_mix, v_g_ffn, v_g_final, v_attn_sink, v_rel_bias)
    small_out = [_unpack_small(o[0]) for o in
                 _adamw(0, w_small[None], m_small[None], v_small[None], g_small, jnp.zeros_like(g_small), None)]

    total_loss = lax.psum(loss, ("x", "y", "c"))

    order = ("w_in", "conv_w", "w_a_out", "w_pool", "pool_scale", "w_attn_out", "attn_sink", "w_o", "g_mix", "g_ffn",
             "w_gu", "w_down", "rel_bias", "g_final")
    outs = [total_loss, grad_x[None]]
    for k in range(4):
        for n in order:
            outs.append(stacked[n][k] if n in stacked else small_out[k][n])
    return tuple(outs)
```

```python
import functools
import math

import numpy as np
import jax
import jax.numpy as jnp
from jax import lax
from jax.experimental import pallas as pl
from jax.experimental.pallas import tpu as pltpu

F32 = jnp.float32
BF16 = jnp.bfloat16

D_MODEL = 1024
DEPTH = 4
N_HEADS = 16
N_KV_HEADS = 4
HEAD_DIM = 64
GQA = N_HEADS // N_KV_HEADS
WINDOW = 128
BLOCK = 128
N_BUCKETS = 32
MAX_DISTANCE = 128
POOL_GROUPS = 4
POOL_CG = D_MODEL // POOL_GROUPS
POOL_WINDOWS = (2, 4, 8, 16)
D_FF = 2816
IN_TOTAL = 8704
OFF_B, OFF_C, OFF_X, OFF_U, OFF_Q, OFF_K, OFF_V, OFF_GA, OFF_GP, OFF_GT = (
    0, 1024, 2048, 3072, 4096, 5120, 5376, 5632, 6656, 7680)
EPS = 1e-6
NEG_INF = -1e30
SM_SCALE = HEAD_DIM ** -0.5

ADAM_LR = 0.001
ADAM_B1 = 0.9
ADAM_B2 = 0.999
ADAM_EPS = 1e-08
ADAM_WD = 0.01
ADAM_STEP = 10

N_CHIPS = 4
HALO = 8
V7X_VMEM_LIMIT = 56 * 1024 * 1024
MESH = pl.DeviceIdType.MESH
ANY = pl.BlockSpec(memory_space=pl.ANY)


def _params(*sem):
    return pltpu.CompilerParams(dimension_semantics=tuple(sem) if sem else None,
                                vmem_limit_bytes=V7X_VMEM_LIMIT)


def _tile(n, pref):
    t = min(pref, n)
    while n % t or t % 128:
        t -= 128
    return t


def _nt(a, b):
    return lax.dot_general(a, b, (((1,), (1,)), ((), ())), preferred_element_type=F32)


def _tn(a, b):
    return lax.dot_general(a, b, (((0,), (0,)), ((), ())), preferred_element_type=F32)


def _nn(a, b):
    return jnp.dot(a, b, preferred_element_type=F32)


def _sigmoid(v):
    return 1.0 / (1.0 + jnp.exp(-v))


def _norm_matmul(x, g, w, name, token):
    S, Dm = x.shape
    N = w.shape[1]
    tm, tn = _tile(S, 1024), _tile(N, N // 4)

    def body(x_ref, g_ref, w_ref, token_ref, h_ref, o_ref):
        @pl.when(pl.program_id(1) == 0)
        def _():
            xv = x_ref[...]
            r = lax.rsqrt(jnp.mean(xv * xv, axis=-1, keepdims=True) + EPS)
            h_ref[...] = (xv * r * g_ref[...]).astype(BF16)
        o_ref[...] = _nn(h_ref[...], w_ref[...]).astype(BF16)

    return pl.pallas_call(
        body, name=name, grid=(S // tm, N // tn),
        in_specs=[pl.BlockSpec((tm, Dm), lambda i, j: (i, 0)),
                  pl.BlockSpec((1, Dm), lambda i, j: (0, 0)),
                  pl.BlockSpec((Dm, tn), lambda i, j: (0, j)), ANY],
        out_specs=[pl.BlockSpec((tm, Dm), lambda i, j: (i, 0)),
                   pl.BlockSpec((tm, tn), lambda i, j: (i, j))],
        out_shape=[jax.ShapeDtypeStruct((S, Dm), BF16), jax.ShapeDtypeStruct((S, N), BF16)],
        compiler_params=_params("parallel", "arbitrary"),
    )(x, g, w, token)


CB = 128
CBW = 128


def _fill_padded(pad_ref, v, S):
    z = jnp.zeros((HALO, v.shape[1]), F32)
    pad_ref[pl.ds(0, HALO), :] = z
    pad_ref[pl.ds(S + HALO, HALO), :] = z
    pad_ref[pl.ds(HALO, S), :] = v


def _shifted(pad_ref, off, S):
    return pad_ref[pl.ds(HALO + off, S), :]


def _conv_fwd(proj, cw8):
    S = proj.shape[0]
    nblk = D_MODEL // CBW

    def body(b_ref, c_ref, x_ref, w_ref, o_ref, pad):
        u = c_ref[...].astype(F32) * x_ref[...].astype(F32)
        _fill_padded(pad, u, S)
        cv = w_ref[0:1, :] * _shifted(pad, -1, S) + w_ref[1:2, :] * u + w_ref[2:3, :] * _shifted(pad, 1, S)
        o_ref[...] = (b_ref[...].astype(F32) * cv).astype(BF16)

    col = lambda base: pl.BlockSpec((S, CBW), lambda j: (0, base // CBW + j))
    return pl.pallas_call(
        body, name="conv_fwd", grid=(nblk,),
        in_specs=[col(OFF_B), col(OFF_C), col(OFF_X), pl.BlockSpec((8, CBW), lambda j: (0, j))],
        out_specs=pl.BlockSpec((S, CBW), lambda j: (0, j)),
        out_shape=jax.ShapeDtypeStruct((S, D_MODEL), BF16),
        scratch_shapes=[pltpu.VMEM((S + 2 * HALO, CBW), F32)],
        compiler_params=_params("parallel"),
    )(proj, proj, proj, cw8)


def _pool_count(S, lo, hi):
    t = lax.broadcasted_iota(jnp.int32, (S, CBW), 0)
    return (jnp.minimum(t + hi, S - 1) - jnp.maximum(t - lo, 0) + 1).astype(F32)


def _pool_fwd(proj):
    S = proj.shape[0]
    nblk = D_MODEL // CBW
    per_group = POOL_CG // CBW

    def body(u_ref, o_ref, pad):
        u = u_ref[...].astype(F32)
        _fill_padded(pad, u, S)
        grp = pl.program_id(0) // per_group
        for gi, w in enumerate(POOL_WINDOWS):
            @pl.when(grp == gi)
            def _(w=w):
                lo, hi = w // 2, w - 1 - w // 2
                acc = _shifted(pad, -lo, S)
                for off in range(-lo + 1, hi + 1):
                    acc = acc + _shifted(pad, off, S)
                o_ref[...] = (acc / _pool_count(S, lo, hi) - u).astype(BF16)

    return pl.pallas_call(
        body, name="pool_fwd", grid=(nblk,),
        in_specs=[pl.BlockSpec((S, CBW), lambda j: (0, OFF_U // CBW + j))],
        out_specs=pl.BlockSpec((S, CBW), lambda j: (0, j)),
        out_shape=jax.ShapeDtypeStruct((S, D_MODEL), BF16),
        scratch_shapes=[pltpu.VMEM((S + 2 * HALO, CBW), F32)],
        compiler_params=_params("parallel"),
    )(proj)


def _attn_specs(S):
    nb = S // BLOCK
    kcol, vcol = OFF_K // (N_KV_HEADS * HEAD_DIM), OFF_V // (N_KV_HEADS * HEAD_DIM)
    kvw = N_KV_HEADS * HEAD_DIM
    prev = lambda i: jnp.maximum(i - 1, 0)
    nxt = lambda i: jnp.minimum(i + 1, nb - 1)
    return [
        pl.BlockSpec((BLOCK, D_MODEL), lambda i: (i, OFF_Q // D_MODEL)),
        pl.BlockSpec((BLOCK, kvw), lambda i: (prev(i), kcol)),
        pl.BlockSpec((BLOCK, kvw), lambda i: (i, kcol)),
        pl.BlockSpec((BLOCK, kvw), lambda i: (nxt(i), kcol)),
        pl.BlockSpec((BLOCK, kvw), lambda i: (prev(i), vcol)),
        pl.BlockSpec((BLOCK, kvw), lambda i: (i, vcol)),
        pl.BlockSpec((BLOCK, kvw), lambda i: (nxt(i), vcol)),
    ]


def _heads_rows(ref_or_val, hk):
    return jnp.concatenate(
        [ref_or_val[:, (GQA * hk + g) * HEAD_DIM:(GQA * hk + g + 1) * HEAD_DIM] for g in range(GQA)], axis=0)


def _kv_rows(p_ref, c_ref, n_ref, hk):
    sl = slice(hk * HEAD_DIM, (hk + 1) * HEAD_DIM)
    return jnp.concatenate([p_ref[:, sl], c_ref[:, sl], n_ref[:, sl]], axis=0)


def _bias_cols(bias_ref, hk):
    return jnp.concatenate([bias_ref[GQA * hk + g] for g in range(GQA)], axis=1)


def _softmax_keys_on_rows(q4s, kc, bias_blk, sink_row):
    s = _nt(kc, q4s) + bias_blk
    m = jnp.maximum(jnp.max(s, axis=0, keepdims=True), sink_row)
    p = jnp.exp(s - m)
    e_sink = jnp.exp(sink_row - m)
    inv = 1.0 / (jnp.sum(p, axis=0, keepdims=True) + e_sink)
    return p * inv, e_sink * inv


TAB = (N_HEADS, 3 * BLOCK, BLOCK)
TAB_FLAT = 3 * BLOCK * BLOCK


def _bias_spec(nb):
    return pl.BlockSpec((None,) + TAB, lambda i: (jnp.where(i == 0, 0, jnp.where(i == nb - 1, 2, 1)), 0, 0, 0))


def _attn_fwd(proj, bias_tabs, sink_rows):
    S = proj.shape[0]
    nb = S // BLOCK
    assert nb >= 2

    def body(q_ref, kp, kc_, kn, vp, vc_, vn, bias_ref, sink_ref, o_ref):
        outs = []
        for hk in range(N_KV_HEADS):
            q4s = _heads_rows(q_ref, hk) * SM_SCALE
            kc = _kv_rows(kp, kc_, kn, hk)
            vc = _kv_rows(vp, vc_, vn, hk)
            pn, _ = _softmax_keys_on_rows(q4s, kc, _bias_cols(bias_ref, hk), sink_ref[hk:hk + 1, :])
            o4 = _tn(pn.astype(BF16), vc)
            outs += [o4[g * BLOCK:(g + 1) * BLOCK, :] for g in range(GQA)]
        o_ref[...] = jnp.concatenate(outs, axis=1).astype(BF16)

    return pl.pallas_call(
        body, name="attn_fwd", grid=(nb,),
        in_specs=_attn_specs(S) + [_bias_spec(nb), pl.BlockSpec((N_KV_HEADS, GQA * BLOCK), lambda i: (0, 0))],
        out_specs=pl.BlockSpec((BLOCK, D_MODEL), lambda i: (i, 0)),
        out_shape=jax.ShapeDtypeStruct((S, D_MODEL), BF16),
        compiler_params=_params("parallel"),
    )(*([proj] * 7), bias_tabs, sink_rows)


GATE_HALF = D_MODEL // 2


def _gate_specs(tm):
    return [pl.BlockSpec((tm, GATE_HALF), lambda i, c=off // GATE_HALF + k: (i, c))
            for off in (OFF_GA, OFF_GP, OFF_GT) for k in (0, 1)]


def _gate(lo_ref, hi_ref):
    return _sigmoid(jnp.concatenate([lo_ref[...], hi_ref[...]], axis=1).astype(F32))


def _pool_mix(p, wp):
    return jnp.concatenate(
        [_nn(p[:, g * POOL_CG:(g + 1) * POOL_CG], wp[g]) for g in range(POOL_GROUPS)], axis=1)


def _mix_fwd(za, p, att, proj, x, wa, wp, ps, wt, wo):
    S = x.shape[0]
    tm = _tile(S, 256)

    def body(za_ref, p_ref, att_ref, ga0, ga1, gp0, gp1, gt0, gt1, x_ref, wa_ref, wp_ref, ps_ref, wt_ref, wo_ref,
             ya_ref, yp_ref, yt_ref, mg_ref, x2_ref):
        ya = _nn(za_ref[...], wa_ref[...])
        ypr = _pool_mix(p_ref[...], wp_ref)
        yt = _nn(att_ref[...], wt_ref[...])
        merged = _gate(ga0, ga1) * ya + _gate(gp0, gp1) * (ypr * ps_ref[...]) + _gate(gt0, gt1) * yt
        mb = merged.astype(BF16)
        ya_ref[...] = ya.astype(BF16)
        yp_ref[...] = ypr.astype(BF16)
        yt_ref[...] = yt.astype(BF16)
        mg_ref[...] = mb
        x2_ref[...] = x_ref[...] + _nn(mb, wo_ref[...])

    row = lambda c=0: pl.BlockSpec((tm, D_MODEL), lambda i: (i, c))
    whole = lambda a: pl.BlockSpec(a.shape, lambda i: (0,) * a.ndim)
    act = jax.ShapeDtypeStruct((S, D_MODEL), BF16)
    return pl.pallas_call(
        body, name="mix_fwd", grid=(S // tm,),
        in_specs=[row(), row(), row()] + _gate_specs(tm) + [row(), whole(wa), whole(wp), whole(ps), whole(wt), whole(wo)],
        out_specs=[row(), row(), row(), row(), row()],
        out_shape=[act, act, act, act, jax.ShapeDtypeStruct((S, D_MODEL), F32)],
        compiler_params=_params("parallel"),
    )(za, p, att, *([proj] * 6), x, wa, wp, ps, wt, wo)


def _ffn_fwd(gu, x2, wd):
    S = x2.shape[0]
    tm = _tile(S, 256)

    def body(g_ref, u_ref, x_ref, w_ref, a_ref, o_ref):
        g = g_ref[...].astype(F32)
        a = (g * _sigmoid(g) * u_ref[...].astype(F32)).astype(BF16)
        a_ref[...] = a
        o_ref[...] = x_ref[...] + _nn(a, w_ref[...])

    return pl.pallas_call(
        body, name="ffn_fwd", grid=(S // tm,),
        in_specs=[pl.BlockSpec((tm, D_FF), lambda i: (i, 0)), pl.BlockSpec((tm, D_FF), lambda i: (i, 1)),
                  pl.BlockSpec((tm, D_MODEL), lambda i: (i, 0)), pl.BlockSpec((D_FF, D_MODEL), lambda i: (0, 0))],
        out_specs=[pl.BlockSpec((tm, D_FF), lambda i: (i, 0)), pl.BlockSpec((tm, D_MODEL), lambda i: (i, 0))],
        out_shape=[jax.ShapeDtypeStruct((S, D_FF), BF16), jax.ShapeDtypeStruct((S, D_MODEL), F32)],
        compiler_params=_params("parallel"),
    )(gu, gu, x2, wd)


def _loss_bwd(x, g, tgt):
    S, Dm = x.shape
    tm = _tile(S, 512)

    def body(x_ref, g_ref, t_ref, l_ref, dx_ref, dg_ref):
        @pl.when(pl.program_id(0) == 0)
        def _():
            l_ref[...] = jnp.zeros_like(l_ref)
            dg_ref[...] = jnp.zeros_like(dg_ref)
        xv, gv = x_ref[...], g_ref[...]
        r = lax.rsqrt(jnp.mean(xv * xv, axis=-1, keepdims=True) + EPS)
        n = xv * r
        err = n * gv - t_ref[...]
        l_ref[...] += 0.5 * jnp.sum(jnp.mean(err * err, axis=-1, keepdims=True), axis=0, keepdims=True)
        dy = err * (1.0 / Dm)
        dn = dy * gv
        dx_ref[...] = r * (dn - n * jnp.mean(dn * n, axis=-1, keepdims=True))
        dg_ref[...] += jnp.sum(dy * n, axis=0, keepdims=True)

    return pl.pallas_call(
        body, name="loss_bwd", grid=(S // tm,),
        in_specs=[pl.BlockSpec((tm, Dm), lambda i: (i, 0)), pl.BlockSpec((1, Dm), lambda i: (0, 0)),
                  pl.BlockSpec((tm, Dm), lambda i: (i, 0))],
        out_specs=[pl.BlockSpec((8, 128), lambda i: (0, 0)), pl.BlockSpec((tm, Dm), lambda i: (i, 0)),
                   pl.BlockSpec((1, Dm), lambda i: (0, 0))],
        out_shape=[jax.ShapeDtypeStruct((8, 128), F32), jax.ShapeDtypeStruct((S, Dm), F32),
                   jax.ShapeDtypeStruct((1, Dm), F32)],
        compiler_params=_params("arbitrary"),
    )(x, g, tgt)


def _ffn_bwd(dx3, gu, wd, token):
    S = dx3.shape[0]
    tm = _tile(S, 256)

    def body(d_ref, g_ref, u_ref, w_ref, token_ref, o_ref):
        dact = _nt(d_ref[...].astype(BF16), w_ref[...])
        g, u = g_ref[...].astype(F32), u_ref[...].astype(F32)
        sg = _sigmoid(g)
        o_ref[:, 0:D_FF] = (dact * u * (sg * (1.0 + g * (1.0 - sg)))).astype(BF16)
        o_ref[:, D_FF:2 * D_FF] = (dact * (g * sg)).astype(BF16)

    return pl.pallas_call(
        body, name="ffn_bwd", grid=(S // tm,),
        in_specs=[pl.BlockSpec((tm, D_MODEL), lambda i: (i, 0)),
                  pl.BlockSpec((tm, D_FF), lambda i: (i, 0)), pl.BlockSpec((tm, D_FF), lambda i: (i, 1)),
                  pl.BlockSpec((D_FF, D_MODEL), lambda i: (0, 0)), ANY],
        out_specs=pl.BlockSpec((tm, 2 * D_FF), lambda i: (i, 0)),
        out_shape=jax.ShapeDtypeStruct((S, 2 * D_FF), BF16),
        compiler_params=_params("parallel"),
    )(dx3, gu, gu, wd, token)


def _wgrad(a, b, name, tk=512, tn=512, out_dtype=BF16, token=None):
    S, K = a.shape
    N = b.shape[1]
    tk, tn, ts = _tile(K, tk), _tile(N, tn), _tile(S, 1024)
    n_s = S // ts
    extra = [] if token is None else [token]

    def body(a_ref, b_ref, *rest):
        o_ref, acc = rest[-2:]
        s = pl.program_id(2)

        @pl.when(s == 0)
        def _():
            acc[...] = jnp.zeros_like(acc)
        acc[...] += _tn(a_ref[...].astype(BF16), b_ref[...].astype(BF16))

        @pl.when(s == n_s - 1)
        def _():
            o_ref[...] = acc[...].astype(out_dtype)

    return pl.pallas_call(
        body, name=name, grid=(K // tk, N // tn, n_s),
        in_specs=[pl.BlockSpec((ts, tk), lambda k, n, s: (s, k)), pl.BlockSpec((ts, tn), lambda k, n, s: (s, n))]
        + [ANY] * len(extra),
        out_specs=pl.BlockSpec((tk, tn), lambda k, n, s: (k, n)),
        out_shape=jax.ShapeDtypeStruct((K, N), out_dtype),
        scratch_shapes=[pltpu.VMEM((tk, tn), F32)],
        compiler_params=_params("parallel", "parallel", "arbitrary"),
    )(a, b, *extra)


def _wgrad_pool(p, dyps):
    S = p.shape[0]
    ts = _tile(S, 4096)
    n_s = S // ts

    def body(a_ref, b_ref, o_ref, acc):
        s = pl.program_id(1)

        @pl.when(s == 0)
        def _():
            acc[...] = jnp.zeros_like(acc)
        acc[...] += _tn(a_ref[...], b_ref[...])

        @pl.when(s == n_s - 1)
        def _():
            o_ref[...] = acc[...].astype(BF16)

    return pl.pallas_call(
        body, name="wgrad_pool", grid=(POOL_GROUPS, n_s),
        in_specs=[pl.BlockSpec((ts, POOL_CG), lambda g, s: (s, g)), pl.BlockSpec((ts, POOL_CG), lambda g, s: (s, g))],
        out_specs=pl.BlockSpec((None, POOL_CG, POOL_CG), lambda g, s: (g, 0, 0)),
        out_shape=jax.ShapeDtypeStruct((POOL_GROUPS, POOL_CG, POOL_CG), BF16),
        scratch_shapes=[pltpu.VMEM((POOL_CG, POOL_CG), F32)],
        compiler_params=_params("parallel", "arbitrary"),
    )(p, dyps)


def _dgrad_norm_bwd(dy, w, x, g, dres, name, tk, token=None):
    S, K = dy.shape
    Dm = x.shape[1]
    tm, tk = _tile(S, 1024), _tile(K, tk)
    n_k = K // tk
    extra = [] if token is None else [token]

    def body(dy_ref, w_ref, x_ref, g_ref, r_ref, *rest):
        dx_ref, dg_ref, acc = rest[-3:]
        i, k = pl.program_id(0), pl.program_id(1)

        @pl.when((i == 0) & (k == 0))
        def _():
            dg_ref[...] = jnp.zeros_like(dg_ref)

        @pl.when(k == 0)
        def _():
            acc[...] = jnp.zeros_like(acc)
        acc[...] += _nt(dy_ref[...], w_ref[...])

        @pl.when(k == n_k - 1)
        def _():
            dh, xv = acc[...], x_ref[...]
            r = lax.rsqrt(jnp.mean(xv * xv, axis=-1, keepdims=True) + EPS)
            n = xv * r
            dn = dh * g_ref[...]
            dx_ref[...] = r_ref[...] + r * (dn - n * jnp.mean(dn * n, axis=-1, keepdims=True))
            dg_ref[...] += jnp.sum(dh * n, axis=0, keepdims=True)

    rowblk = pl.BlockSpec((tm, Dm), lambda i, k: (i, 0))
    vec = pl.BlockSpec((1, Dm), lambda i, k: (0, 0))
    return pl.pallas_call(
        body, name=name, grid=(S // tm, n_k),
        in_specs=[pl.BlockSpec((tm, tk), lambda i, k: (i, k)), pl.BlockSpec((Dm, tk), lambda i, k: (0, k)),
                  rowblk, vec, rowblk] + [ANY] * len(extra),
        out_specs=[rowblk, vec],
        out_shape=[jax.ShapeDtypeStruct((S, Dm), F32), jax.ShapeDtypeStruct((1, Dm), F32)],
        scratch_shapes=[pltpu.VMEM((tm, Dm), F32)],
        compiler_params=_params("arbitrary", "arbitrary"),
    )(dy, w, x, g, dres, *extra)


def _mix_bwd(dx2, ya, ypr, yt, proj, ps, wa, wp, wt, wo, token):
    S = dx2.shape[0]
    tm = _tile(S, 256)

    n_tiles = S // tm

    def body(dx_ref, ya_ref, yp_ref, yt_ref, ga0, ga1, gp0, gp1, gt0, gt1, ps_ref, wa_ref, wp_ref, wt_ref, wo_ref,
             token_ref, dya_ref, dyt_ref, dyps_ref, dza_ref, datt_ref, dp_ref, dproj_hbm, dps_ref, dgates, sem):
        i = pl.program_id(0)
        to_dproj = pltpu.make_async_copy(
            dgates, dproj_hbm.at[pl.ds(pl.multiple_of(i * tm, tm), tm), pl.ds(OFF_GA, 3 * D_MODEL)], sem)

        @pl.when(i == 0)
        def _():
            dps_ref[...] = jnp.zeros_like(dps_ref)
        dm = _nt(dx_ref[...].astype(BF16), wo_ref[...])
        sa, sp, st = _gate(ga0, ga1), _gate(gp0, gp1), _gate(gt0, gt1)
        psv = ps_ref[...]
        ypr_v = yp_ref[...].astype(F32)
        dya = (sa * dm).astype(BF16)
        dyt = (st * dm).astype(BF16)
        dyp = sp * dm
        dyps = (dyp * psv).astype(BF16)
        dya_ref[...] = dya
        dyt_ref[...] = dyt
        dyps_ref[...] = dyps
        dg = jnp.concatenate(
            [dm * ya_ref[...].astype(F32) * (sa * (1.0 - sa)), dm * (ypr_v * psv) * (sp * (1.0 - sp)),
             dm * yt_ref[...].astype(F32) * (st * (1.0 - st))], axis=1).astype(BF16)

        @pl.when(i > 0)
        def _():
            to_dproj.wait()
        dgates[...] = dg
        to_dproj.start()
        dps_ref[...] += jnp.sum(dyp * ypr_v, axis=0, keepdims=True)
        dza_ref[...] = _nt(dya, wa_ref[...]).astype(BF16)
        datt_ref[...] = _nt(dyt, wt_ref[...]).astype(BF16)
        dp_ref[...] = jnp.concatenate(
            [_nt(dyps[:, g * POOL_CG:(g + 1) * POOL_CG], wp_ref[g]) for g in range(POOL_GROUPS)], axis=1).astype(BF16)

        @pl.when(i == n_tiles - 1)
        def _():
            to_dproj.wait()

    row = lambda c=0: pl.BlockSpec((tm, D_MODEL), lambda i: (i, c))
    whole = lambda a: pl.BlockSpec(a.shape, lambda i: (0,) * a.ndim)
    act = jax.ShapeDtypeStruct((S, D_MODEL), BF16)
    return pl.pallas_call(
        body, name="mix_bwd", grid=(n_tiles,),
        in_specs=[row(), row(), row(), row()] + _gate_specs(tm)
        + [whole(ps), whole(wa), whole(wp), whole(wt), whole(wo), ANY],
        out_specs=[row()] * 6 + [ANY, pl.BlockSpec((1, D_MODEL), lambda i: (0, 0))],
        out_shape=[act] * 6 + [jax.ShapeDtypeStruct((S, IN_TOTAL), BF16), jax.ShapeDtypeStruct((1, D_MODEL), F32)],
        scratch_shapes=[pltpu.VMEM((tm, 3 * D_MODEL), BF16), pltpu.SemaphoreType.DMA],
        compiler_params=_params("arbitrary"),
    )(dx2, ya, ypr, yt, *([proj] * 6), ps, wa, wp, wt, wo, token)


def _conv_bwd(dza, proj, cw8, dproj):
    S = proj.shape[0]
    nblk = D_MODEL // CB

    def body(d_ref, b_ref, c_ref, x_ref, w_ref, dproj_in, dproj_hbm, dw_ref, pad_u, pad_d, parts, sems):
        cb = pl.program_id(0)
        to_dproj = [pltpu.make_async_copy(
            parts.at[k], dproj_hbm.at[:, pl.ds(pl.multiple_of(off + cb * CB, CB), CB)], sems.at[k])
            for k, off in enumerate((OFF_B, OFF_C, OFF_X))]
        c, xa = c_ref[...].astype(F32), x_ref[...].astype(F32)
        u = c * xa
        _fill_padded(pad_u, u, S)
        u_prev, u_next = _shifted(pad_u, -1, S), _shifted(pad_u, 1, S)
        cv = w_ref[0:1, :] * u_prev + w_ref[1:2, :] * u + w_ref[2:3, :] * u_next
        dza_v = d_ref[...].astype(F32)
        dcv = dza_v * b_ref[...].astype(F32)
        _fill_padded(pad_d, dcv, S)
        du = w_ref[0:1, :] * _shifted(pad_d, 1, S) + w_ref[1:2, :] * dcv + w_ref[2:3, :] * _shifted(pad_d, -1, S)

        @pl.when(cb > 0)
        def _():
            for cp in to_dproj:
                cp.wait()
        parts[0] = (dza_v * cv).astype(BF16)
        parts[1] = (du * xa).astype(BF16)
        parts[2] = (du * c).astype(BF16)
        for cp in to_dproj:
            cp.start()
        dw_ref[...] = jnp.concatenate(
            [jnp.sum(dcv * u_prev, axis=0, keepdims=True), jnp.sum(dcv * u, axis=0, keepdims=True),
             jnp.sum(dcv * u_next, axis=0, keepdims=True), jnp.zeros((5, CB), F32)], axis=0)

        @pl.when(cb == nblk - 1)
        def _():
            for cp in to_dproj:
                cp.wait()

    col = lambda base: pl.BlockSpec((S, CB), lambda cb: (0, base // CB + cb))
    taps = pl.BlockSpec((8, CB), lambda cb: (0, cb))
    return pl.pallas_call(
        body, name="conv_bwd", grid=(nblk,),
        in_specs=[col(0), col(OFF_B), col(OFF_C), col(OFF_X), taps, ANY],
        out_specs=[ANY, taps],
        out_shape=[jax.ShapeDtypeStruct(dproj.shape, dproj.dtype), jax.ShapeDtypeStruct((8, D_MODEL), F32)],
        scratch_shapes=[pltpu.VMEM((S + 2 * HALO, CB), F32), pltpu.VMEM((S + 2 * HALO, CB), F32),
                        pltpu.VMEM((3, S, CB), BF16), pltpu.SemaphoreType.DMA((3,))],
        input_output_aliases={5: 0},
        compiler_params=_params("arbitrary"),
    )(dza, proj, proj, proj, cw8, dproj)


def _pool_bwd(dp, dproj):
    S = dp.shape[0]
    nblk = D_MODEL // CBW
    per_group = POOL_CG // CBW

    def body(d_ref, dproj_in, o_ref, pad):
        d = d_ref[...].astype(F32)
        grp = pl.program_id(0) // per_group
        for gi, w in enumerate(POOL_WINDOWS):
            @pl.when(grp == gi)
            def _(w=w):
                lo, hi = w // 2, w - 1 - w // 2
                _fill_padded(pad, d / _pool_count(S, lo, hi), S)
                acc = _shifted(pad, -hi, S)
                for off in range(-hi + 1, lo + 1):
                    acc = acc + _shifted(pad, off, S)
                o_ref[...] = (acc - d).astype(BF16)

    return pl.pallas_call(
        body, name="pool_bwd", grid=(nblk,),
        in_specs=[pl.BlockSpec((S, CBW), lambda j: (0, j)), ANY],
        out_specs=pl.BlockSpec((S, CBW), lambda j: (0, OFF_U // CBW + j)),
        out_shape=jax.ShapeDtypeStruct(dproj.shape, dproj.dtype),
        scratch_shapes=[pltpu.VMEM((S + 2 * HALO, CBW), F32)],
        input_output_aliases={1: 0},
        compiler_params=_params("parallel"),
    )(dp, dproj)


def _attn_bwd(proj, datt, bias_tabs, sink_rows, dbias_in, dproj):
    S = proj.shape[0]
    nb = S // BLOCK
    kvw = N_KV_HEADS * HEAD_DIM

    def body(q_ref, kp, kc_, kn, vp, vc_, vn, do_ref, bias_ref, sink_ref, dbin_ref, dproj_in,
             dq_ref, dk_ref, dv_ref, db_ref, ds_ref):
        i = pl.program_id(0)

        @pl.when(i == 0)
        def _():
            dk_ref[...] = jnp.zeros_like(dk_ref)
            dv_ref[...] = jnp.zeros_like(dv_ref)
            db_ref[...] = dbin_ref[...]
            ds_ref[...] = jnp.zeros_like(ds_ref)
        dqs, dks, dvs = [], [], []
        for hk in range(N_KV_HEADS):
            q4s = _heads_rows(q_ref, hk) * SM_SCALE
            do4 = _heads_rows(do_ref, hk)
            kc = _kv_rows(kp, kc_, kn, hk)
            vc = _kv_rows(vp, vc_, vn, hk)
            pn, p_sink = _softmax_keys_on_rows(q4s, kc, _bias_cols(bias_ref, hk), sink_ref[hk:hk + 1, :])
            dpm = _nt(vc, do4)
            delta = jnp.sum(pn * dpm, axis=0, keepdims=True)
            dsc = pn * (dpm - delta)
            for g in range(GQA):
                db_ref[GQA * hk + g] += dsc[:, g * BLOCK:(g + 1) * BLOCK]
            ds_ref[hk:hk + 1, :] += -p_sink * delta
            dsb = dsc.astype(BF16)
            dq4 = _tn(dsb, kc) * SM_SCALE
            dqs += [dq4[g * BLOCK:(g + 1) * BLOCK, :] for g in range(GQA)]
            dks.append(_nn(dsb, q4s))
            dvs.append(_nn(pn.astype(BF16), do4))
        dq_ref[...] = jnp.concatenate(dqs, axis=1).astype(BF16)
        r0 = pl.multiple_of(i * BLOCK, BLOCK)
        dk_ref[pl.ds(r0, 3 * BLOCK), :] += jnp.concatenate(dks, axis=1)
        dv_ref[pl.ds(r0, 3 * BLOCK), :] += jnp.concatenate(dvs, axis=1)

    const = lambda shape: pl.BlockSpec(shape, lambda i: (0,) * len(shape))
    sink_shape = (N_KV_HEADS, GQA * BLOCK)
    return pl.pallas_call(
        body, name="attn_bwd", grid=(nb,),
        in_specs=_attn_specs(S) + [pl.BlockSpec((BLOCK, D_MODEL), lambda i: (i, 0)),
                                   _bias_spec(nb), const(sink_shape), const(TAB), ANY],
        out_specs=[pl.BlockSpec((BLOCK, D_MODEL), lambda i: (i, OFF_Q // D_MODEL)),
                   const((S + 2 * BLOCK, kvw)), const((S + 2 * BLOCK, kvw)), const(TAB), const(sink_shape)],
        out_shape=[jax.ShapeDtypeStruct(dproj.shape, dproj.dtype),
                   jax.ShapeDtypeStruct((S + 2 * BLOCK, kvw), F32), jax.ShapeDtypeStruct((S + 2 * BLOCK, kvw), F32),
                   jax.ShapeDtypeStruct(TAB, F32), jax.ShapeDtypeStruct(sink_shape, F32)],
        input_output_aliases={11: 0},
        compiler_params=_params("arbitrary"),
    )(*([proj] * 7), datt, bias_tabs, sink_rows, dbias_in, dproj)


def _kv_finish(dkp, dvp, dproj):
    S = dproj.shape[0]
    kvw = N_KV_HEADS * HEAD_DIM

    def body(dk_ref, dv_ref, dproj_in, o_ref):
        o_ref[:, 0:kvw] = dk_ref[pl.ds(BLOCK, S), :].astype(BF16)
        o_ref[:, kvw:2 * kvw] = dv_ref[pl.ds(BLOCK, S), :].astype(BF16)

    whole = pl.BlockSpec((S + 2 * BLOCK, kvw), lambda i: (0, 0))
    return pl.pallas_call(
        body, name="kv_finish", grid=(1,), in_specs=[whole, whole, ANY],
        out_specs=pl.BlockSpec((S, 2 * kvw), lambda i: (0, OFF_K // (2 * kvw))),
        out_shape=jax.ShapeDtypeStruct(dproj.shape, dproj.dtype),
        input_output_aliases={2: 0},
        compiler_params=_params("arbitrary"),
    )(dkp, dvp, dproj)


def _bucket_constants():
    half = N_BUCKETS // 2
    max_exact = half // 2
    qi = np.arange(BLOCK)[None, :]
    kj = np.arange(3 * BLOCK)[:, None]
    rel = kj - BLOCK - qi
    n = np.abs(rel)
    nf = np.maximum(n, 1).astype(np.float32)
    large = max_exact + (np.log(nf / np.float32(max_exact)) / np.float32(math.log(MAX_DISTANCE / max_exact))
                         * np.float32(half - max_exact)).astype(np.int32)
    large = np.minimum(large, half - 1)
    bucket = np.where(rel > 0, half, 0) + np.where(n < max_exact, n, large)
    onehot = (bucket.reshape(1, -1) == np.arange(N_BUCKETS)[:, None]).astype(np.float32)
    window = n <= WINDOW
    first = window & (kj >= BLOCK)
    last = window & (kj < 2 * BLOCK)
    masks = np.stack([np.where(v, 0.0, NEG_INF).astype(np.float32).reshape(-1) for v in (first, window, last)])
    return onehot, masks


def _bias_expand(rel_bias_t, onehot, masks):
    def body(r_ref, oh_ref, m_ref, o_ref):
        tab = jnp.dot(r_ref[...], oh_ref[...], preferred_element_type=F32, precision=lax.Precision.HIGHEST)
        for v in range(3):
            o_ref[v] = tab + m_ref[v:v + 1, :]

    return pl.pallas_call(
        body, name="bias_expand", out_shape=jax.ShapeDtypeStruct((3, N_HEADS, onehot.shape[1]), F32),
        compiler_params=_params(),
    )(rel_bias_t, onehot, masks)


def _bias_reduce(dtab, dsink_rows, onehot):
    def body(d_ref, s_ref, oh_ref, o_ref, so_ref):
        o_ref[...] = lax.dot_general(oh_ref[...], d_ref[...], (((1,), (1,)), ((), ())),
                                     preferred_element_type=F32, precision=lax.Precision.HIGHEST)
        so_ref[...] = jnp.sum(s_ref[...], axis=-1, keepdims=True)

    return pl.pallas_call(
        body, name="bias_reduce",
        out_shape=[jax.ShapeDtypeStruct((N_BUCKETS, N_HEADS), F32),
                   jax.ShapeDtypeStruct((dsink_rows.shape[0], 1), F32)],
        compiler_params=_params(),
    )(dtab, dsink_rows, onehot)


GROUPS = dict(mix=("w_in", "conv_w", "w_a_out", "w_pool", "w_attn_out", "w_o"), ffn=("w_gu", "w_down"))
WEIGHT_NAMES = GROUPS["mix"] + GROUPS["ffn"]


def _layer_fwd(l, x, weights_of, ps, g_mix, g_ffn, bias_tabs, sink_rows):
    W, token = weights_of(l, "mix", x)
    h, proj = _norm_matmul(x, g_mix, W["w_in"], "norm_proj", token)
    za = _conv_fwd(proj, W["conv_w"])
    p = _pool_fwd(proj)
    att = _attn_fwd(proj, bias_tabs, sink_rows)
    ya, ypr, yt, merged, x2 = _mix_fwd(za, p, att, proj, x, W["w_a_out"], W["w_pool"], ps, W["w_attn_out"], W["w_o"])
    Wf, token = weights_of(l, "ffn", x2)
    h2, gu = _norm_matmul(x2, g_ffn, Wf["w_gu"], "norm_gu", token)
    act, x3 = _ffn_fwd(gu, x2, Wf["w_down"])
    saved = dict(x=x, h=h, proj=proj, za=za, p=p, att=att, ya=ya, ypr=ypr, yt=yt, merged=merged, x2=x2, h2=h2,
                 gu=gu, act=act, W={**W, **Wf}, sink_rows=sink_rows)
    return x3, saved, token


def _layer_bwd(l, dx3, sv, grads_to, ps, g_mix, g_ffn, bias_tabs, dbias, token):
    W, sink_rows = sv["W"], sv["sink_rows"]
    dgu = _ffn_bwd(dx3, sv["gu"], W["w_down"], token)
    g_w_down = _wgrad(sv["act"], dx3, "wgrad_down", tk=1408, tn=1024, token=token)
    g_w_gu = _wgrad(sv["h2"], dgu, "wgrad_gu", tk=1024, tn=1408)
    dx2, dg_ffn = _dgrad_norm_bwd(dgu, W["w_gu"], sv["x2"], g_ffn, dx3, "dgrad_gu", tk=1408)
    token = grads_to(l, "ffn", dict(w_gu=g_w_gu, w_down=g_w_down), dx2)
    dya, dyt, dyps, dza, datt, dp, dproj, dps = _mix_bwd(
        dx2, sv["ya"], sv["ypr"], sv["yt"], sv["proj"], ps, W["w_a_out"], W["w_pool"], W["w_attn_out"], W["w_o"], token)
    g_w_o = _wgrad(sv["merged"], dx2, "wgrad_sq_f32", tk=1024, tn=1024)
    g_w_a_out = _wgrad(sv["za"], dya, "wgrad_sq", tk=1024, tn=1024)
    g_w_attn_out = _wgrad(sv["att"], dyt, "wgrad_sq", tk=1024, tn=1024)
    g_w_pool = _wgrad_pool(sv["p"], dyps)
    dproj, g_conv = _conv_bwd(dza, sv["proj"], W["conv_w"], dproj)
    dproj = _pool_bwd(dp, dproj)
    dproj, dkp, dvp, dbias, dsink = _attn_bwd(sv["proj"], datt, bias_tabs, sink_rows, dbias, dproj)
    dproj = _kv_finish(dkp, dvp, dproj)
    g_w_in = _wgrad(sv["h"], dproj, "wgrad_in", tk=1024, tn=2176)
    token = grads_to(l, "mix", dict(w_in=g_w_in, conv_w=g_conv, w_a_out=g_w_a_out, w_pool=g_w_pool,
                                    w_attn_out=g_w_attn_out, w_o=g_w_o), dproj)
    dx, dg_mix = _dgrad_norm_bwd(dproj, W["w_in"], sv["x"], g_mix, dx2, "dgrad_in", tk=2176, token=token)
    return dx, dict(pool_scale=dps, g_mix=dg_mix, g_ffn=dg_ffn, attn_sink=dsink), dbias, token


def _local_step(x, tgt, weights_of, grads_to, pool_scale, attn_sink, g_mix, g_ffn, rel_bias, g_final):
    onehot_np, masks_np = _bucket_constants()
    onehot, masks = jnp.asarray(onehot_np), jnp.asarray(masks_np)
    bias_tabs = _bias_expand(rel_bias.T, onehot, masks).reshape((3,) + TAB)
    saved = []
    for l in range(DEPTH):
        sink_rows = jnp.repeat(attn_sink[l], BLOCK).reshape(N_KV_HEADS, GQA * BLOCK)
        x, sv, token = _layer_fwd(l, x, weights_of, pool_scale[l:l + 1], g_mix[l:l + 1], g_ffn[l:l + 1], bias_tabs,
                                  sink_rows)
        saved.append(sv)
    loss, dx, dg_final = _loss_bwd(x, g_final.reshape(1, D_MODEL), tgt)
    dbias = jnp.zeros(TAB, F32)
    small = [None] * DEPTH
    for l in reversed(range(DEPTH)):
        dx, small[l], dbias, token = _layer_bwd(
            l, dx, saved[l], grads_to, pool_scale[l:l + 1], g_mix[l:l + 1], g_ffn[l:l + 1], bias_tabs, dbias, token)
    dsink_rows = jnp.concatenate([small[l]["attn_sink"].reshape(N_HEADS, BLOCK) for l in range(DEPTH)], axis=0)
    d_rel_bias, d_sink = _bias_reduce(dbias.reshape(N_HEADS, TAB_FLAT), dsink_rows, onehot)
    cat = lambda k: jnp.concatenate([small[l][k] for l in range(DEPTH)], axis=0)
    smalls = dict(pool_scale=cat("pool_scale"), g_mix=cat("g_mix"), g_ffn=cat("g_ffn"),
                  attn_sink=d_sink.reshape(DEPTH, N_HEADS), rel_bias=d_rel_bias, g_final=dg_final)
    return loss[0, 0], dx, smalls


SHARD_AXIS = dict(w_in=(1, IN_TOTAL // N_CHIPS), conv_w=(1, D_MODEL // N_CHIPS), w_a_out=(0, D_MODEL // N_CHIPS),
                  w_pool=(1, POOL_CG // N_CHIPS), w_attn_out=(0, D_MODEL // N_CHIPS), w_o=(0, D_MODEL // N_CHIPS),
                  w_gu=(1, 2 * D_FF // N_CHIPS), w_down=(0, D_FF // N_CHIPS))
HBM = pl.BlockSpec(memory_space=pltpu.HBM)
SEM = pl.BlockSpec(memory_space=pltpu.SEMAPHORE)
DATAFLOW = pltpu.SideEffectType.DATAFLOW_SIDE_EFFECTING
TOKEN = jax.ShapeDtypeStruct((8, 128), F32)


def _shard_of(ref, name, chip):
    axis, n = SHARD_AXIS[name]
    idx = [slice(None)] * len(ref.shape)
    idx[axis] = pl.ds(chip * n, n)
    return ref.at[tuple(idx)]


def _with_shard_axis(name, shape, size):
    axis, _ = SHARD_AXIS[name]
    s = list(shape)
    s[axis] = size
    return tuple(s)


HALF_AXIS = dict(w_in=0, conv_w=1, w_a_out=0, w_pool=1, w_attn_out=0, w_o=0, w_gu=0, w_down=0)


def _half_of_shard(ref, name, core):
    axis = HALF_AXIS[name]
    n = ref.shape[axis] // 2
    idx = [slice(None)] * len(ref.shape)
    idx[axis] = pl.ds(core * n, n)
    return ref.at[tuple(idx)]


def _half_in_full(ref, name, chip, core):
    saxis, n = SHARD_AXIS[name]
    haxis = HALF_AXIS[name]
    idx = [slice(None)] * len(ref.shape)
    if haxis == saxis:
        idx[saxis] = pl.ds(chip * n + core * (n // 2), n // 2)
    else:
        h = ref.shape[haxis] // 2
        idx[saxis] = pl.ds(chip * n, n)
        idx[haxis] = pl.ds(core * h, h)
    return ref.at[tuple(idx)]


def _on_each_device(fn):
    me = 2 * lax.axis_index("x") + lax.axis_index("y")
    c = lax.axis_index("c")
    for chip in range(N_CHIPS):
        for core in range(2):
            pl.when((me == chip) & (c == core))(functools.partial(fn, chip, core))


def _chip_peers(x, y):
    return [(1 - x, y), (x, 1 - y), (1 - x, 1 - y)]


RELATION_XOR = (2, 1, 3)


def _group_copies(kind, group, srcs, lands, send_sems, recv_sems, local_sems, chip, core):
    x, y, c = lax.axis_index("x"), lax.axis_index("y"), lax.axis_index("c")
    copies = []
    for t, name in enumerate(GROUPS[group]):
        for j, (px, py) in enumerate(_chip_peers(x, y)):
            if kind == "gather":
                src, dst = _half_of_shard(srcs[t], name, core), _half_in_full(lands[t], name, chip, core)
            else:
                src, dst = _shard_of(srcs[t], name, chip ^ RELATION_XOR[j]), lands[t].at[j]
            copies.append(pltpu.make_async_remote_copy(
                src_ref=src, dst_ref=dst, send_sem=send_sems.at[3 * t + j], recv_sem=recv_sems.at[3 * t + j],
                device_id=(px, py, c), device_id_type=MESH))
        if kind == "gather":
            src, dst = srcs[t], _shard_of(lands[t], name, chip)
        else:
            src, dst = _shard_of(srcs[t], name, chip), lands[t].at[N_CHIPS - 1]
        copies.append(pltpu.make_async_copy(src, dst, local_sems.at[t]))
    return copies


def _exchange_start(kind, group, srcs, land_shapes, after):
    nw = len(GROUPS[group])

    def body(*refs):
        srcs_r, lands_r = refs[:nw], refs[nw:2 * nw]
        send_sems, recv_sems, local_sems = refs[2 * nw + 1:2 * nw + 4]
        token = refs[-1]

        def issue(chip, core):
            for cp in _group_copies(kind, group, srcs_r, lands_r, send_sems, recv_sems, local_sems, chip, core):
                cp.start()
        _on_each_device(issue)
        token[...] = jnp.zeros_like(token)

    lands = [pltpu.with_memory_space_constraint(lax.empty(s.shape, s.dtype), pltpu.HBM) for s in land_shapes]
    srcs = [pltpu.with_memory_space_constraint(a, pltpu.HBM) for a in srcs]
    thru = [pltpu.HBM(a.shape, a.dtype) for a in srcs + lands]
    outs = pl.pallas_call(
        body, name=f"{kind}_{group}_start",
        in_specs=[HBM] * (2 * nw) + [ANY],
        out_specs=[SEM, SEM, SEM] + [HBM] * (2 * nw) + [pl.BlockSpec(memory_space=pltpu.VMEM)],
        out_shape=[pltpu.SemaphoreType.DMA((3 * nw,)), pltpu.SemaphoreType.DMA((3 * nw,)),
                   pltpu.SemaphoreType.DMA((nw,))] + thru + [TOKEN],
        input_output_aliases={t: 3 + t for t in range(2 * nw)},
        compiler_params=pltpu.CompilerParams(has_side_effects=DATAFLOW),
    )(*srcs, *lands, after)
    return dict(sems=outs[0:3], srcs=outs[3:3 + nw], lands=outs[3 + nw:3 + 2 * nw], token=outs[-1])


def _exchange_wait(kind, group, started, after):
    nw = len(GROUPS[group])

    def body(*refs):
        srcs_r, lands_r = refs[:nw], refs[nw:2 * nw]
        send_sems, recv_sems, local_sems = refs[2 * nw:2 * nw + 3]
        for cp in _group_copies(kind, group, srcs_r, lands_r, send_sems, recv_sems, local_sems, 0, 0):
            cp.wait()

    srcs, lands = list(started["srcs"]), list(started["lands"])
    outs = pl.pallas_call(
        body, name=f"{kind}_{group}_wait",
        in_specs=[HBM] * (2 * nw) + [SEM, SEM, SEM, ANY],
        out_specs=[HBM] * (2 * nw),
        out_shape=[pltpu.HBM(a.shape, a.dtype) for a in srcs + lands],
        input_output_aliases={t: t for t in range(2 * nw)},
        compiler_params=pltpu.CompilerParams(has_side_effects=DATAFLOW),
    )(*srcs, *lands, *started["sems"], after)
    return dict(zip(GROUPS[group], outs[nw:]))


def _gather_start(group, shards, after):
    names = GROUPS[group]
    shapes = [jax.ShapeDtypeStruct(_with_shard_axis(n, shards[n].shape, SHARD_AXIS[n][1] * N_CHIPS), shards[n].dtype)
              for n in names]
    return _exchange_start("gather", group, [shards[n] for n in names], shapes, after)


def _scatter_start(group, grads, after):
    names = GROUPS[group]
    shapes = [jax.ShapeDtypeStruct((N_CHIPS,) + _with_shard_axis(n, grads[n].shape, SHARD_AXIS[n][1]), grads[n].dtype)
              for n in names]
    return _exchange_start("scatter", group, [grads[n] for n in names], shapes, after)


def _sibling_exchange(parts):
    n = len(parts)

    def body(*refs):
        ins, outs = refs[:n], refs[n:2 * n]
        send_sems, recv_sems = refs[2 * n:]
        sibling = (lax.axis_index("x"), lax.axis_index("y"), 1 - lax.axis_index("c"))
        copies = [pltpu.make_async_remote_copy(src_ref=ins[t], dst_ref=outs[t], send_sem=send_sems.at[t],
                                               recv_sem=recv_sems.at[t], device_id=sibling, device_id_type=MESH)
                  for t in range(n)]
        for cp in copies:
            cp.start()
        for cp in copies:
            cp.wait()

    outs = pl.pallas_call(
        body, name="sibling_exchange", in_specs=[ANY] * n, out_specs=[ANY] * n,
        out_shape=[jax.ShapeDtypeStruct(p.shape, p.dtype) for p in parts],
        scratch_shapes=[pltpu.SemaphoreType.DMA((n,)), pltpu.SemaphoreType.DMA((n,))],
        compiler_params=pltpu.CompilerParams(has_side_effects=True),
    )(*parts)
    return list(outs)


def _sibling_fill(group, fulls):
    names = GROUPS[group]
    nw = len(names)

    def body(*refs):
        ins, outs = refs[:nw], refs[nw:2 * nw]
        send_sems, recv_sems = refs[2 * nw:]
        sibling = (lax.axis_index("x"), lax.axis_index("y"), 1 - lax.axis_index("c"))

        def forward(chip, core):
            copies = []
            for t, name in enumerate(names):
                for j in range(3):
                    other = chip ^ RELATION_XOR[j]
                    copies.append(pltpu.make_async_remote_copy(
                        src_ref=_half_in_full(ins[t], name, other, core),
                        dst_ref=_half_in_full(outs[t], name, other, core),
                        send_sem=send_sems.at[3 * t + j], recv_sem=recv_sems.at[3 * t + j],
                        device_id=sibling, device_id_type=MESH))
            for cp in copies:
                cp.start()
            for cp in copies:
                cp.wait()
        _on_each_device(forward)

    arrays = [fulls[n] for n in names]
    outs = pl.pallas_call(
        body, name=f"sibling_fill_{group}", in_specs=[ANY] * nw, out_specs=[ANY] * nw,
        out_shape=[jax.ShapeDtypeStruct(a.shape, a.dtype) for a in arrays],
        scratch_shapes=[pltpu.SemaphoreType.DMA((3 * nw,)), pltpu.SemaphoreType.DMA((3 * nw,))],
        input_output_aliases={t: t for t in range(nw)},
        compiler_params=pltpu.CompilerParams(has_side_effects=True),
    )(*arrays)
    return dict(zip(names, outs))


N_DEV = 8


def _all_reduce_small(v, after):
    R, C = v.shape

    def body(v_ref, after_ref, o_ref, slots, send_sems, recv_sems):
        x, y, c = lax.axis_index("x"), lax.axis_index("y"), lax.axis_index("c")
        me = 4 * x + 2 * y + c
        slots[me] = v_ref[...]
        copies = []
        for k in range(1, N_DEV):
            peer = me ^ k
            cp = pltpu.make_async_remote_copy(
                src_ref=v_ref, dst_ref=slots.at[me], send_sem=send_sems.at[k - 1], recv_sem=recv_sems.at[k - 1],
                device_id=(peer // 4, (peer // 2) % 2, peer % 2), device_id_type=MESH)
            cp.start()
            copies.append(cp)
        for cp in copies:
            cp.wait()
        acc = slots[0]
        for k in range(1, N_DEV):
            acc = acc + slots[k]
        o_ref[...] = acc

    return pl.pallas_call(
        body, name="all_reduce_small", out_shape=jax.ShapeDtypeStruct((R, C), F32),
        in_specs=[pl.BlockSpec(memory_space=pltpu.VMEM), ANY], out_specs=pl.BlockSpec(memory_space=pltpu.VMEM),
        scratch_shapes=[pltpu.VMEM((N_DEV, R, C), F32), pltpu.SemaphoreType.DMA((N_DEV - 1,)),
                        pltpu.SemaphoreType.DMA((N_DEV - 1,))],
        compiler_params=pltpu.CompilerParams(has_side_effects=True),
    )(v, after)


def _as2d(shape):
    return (int(np.prod(shape[:-1])), shape[-1])


def _row_block(rows, cols, n_arrays):
    budget = V7X_VMEM_LIMIT // 2
    tr = rows
    while tr % 16 == 0 and 2 * n_arrays * tr * cols * 4 > budget:
        tr //= 2
    return tr


def _sum_slots(slots):
    _, R, C = slots.shape
    tr = _row_block(R, C, 5)

    def body(s_ref, o_ref):
        acc = s_ref[0].astype(F32)
        for k in range(1, N_CHIPS):
            acc = acc + s_ref[k].astype(F32)
        o_ref[...] = acc.astype(BF16)

    return pl.pallas_call(
        body, name="sum_slots", grid=(R // tr,),
        in_specs=[pl.BlockSpec((N_CHIPS, tr, C), lambda i: (0, i, 0))],
        out_specs=pl.BlockSpec((tr, C), lambda i: (i, 0)),
        out_shape=jax.ShapeDtypeStruct((R, C), BF16),
        compiler_params=_params("parallel"),
    )(slots)


def _adamw(l, w, m, v, g_a, g_b, prev):
    L, R, C = w.shape
    tr = _row_block(R, C, 9)
    c1 = 1.0 - ADAM_B1 ** ADAM_STEP
    c2 = 1.0 - ADAM_B2 ** ADAM_STEP

    def body(w_ref, m_ref, v_ref, a_ref, b_ref, *rest):
        g_ref, d_ref, nm_ref, nv_ref = rest[-4:]
        g = a_ref[...].astype(F32) + b_ref[...].astype(F32)
        nm = ADAM_B1 * m_ref[...] + (1.0 - ADAM_B1) * g
        nv = ADAM_B2 * v_ref[...] + (1.0 - ADAM_B2) * (g * g)
        g_ref[...] = g
        nm_ref[...] = nm
        nv_ref[...] = nv
        d_ref[...] = -ADAM_LR * ((nm / c1) / (jnp.sqrt(nv / c2) + ADAM_EPS) + ADAM_WD * w_ref[...])

    layer = pl.BlockSpec((None, tr, C), lambda i: (l, i, 0))
    blk = pl.BlockSpec((tr, C), lambda i: (i, 0))
    out = jax.ShapeDtypeStruct((L, R, C), F32)
    prev = [] if prev is None else list(prev)
    return pl.pallas_call(
        body, name="adamw", grid=(R // tr,), in_specs=[layer] * 3 + [blk] * 2 + [ANY] * len(prev),
        out_specs=[layer] * 4, out_shape=[out] * 4,
        input_output_aliases={5 + k: k for k in range(len(prev))},
        compiler_params=_params("parallel"),
    )(w, m, v, g_a, g_b, *prev)


SMALL_ROWS = 16


def _pack_small(pool_scale, g_mix, g_ffn, g_final, attn_sink, rel_bias):
    tail = jnp.concatenate([attn_sink.reshape(-1), rel_bias.reshape(-1)])
    tail = jnp.pad(tail, (0, D_MODEL - tail.shape[0])).reshape(1, D_MODEL)
    rows = jnp.concatenate([pool_scale, g_mix, g_ffn, g_final.reshape(1, D_MODEL), tail], axis=0)
    return jnp.pad(rows, ((0, SMALL_ROWS - rows.shape[0]), (0, 0)))


def _unpack_small(packed):
    n_sink = DEPTH * N_HEADS
    return dict(pool_scale=packed[0:4], g_mix=packed[4:8], g_ffn=packed[8:12], g_final=packed[12],
                attn_sink=packed[13, 0:n_sink].reshape(DEPTH, N_HEADS),
                rel_bias=packed[13, n_sink:n_sink + N_BUCKETS * N_HEADS].reshape(N_BUCKETS, N_HEADS))


def _group_shards(l, group, masters):
    out = {}
    for n in GROUPS[group]:
        w = masters[n][l]
        out[n] = jnp.pad(w.reshape(3, -1), ((0, 5), (0, 0))) if n == "conv_w" else w.astype(BF16)
    return out


def kernel(x, w_in, conv_w, w_a_out, w_pool, pool_scale, w_attn_out, attn_sink, w_o, g_mix, g_ffn, w_gu, w_down, rel_bias, g_final, loss_target, m_w_in, m_conv_w, m_w_a_out, m_w_pool, m_pool_scale, m_w_attn_out, m_attn_sink, m_w_o, m_g_mix, m_g_ffn, m_w_gu, m_w_down, m_rel_bias, m_g_final, v_w_in, v_conv_w, v_w_a_out, v_w_pool, v_pool_scale, v_w_attn_out, v_attn_sink, v_w_o, v_g_mix, v_g_ffn, v_w_gu, v_w_down, v_rel_bias, v_g_final):
    big = dict(w_in=(w_in, m_w_in, v_w_in), conv_w=(conv_w, m_conv_w, v_conv_w), w_a_out=(w_a_out, m_w_a_out, v_w_a_out),
               w_pool=(w_pool, m_w_pool, v_w_pool), w_attn_out=(w_attn_out, m_w_attn_out, v_w_attn_out),
               w_o=(w_o, m_w_o, v_w_o), w_gu=(w_gu, m_w_gu, v_w_gu), w_down=(w_down, m_w_down, v_w_down))

    big3 = {n: tuple(a.reshape((DEPTH,) + _as2d(a.shape[1:])) for a in big[n]) for n in WEIGHT_NAMES}
    masters = {n: big[n][0] for n in WEIGHT_NAMES}

    gathers = {(0, "mix"): _gather_start("mix", _group_shards(0, "mix", masters), rel_bias)}
    newest = {"token": gathers[0, "mix"]["token"]}
    masters = dict(zip(WEIGHT_NAMES, lax.optimization_barrier(
        (tuple(masters[n] for n in WEIGHT_NAMES), newest["token"]))[0]))

    def weights_of(l, group, a):
        W = _sibling_fill(group, _exchange_wait("gather", group, gathers.pop((l, group)), a))
        if group == "mix":
            gathers[l, "ffn"] = _gather_start("ffn", _group_shards(l, "ffn", masters), W["w_in"])
            newest["token"] = gathers[l, "ffn"]["token"]
            if l + 1 < DEPTH:
                gathers[l + 1, "mix"] = _gather_start("mix", _group_shards(l + 1, "mix", masters), newest["token"])
                newest["token"] = gathers[l + 1, "mix"]["token"]
        return W, newest["token"]

    results = {n: None for n in WEIGHT_NAMES}
    scatters = {}

    def finish(l, after):
        slots = {}
        for group in GROUPS:
            slots.update(_exchange_wait("scatter", group, scatters.pop((l, group)), after))
        parts = [_sum_slots(slots[n].reshape((N_CHIPS,) + _as2d(slots[n].shape[1:]))) for n in WEIGHT_NAMES]
        others = _sibling_exchange(parts)
        for n, mine, other in zip(WEIGHT_NAMES, parts, others):
            if n == "conv_w":
                mine, other = mine[0:3], other[0:3]
            results[n] = _adamw(l, *big3[n], mine, other, results[n])

    def grads_to(l, group, wgrads, a):
        scatters[l, group] = _scatter_start(group, wgrads, a)
        token = scatters[l, group]["token"]
        if group == "mix" and l + 1 < DEPTH:
            finish(l + 1, token)
        return token

    loss, grad_x, smalls = _local_step(x[0], loss_target[0], weights_of, grads_to, pool_scale, attn_sink, g_mix, g_ffn,
                                       rel_bias, g_final)
    finish(0, grad_x)
    stacked = {n: [o.reshape(big[n][0].shape) for o in results[n]] for n in WEIGHT_NAMES}

    g_small = _all_reduce_small(_pack_small(smalls["pool_scale"], smalls["g_mix"], smalls["g_ffn"], smalls["g_final"],
                                            smalls["attn_sink"], smalls["rel_bias"]), results["w_in"][0])
    w_small = _pack_small(pool_scale, g_mix, g_ffn, g_final, attn_sink, rel_bias)
    m_small = _pack_small(m_pool_scale, m_g_mix, m_g_ffn, m_g_final, m_attn_sink, m_rel_bias)
    v_small = _pack_small(v_pool_scale, v_g_mix, v_g_ffn, v_g_final, v_attn_sink, v_rel_bias)
    small_out = [_unpack_small(o[0]) for o in
                 _adamw(0, w_small[None], m_small[None], v_small[None], g_small, jnp.zeros_like(g_small), None)]

    total_loss = lax.psum(loss, ("x", "y", "c"))

    order = ("w_in", "conv_w", "w_a_out", "w_pool", "pool_scale", "w_attn_out", "attn_sink", "w_o", "g_mix", "g_ffn",
             "w_gu", "w_down", "rel_bias", "g_final")
    outs = [total_loss, grad_x[None]]
    for k in range(4):
        for n in order:
            outs.append(stacked[n][k] if n in stacked else small_out[k][n])
    return tuple(outs)
```

```python
import functools
import math

import numpy as np
import jax
import jax.numpy as jnp
from jax import lax
from jax.experimental import pallas as pl
from jax.experimental.pallas import tpu as pltpu

F32 = jnp.float32
BF16 = jnp.bfloat16

D_MODEL = 1024
DEPTH = 4
N_HEADS = 16
N_KV_HEADS = 4
HEAD_DIM = 64
GQA = N_HEADS // N_KV_HEADS
WINDOW = 128
BLOCK = 128
N_BUCKETS = 32
MAX_DISTANCE = 128
POOL_GROUPS = 4
POOL_CG = D_MODEL // POOL_GROUPS
POOL_WINDOWS = (2, 4, 8, 16)
D_FF = 2816
IN_TOTAL = 8704
OFF_B, OFF_C, OFF_X, OFF_U, OFF_Q, OFF_K, OFF_V, OFF_GA, OFF_GP, OFF_GT = (
    0, 1024, 2048, 3072, 4096, 5120, 5376, 5632, 6656, 7680)
EPS = 1e-6
NEG_INF = -1e30
SM_SCALE = HEAD_DIM ** -0.5

ADAM_LR = 0.001
ADAM_B1 = 0.9
ADAM_B2 = 0.999
ADAM_EPS = 1e-08
ADAM_WD = 0.01
ADAM_STEP = 10

N_CHIPS = 4
HALO = 8
V7X_VMEM_LIMIT = 56 * 1024 * 1024
MESH = pl.DeviceIdType.MESH
ANY = pl.BlockSpec(memory_space=pl.ANY)


def _params(*sem):
    return pltpu.CompilerParams(dimension_semantics=tuple(sem) if sem else None,
                                vmem_limit_bytes=V7X_VMEM_LIMIT)


def _tile(n, pref):
    t = min(pref, n)
    while n % t or t % 128:
        t -= 128
    return t


def _nt(a, b):
    return lax.dot_general(a, b, (((1,), (1,)), ((), ())), preferred_element_type=F32)


def _tn(a, b):
    return lax.dot_general(a, b, (((0,), (0,)), ((), ())), preferred_element_type=F32)


def _nn(a, b):
    return jnp.dot(a, b, preferred_element_type=F32)


def _sigmoid(v):
    return 1.0 / (1.0 + jnp.exp(-v))


def _norm_matmul(x, g, w, name, token):
    S, Dm = x.shape
    N = w.shape[1]
    tm, tn = _tile(S, 1024), _tile(N, N // 4)

    def body(x_ref, g_ref, w_ref, token_ref, h_ref, o_ref):
        @pl.when(pl.program_id(1) == 0)
        def _():
            xv = x_ref[...]
            r = lax.rsqrt(jnp.mean(xv * xv, axis=-1, keepdims=True) + EPS)
            h_ref[...] = (xv * r * g_ref[...]).astype(BF16)
        o_ref[...] = _nn(h_ref[...], w_ref[...]).astype(BF16)

    return pl.pallas_call(
        body, name=name, grid=(S // tm, N // tn),
        in_specs=[pl.BlockSpec((tm, Dm), lambda i, j: (i, 0)),
                  pl.BlockSpec((1, Dm), lambda i, j: (0, 0)),
                  pl.BlockSpec((Dm, tn), lambda i, j: (0, j)), ANY],
        out_specs=[pl.BlockSpec((tm, Dm), lambda i, j: (i, 0)),
                   pl.BlockSpec((tm, tn), lambda i, j: (i, j))],
        out_shape=[jax.ShapeDtypeStruct((S, Dm), BF16), jax.ShapeDtypeStruct((S, N), BF16)],
        compiler_params=_params("parallel", "arbitrary"),
    )(x, g, w, token)


CB = 128
CBW = 128


def _fill_padded(pad_ref, v, S):
    z = jnp.zeros((HALO, v.shape[1]), F32)
    pad_ref[pl.ds(0, HALO), :] = z
    pad_ref[pl.ds(S + HALO, HALO), :] = z
    pad_ref[pl.ds(HALO, S), :] = v


def _shifted(pad_ref, off, S):
    return pad_ref[pl.ds(HALO + off, S), :]


def _conv_fwd(proj, cw8):
    S = proj.shape[0]
    nblk = D_MODEL // CBW

    def body(b_ref, c_ref, x_ref, w_ref, o_ref, pad):
        u = c_ref[...].astype(F32) * x_ref[...].astype(F32)
        _fill_padded(pad, u, S)
        cv = w_ref[0:1, :] * _shifted(pad, -1, S) + w_ref[1:2, :] * u + w_ref[2:3, :] * _shifted(pad, 1, S)
        o_ref[...] = (b_ref[...].astype(F32) * cv).astype(BF16)

    col = lambda base: pl.BlockSpec((S, CBW), lambda j: (0, base // CBW + j))
    return pl.pallas_call(
        body, name="conv_fwd", grid=(nblk,),
        in_specs=[col(OFF_B), col(OFF_C), col(OFF_X), pl.BlockSpec((8, CBW), lambda j: (0, j))],
        out_specs=pl.BlockSpec((S, CBW), lambda j: (0, j)),
        out_shape=jax.ShapeDtypeStruct((S, D_MODEL), BF16),
        scratch_shapes=[pltpu.VMEM((S + 2 * HALO, CBW), F32)],
        compiler_params=_params("parallel"),
    )(proj, proj, proj, cw8)


def _pool_count(S, lo, hi):
    t = lax.broadcasted_iota(jnp.int32, (S, CBW), 0)
    return (jnp.minimum(t + hi, S - 1) - jnp.maximum(t - lo, 0) + 1).astype(F32)


def _pool_fwd(proj):
    S = proj.shape[0]
    nblk = D_MODEL // CBW
    per_group = POOL_CG // CBW

    def body(u_ref, o_ref, pad):
        u = u_ref[...].astype(F32)
        _fill_padded(pad, u, S)
        grp = pl.program_id(0) // per_group
        for gi, w in enumerate(POOL_WINDOWS):
            @pl.when(grp == gi)
            def _(w=w):
                lo, hi = w // 2, w - 1 - w // 2
                acc = _shifted(pad, -lo, S)
                for off in range(-lo + 1, hi + 1):
                    acc = acc + _shifted(pad, off, S)
                o_ref[...] = (acc / _pool_count(S, lo, hi) - u).astype(BF16)

    return pl.pallas_call(
        body, name="pool_fwd", grid=(nblk,),
        in_specs=[pl.BlockSpec((S, CBW), lambda j: (0, OFF_U // CBW + j))],
        out_specs=pl.BlockSpec((S, CBW), lambda j: (0, j)),
        out_shape=jax.ShapeDtypeStruct((S, D_MODEL), BF16),
        scratch_shapes=[pltpu.VMEM((S + 2 * HALO, CBW), F32)],
        compiler_params=_params("parallel"),
    )(proj)


def _attn_specs(S):
    nb = S // BLOCK
    kcol, vcol = OFF_K // (N_KV_HEADS * HEAD_DIM), OFF_V // (N_KV_HEADS * HEAD_DIM)
    kvw = N_KV_HEADS * HEAD_DIM
    prev = lambda i: jnp.maximum(i - 1, 0)
    nxt = lambda i: jnp.minimum(i + 1, nb - 1)
    return [
        pl.BlockSpec((BLOCK, D_MODEL), lambda i: (i, OFF_Q // D_MODEL)),
        pl.BlockSpec((BLOCK, kvw), lambda i: (prev(i), kcol)),
        pl.BlockSpec((BLOCK, kvw), lambda i: (i, kcol)),
        pl.BlockSpec((BLOCK, kvw), lambda i: (nxt(i), kcol)),
        pl.BlockSpec((BLOCK, kvw), lambda i: (prev(i), vcol)),
        pl.BlockSpec((BLOCK, kvw), lambda i: (i, vcol)),
        pl.BlockSpec((BLOCK, kvw), lambda i: (nxt(i), vcol)),
    ]


def _heads_rows(ref_or_val, hk):
    return jnp.concatenate(
        [ref_or_val[:, (GQA * hk + g) * HEAD_DIM:(GQA * hk + g + 1) * HEAD_DIM] for g in range(GQA)], axis=0)


def _kv_rows(p_ref, c_ref, n_ref, hk):
    sl = slice(hk * HEAD_DIM, (hk + 1) * HEAD_DIM)
    return jnp.concatenate([p_ref[:, sl], c_ref[:, sl], n_ref[:, sl]], axis=0)


def _bias_cols(bias_ref, hk):
    return jnp.concatenate([bias_ref[GQA * hk + g] for g in range(GQA)], axis=1)


def _softmax_keys_on_rows(q4s, kc, bias_blk, sink_row):
    s = _nt(kc, q4s) + bias_blk
    m = jnp.maximum(jnp.max(s, axis=0, keepdims=True), sink_row)
    p = jnp.exp(s - m)
    e_sink = jnp.exp(sink_row - m)
    inv = 1.0 / (jnp.sum(p, axis=0, keepdims=True) + e_sink)
    return p * inv, e_sink * inv


TAB = (N_HEADS, 3 * BLOCK, BLOCK)
TAB_FLAT = 3 * BLOCK * BLOCK


def _bias_spec(nb):
    return pl.BlockSpec((None,) + TAB, lambda i: (jnp.where(i == 0, 0, jnp.where(i == nb - 1, 2, 1)), 0, 0, 0))


def _attn_fwd(proj, bias_tabs, sink_rows):
    S = proj.shape[0]
    nb = S // BLOCK
    assert nb >= 2

    def body(q_ref, kp, kc_, kn, vp, vc_, vn, bias_ref, sink_ref, o_ref):
        outs = []
        for hk in range(N_KV_HEADS):
            q4s = _heads_rows(q_ref, hk) * SM_SCALE
            kc = _kv_rows(kp, kc_, kn, hk)
            vc = _kv_rows(vp, vc_, vn, hk)
            pn, _ = _softmax_keys_on_rows(q4s, kc, _bias_cols(bias_ref, hk), sink_ref[hk:hk + 1, :])
            o4 = _tn(pn.astype(BF16), vc)
            outs += [o4[g * BLOCK:(g + 1) * BLOCK, :] for g in range(GQA)]
        o_ref[...] = jnp.concatenate(outs, axis=1).astype(BF16)

    return pl.pallas_call(
        body, name="attn_fwd", grid=(nb,),
        in_specs=_attn_specs(S) + [_bias_spec(nb), pl.BlockSpec((N_KV_HEADS, GQA * BLOCK), lambda i: (0, 0))],
        out_specs=pl.BlockSpec((BLOCK, D_MODEL), lambda i: (i, 0)),
        out_shape=jax.ShapeDtypeStruct((S, D_MODEL), BF16),
        compiler_params=_params("parallel"),
    )(*([proj] * 7), bias_tabs, sink_rows)


GATE_HALF = D_MODEL // 2


def _gate_specs(tm):
    return [pl.BlockSpec((tm, GATE_HALF), lambda i, c=off // GATE_HALF + k: (i, c))
            for off in (OFF_GA, OFF_GP, OFF_GT) for k in (0, 1)]


def _gate(lo_ref, hi_ref):
    return _sigmoid(jnp.concatenate([lo_ref[...], hi_ref[...]], axis=1).astype(F32))


def _pool_mix(p, wp):
    return jnp.concatenate(
        [_nn(p[:, g * POOL_CG:(g + 1) * POOL_CG], wp[g]) for g in range(POOL_GROUPS)], axis=1)


def _mix_fwd(za, p, att, proj, x, wa, wp, ps, wt, wo):
    S = x.shape[0]
    tm = _tile(S, 256)

    def body(za_ref, p_ref, att_ref, ga0, ga1, gp0, gp1, gt0, gt1, x_ref, wa_ref, wp_ref, ps_ref, wt_ref, wo_ref,
             ya_ref, yp_ref, yt_ref, mg_ref, x2_ref):
        ya = _nn(za_ref[...], wa_ref[...])
        ypr = _pool_mix(p_ref[...], wp_ref)
        yt = _nn(att_ref[...], wt_ref[...])
        merged = _gate(ga0, ga1) * ya + _gate(gp0, gp1) * (ypr * ps_ref[...]) + _gate(gt0, gt1) * yt
        mb = merged.astype(BF16)
        ya_ref[...] = ya.astype(BF16)
        yp_ref[...] = ypr.astype(BF16)
        yt_ref[...] = yt.astype(BF16)
        mg_ref[...] = mb
        x2_ref[...] = x_ref[...] + _nn(mb, wo_ref[...])

    row = lambda c=0: pl.BlockSpec((tm, D_MODEL), lambda i: (i, c))
    whole = lambda a: pl.BlockSpec(a.shape, lambda i: (0,) * a.ndim)
    act = jax.ShapeDtypeStruct((S, D_MODEL), BF16)
    return pl.pallas_call(
        body, name="mix_fwd", grid=(S // tm,),
        in_specs=[row(), row(), row()] + _gate_specs(tm) + [row(), whole(wa), whole(wp), whole(ps), whole(wt), whole(wo)],
        out_specs=[row(), row(), row(), row(), row()],
        out_shape=[act, act, act, act, jax.ShapeDtypeStruct((S, D_MODEL), F32)],
        compiler_params=_params("parallel"),
    )(za, p, att, *([proj] * 6), x, wa, wp, ps, wt, wo)


def _ffn_fwd(gu, x2, wd):
    S = x2.shape[0]
    tm = _tile(S, 256)

    def body(g_ref, u_ref, x_ref, w_ref, a_ref, o_ref):
        g = g_ref[...].astype(F32)
        a = (g * _sigmoid(g) * u_ref[...].astype(F32)).astype(BF16)
        a_ref[...] = a
        o_ref[...] = x_ref[...] + _nn(a, w_ref[...])

    return pl.pallas_call(
        body, name="ffn_fwd", grid=(S // tm,),
        in_specs=[pl.BlockSpec((tm, D_FF), lambda i: (i, 0)), pl.BlockSpec((tm, D_FF), lambda i: (i, 1)),
                  pl.BlockSpec((tm, D_MODEL), lambda i: (i, 0)), pl.BlockSpec((D_FF, D_MODEL), lambda i: (0, 0))],
        out_specs=[pl.BlockSpec((tm, D_FF), lambda i: (i, 0)), pl.BlockSpec((tm, D_MODEL), lambda i: (i, 0))],
        out_shape=[jax.ShapeDtypeStruct((S, D_FF), BF16), jax.ShapeDtypeStruct((S, D_MODEL), F32)],
        compiler_params=_params("parallel"),
    )(gu, gu, x2, wd)


def _loss_bwd(x, g, tgt):
    S, Dm = x.shape
    tm = _tile(S, 512)

    def body(x_ref, g_ref, t_ref, l_ref, dx_ref, dg_ref):
        @pl.when(pl.program_id(0) == 0)
        def _():
            l_ref[...] = jnp.zeros_like(l_ref)
            dg_ref[...] = jnp.zeros_like(dg_ref)
        xv, gv = x_ref[...], g_ref[...]
        r = lax.rsqrt(jnp.mean(xv * xv, axis=-1, keepdims=True) + EPS)
        n = xv * r
        err = n * gv - t_ref[...]
        l_ref[...] += 0.5 * jnp.sum(jnp.mean(err * err, axis=-1, keepdims=True), axis=0, keepdims=True)
        dy = err * (1.0 / Dm)
        dn = dy * gv
        dx_ref[...] = r * (dn - n * jnp.mean(dn * n, axis=-1, keepdims=True))
        dg_ref[...] += jnp.sum(dy * n, axis=0, keepdims=True)

    return pl.pallas_call(
        body, name="loss_bwd", grid=(S // tm,),
        in_specs=[pl.BlockSpec((tm, Dm), lambda i: (i, 0)), pl.BlockSpec((1, Dm), lambda i: (0, 0)),
                  pl.BlockSpec((tm, Dm), lambda i: (i, 0))],
        out_specs=[pl.BlockSpec((8, 128), lambda i: (0, 0)), pl.BlockSpec((tm, Dm), lambda i: (i, 0)),
                   pl.BlockSpec((1, Dm), lambda i: (0, 0))],
        out_shape=[jax.ShapeDtypeStruct((8, 128), F32), jax.ShapeDtypeStruct((S, Dm), F32),
                   jax.ShapeDtypeStruct((1, Dm), F32)],
        compiler_params=_params("arbitrary"),
    )(x, g, tgt)


def _ffn_bwd(dx3, gu, wd, token):
    S = dx3.shape[0]
    tm = _tile(S, 256)

    def body(d_ref, g_ref, u_ref, w_ref, token_ref, o_ref):
        dact = _nt(d_ref[...].astype(BF16), w_ref[...])
        g, u = g_ref[...].astype(F32), u_ref[...].astype(F32)
        sg = _sigmoid(g)
        o_ref[:, 0:D_FF] = (dact * u * (sg * (1.0 + g * (1.0 - sg)))).astype(BF16)
        o_ref[:, D_FF:2 * D_FF] = (dact * (g * sg)).astype(BF16)

    return pl.pallas_call(
        body, name="ffn_bwd", grid=(S // tm,),
        in_specs=[pl.BlockSpec((tm, D_MODEL), lambda i: (i, 0)),
                  pl.BlockSpec((tm, D_FF), lambda i: (i, 0)), pl.BlockSpec((tm, D_FF), lambda i: (i, 1)),
                  pl.BlockSpec((D_FF, D_MODEL), lambda i: (0, 0)), ANY],
        out_specs=pl.BlockSpec((tm, 2 * D_FF), lambda i: (i, 0)),
        out_shape=jax.ShapeDtypeStruct((S, 2 * D_FF), BF16),
        compiler_params=_params("parallel"),
    )(dx3, gu, gu, wd, token)


def _wgrad(a, b, name, tk=512, tn=512, out_dtype=BF16, token=None):
    S, K = a.shape
    N = b.shape[1]
    tk, tn, ts = _tile(K, tk), _tile(N, tn), _tile(S, 1024)
    n_s = S // ts
    extra = [] if token is None else [token]

    def body(a_ref, b_ref, *rest):
        o_ref, acc = rest[-2:]
        s = pl.program_id(2)

        @pl.when(s == 0)
        def _():
            acc[...] = jnp.zeros_like(acc)
        acc[...] += _tn(a_ref[...].astype(BF16), b_ref[...].astype(BF16))

        @pl.when(s == n_s - 1)
        def _():
            o_ref[...] = acc[...].astype(out_dtype)

    return pl.pallas_call(
        body, name=name, grid=(K // tk, N // tn, n_s),
        in_specs=[pl.BlockSpec((ts, tk), lambda k, n, s: (s, k)), pl.BlockSpec((ts, tn), lambda k, n, s: (s, n))]
        + [ANY] * len(extra),
        out_specs=pl.BlockSpec((tk, tn), lambda k, n, s: (k, n)),
        out_shape=jax.ShapeDtypeStruct((K, N), out_dtype),
        scratch_shapes=[pltpu.VMEM((tk, tn), F32)],
        compiler_params=_params("parallel", "parallel", "arbitrary"),
    )(a, b, *extra)


def _wgrad_pool(p, dyps):
    S = p.shape[0]
    ts = _tile(S, 4096)
    n_s = S // ts

    def body(a_ref, b_ref, o_ref, acc):
        s = pl.program_id(1)

        @pl.when(s == 0)
        def _():
            acc[...] = jnp.zeros_like(acc)
        acc[...] += _tn(a_ref[...], b_ref[...])

        @pl.when(s == n_s - 1)
        def _():
            o_ref[...] = acc[...].astype(BF16)

    return pl.pallas_call(
        body, name="wgrad_pool", grid=(POOL_GROUPS, n_s),
        in_specs=[pl.BlockSpec((ts, POOL_CG), lambda g, s: (s, g)), pl.BlockSpec((ts, POOL_CG), lambda g, s: (s, g))],
        out_specs=pl.BlockSpec((None, POOL_CG, POOL_CG), lambda g, s: (g, 0, 0)),
        out_shape=jax.ShapeDtypeStruct((POOL_GROUPS, POOL_CG, POOL_CG), BF16),
        scratch_shapes=[pltpu.VMEM((POOL_CG, POOL_CG), F32)],
        compiler_params=_params("parallel", "arbitrary"),
    )(p, dyps)


def _dgrad_norm_bwd(dy, w, x, g, dres, name, tk, token=None):
    S, K = dy.shape
    Dm = x.shape[1]
    tm, tk = _tile(S, 1024), _tile(K, tk)
    n_k = K // tk
    extra = [] if token is None else [token]

    def body(dy_ref, w_ref, x_ref, g_ref, r_ref, *rest):
        dx_ref, dg_ref, acc = rest[-3:]
        i, k = pl.program_id(0), pl.program_id(1)

        @pl.when((i == 0) & (k == 0))
        def _():
            dg_ref[...] = jnp.zeros_like(dg_ref)

        @pl.when(k == 0)
        def _():
            acc[...] = jnp.zeros_like(acc)
        acc[...] += _nt(dy_ref[...], w_ref[...])

        @pl.when(k == n_k - 1)
        def _():
            dh, xv = acc[...], x_ref[...]
            r = lax.rsqrt(jnp.mean(xv * xv, axis=-1, keepdims=True) + EPS)
            n = xv * r
            dn = dh * g_ref[...]
            dx_ref[...] = r_ref[...] + r * (dn - n * jnp.mean(dn * n, axis=-1, keepdims=True))
            dg_ref[...] += jnp.sum(dh * n, axis=0, keepdims=True)

    rowblk = pl.BlockSpec((tm, Dm), lambda i, k: (i, 0))
    vec = pl.BlockSpec((1, Dm), lambda i, k: (0, 0))
    return pl.pallas_call(
        body, name=name, grid=(S // tm, n_k),
        in_specs=[pl.BlockSpec((tm, tk), lambda i, k: (i, k)), pl.BlockSpec((Dm, tk), lambda i, k: (0, k)),
                  rowblk, vec, rowblk] + [ANY] * len(extra),
        out_specs=[rowblk, vec],
        out_shape=[jax.ShapeDtypeStruct((S, Dm), F32), jax.ShapeDtypeStruct((1, Dm), F32)],
        scratch_shapes=[pltpu.VMEM((tm, Dm), F32)],
        compiler_params=_params("arbitrary", "arbitrary"),
    )(dy, w, x, g, dres, *extra)


def _mix_bwd(dx2, ya, ypr, yt, proj, ps, wa, wp, wt, wo, token):
    S = dx2.shape[0]
    tm = _tile(S, 256)

    n_tiles = S // tm

    def body(dx_ref, ya_ref, yp_ref, yt_ref, ga0, ga1, gp0, gp1, gt0, gt1, ps_ref, wa_ref, wp_ref, wt_ref, wo_ref,
             token_ref, dya_ref, dyt_ref, dyps_ref, dza_ref, datt_ref, dp_ref, dproj_hbm, dps_ref, dgates, sem):
        i = pl.program_id(0)
        to_dproj = pltpu.make_async_copy(
            dgates, dproj_hbm.at[pl.ds(pl.multiple_of(i * tm, tm), tm), pl.ds(OFF_GA, 3 * D_MODEL)], sem)

        @pl.when(i == 0)
        def _():
            dps_ref[...] = jnp.zeros_like(dps_ref)
        dm = _nt(dx_ref[...].astype(BF16), wo_ref[...])
        sa, sp, st = _gate(ga0, ga1), _gate(gp0, gp1), _gate(gt0, gt1)
        psv = ps_ref[...]
        ypr_v = yp_ref[...].astype(F32)
        dya = (sa * dm).astype(BF16)
        dyt = (st * dm).astype(BF16)
        dyp = sp * dm
        dyps = (dyp * psv).astype(BF16)
        dya_ref[...] = dya
        dyt_ref[...] = dyt
        dyps_ref[...] = dyps
        dg = jnp.concatenate(
            [dm * ya_ref[...].astype(F32) * (sa * (1.0 - sa)), dm * (ypr_v * psv) * (sp * (1.0 - sp)),
             dm * yt_ref[...].astype(F32) * (st * (1.0 - st))], axis=1).astype(BF16)

        @pl.when(i > 0)
        def _():
            to_dproj.wait()
        dgates[...] = dg
        to_dproj.start()
        dps_ref[...] += jnp.sum(dyp * ypr_v, axis=0, keepdims=True)
        dza_ref[...] = _nt(dya, wa_ref[...]).astype(BF16)
        datt_ref[...] = _nt(dyt, wt_ref[...]).astype(BF16)
        dp_ref[...] = jnp.concatenate(
            [_nt(dyps[:, g * POOL_CG:(g + 1) * POOL_CG], wp_ref[g]) for g in range(POOL_GROUPS)], axis=1).astype(BF16)

        @pl.when(i == n_tiles - 1)
        def _():
            to_dproj.wait()

    row = lambda c=0: pl.BlockSpec((tm, D_MODEL), lambda i: (i, c))
    whole = lambda a: pl.BlockSpec(a.shape, lambda i: (0,) * a.ndim)
    act = jax.ShapeDtypeStruct((S, D_MODEL), BF16)
    return pl.pallas_call(
        body, name="mix_bwd", grid=(n_tiles,),
        in_specs=[row(), row(), row(), row()] + _gate_specs(tm)
        + [whole(ps), whole(wa), whole(wp), whole(wt), whole(wo), ANY],
        out_specs=[row()] * 6 + [ANY, pl.BlockSpec((1, D_MODEL), lambda i: (0, 0))],
        out_shape=[act] * 6 + [jax.ShapeDtypeStruct((S, IN_TOTAL), BF16), jax.ShapeDtypeStruct((1, D_MODEL), F32)],
        scratch_shapes=[pltpu.VMEM((tm, 3 * D_MODEL), BF16), pltpu.SemaphoreType.DMA],
        compiler_params=_params("arbitrary"),
    )(dx2, ya, ypr, yt, *([proj] * 6), ps, wa, wp, wt, wo, token)


def _conv_bwd(dza, proj, cw8, dproj):
    S = proj.shape[0]
    nblk = D_MODEL // CB

    def body(d_ref, b_ref, c_ref, x_ref, w_ref, dproj_in, dproj_hbm, dw_ref, pad_u, pad_d, parts, sems):
        cb = pl.program_id(0)
        to_dproj = [pltpu.make_async_copy(
            parts.at[k], dproj_hbm.at[:, pl.ds(pl.multiple_of(off + cb * CB, CB), CB)], sems.at[k])
            for k, off in enumerate((OFF_B, OFF_C, OFF_X))]
        c, xa = c_ref[...].astype(F32), x_ref[...].astype(F32)
        u = c * xa
        _fill_padded(pad_u, u, S)
        u_prev, u_next = _shifted(pad_u, -1, S), _shifted(pad_u, 1, S)
        cv = w_ref[0:1, :] * u_prev + w_ref[1:2, :] * u + w_ref[2:3, :] * u_next
        dza_v = d_ref[...].astype(F32)
        dcv = dza_v * b_ref[...].astype(F32)
        _fill_padded(pad_d, dcv, S)
        du = w_ref[0:1, :] * _shifted(pad_d, 1, S) + w_ref[1:2, :] * dcv + w_ref[2:3, :] * _shifted(pad_d, -1, S)

        @pl.when(cb > 0)
        def _():
            for cp in to_dproj:
                cp.wait()
        parts[0] = (dza_v * cv).astype(BF16)
        parts[1] = (du * xa).astype(BF16)
        parts[2] = (du * c).astype(BF16)
        for cp in to_dproj:
            cp.start()
        dw_ref[...] = jnp.concatenate(
            [jnp.sum(dcv * u_prev, axis=0, keepdims=True), jnp.sum(dcv * u, axis=0, keepdims=True),
             jnp.sum(dcv * u_next, axis=0, keepdims=True), jnp.zeros((5, CB), F32)], axis=0)

        @pl.when(cb == nblk - 1)
        def _():
            for cp in to_dproj:
                cp.wait()

    col = lambda base: pl.BlockSpec((S, CB), lambda cb: (0, base // CB + cb))
    taps = pl.BlockSpec((8, CB), lambda cb: (0, cb))
    return pl.pallas_call(
        body, name="conv_bwd", grid=(nblk,),
        in_specs=[col(0), col(OFF_B), col(OFF_C), col(OFF_X), taps, ANY],
        out_specs=[ANY, taps],
        out_shape=[jax.ShapeDtypeStruct(dproj.shape, dproj.dtype), jax.ShapeDtypeStruct((8, D_MODEL), F32)],
        scratch_shapes=[pltpu.VMEM((S + 2 * HALO, CB), F32), pltpu.VMEM((S + 2 * HALO, CB), F32),
                        pltpu.VMEM((3, S, CB), BF16), pltpu.SemaphoreType.DMA((3,))],
        input_output_aliases={5: 0},
        compiler_params=_params("arbitrary"),
    )(dza, proj, proj, proj, cw8, dproj)


def _pool_bwd(dp, dproj):
    S = dp.shape[0]
    nblk = D_MODEL // CBW
    per_group = POOL_CG // CBW

    def body(d_ref, dproj_in, o_ref, pad):
        d = d_ref[...].astype(F32)
        grp = pl.program_id(0) // per_group
        for gi, w in enumerate(POOL_WINDOWS):
            @pl.when(grp == gi)
            def _(w=w):
                lo, hi = w // 2, w - 1 - w // 2
                _fill_padded(pad, d / _pool_count(S, lo, hi), S)
                acc = _shifted(pad, -hi, S)
                for off in range(-hi + 1, lo + 1):
                    acc = acc + _shifted(pad, off, S)
                o_ref[...] = (acc - d).astype(BF16)

    return pl.pallas_call(
        body, name="pool_bwd", grid=(nblk,),
        in_specs=[pl.BlockSpec((S, CBW), lambda j: (0, j)), ANY],
        out_specs=pl.BlockSpec((S, CBW), lambda j: (0, OFF_U // CBW + j)),
        out_shape=jax.ShapeDtypeStruct(dproj.shape, dproj.dtype),
        scratch_shapes=[pltpu.VMEM((S + 2 * HALO, CBW), F32)],
        input_output_aliases={1: 0},
        compiler_params=_params("parallel"),
    )(dp, dproj)


def _attn_bwd(proj, datt, bias_tabs, sink_rows, dbias_in, dproj):
    S = proj.shape[0]
    nb = S // BLOCK
    kvw = N_KV_HEADS * HEAD_DIM

    def body(q_ref, kp, kc_, kn, vp, vc_, vn, do_ref, bias_ref, sink_ref, dbin_ref, dproj_in,
             dq_ref, dk_ref, dv_ref, db_ref, ds_ref):
        i = pl.program_id(0)

        @pl.when(i == 0)
        def _():
            dk_ref[...] = jnp.zeros_like(dk_ref)
            dv_ref[...] = jnp.zeros_like(dv_ref)
            db_ref[...] = dbin_ref[...]
            ds_ref[...] = jnp.zeros_like(ds_ref)
        dqs, dks, dvs = [], [], []
        for hk in range(N_KV_HEADS):
            q4s = _heads_rows(q_ref, hk) * SM_SCALE
            do4 = _heads_rows(do_ref, hk)
            kc = _kv_rows(kp, kc_, kn, hk)
            vc = _kv_rows(vp, vc_, vn, hk)
            pn, p_sink = _softmax_keys_on_rows(q4s, kc, _bias_cols(bias_ref, hk), sink_ref[hk:hk + 1, :])
            dpm = _nt(vc, do4)
            delta = jnp.sum(pn * dpm, axis=0, keepdims=True)
            dsc = pn * (dpm - delta)
            for g in range(GQA):
                db_ref[GQA * hk + g] += dsc[:, g * BLOCK:(g + 1) * BLOCK]
            ds_ref[hk:hk + 1, :] += -p_sink * delta
            dsb = dsc.astype(BF16)
            dq4 = _tn(dsb, kc) * SM_SCALE
            dqs += [dq4[g * BLOCK:(g + 1) * BLOCK, :] for g in range(GQA)]
            dks.append(_nn(dsb, q4s))
            dvs.append(_nn(pn.astype(BF16), do4))
        dq_ref[...] = jnp.concatenate(dqs, axis=1).astype(BF16)
        r0 = pl.multiple_of(i * BLOCK, BLOCK)
        dk_ref[pl.ds(r0, 3 * BLOCK), :] += jnp.concatenate(dks, axis=1)
        dv_ref[pl.ds(r0, 3 * BLOCK), :] += jnp.concatenate(dvs, axis=1)

    const = lambda shape: pl.BlockSpec(shape, lambda i: (0,) * len(shape))
    sink_shape = (N_KV_HEADS, GQA * BLOCK)
    return pl.pallas_call(
        body, name="attn_bwd", grid=(nb,),
        in_specs=_attn_specs(S) + [pl.BlockSpec((BLOCK, D_MODEL), lambda i: (i, 0)),
                                   _bias_spec(nb), const(sink_shape), const(TAB), ANY],
        out_specs=[pl.BlockSpec((BLOCK, D_MODEL), lambda i: (i, OFF_Q // D_MODEL)),
                   const((S + 2 * BLOCK, kvw)), const((S + 2 * BLOCK, kvw)), const(TAB), const(sink_shape)],
        out_shape=[jax.ShapeDtypeStruct(dproj.shape, dproj.dtype),
                   jax.ShapeDtypeStruct((S + 2 * BLOCK, kvw), F32), jax.ShapeDtypeStruct((S + 2 * BLOCK, kvw), F32),
                   jax.ShapeDtypeStruct(TAB, F32), jax.ShapeDtypeStruct(sink_shape, F32)],
        input_output_aliases={11: 0},
        compiler_params=_params("arbitrary"),
    )(*([proj] * 7), datt, bias_tabs, sink_rows, dbias_in, dproj)


def _kv_finish(dkp, dvp, dproj):
    S = dproj.shape[0]
    kvw = N_KV_HEADS * HEAD_DIM

    def body(dk_ref, dv_ref, dproj_in, o_ref):
        o_ref[:, 0:kvw] = dk_ref[pl.ds(BLOCK, S), :].astype(BF16)
        o_ref[:, kvw:2 * kvw] = dv_ref[pl.ds(BLOCK, S), :].astype(BF16)

    whole = pl.BlockSpec((S + 2 * BLOCK, kvw), lambda i: (0, 0))
    return pl.pallas_call(
        body, name="kv_finish", grid=(1,), in_specs=[whole, whole, ANY],
        out_specs=pl.BlockSpec((S, 2 * kvw), lambda i: (0, OFF_K // (2 * kvw))),
        out_shape=jax.ShapeDtypeStruct(dproj.shape, dproj.dtype),
        input_output_aliases={2: 0},
        compiler_params=_params("arbitrary"),
    )(dkp, dvp, dproj)


def _bucket_constants():
    half = N_BUCKETS // 2
    max_exact = half // 2
    qi = np.arange(BLOCK)[None, :]
    kj = np.arange(3 * BLOCK)[:, None]
    rel = kj - BLOCK - qi
    n = np.abs(rel)
    nf = np.maximum(n, 1).astype(np.float32)
    large = max_exact + (np.log(nf / np.float32(max_exact)) / np.float32(math.log(MAX_DISTANCE / max_exact))
                         * np.float32(half - max_exact)).astype(np.int32)
    large = np.minimum(large, half - 1)
    bucket = np.where(rel > 0, half, 0) + np.where(n < max_exact, n, large)
    onehot = (bucket.reshape(1, -1) == np.arange(N_BUCKETS)[:, None]).astype(np.float32)
    window = n <= WINDOW
    first = window & (kj >= BLOCK)
    last = window & (kj < 2 * BLOCK)
    masks = np.stack([np.where(v, 0.0, NEG_INF).astype(np.float32).reshape(-1) for v in (first, window, last)])
    return onehot, masks


def _bias_expand(rel_bias_t, onehot, masks):
    def body(r_ref, oh_ref, m_ref, o_ref):
        tab = jnp.dot(r_ref[...], oh_ref[...], preferred_element_type=F32, precision=lax.Precision.HIGHEST)
        for v in range(3):
            o_ref[v] = tab + m_ref[v:v + 1, :]

    return pl.pallas_call(
        body, name="bias_expand", out_shape=jax.ShapeDtypeStruct((3, N_HEADS, onehot.shape[1]), F32),
        compiler_params=_params(),
    )(rel_bias_t, onehot, masks)


def _bias_reduce(dtab, dsink_rows, onehot):
    def body(d_ref, s_ref, oh_ref, o_ref, so_ref):
        o_ref[...] = lax.dot_general(oh_ref[...], d_ref[...], (((1,), (1,)), ((), ())),
                                     preferred_element_type=F32, precision=lax.Precision.HIGHEST)
        so_ref[...] = jnp.sum(s_ref[...], axis=-1, keepdims=True)

    return pl.pallas_call(
        body, name="bias_reduce",
        out_shape=[jax.ShapeDtypeStruct((N_BUCKETS, N_HEADS), F32),
                   jax.ShapeDtypeStruct((dsink_rows.shape[0], 1), F32)],
        compiler_params=_params(),
    )(dtab, dsink_rows, onehot)


GROUPS = dict(mix=("w_in", "conv_w", "w_a_out", "w_pool", "w_attn_out", "w_o"), ffn=("w_gu", "w_down"))
WEIGHT_NAMES = GROUPS["mix"] + GROUPS["ffn"]


def _layer_fwd(l, x, weights_of, ps, g_mix, g_ffn, bias_tabs, sink_rows):
    W, token = weights_of(l, "mix", x)
    h, proj = _norm_matmul(x, g_mix, W["w_in"], "norm_proj", token)
    za = _conv_fwd(proj, W["conv_w"])
    p = _pool_fwd(proj)
    att = _attn_fwd(proj, bias_tabs, sink_rows)
    ya, ypr, yt, merged, x2 = _mix_fwd(za, p, att, proj, x, W["w_a_out"], W["w_pool"], ps, W["w_attn_out"], W["w_o"])
    Wf, token = weights_of(l, "ffn", x2)
    h2, gu = _norm_matmul(x2, g_ffn, Wf["w_gu"], "norm_gu", token)
    act, x3 = _ffn_fwd(gu, x2, Wf["w_down"])
    saved = dict(x=x, h=h, proj=proj, za=za, p=p, att=att, ya=ya, ypr=ypr, yt=yt, merged=merged, x2=x2, h2=h2,
                 gu=gu, act=act, W={**W, **Wf}, sink_rows=sink_rows)
    return x3, saved, token


def _layer_bwd(l, dx3, sv, grads_to, ps, g_mix, g_ffn, bias_tabs, dbias, token):
    W, sink_rows = sv["W"], sv["sink_rows"]
    dgu = _ffn_bwd(dx3, sv["gu"], W["w_down"], token)
    g_w_down = _wgrad(sv["act"], dx3, "wgrad_down", tk=1408, tn=1024, token=token)
    g_w_gu = _wgrad(sv["h2"], dgu, "wgrad_gu", tk=1024, tn=1408)
    dx2, dg_ffn = _dgrad_norm_bwd(dgu, W["w_gu"], sv["x2"], g_ffn, dx3, "dgrad_gu", tk=1408)
    token = grads_to(l, "ffn", dict(w_gu=g_w_gu, w_down=g_w_down), dx2)
    dya, dyt, dyps, dza, datt, dp, dproj, dps = _mix_bwd(
        dx2, sv["ya"], sv["ypr"], sv["yt"], sv["proj"], ps, W["w_a_out"], W["w_pool"], W["w_attn_out"], W["w_o"], token)
    g_w_o = _wgrad(sv["merged"], dx2, "wgrad_sq_f32", tk=1024, tn=1024)
    g_w_a_out = _wgrad(sv["za"], dya, "wgrad_sq", tk=1024, tn=1024)
    g_w_attn_out = _wgrad(sv["att"], dyt, "wgrad_sq", tk=1024, tn=1024)
    g_w_pool = _wgrad_pool(sv["p"], dyps)
    dproj, g_conv = _conv_bwd(dza, sv["proj"], W["conv_w"], dproj)
    dproj = _pool_bwd(dp, dproj)
    dproj, dkp, dvp, dbias, dsink = _attn_bwd(sv["proj"], datt, bias_tabs, sink_rows, dbias, dproj)
    dproj = _kv_finish(dkp, dvp, dproj)
    g_w_in = _wgrad(sv["h"], dproj, "wgrad_in", tk=1024, tn=2176)
    token = grads_to(l, "mix", dict(w_in=g_w_in, conv_w=g_conv, w_a_out=g_w_a_out, w_pool=g_w_pool,
                                    w_attn_out=g_w_attn_out, w_o=g_w_o), dproj)
    dx, dg_mix = _dgrad_norm_bwd(dproj, W["w_in"], sv["x"], g_mix, dx2, "dgrad_in", tk=2176, token=token)
    grads_to(l, "done", None, dx)
    return dx, dict(pool_scale=dps, g_mix=dg_mix, g_ffn=dg_ffn, attn_sink=dsink), dbias, token


def _local_step(x, tgt, weights_of, grads_to, pool_scale, attn_sink, g_mix, g_ffn, rel_bias, g_final):
    onehot_np, masks_np = _bucket_constants()
    onehot, masks = jnp.asarray(onehot_np), jnp.asarray(masks_np)
    bias_tabs = _bias_expand(rel_bias.T, onehot, masks).reshape((3,) + TAB)
    saved = []
    for l in range(DEPTH):
        sink_rows = jnp.repeat(attn_sink[l], BLOCK).reshape(N_KV_HEADS, GQA * BLOCK)
        x, sv, token = _layer_fwd(l, x, weights_of, pool_scale[l:l + 1], g_mix[l:l + 1], g_ffn[l:l + 1], bias_tabs,
                                  sink_rows)
        saved.append(sv)
    loss, dx, dg_final = _loss_bwd(x, g_final.reshape(1, D_MODEL), tgt)
    dbias = jnp.zeros(TAB, F32)
    small = [None] * DEPTH
    for l in reversed(range(DEPTH)):
        dx, small[l], dbias, token = _layer_bwd(
            l, dx, saved[l], grads_to, pool_scale[l:l + 1], g_mix[l:l + 1], g_ffn[l:l + 1], bias_tabs, dbias, token)
    dsink_rows = jnp.concatenate([small[l]["attn_sink"].reshape(N_HEADS, BLOCK) for l in range(DEPTH)], axis=0)
    d_rel_bias, d_sink = _bias_reduce(dbias.reshape(N_HEADS, TAB_FLAT), dsink_rows, onehot)
    cat = lambda k: jnp.concatenate([small[l][k] for l in range(DEPTH)], axis=0)
    smalls = dict(pool_scale=cat("pool_scale"), g_mix=cat("g_mix"), g_ffn=cat("g_ffn"),
                  attn_sink=d_sink.reshape(DEPTH, N_HEADS), rel_bias=d_rel_bias, g_final=dg_final)
    return loss[0, 0], dx, smalls


SHARD_AXIS = dict(w_in=(1, IN_TOTAL // N_CHIPS), conv_w=(1, D_MODEL // N_CHIPS), w_a_out=(0, D_MODEL // N_CHIPS),
                  w_pool=(1, POOL_CG // N_CHIPS), w_attn_out=(0, D_MODEL // N_CHIPS), w_o=(0, D_MODEL // N_CHIPS),
                  w_gu=(1, 2 * D_FF // N_CHIPS), w_down=(0, D_FF // N_CHIPS))
HBM = pl.BlockSpec(memory_space=pltpu.HBM)
SEM = pl.BlockSpec(memory_space=pltpu.SEMAPHORE)
DATAFLOW = pltpu.SideEffectType.DATAFLOW_SIDE_EFFECTING
TOKEN = jax.ShapeDtypeStruct((8, 128), F32)


def _shard_of(ref, name, chip):
    axis, n = SHARD_AXIS[name]
    idx = [slice(None)] * len(ref.shape)
    idx[axis] = pl.ds(chip * n, n)
    return ref.at[tuple(idx)]


def _with_shard_axis(name, shape, size):
    axis, _ = SHARD_AXIS[name]
    s = list(shape)
    s[axis] = size
    return tuple(s)


HALF_AXIS = dict(w_in=0, conv_w=1, w_a_out=0, w_pool=1, w_attn_out=0, w_o=0, w_gu=0, w_down=0)


def _half_of_shard(ref, name, core):
    axis = HALF_AXIS[name]
    n = ref.shape[axis] // 2
    idx = [slice(None)] * len(ref.shape)
    idx[axis] = pl.ds(core * n, n)
    return ref.at[tuple(idx)]


def _half_in_full(ref, name, chip, core):
    saxis, n = SHARD_AXIS[name]
    haxis = HALF_AXIS[name]
    idx = [slice(None)] * len(ref.shape)
    if haxis == saxis:
        idx[saxis] = pl.ds(chip * n + core * (n // 2), n // 2)
    else:
        h = ref.shape[haxis] // 2
        idx[saxis] = pl.ds(chip * n, n)
        idx[haxis] = pl.ds(core * h, h)
    return ref.at[tuple(idx)]


def _on_each_device(fn):
    me = 2 * lax.axis_index("x") + lax.axis_index("y")
    c = lax.axis_index("c")
    for chip in range(N_CHIPS):
        for core in range(2):
            pl.when((me == chip) & (c == core))(functools.partial(fn, chip, core))


def _chip_peers(x, y):
    return [(1 - x, y), (x, 1 - y), (1 - x, 1 - y)]


RELATION_XOR = (2, 1, 3)


def _group_copies(kind, group, srcs, lands, send_sems, recv_sems, local_sems, chip, core):
    x, y, c = lax.axis_index("x"), lax.axis_index("y"), lax.axis_index("c")
    copies = []
    for t, name in enumerate(GROUPS[group]):
        for j, (px, py) in enumerate(_chip_peers(x, y)):
            if kind == "gather":
                src, dst = _half_of_shard(srcs[t], name, core), _half_in_full(lands[t], name, chip, core)
            else:
                src, dst = _shard_of(srcs[t], name, chip ^ RELATION_XOR[j]), lands[t].at[j]
            copies.append(pltpu.make_async_remote_copy(
                src_ref=src, dst_ref=dst, send_sem=send_sems.at[3 * t + j], recv_sem=recv_sems.at[3 * t + j],
                device_id=(px, py, c), device_id_type=MESH))
        if kind == "gather":
            src, dst = srcs[t], _shard_of(lands[t], name, chip)
        else:
            src, dst = _shard_of(srcs[t], name, chip), lands[t].at[N_CHIPS - 1]
        copies.append(pltpu.make_async_copy(src, dst, local_sems.at[t]))
    return copies


def _exchange_start(kind, group, srcs, land_shapes, after):
    nw = len(GROUPS[group])

    def body(*refs):
        srcs_r, lands_r = refs[:nw], refs[nw:2 * nw]
        send_sems, recv_sems, local_sems = refs[2 * nw + 1:2 * nw + 4]
        token = refs[-1]

        def issue(chip, core):
            for cp in _group_copies(kind, group, srcs_r, lands_r, send_sems, recv_sems, local_sems, chip, core):
                cp.start()
        _on_each_device(issue)
        token[...] = jnp.zeros_like(token)

    lands = [pltpu.with_memory_space_constraint(lax.empty(s.shape, s.dtype), pltpu.HBM) for s in land_shapes]
    srcs = [pltpu.with_memory_space_constraint(a, pltpu.HBM) for a in srcs]
    thru = [pltpu.HBM(a.shape, a.dtype) for a in srcs + lands]
    outs = pl.pallas_call(
        body, name=f"{kind}_{group}_start",
        in_specs=[HBM] * (2 * nw) + [ANY],
        out_specs=[SEM, SEM, SEM] + [HBM] * (2 * nw) + [pl.BlockSpec(memory_space=pltpu.VMEM)],
        out_shape=[pltpu.SemaphoreType.DMA((3 * nw,)), pltpu.SemaphoreType.DMA((3 * nw,)),
                   pltpu.SemaphoreType.DMA((nw,))] + thru + [TOKEN],
        input_output_aliases={t: 3 + t for t in range(2 * nw)},
        compiler_params=pltpu.CompilerParams(has_side_effects=DATAFLOW),
    )(*srcs, *lands, after)
    return dict(sems=outs[0:3], srcs=outs[3:3 + nw], lands=outs[3 + nw:3 + 2 * nw], token=outs[-1])


def _exchange_wait(kind, group, started, after):
    nw = len(GROUPS[group])

    def body(*refs):
        srcs_r, lands_r = refs[:nw], refs[nw:2 * nw]
        send_sems, recv_sems, local_sems = refs[2 * nw:2 * nw + 3]
        for cp in _group_copies(kind, group, srcs_r, lands_r, send_sems, recv_sems, local_sems, 0, 0):
            cp.wait()

    srcs, lands = list(started["srcs"]), list(started["lands"])
    outs = pl.pallas_call(
        body, name=f"{kind}_{group}_wait",
        in_specs=[HBM] * (2 * nw) + [SEM, SEM, SEM, ANY],
        out_specs=[HBM] * (2 * nw),
        out_shape=[pltpu.HBM(a.shape, a.dtype) for a in srcs + lands],
        input_output_aliases={t: t for t in range(2 * nw)},
        compiler_params=pltpu.CompilerParams(has_side_effects=DATAFLOW),
    )(*srcs, *lands, *started["sems"], after)
    return dict(zip(GROUPS[group], outs[nw:]))


def _gather_start(group, shards, after):
    names = GROUPS[group]
    shapes = [jax.ShapeDtypeStruct(_with_shard_axis(n, shards[n].shape, SHARD_AXIS[n][1] * N_CHIPS), shards[n].dtype)
              for n in names]
    return _exchange_start("gather", group, [shards[n] for n in names], shapes, after)


def _scatter_start(group, grads, after):
    names = GROUPS[group]
    shapes = [jax.ShapeDtypeStruct((N_CHIPS,) + _with_shard_axis(n, grads[n].shape, SHARD_AXIS[n][1]), grads[n].dtype)
              for n in names]
    return _exchange_start("scatter", group, [grads[n] for n in names], shapes, after)


def _sibling_exchange(parts):
    n = len(parts)

    def body(*refs):
        ins, outs = refs[:n], refs[n:2 * n]
        send_sems, recv_sems = refs[2 * n:]
        sibling = (lax.axis_index("x"), lax.axis_index("y"), 1 - lax.axis_index("c"))
        copies = [pltpu.make_async_remote_copy(src_ref=ins[t], dst_ref=outs[t], send_sem=send_sems.at[t],
                                               recv_sem=recv_sems.at[t], device_id=sibling, device_id_type=MESH)
                  for t in range(n)]
        for cp in copies:
            cp.start()
        for cp in copies:
            cp.wait()

    outs = pl.pallas_call(
        body, name="sibling_exchange", in_specs=[ANY] * n, out_specs=[ANY] * n,
        out_shape=[jax.ShapeDtypeStruct(p.shape, p.dtype) for p in parts],
        scratch_shapes=[pltpu.SemaphoreType.DMA((n,)), pltpu.SemaphoreType.DMA((n,))],
        compiler_params=pltpu.CompilerParams(has_side_effects=True),
    )(*parts)
    return list(outs)


def _sibling_fill(group, fulls):
    names = GROUPS[group]
    nw = len(names)

    def body(*refs):
        ins, outs = refs[:nw], refs[nw:2 * nw]
        send_sems, recv_sems = refs[2 * nw:]
        sibling = (lax.axis_index("x"), lax.axis_index("y"), 1 - lax.axis_index("c"))

        def forward(chip, core):
            copies = []
            for t, name in enumerate(names):
                for j in range(3):
                    other = chip ^ RELATION_XOR[j]
                    copies.append(pltpu.make_async_remote_copy(
                        src_ref=_half_in_full(ins[t], name, other, core),
                        dst_ref=_half_in_full(outs[t], name, other, core),
                        send_sem=send_sems.at[3 * t + j], recv_sem=recv_sems.at[3 * t + j],
                        device_id=sibling, device_id_type=MESH))
            for cp in copies:
                cp.start()
            for cp in copies:
                cp.wait()
        _on_each_device(forward)

    arrays = [fulls[n] for n in names]
    outs = pl.pallas_call(
        body, name=f"sibling_fill_{group}", in_specs=[ANY] * nw, out_specs=[ANY] * nw,
        out_shape=[jax.ShapeDtypeStruct(a.shape, a.dtype) for a in arrays],
        scratch_shapes=[pltpu.SemaphoreType.DMA((3 * nw,)), pltpu.SemaphoreType.DMA((3 * nw,))],
        input_output_aliases={t: t for t in range(nw)},
        compiler_params=pltpu.CompilerParams(has_side_effects=True),
    )(*arrays)
    return dict(zip(names, outs))


N_DEV = 8


def _all_reduce_small(v, after):
    R, C = v.shape

    def body(v_ref, after_ref, o_ref, slots, send_sems, recv_sems):
        x, y, c = lax.axis_index("x"), lax.axis_index("y"), lax.axis_index("c")
        me = 4 * x + 2 * y + c
        slots[me] = v_ref[...]
        copies = []
        for k in range(1, N_DEV):
            peer = me ^ k
            cp = pltpu.make_async_remote_copy(
                src_ref=v_ref, dst_ref=slots.at[me], send_sem=send_sems.at[k - 1], recv_sem=recv_sems.at[k - 1],
                device_id=(peer // 4, (peer // 2) % 2, peer % 2), device_id_type=MESH)
            cp.start()
            copies.append(cp)
        for cp in copies:
            cp.wait()
        acc = slots[0]
        for k in range(1, N_DEV):
            acc = acc + slots[k]
        o_ref[...] = acc

    return pl.pallas_call(
        body, name="all_reduce_small", out_shape=jax.ShapeDtypeStruct((R, C), F32),
        in_specs=[pl.BlockSpec(memory_space=pltpu.VMEM), ANY], out_specs=pl.BlockSpec(memory_space=pltpu.VMEM),
        scratch_shapes=[pltpu.VMEM((N_DEV, R, C), F32), pltpu.SemaphoreType.DMA((N_DEV - 1,)),
                        pltpu.SemaphoreType.DMA((N_DEV - 1,))],
        compiler_params=pltpu.CompilerParams(has_side_effects=True),
    )(v, after)


def _as2d(shape):
    return (int(np.prod(shape[:-1])), shape[-1])


def _row_block(rows, cols, n_arrays):
    budget = V7X_VMEM_LIMIT // 2
    tr = rows
    while tr % 16 == 0 and 2 * n_arrays * tr * cols * 4 > budget:
        tr //= 2
    return tr


def _sum_slots(slots):
    _, R, C = slots.shape
    tr = _row_block(R, C, 5)

    def body(s_ref, o_ref):
        acc = s_ref[0].astype(F32)
        for k in range(1, N_CHIPS):
            acc = acc + s_ref[k].astype(F32)
        o_ref[...] = acc.astype(BF16)

    return pl.pallas_call(
        body, name="sum_slots", grid=(R // tr,),
        in_specs=[pl.BlockSpec((N_CHIPS, tr, C), lambda i: (0, i, 0))],
        out_specs=pl.BlockSpec((tr, C), lambda i: (i, 0)),
        out_shape=jax.ShapeDtypeStruct((R, C), BF16),
        compiler_params=_params("parallel"),
    )(slots)


def _adamw(l, w, m, v, g_a, g_b, prev):
    L, R, C = w.shape
    tr = _row_block(R, C, 9)
    c1 = 1.0 - ADAM_B1 ** ADAM_STEP
    c2 = 1.0 - ADAM_B2 ** ADAM_STEP

    def body(w_ref, m_ref, v_ref, a_ref, b_ref, *rest):
        g_ref, d_ref, nm_ref, nv_ref = rest[-4:]
        g = a_ref[...].astype(F32) + b_ref[...].astype(F32)
        nm = ADAM_B1 * m_ref[...] + (1.0 - ADAM_B1) * g
        nv = ADAM_B2 * v_ref[...] + (1.0 - ADAM_B2) * (g * g)
        g_ref[...] = g
        nm_ref[...] = nm
        nv_ref[...] = nv
        d_ref[...] = -ADAM_LR * ((nm / c1) / (jnp.sqrt(nv / c2) + ADAM_EPS) + ADAM_WD * w_ref[...])

    layer = pl.BlockSpec((None, tr, C), lambda i: (l, i, 0))
    blk = pl.BlockSpec((tr, C), lambda i: (i, 0))
    out = jax.ShapeDtypeStruct((L, R, C), F32)
    prev = [] if prev is None else list(prev)
    return pl.pallas_call(
        body, name="adamw", grid=(R // tr,), in_specs=[layer] * 3 + [blk] * 2 + [ANY] * len(prev),
        out_specs=[layer] * 4, out_shape=[out] * 4,
        input_output_aliases={5 + k: k for k in range(len(prev))},
        compiler_params=_params("parallel"),
    )(w, m, v, g_a, g_b, *prev)


SMALL_ROWS = 16


def _pack_small(pool_scale, g_mix, g_ffn, g_final, attn_sink, rel_bias):
    tail = jnp.concatenate([attn_sink.reshape(-1), rel_bias.reshape(-1)])
    tail = jnp.pad(tail, (0, D_MODEL - tail.shape[0])).reshape(1, D_MODEL)
    rows = jnp.concatenate([pool_scale, g_mix, g_ffn, g_final.reshape(1, D_MODEL), tail], axis=0)
    return jnp.pad(rows, ((0, SMALL_ROWS - rows.shape[0]), (0, 0)))


def _unpack_small(packed):
    n_sink = DEPTH * N_HEADS
    return dict(pool_scale=packed[0:4], g_mix=packed[4:8], g_ffn=packed[8:12], g_final=packed[12],
                attn_sink=packed[13, 0:n_sink].reshape(DEPTH, N_HEADS),
                rel_bias=packed[13, n_sink:n_sink + N_BUCKETS * N_HEADS].reshape(N_BUCKETS, N_HEADS))


def _group_shards(l, group, masters):
    out = {}
    for n in GROUPS[group]:
        w = masters[n][l]
        out[n] = jnp.pad(w.reshape(3, -1), ((0, 5), (0, 0))) if n == "conv_w" else w.astype(BF16)
    return out


def kernel(x, w_in, conv_w, w_a_out, w_pool, pool_scale, w_attn_out, attn_sink, w_o, g_mix, g_ffn, w_gu, w_down, rel_bias, g_final, loss_target, m_w_in, m_conv_w, m_w_a_out, m_w_pool, m_pool_scale, m_w_attn_out, m_attn_sink, m_w_o, m_g_mix, m_g_ffn, m_w_gu, m_w_down, m_rel_bias, m_g_final, v_w_in, v_conv_w, v_w_a_out, v_w_pool, v_pool_scale, v_w_attn_out, v_attn_sink, v_w_o, v_g_mix, v_g_ffn, v_w_gu, v_w_down, v_rel_bias, v_g_final):
    big = dict(w_in=(w_in, m_w_in, v_w_in), conv_w=(conv_w, m_conv_w, v_conv_w), w_a_out=(w_a_out, m_w_a_out, v_w_a_out),
               w_pool=(w_pool, m_w_pool, v_w_pool), w_attn_out=(w_attn_out, m_w_attn_out, v_w_attn_out),
               w_o=(w_o, m_w_o, v_w_o), w_gu=(w_gu, m_w_gu, v_w_gu), w_down=(w_down, m_w_down, v_w_down))

    big3 = {n: tuple(a.reshape((DEPTH,) + _as2d(a.shape[1:])) for a in big[n]) for n in WEIGHT_NAMES}
    masters = {n: big[n][0] for n in WEIGHT_NAMES}

    gathers = {(0, "mix"): _gather_start("mix", _group_shards(0, "mix", masters), rel_bias)}
    newest = {"token": gathers[0, "mix"]["token"]}
    masters = dict(zip(WEIGHT_NAMES, lax.optimization_barrier(
        (tuple(masters[n] for n in WEIGHT_NAMES), newest["token"]))[0]))

    def weights_of(l, group, a):
        W = _sibling_fill(group, _exchange_wait("gather", group, gathers.pop((l, group)), a))
        if group == "mix":
            gathers[l, "ffn"] = _gather_start("ffn", _group_shards(l, "ffn", masters), W["w_in"])
            newest["token"] = gathers[l, "ffn"]["token"]
            if l + 1 < DEPTH:
                gathers[l + 1, "mix"] = _gather_start("mix", _group_shards(l + 1, "mix", masters), newest["token"])
                newest["token"] = gathers[l + 1, "mix"]["token"]
        return W, newest["token"]

    results = {n: None for n in WEIGHT_NAMES}
    scatters = {}

    def finish(l, after):
        slots = {}
        for group in GROUPS:
            slots.update(_exchange_wait("scatter", group, scatters.pop((l, group)), after))
        parts = [_sum_slots(slots[n].reshape((N_CHIPS,) + _as2d(slots[n].shape[1:]))) for n in WEIGHT_NAMES]
        others = _sibling_exchange(parts)
        for n, mine, other in zip(WEIGHT_NAMES, parts, others):
            if n == "conv_w":
                mine, other = mine[0:3], other[0:3]
            results[n] = _adamw(l, *big3[n], mine, other, results[n])

    def grads_to(l, group, wgrads, a):
        if group == "done":
            if l + 1 < DEPTH:
                finish(l + 1, a)
            return None
        scatters[l, group] = _scatter_start(group, wgrads, a)
        return scatters[l, group]["token"]

    loss, grad_x, smalls = _local_step(x[0], loss_target[0], weights_of, grads_to, pool_scale, attn_sink, g_mix, g_ffn,
                                       rel_bias, g_final)
    finish(0, results["w_down"][0])
    stacked = {n: [o.reshape(big[n][0].shape) for o in results[n]] for n in WEIGHT_NAMES}

    g_small = _all_reduce_small(_pack_small(smalls["pool_scale"], smalls["g_mix"], smalls["g_ffn"], smalls["g_final"],
                                            smalls["attn_sink"], smalls["rel_bias"]), results["w_in"][0])
    w_small = _pack_small(pool_scale, g_mix, g_ffn, g_final, attn_sink, rel_bias)
    m_small = _pack_small(m_pool_scale, m_g_mix, m_g_ffn, m_g_final, m_attn_sink, m_rel_bias)
    v_small = _pack_small(v_pool_scale, v_g_mix, v_g_ffn, v_g_final, v_attn_sink, v_rel_bias)
    small_out = [_unpack_small(o[0]) for o in
                 _adamw(0, w_small[None], m_small[None], v_small[None], g_small, jnp.zeros_like(g_small), None)]

    total_loss = lax.psum(loss, ("x", "y", "c"))

    order = ("w_in", "conv_w", "w_a_out", "w_pool", "pool_scale", "w_attn_out", "attn_sink", "w_o", "g_mix", "g_ffn",
             "w_gu", "w_down", "rel_bias", "g_final")
    outs = [total_loss, grad_x[None]]
    for k in range(4):
        for n in order:
            outs.append(stacked[n][k] if n in stacked else small_out[k][n])
    return tuple(outs)
```

```python
import functools
import math

import numpy as np
import jax
import jax.numpy as jnp
from jax import lax
from jax.experimental import pallas as pl
from jax.experimental.pallas import tpu as pltpu

F32 = jnp.float32
BF16 = jnp.bfloat16

D_MODEL = 1024
DEPTH = 4
N_HEADS = 16
N_KV_HEADS = 4
HEAD_DIM = 64
GQA = N_HEADS // N_KV_HEADS
WINDOW = 128
BLOCK = 128
N_BUCKETS = 32
MAX_DISTANCE = 128
POOL_GROUPS = 4
POOL_CG = D_MODEL // POOL_GROUPS
POOL_WINDOWS = (2, 4, 8, 16)
D_FF = 2816
IN_TOTAL = 8704
OFF_B, OFF_C, OFF_X, OFF_U, OFF_Q, OFF_K, OFF_V, OFF_GA, OFF_GP, OFF_GT = (
    0, 1024, 2048, 3072, 4096, 5120, 5376, 5632, 6656, 7680)
EPS = 1e-6
NEG_INF = -1e30
SM_SCALE = HEAD_DIM ** -0.5

ADAM_LR = 0.001
ADAM_B1 = 0.9
ADAM_B2 = 0.999
ADAM_EPS = 1e-08
ADAM_WD = 0.01
ADAM_STEP = 10

N_CHIPS = 4
HALO = 8
V7X_VMEM_LIMIT = 56 * 1024 * 1024
MESH = pl.DeviceIdType.MESH
ANY = pl.BlockSpec(memory_space=pl.ANY)


def _params(*sem):
    return pltpu.CompilerParams(dimension_semantics=tuple(sem) if sem else None,
                                vmem_limit_bytes=V7X_VMEM_LIMIT)


def _tile(n, pref):
    t = min(pref, n)
    while n % t or t % 128:
        t -= 128
    return t


def _nt(a, b):
    return lax.dot_general(a, b, (((1,), (1,)), ((), ())), preferred_element_type=F32)


def _tn(a, b):
    return lax.dot_general(a, b, (((0,), (0,)), ((), ())), preferred_element_type=F32)


def _nn(a, b):
    return jnp.dot(a, b, preferred_element_type=F32)


def _sigmoid(v):
    return 1.0 / (1.0 + jnp.exp(-v))


def _norm_matmul(x, g, w, name, token):
    S, Dm = x.shape
    N = w.shape[1]
    tm, tn = _tile(S, 1024), _tile(N, N // 4)

    def body(x_ref, g_ref, w_ref, token_ref, h_ref, o_ref):
        @pl.when(pl.program_id(1) == 0)
        def _():
            xv = x_ref[...]
            r = lax.rsqrt(jnp.mean(xv * xv, axis=-1, keepdims=True) + EPS)
            h_ref[...] = (xv * r * g_ref[...]).astype(BF16)
        o_ref[...] = _nn(h_ref[...], w_ref[...]).astype(BF16)

    return pl.pallas_call(
        body, name=name, grid=(S // tm, N // tn),
        in_specs=[pl.BlockSpec((tm, Dm), lambda i, j: (i, 0)),
                  pl.BlockSpec((1, Dm), lambda i, j: (0, 0)),
                  pl.BlockSpec((Dm, tn), lambda i, j: (0, j)), ANY],
        out_specs=[pl.BlockSpec((tm, Dm), lambda i, j: (i, 0)),
                   pl.BlockSpec((tm, tn), lambda i, j: (i, j))],
        out_shape=[jax.ShapeDtypeStruct((S, Dm), BF16), jax.ShapeDtypeStruct((S, N), BF16)],
        compiler_params=_params("parallel", "arbitrary"),
    )(x, g, w, token)


CB = 128
CBW = 128


def _fill_padded(pad_ref, v, S):
    z = jnp.zeros((HALO, v.shape[1]), F32)
    pad_ref[pl.ds(0, HALO), :] = z
    pad_ref[pl.ds(S + HALO, HALO), :] = z
    pad_ref[pl.ds(HALO, S), :] = v


def _shifted(pad_ref, off, S):
    return pad_ref[pl.ds(HALO + off, S), :]


def _conv_fwd(proj, cw8):
    S = proj.shape[0]
    nblk = D_MODEL // CBW

    def body(b_ref, c_ref, x_ref, w_ref, o_ref, pad):
        u = c_ref[...].astype(F32) * x_ref[...].astype(F32)
        _fill_padded(pad, u, S)
        cv = w_ref[0:1, :] * _shifted(pad, -1, S) + w_ref[1:2, :] * u + w_ref[2:3, :] * _shifted(pad, 1, S)
        o_ref[...] = (b_ref[...].astype(F32) * cv).astype(BF16)

    col = lambda base: pl.BlockSpec((S, CBW), lambda j: (0, base // CBW + j))
    return pl.pallas_call(
        body, name="conv_fwd", grid=(nblk,),
        in_specs=[col(OFF_B), col(OFF_C), col(OFF_X), pl.BlockSpec((8, CBW), lambda j: (0, j))],
        out_specs=pl.BlockSpec((S, CBW), lambda j: (0, j)),
        out_shape=jax.ShapeDtypeStruct((S, D_MODEL), BF16),
        scratch_shapes=[pltpu.VMEM((S + 2 * HALO, CBW), F32)],
        compiler_params=_params("parallel"),
    )(proj, proj, proj, cw8)


def _pool_count(S, lo, hi):
    t = lax.broadcasted_iota(jnp.int32, (S, CBW), 0)
    return (jnp.minimum(t + hi, S - 1) - jnp.maximum(t - lo, 0) + 1).astype(F32)


def _pool_fwd(proj):
    S = proj.shape[0]
    nblk = D_MODEL // CBW
    per_group = POOL_CG // CBW

    def body(u_ref, o_ref, pad):
        u = u_ref[...].astype(F32)
        _fill_padded(pad, u, S)
        grp = pl.program_id(0) // per_group
        for gi, w in enumerate(POOL_WINDOWS):
            @pl.when(grp == gi)
            def _(w=w):
                lo, hi = w // 2, w - 1 - w // 2
                acc = _shifted(pad, -lo, S)
                for off in range(-lo + 1, hi + 1):
                    acc = acc + _shifted(pad, off, S)
                o_ref[...] = (acc / _pool_count(S, lo, hi) - u).astype(BF16)

    return pl.pallas_call(
        body, name="pool_fwd", grid=(nblk,),
        in_specs=[pl.BlockSpec((S, CBW), lambda j: (0, OFF_U // CBW + j))],
        out_specs=pl.BlockSpec((S, CBW), lambda j: (0, j)),
        out_shape=jax.ShapeDtypeStruct((S, D_MODEL), BF16),
        scratch_shapes=[pltpu.VMEM((S + 2 * HALO, CBW), F32)],
        compiler_params=_params("parallel"),
    )(proj)


def _attn_specs(S):
    nb = S // BLOCK
    kcol, vcol = OFF_K // (N_KV_HEADS * HEAD_DIM), OFF_V // (N_KV_HEADS * HEAD_DIM)
    kvw = N_KV_HEADS * HEAD_DIM
    prev = lambda i: jnp.maximum(i - 1, 0)
    nxt = lambda i: jnp.minimum(i + 1, nb - 1)
    return [
        pl.BlockSpec((BLOCK, D_MODEL), lambda i: (i, OFF_Q // D_MODEL)),
        pl.BlockSpec((BLOCK, kvw), lambda i: (prev(i), kcol)),
        pl.BlockSpec((BLOCK, kvw), lambda i: (i, kcol)),
        pl.BlockSpec((BLOCK, kvw), lambda i: (nxt(i), kcol)),
        pl.BlockSpec((BLOCK, kvw), lambda i: (prev(i), vcol)),
        pl.BlockSpec((BLOCK, kvw), lambda i: (i, vcol)),
        pl.BlockSpec((BLOCK, kvw), lambda i: (nxt(i), vcol)),
    ]


def _heads_rows(ref_or_val, hk):
    return jnp.concatenate(
        [ref_or_val[:, (GQA * hk + g) * HEAD_DIM:(GQA * hk + g + 1) * HEAD_DIM] for g in range(GQA)], axis=0)


def _kv_rows(p_ref, c_ref, n_ref, hk):
    sl = slice(hk * HEAD_DIM, (hk + 1) * HEAD_DIM)
    return jnp.concatenate([p_ref[:, sl], c_ref[:, sl], n_ref[:, sl]], axis=0)


def _bias_cols(bias_ref, hk):
    return jnp.concatenate([bias_ref[GQA * hk + g] for g in range(GQA)], axis=1)


def _softmax_keys_on_rows(q4s, kc, bias_blk, sink_row):
    s = _nt(kc, q4s) + bias_blk
    m = jnp.maximum(jnp.max(s, axis=0, keepdims=True), sink_row)
    p = jnp.exp(s - m)
    e_sink = jnp.exp(sink_row - m)
    inv = 1.0 / (jnp.sum(p, axis=0, keepdims=True) + e_sink)
    return p * inv, e_sink * inv


TAB = (N_HEADS, 3 * BLOCK, BLOCK)
TAB_FLAT = 3 * BLOCK * BLOCK


def _bias_spec(nb):
    return pl.BlockSpec((None,) + TAB, lambda i: (jnp.where(i == 0, 0, jnp.where(i == nb - 1, 2, 1)), 0, 0, 0))


def _attn_fwd(proj, bias_tabs, sink_rows):
    S = proj.shape[0]
    nb = S // BLOCK
    assert nb >= 2

    def body(q_ref, kp, kc_, kn, vp, vc_, vn, bias_ref, sink_ref, o_ref):
        outs = []
        for hk in range(N_KV_HEADS):
            q4s = _heads_rows(q_ref, hk) * SM_SCALE
            kc = _kv_rows(kp, kc_, kn, hk)
            vc = _kv_rows(vp, vc_, vn, hk)
            pn, _ = _softmax_keys_on_rows(q4s, kc, _bias_cols(bias_ref, hk), sink_ref[hk:hk + 1, :])
            o4 = _tn(pn.astype(BF16), vc)
            outs += [o4[g * BLOCK:(g + 1) * BLOCK, :] for g in range(GQA)]
        o_ref[...] = jnp.concatenate(outs, axis=1).astype(BF16)

    return pl.pallas_call(
        body, name="attn_fwd", grid=(nb,),
        in_specs=_attn_specs(S) + [_bias_spec(nb), pl.BlockSpec((N_KV_HEADS, GQA * BLOCK), lambda i: (0, 0))],
        out_specs=pl.BlockSpec((BLOCK, D_MODEL), lambda i: (i, 0)),
        out_shape=jax.ShapeDtypeStruct((S, D_MODEL), BF16),
        compiler_params=_params("parallel"),
    )(*([proj] * 7), bias_tabs, sink_rows)


GATE_HALF = D_MODEL // 2


def _gate_specs(tm):
    return [pl.BlockSpec((tm, GATE_HALF), lambda i, c=off // GATE_HALF + k: (i, c))
            for off in (OFF_GA, OFF_GP, OFF_GT) for k in (0, 1)]


def _gate(lo_ref, hi_ref):
    return _sigmoid(jnp.concatenate([lo_ref[...], hi_ref[...]], axis=1).astype(F32))


def _pool_mix(p, wp):
    return jnp.concatenate(
        [_nn(p[:, g * POOL_CG:(g + 1) * POOL_CG], wp[g]) for g in range(POOL_GROUPS)], axis=1)


def _mix_fwd(za, p, att, proj, x, wa, wp, ps, wt, wo, token):
    S = x.shape[0]
    tm = _tile(S, 256)

    def body(za_ref, p_ref, att_ref, ga0, ga1, gp0, gp1, gt0, gt1, x_ref, wa_ref, wp_ref, ps_ref, wt_ref, wo_ref,
             token_ref, ya_ref, yp_ref, yt_ref, mg_ref, x2_ref):
        ya = _nn(za_ref[...], wa_ref[...])
        ypr = _pool_mix(p_ref[...], wp_ref)
        yt = _nn(att_ref[...], wt_ref[...])
        merged = _gate(ga0, ga1) * ya + _gate(gp0, gp1) * (ypr * ps_ref[...]) + _gate(gt0, gt1) * yt
        mb = merged.astype(BF16)
        ya_ref[...] = ya.astype(BF16)
        yp_ref[...] = ypr.astype(BF16)
        yt_ref[...] = yt.astype(BF16)
        mg_ref[...] = mb
        x2_ref[...] = x_ref[...] + _nn(mb, wo_ref[...])

    row = lambda c=0: pl.BlockSpec((tm, D_MODEL), lambda i: (i, c))
    whole = lambda a: pl.BlockSpec(a.shape, lambda i: (0,) * a.ndim)
    act = jax.ShapeDtypeStruct((S, D_MODEL), BF16)
    return pl.pallas_call(
        body, name="mix_fwd", grid=(S // tm,),
        in_specs=[row(), row(), row()] + _gate_specs(tm)
        + [row(), whole(wa), whole(wp), whole(ps), whole(wt), whole(wo), ANY],
        out_specs=[row(), row(), row(), row(), row()],
        out_shape=[act, act, act, act, jax.ShapeDtypeStruct((S, D_MODEL), F32)],
        compiler_params=_params("parallel"),
    )(za, p, att, *([proj] * 6), x, wa, wp, ps, wt, wo, token)


def _ffn_fwd(gu, x2, wd):
    S = x2.shape[0]
    tm = _tile(S, 256)

    def body(g_ref, u_ref, x_ref, w_ref, a_ref, o_ref):
        g = g_ref[...].astype(F32)
        a = (g * _sigmoid(g) * u_ref[...].astype(F32)).astype(BF16)
        a_ref[...] = a
        o_ref[...] = x_ref[...] + _nn(a, w_ref[...])

    return pl.pallas_call(
        body, name="ffn_fwd", grid=(S // tm,),
        in_specs=[pl.BlockSpec((tm, D_FF), lambda i: (i, 0)), pl.BlockSpec((tm, D_FF), lambda i: (i, 1)),
                  pl.BlockSpec((tm, D_MODEL), lambda i: (i, 0)), pl.BlockSpec((D_FF, D_MODEL), lambda i: (0, 0))],
        out_specs=[pl.BlockSpec((tm, D_FF), lambda i: (i, 0)), pl.BlockSpec((tm, D_MODEL), lambda i: (i, 0))],
        out_shape=[jax.ShapeDtypeStruct((S, D_FF), BF16), jax.ShapeDtypeStruct((S, D_MODEL), F32)],
        compiler_params=_params("parallel"),
    )(gu, gu, x2, wd)


def _loss_bwd(x, g, tgt):
    S, Dm = x.shape
    tm = _tile(S, 512)

    def body(x_ref, g_ref, t_ref, l_ref, dx_ref, dg_ref):
        @pl.when(pl.program_id(0) == 0)
        def _():
            l_ref[...] = jnp.zeros_like(l_ref)
            dg_ref[...] = jnp.zeros_like(dg_ref)
        xv, gv = x_ref[...], g_ref[...]
        r = lax.rsqrt(jnp.mean(xv * xv, axis=-1, keepdims=True) + EPS)
        n = xv * r
        err = n * gv - t_ref[...]
        l_ref[...] += 0.5 * jnp.sum(jnp.mean(err * err, axis=-1, keepdims=True), axis=0, keepdims=True)
        dy = err * (1.0 / Dm)
        dn = dy * gv
        dx_ref[...] = r * (dn - n * jnp.mean(dn * n, axis=-1, keepdims=True))
        dg_ref[...] += jnp.sum(dy * n, axis=0, keepdims=True)

    return pl.pallas_call(
        body, name="loss_bwd", grid=(S // tm,),
        in_specs=[pl.BlockSpec((tm, Dm), lambda i: (i, 0)), pl.BlockSpec((1, Dm), lambda i: (0, 0)),
                  pl.BlockSpec((tm, Dm), lambda i: (i, 0))],
        out_specs=[pl.BlockSpec((8, 128), lambda i: (0, 0)), pl.BlockSpec((tm, Dm), lambda i: (i, 0)),
                   pl.BlockSpec((1, Dm), lambda i: (0, 0))],
        out_shape=[jax.ShapeDtypeStruct((8, 128), F32), jax.ShapeDtypeStruct((S, Dm), F32),
                   jax.ShapeDtypeStruct((1, Dm), F32)],
        compiler_params=_params("arbitrary"),
    )(x, g, tgt)


def _ffn_bwd(dx3, gu, wd, token):
    S = dx3.shape[0]
    tm = _tile(S, 256)

    def body(d_ref, g_ref, u_ref, w_ref, token_ref, o_ref):
        dact = _nt(d_ref[...].astype(BF16), w_ref[...])
        g, u = g_ref[...].astype(F32), u_ref[...].astype(F32)
        sg = _sigmoid(g)
        o_ref[:, 0:D_FF] = (dact * u * (sg * (1.0 + g * (1.0 - sg)))).astype(BF16)
        o_ref[:, D_FF:2 * D_FF] = (dact * (g * sg)).astype(BF16)

    return pl.pallas_call(
        body, name="ffn_bwd", grid=(S // tm,),
        in_specs=[pl.BlockSpec((tm, D_MODEL), lambda i: (i, 0)),
                  pl.BlockSpec((tm, D_FF), lambda i: (i, 0)), pl.BlockSpec((tm, D_FF), lambda i: (i, 1)),
                  pl.BlockSpec((D_FF, D_MODEL), lambda i: (0, 0)), ANY],
        out_specs=pl.BlockSpec((tm, 2 * D_FF), lambda i: (i, 0)),
        out_shape=jax.ShapeDtypeStruct((S, 2 * D_FF), BF16),
        compiler_params=_params("parallel"),
    )(dx3, gu, gu, wd, token)


def _wgrad(a, b, name, tk=512, tn=512, out_dtype=BF16, token=None):
    S, K = a.shape
    N = b.shape[1]
    tk, tn, ts = _tile(K, tk), _tile(N, tn), _tile(S, 1024)
    n_s = S // ts
    extra = [] if token is None else [token]

    def body(a_ref, b_ref, *rest):
        o_ref, acc = rest[-2:]
        s = pl.program_id(2)

        @pl.when(s == 0)
        def _():
            acc[...] = jnp.zeros_like(acc)
        acc[...] += _tn(a_ref[...].astype(BF16), b_ref[...].astype(BF16))

        @pl.when(s == n_s - 1)
        def _():
            o_ref[...] = acc[...].astype(out_dtype)

    return pl.pallas_call(
        body, name=name, grid=(K // tk, N // tn, n_s),
        in_specs=[pl.BlockSpec((ts, tk), lambda k, n, s: (s, k)), pl.BlockSpec((ts, tn), lambda k, n, s: (s, n))]
        + [ANY] * len(extra),
        out_specs=pl.BlockSpec((tk, tn), lambda k, n, s: (k, n)),
        out_shape=jax.ShapeDtypeStruct((K, N), out_dtype),
        scratch_shapes=[pltpu.VMEM((tk, tn), F32)],
        compiler_params=_params("parallel", "parallel", "arbitrary"),
    )(a, b, *extra)


def _wgrad_pool(p, dyps):
    S = p.shape[0]
    ts = _tile(S, 4096)
    n_s = S // ts

    def body(a_ref, b_ref, o_ref, acc):
        s = pl.program_id(1)

        @pl.when(s == 0)
        def _():
            acc[...] = jnp.zeros_like(acc)
        acc[...] += _tn(a_ref[...], b_ref[...])

        @pl.when(s == n_s - 1)
        def _():
            o_ref[...] = acc[...].astype(BF16)

    return pl.pallas_call(
        body, name="wgrad_pool", grid=(POOL_GROUPS, n_s),
        in_specs=[pl.BlockSpec((ts, POOL_CG), lambda g, s: (s, g)), pl.BlockSpec((ts, POOL_CG), lambda g, s: (s, g))],
        out_specs=pl.BlockSpec((None, POOL_CG, POOL_CG), lambda g, s: (g, 0, 0)),
        out_shape=jax.ShapeDtypeStruct((POOL_GROUPS, POOL_CG, POOL_CG), BF16),
        scratch_shapes=[pltpu.VMEM((POOL_CG, POOL_CG), F32)],
        compiler_params=_params("parallel", "arbitrary"),
    )(p, dyps)


def _dgrad_norm_bwd(dy, w, x, g, dres, name, tk, token=None):
    S, K = dy.shape
    Dm = x.shape[1]
    tm, tk = _tile(S, 1024), _tile(K, tk)
    n_k = K // tk
    extra = [] if token is None else [token]

    def body(dy_ref, w_ref, x_ref, g_ref, r_ref, *rest):
        dx_ref, dg_ref, acc = rest[-3:]
        i, k = pl.program_id(0), pl.program_id(1)

        @pl.when((i == 0) & (k == 0))
        def _():
            dg_ref[...] = jnp.zeros_like(dg_ref)

        @pl.when(k == 0)
        def _():
            acc[...] = jnp.zeros_like(acc)
        acc[...] += _nt(dy_ref[...], w_ref[...])

        @pl.when(k == n_k - 1)
        def _():
            dh, xv = acc[...], x_ref[...]
            r = lax.rsqrt(jnp.mean(xv * xv, axis=-1, keepdims=True) + EPS)
            n = xv * r
            dn = dh * g_ref[...]
            dx_ref[...] = r_ref[...] + r * (dn - n * jnp.mean(dn * n, axis=-1, keepdims=True))
            dg_ref[...] += jnp.sum(dh * n, axis=0, keepdims=True)

    rowblk = pl.BlockSpec((tm, Dm), lambda i, k: (i, 0))
    vec = pl.BlockSpec((1, Dm), lambda i, k: (0, 0))
    return pl.pallas_call(
        body, name=name, grid=(S // tm, n_k),
        in_specs=[pl.BlockSpec((tm, tk), lambda i, k: (i, k)), pl.BlockSpec((Dm, tk), lambda i, k: (0, k)),
                  rowblk, vec, rowblk] + [ANY] * len(extra),
        out_specs=[rowblk, vec],
        out_shape=[jax.ShapeDtypeStruct((S, Dm), F32), jax.ShapeDtypeStruct((1, Dm), F32)],
        scratch_shapes=[pltpu.VMEM((tm, Dm), F32)],
        compiler_params=_params("arbitrary", "arbitrary"),
    )(dy, w, x, g, dres, *extra)


def _mix_bwd(dx2, ya, ypr, yt, proj, ps, wa, wp, wt, wo, token):
    S = dx2.shape[0]
    tm = _tile(S, 256)

    n_tiles = S // tm

    def body(dx_ref, ya_ref, yp_ref, yt_ref, ga0, ga1, gp0, gp1, gt0, gt1, ps_ref, wa_ref, wp_ref, wt_ref, wo_ref,
             token_ref, dya_ref, dyt_ref, dyps_ref, dza_ref, datt_ref, dp_ref, dproj_hbm, dps_ref, dgates, sem):
        i = pl.program_id(0)
        to_dproj = pltpu.make_async_copy(
            dgates, dproj_hbm.at[pl.ds(pl.multiple_of(i * tm, tm), tm), pl.ds(OFF_GA, 3 * D_MODEL)], sem)

        @pl.when(i == 0)
        def _():
            dps_ref[...] = jnp.zeros_like(dps_ref)
        dm = _nt(dx_ref[...].astype(BF16), wo_ref[...])
        sa, sp, st = _gate(ga0, ga1), _gate(gp0, gp1), _gate(gt0, gt1)
        psv = ps_ref[...]
        ypr_v = yp_ref[...].astype(F32)
        dya = (sa * dm).astype(BF16)
        dyt = (st * dm).astype(BF16)
        dyp = sp * dm
        dyps = (dyp * psv).astype(BF16)
        dya_ref[...] = dya
        dyt_ref[...] = dyt
        dyps_ref[...] = dyps
        dg = jnp.concatenate(
            [dm * ya_ref[...].astype(F32) * (sa * (1.0 - sa)), dm * (ypr_v * psv) * (sp * (1.0 - sp)),
             dm * yt_ref[...].astype(F32) * (st * (1.0 - st))], axis=1).astype(BF16)

        @pl.when(i > 0)
        def _():
            to_dproj.wait()
        dgates[...] = dg
        to_dproj.start()
        dps_ref[...] += jnp.sum(dyp * ypr_v, axis=0, keepdims=True)
        dza_ref[...] = _nt(dya, wa_ref[...]).astype(BF16)
        datt_ref[...] = _nt(dyt, wt_ref[...]).astype(BF16)
        dp_ref[...] = jnp.concatenate(
            [_nt(dyps[:, g * POOL_CG:(g + 1) * POOL_CG], wp_ref[g]) for g in range(POOL_GROUPS)], axis=1).astype(BF16)

        @pl.when(i == n_tiles - 1)
        def _():
            to_dproj.wait()

    row = lambda c=0: pl.BlockSpec((tm, D_MODEL), lambda i: (i, c))
    whole = lambda a: pl.BlockSpec(a.shape, lambda i: (0,) * a.ndim)
    act = jax.ShapeDtypeStruct((S, D_MODEL), BF16)
    return pl.pallas_call(
        body, name="mix_bwd", grid=(n_tiles,),
        in_specs=[row(), row(), row(), row()] + _gate_specs(tm)
        + [whole(ps), whole(wa), whole(wp), whole(wt), whole(wo), ANY],
        out_specs=[row()] * 6 + [ANY, pl.BlockSpec((1, D_MODEL), lambda i: (0, 0))],
        out_shape=[act] * 6 + [jax.ShapeDtypeStruct((S, IN_TOTAL), BF16), jax.ShapeDtypeStruct((1, D_MODEL), F32)],
        scratch_shapes=[pltpu.VMEM((tm, 3 * D_MODEL), BF16), pltpu.SemaphoreType.DMA],
        compiler_params=_params("arbitrary"),
    )(dx2, ya, ypr, yt, *([proj] * 6), ps, wa, wp, wt, wo, token)


def _conv_bwd(dza, proj, cw8, dproj):
    S = proj.shape[0]
    nblk = D_MODEL // CB

    def body(d_ref, b_ref, c_ref, x_ref, w_ref, dproj_in, dproj_hbm, dw_ref, pad_u, pad_d, parts, sems):
        cb = pl.program_id(0)
        to_dproj = [pltpu.make_async_copy(
            parts.at[k], dproj_hbm.at[:, pl.ds(pl.multiple_of(off + cb * CB, CB), CB)], sems.at[k])
            for k, off in enumerate((OFF_B, OFF_C, OFF_X))]
        c, xa = c_ref[...].astype(F32), x_ref[...].astype(F32)
        u = c * xa
        _fill_padded(pad_u, u, S)
        u_prev, u_next = _shifted(pad_u, -1, S), _shifted(pad_u, 1, S)
        cv = w_ref[0:1, :] * u_prev + w_ref[1:2, :] * u + w_ref[2:3, :] * u_next
        dza_v = d_ref[...].astype(F32)
        dcv = dza_v * b_ref[...].astype(F32)
        _fill_padded(pad_d, dcv, S)
        du = w_ref[0:1, :] * _shifted(pad_d, 1, S) + w_ref[1:2, :] * dcv + w_ref[2:3, :] * _shifted(pad_d, -1, S)

        @pl.when(cb > 0)
        def _():
            for cp in to_dproj:
                cp.wait()
        parts[0] = (dza_v * cv).astype(BF16)
        parts[1] = (du * xa).astype(BF16)
        parts[2] = (du * c).astype(BF16)
        for cp in to_dproj:
            cp.start()
        dw_ref[...] = jnp.concatenate(
            [jnp.sum(dcv * u_prev, axis=0, keepdims=True), jnp.sum(dcv * u, axis=0, keepdims=True),
             jnp.sum(dcv * u_next, axis=0, keepdims=True), jnp.zeros((5, CB), F32)], axis=0)

        @pl.when(cb == nblk - 1)
        def _():
            for cp in to_dproj:
                cp.wait()

    col = lambda base: pl.BlockSpec((S, CB), lambda cb: (0, base // CB + cb))
    taps = pl.BlockSpec((8, CB), lambda cb: (0, cb))
    return pl.pallas_call(
        body, name="conv_bwd", grid=(nblk,),
        in_specs=[col(0), col(OFF_B), col(OFF_C), col(OFF_X), taps, ANY],
        out_specs=[ANY, taps],
        out_shape=[jax.ShapeDtypeStruct(dproj.shape, dproj.dtype), jax.ShapeDtypeStruct((8, D_MODEL), F32)],
        scratch_shapes=[pltpu.VMEM((S + 2 * HALO, CB), F32), pltpu.VMEM((S + 2 * HALO, CB), F32),
                        pltpu.VMEM((3, S, CB), BF16), pltpu.SemaphoreType.DMA((3,))],
        input_output_aliases={5: 0},
        compiler_params=_params("arbitrary"),
    )(dza, proj, proj, proj, cw8, dproj)


def _pool_bwd(dp, dproj):
    S = dp.shape[0]
    nblk = D_MODEL // CBW
    per_group = POOL_CG // CBW

    def body(d_ref, dproj_in, o_ref, pad):
        d = d_ref[...].astype(F32)
        grp = pl.program_id(0) // per_group
        for gi, w in enumerate(POOL_WINDOWS):
            @pl.when(grp == gi)
            def _(w=w):
                lo, hi = w // 2, w - 1 - w // 2
                _fill_padded(pad, d / _pool_count(S, lo, hi), S)
                acc = _shifted(pad, -hi, S)
                for off in range(-hi + 1, lo + 1):
                    acc = acc + _shifted(pad, off, S)
                o_ref[...] = (acc - d).astype(BF16)

    return pl.pallas_call(
        body, name="pool_bwd", grid=(nblk,),
        in_specs=[pl.BlockSpec((S, CBW), lambda j: (0, j)), ANY],
        out_specs=pl.BlockSpec((S, CBW), lambda j: (0, OFF_U // CBW + j)),
        out_shape=jax.ShapeDtypeStruct(dproj.shape, dproj.dtype),
        scratch_shapes=[pltpu.VMEM((S + 2 * HALO, CBW), F32)],
        input_output_aliases={1: 0},
        compiler_params=_params("parallel"),
    )(dp, dproj)


def _attn_bwd(proj, datt, bias_tabs, sink_rows, dbias_in, dproj):
    S = proj.shape[0]
    nb = S // BLOCK
    kvw = N_KV_HEADS * HEAD_DIM

    def body(q_ref, kp, kc_, kn, vp, vc_, vn, do_ref, bias_ref, sink_ref, dbin_ref, dproj_in,
             dq_ref, dk_ref, dv_ref, db_ref, ds_ref):
        i = pl.program_id(0)

        @pl.when(i == 0)
        def _():
            dk_ref[...] = jnp.zeros_like(dk_ref)
            dv_ref[...] = jnp.zeros_like(dv_ref)
            db_ref[...] = dbin_ref[...]
            ds_ref[...] = jnp.zeros_like(ds_ref)
        dqs, dks, dvs = [], [], []
        for hk in range(N_KV_HEADS):
            q4s = _heads_rows(q_ref, hk) * SM_SCALE
            do4 = _heads_rows(do_ref, hk)
            kc = _kv_rows(kp, kc_, kn, hk)
            vc = _kv_rows(vp, vc_, vn, hk)
            pn, p_sink = _softmax_keys_on_rows(q4s, kc, _bias_cols(bias_ref, hk), sink_ref[hk:hk + 1, :])
            dpm = _nt(vc, do4)
            delta = jnp.sum(pn * dpm, axis=0, keepdims=True)
            dsc = pn * (dpm - delta)
            for g in range(GQA):
                db_ref[GQA * hk + g] += dsc[:, g * BLOCK:(g + 1) * BLOCK]
            ds_ref[hk:hk + 1, :] += -p_sink * delta
            dsb = dsc.astype(BF16)
            dq4 = _tn(dsb, kc) * SM_SCALE
            dqs += [dq4[g * BLOCK:(g + 1) * BLOCK, :] for g in range(GQA)]
            dks.append(_nn(dsb, q4s))
            dvs.append(_nn(pn.astype(BF16), do4))
        dq_ref[...] = jnp.concatenate(dqs, axis=1).astype(BF16)
        r0 = pl.multiple_of(i * BLOCK, BLOCK)
        dk_ref[pl.ds(r0, 3 * BLOCK), :] += jnp.concatenate(dks, axis=1)
        dv_ref[pl.ds(r0, 3 * BLOCK), :] += jnp.concatenate(dvs, axis=1)

    const = lambda shape: pl.BlockSpec(shape, lambda i: (0,) * len(shape))
    sink_shape = (N_KV_HEADS, GQA * BLOCK)
    return pl.pallas_call(
        body, name="attn_bwd", grid=(nb,),
        in_specs=_attn_specs(S) + [pl.BlockSpec((BLOCK, D_MODEL), lambda i: (i, 0)),
                                   _bias_spec(nb), const(sink_shape), const(TAB), ANY],
        out_specs=[pl.BlockSpec((BLOCK, D_MODEL), lambda i: (i, OFF_Q // D_MODEL)),
                   const((S + 2 * BLOCK, kvw)), const((S + 2 * BLOCK, kvw)), const(TAB), const(sink_shape)],
        out_shape=[jax.ShapeDtypeStruct(dproj.shape, dproj.dtype),
                   jax.ShapeDtypeStruct((S + 2 * BLOCK, kvw), F32), jax.ShapeDtypeStruct((S + 2 * BLOCK, kvw), F32),
                   jax.ShapeDtypeStruct(TAB, F32), jax.ShapeDtypeStruct(sink_shape, F32)],
        input_output_aliases={11: 0},
        compiler_params=_params("arbitrary"),
    )(*([proj] * 7), datt, bias_tabs, sink_rows, dbias_in, dproj)


def _kv_finish(dkp, dvp, dproj):
    S = dproj.shape[0]
    kvw = N_KV_HEADS * HEAD_DIM

    def body(dk_ref, dv_ref, dproj_in, o_ref):
        o_ref[:, 0:kvw] = dk_ref[pl.ds(BLOCK, S), :].astype(BF16)
        o_ref[:, kvw:2 * kvw] = dv_ref[pl.ds(BLOCK, S), :].astype(BF16)

    whole = pl.BlockSpec((S + 2 * BLOCK, kvw), lambda i: (0, 0))
    return pl.pallas_call(
        body, name="kv_finish", grid=(1,), in_specs=[whole, whole, ANY],
        out_specs=pl.BlockSpec((S, 2 * kvw), lambda i: (0, OFF_K // (2 * kvw))),
        out_shape=jax.ShapeDtypeStruct(dproj.shape, dproj.dtype),
        input_output_aliases={2: 0},
        compiler_params=_params("arbitrary"),
    )(dkp, dvp, dproj)


def _bucket_constants():
    half = N_BUCKETS // 2
    max_exact = half // 2
    qi = np.arange(BLOCK)[None, :]
    kj = np.arange(3 * BLOCK)[:, None]
    rel = kj - BLOCK - qi
    n = np.abs(rel)
    nf = np.maximum(n, 1).astype(np.float32)
    large = max_exact + (np.log(nf / np.float32(max_exact)) / np.float32(math.log(MAX_DISTANCE / max_exact))
                         * np.float32(half - max_exact)).astype(np.int32)
    large = np.minimum(large, half - 1)
    bucket = np.where(rel > 0, half, 0) + np.where(n < max_exact, n, large)
    onehot = (bucket.reshape(1, -1) == np.arange(N_BUCKETS)[:, None]).astype(np.float32)
    window = n <= WINDOW
    first = window & (kj >= BLOCK)
    last = window & (kj < 2 * BLOCK)
    masks = np.stack([np.where(v, 0.0, NEG_INF).astype(np.float32).reshape(-1) for v in (first, window, last)])
    return onehot, masks


def _bias_expand(rel_bias_t, onehot, masks):
    def body(r_ref, oh_ref, m_ref, o_ref):
        tab = jnp.dot(r_ref[...], oh_ref[...], preferred_element_type=F32, precision=lax.Precision.HIGHEST)
        for v in range(3):
            o_ref[v] = tab + m_ref[v:v + 1, :]

    return pl.pallas_call(
        body, name="bias_expand", out_shape=jax.ShapeDtypeStruct((3, N_HEADS, onehot.shape[1]), F32),
        compiler_params=_params(),
    )(rel_bias_t, onehot, masks)


def _bias_reduce(dtab, dsink_rows, onehot):
    def body(d_ref, s_ref, oh_ref, o_ref, so_ref):
        o_ref[...] = lax.dot_general(oh_ref[...], d_ref[...], (((1,), (1,)), ((), ())),
                                     preferred_element_type=F32, precision=lax.Precision.HIGHEST)
        so_ref[...] = jnp.sum(s_ref[...], axis=-1, keepdims=True)

    return pl.pallas_call(
        body, name="bias_reduce",
        out_shape=[jax.ShapeDtypeStruct((N_BUCKETS, N_HEADS), F32),
                   jax.ShapeDtypeStruct((dsink_rows.shape[0], 1), F32)],
        compiler_params=_params(),
    )(dtab, dsink_rows, onehot)


GROUPS = dict(mix=("w_in", "conv_w", "w_a_out", "w_pool", "w_attn_out", "w_o"), ffn=("w_gu", "w_down"))
WEIGHT_NAMES = GROUPS["mix"] + GROUPS["ffn"]


def _layer_fwd(l, x, weights_of, ps, g_mix, g_ffn, bias_tabs, sink_rows):
    W, token = weights_of(l, "mix", x)
    h, proj = _norm_matmul(x, g_mix, W["w_in"], "norm_proj", token)
    za = _conv_fwd(proj, W["conv_w"])
    p = _pool_fwd(proj)
    att = _attn_fwd(proj, bias_tabs, sink_rows)
    _, token = weights_of(l, "ffn on its way", att)
    ya, ypr, yt, merged, x2 = _mix_fwd(za, p, att, proj, x, W["w_a_out"], W["w_pool"], ps, W["w_attn_out"], W["w_o"],
                                       token)
    Wf, token = weights_of(l, "ffn", x2)
    h2, gu = _norm_matmul(x2, g_ffn, Wf["w_gu"], "norm_gu", token)
    act, x3 = _ffn_fwd(gu, x2, Wf["w_down"])
    saved = dict(x=x, h=h, proj=proj, za=za, p=p, att=att, ya=ya, ypr=ypr, yt=yt, merged=merged, x2=x2, h2=h2,
                 gu=gu, act=act, W={**W, **Wf}, sink_rows=sink_rows)
    return x3, saved, token


def _layer_bwd(l, dx3, sv, grads_to, ps, g_mix, g_ffn, bias_tabs, dbias, token):
    W, sink_rows = sv["W"], sv["sink_rows"]
    dgu = _ffn_bwd(dx3, sv["gu"], W["w_down"], token)
    g_w_down = _wgrad(sv["act"], dx3, "wgrad_down", tk=1408, tn=1024, token=token)
    g_w_gu = _wgrad(sv["h2"], dgu, "wgrad_gu", tk=1024, tn=1408)
    dx2, dg_ffn = _dgrad_norm_bwd(dgu, W["w_gu"], sv["x2"], g_ffn, dx3, "dgrad_gu", tk=1408)
    token = grads_to(l, "ffn", dict(w_gu=g_w_gu, w_down=g_w_down), dx2)
    dya, dyt, dyps, dza, datt, dp, dproj, dps = _mix_bwd(
        dx2, sv["ya"], sv["ypr"], sv["yt"], sv["proj"], ps, W["w_a_out"], W["w_pool"], W["w_attn_out"], W["w_o"], token)
    g_w_o = _wgrad(sv["merged"], dx2, "wgrad_sq_f32", tk=1024, tn=1024)
    g_w_a_out = _wgrad(sv["za"], dya, "wgrad_sq", tk=1024, tn=1024)
    g_w_attn_out = _wgrad(sv["att"], dyt, "wgrad_sq", tk=1024, tn=1024)
    g_w_pool = _wgrad_pool(sv["p"], dyps)
    dproj, g_conv = _conv_bwd(dza, sv["proj"], W["conv_w"], dproj)
    dproj = _pool_bwd(dp, dproj)
    dproj, dkp, dvp, dbias, dsink = _attn_bwd(sv["proj"], datt, bias_tabs, sink_rows, dbias, dproj)
    dproj = _kv_finish(dkp, dvp, dproj)
    g_w_in = _wgrad(sv["h"], dproj, "wgrad_in", tk=1024, tn=2176)
    token = grads_to(l, "mix", dict(w_in=g_w_in, conv_w=g_conv, w_a_out=g_w_a_out, w_pool=g_w_pool,
                                    w_attn_out=g_w_attn_out, w_o=g_w_o), dproj)
    dx, dg_mix = _dgrad_norm_bwd(dproj, W["w_in"], sv["x"], g_mix, dx2, "dgrad_in", tk=2176, token=token)
    grads_to(l, "done", None, dx)
    return dx, dict(pool_scale=dps, g_mix=dg_mix, g_ffn=dg_ffn, attn_sink=dsink), dbias, token


def _local_step(x, tgt, weights_of, grads_to, pool_scale, attn_sink, g_mix, g_ffn, rel_bias, g_final):
    onehot_np, masks_np = _bucket_constants()
    onehot, masks = jnp.asarray(onehot_np), jnp.asarray(masks_np)
    bias_tabs = _bias_expand(rel_bias.T, onehot, masks).reshape((3,) + TAB)
    saved = []
    for l in range(DEPTH):
        sink_rows = jnp.repeat(attn_sink[l], BLOCK).reshape(N_KV_HEADS, GQA * BLOCK)
        x, sv, token = _layer_fwd(l, x, weights_of, pool_scale[l:l + 1], g_mix[l:l + 1], g_ffn[l:l + 1], bias_tabs,
                                  sink_rows)
        saved.append(sv)
    loss, dx, dg_final = _loss_bwd(x, g_final.reshape(1, D_MODEL), tgt)
    dbias = jnp.zeros(TAB, F32)
    small = [None] * DEPTH
    for l in reversed(range(DEPTH)):
        dx, small[l], dbias, token = _layer_bwd(
            l, dx, saved[l], grads_to, pool_scale[l:l + 1], g_mix[l:l + 1], g_ffn[l:l + 1], bias_tabs, dbias, token)
    dsink_rows = jnp.concatenate([small[l]["attn_sink"].reshape(N_HEADS, BLOCK) for l in range(DEPTH)], axis=0)
    d_rel_bias, d_sink = _bias_reduce(dbias.reshape(N_HEADS, TAB_FLAT), dsink_rows, onehot)
    cat = lambda k: jnp.concatenate([small[l][k] for l in range(DEPTH)], axis=0)
    smalls = dict(pool_scale=cat("pool_scale"), g_mix=cat("g_mix"), g_ffn=cat("g_ffn"),
                  attn_sink=d_sink.reshape(DEPTH, N_HEADS), rel_bias=d_rel_bias, g_final=dg_final)
    return loss[0, 0], dx, smalls


SHARD_AXIS = dict(w_in=(1, IN_TOTAL // N_CHIPS), conv_w=(1, D_MODEL // N_CHIPS), w_a_out=(0, D_MODEL // N_CHIPS),
                  w_pool=(1, POOL_CG // N_CHIPS), w_attn_out=(0, D_MODEL // N_CHIPS), w_o=(0, D_MODEL // N_CHIPS),
                  w_gu=(1, 2 * D_FF // N_CHIPS), w_down=(0, D_FF // N_CHIPS))
HBM = pl.BlockSpec(memory_space=pltpu.HBM)
SEM = pl.BlockSpec(memory_space=pltpu.SEMAPHORE)
DATAFLOW = pltpu.SideEffectType.DATAFLOW_SIDE_EFFECTING
TOKEN = jax.ShapeDtypeStruct((8, 128), F32)


def _shard_of(ref, name, chip):
    axis, n = SHARD_AXIS[name]
    idx = [slice(None)] * len(ref.shape)
    idx[axis] = pl.ds(chip * n, n)
    return ref.at[tuple(idx)]


def _with_shard_axis(name, shape, size):
    axis, _ = SHARD_AXIS[name]
    s = list(shape)
    s[axis] = size
    return tuple(s)


HALF_AXIS = dict(w_in=0, conv_w=1, w_a_out=0, w_pool=1, w_attn_out=0, w_o=0, w_gu=0, w_down=0)


def _half_of_shard(ref, name, core):
    axis = HALF_AXIS[name]
    n = ref.shape[axis] // 2
    idx = [slice(None)] * len(ref.shape)
    idx[axis] = pl.ds(core * n, n)
    return ref.at[tuple(idx)]


def _half_in_full(ref, name, chip, core):
    saxis, n = SHARD_AXIS[name]
    haxis = HALF_AXIS[name]
    idx = [slice(None)] * len(ref.shape)
    if haxis == saxis:
        idx[saxis] = pl.ds(chip * n + core * (n // 2), n // 2)
    else:
        h = ref.shape[haxis] // 2
        idx[saxis] = pl.ds(chip * n, n)
        idx[haxis] = pl.ds(core * h, h)
    return ref.at[tuple(idx)]


def _on_each_device(fn):
    me = 2 * lax.axis_index("x") + lax.axis_index("y")
    c = lax.axis_index("c")
    for chip in range(N_CHIPS):
        for core in range(2):
            pl.when((me == chip) & (c == core))(functools.partial(fn, chip, core))


def _chip_peers(x, y):
    return [(1 - x, y), (x, 1 - y), (1 - x, 1 - y)]


RELATION_XOR = (2, 1, 3)


def _group_copies(kind, group, srcs, lands, send_sems, recv_sems, local_sems, chip, core):
    x, y, c = lax.axis_index("x"), lax.axis_index("y"), lax.axis_index("c")
    copies = []
    for t, name in enumerate(GROUPS[group]):
        for j, (px, py) in enumerate(_chip_peers(x, y)):
            if kind == "gather":
                src, dst = _half_of_shard(srcs[t], name, core), _half_in_full(lands[t], name, chip, core)
            else:
                src, dst = _shard_of(srcs[t], name, chip ^ RELATION_XOR[j]), lands[t].at[j]
            copies.append(pltpu.make_async_remote_copy(
                src_ref=src, dst_ref=dst, send_sem=send_sems.at[3 * t + j], recv_sem=recv_sems.at[3 * t + j],
                device_id=(px, py, c), device_id_type=MESH))
        if kind == "gather":
            src, dst = srcs[t], _shard_of(lands[t], name, chip)
        else:
            src, dst = _shard_of(srcs[t], name, chip), lands[t].at[N_CHIPS - 1]
        copies.append(pltpu.make_async_copy(src, dst, local_sems.at[t]))
    return copies


def _exchange_start(kind, group, srcs, land_shapes, after):
    nw = len(GROUPS[group])

    def body(*refs):
        srcs_r, lands_r = refs[:nw], refs[nw:2 * nw]
        send_sems, recv_sems, local_sems = refs[2 * nw + 1:2 * nw + 4]
        token = refs[-1]

        def issue(chip, core):
            for cp in _group_copies(kind, group, srcs_r, lands_r, send_sems, recv_sems, local_sems, chip, core):
                cp.start()
        _on_each_device(issue)
        token[...] = jnp.zeros_like(token)

    lands = [pltpu.with_memory_space_constraint(lax.empty(s.shape, s.dtype), pltpu.HBM) for s in land_shapes]
    srcs = [pltpu.with_memory_space_constraint(a, pltpu.HBM) for a in srcs]
    thru = [pltpu.HBM(a.shape, a.dtype) for a in srcs + lands]
    outs = pl.pallas_call(
        body, name=f"{kind}_{group}_start",
        in_specs=[HBM] * (2 * nw) + [ANY],
        out_specs=[SEM, SEM, SEM] + [HBM] * (2 * nw) + [pl.BlockSpec(memory_space=pltpu.VMEM)],
        out_shape=[pltpu.SemaphoreType.DMA((3 * nw,)), pltpu.SemaphoreType.DMA((3 * nw,)),
                   pltpu.SemaphoreType.DMA((nw,))] + thru + [TOKEN],
        input_output_aliases={t: 3 + t for t in range(2 * nw)},
        compiler_params=pltpu.CompilerParams(has_side_effects=DATAFLOW),
    )(*srcs, *lands, after)
    return dict(sems=outs[0:3], srcs=outs[3:3 + nw], lands=outs[3 + nw:3 + 2 * nw], token=outs[-1])


def _exchange_wait(kind, group, started, after):
    nw = len(GROUPS[group])

    def body(*refs):
        srcs_r, lands_r = refs[:nw], refs[nw:2 * nw]
        send_sems, recv_sems, local_sems = refs[2 * nw:2 * nw + 3]
        for cp in _group_copies(kind, group, srcs_r, lands_r, send_sems, recv_sems, local_sems, 0, 0):
            cp.wait()

    srcs, lands = list(started["srcs"]), list(started["lands"])
    outs = pl.pallas_call(
        body, name=f"{kind}_{group}_wait",
        in_specs=[HBM] * (2 * nw) + [SEM, SEM, SEM, ANY],
        out_specs=[HBM] * (2 * nw),
        out_shape=[pltpu.HBM(a.shape, a.dtype) for a in srcs + lands],
        input_output_aliases={t: t for t in range(2 * nw)},
        compiler_params=pltpu.CompilerParams(has_side_effects=DATAFLOW),
    )(*srcs, *lands, *started["sems"], after)
    return dict(zip(GROUPS[group], outs[nw:]))


def _gather_start(group, shards, after):
    names = GROUPS[group]
    shapes = [jax.ShapeDtypeStruct(_with_shard_axis(n, shards[n].shape, SHARD_AXIS[n][1] * N_CHIPS), shards[n].dtype)
              for n in names]
    return _exchange_start("gather", group, [shards[n] for n in names], shapes, after)


def _scatter_start(group, grads, after):
    names = GROUPS[group]
    shapes = [jax.ShapeDtypeStruct((N_CHIPS,) + _with_shard_axis(n, grads[n].shape, SHARD_AXIS[n][1]), grads[n].dtype)
              for n in names]
    return _exchange_start("scatter", group, [grads[n] for n in names], shapes, after)


def _sibling_exchange(parts):
    n = len(parts)

    def body(*refs):
        ins, outs = refs[:n], refs[n:2 * n]
        send_sems, recv_sems = refs[2 * n:]
        sibling = (lax.axis_index("x"), lax.axis_index("y"), 1 - lax.axis_index("c"))
        copies = [pltpu.make_async_remote_copy(src_ref=ins[t], dst_ref=outs[t], send_sem=send_sems.at[t],
                                               recv_sem=recv_sems.at[t], device_id=sibling, device_id_type=MESH)
                  for t in range(n)]
        for cp in copies:
            cp.start()
        for cp in copies:
            cp.wait()

    outs = pl.pallas_call(
        body, name="sibling_exchange", in_specs=[ANY] * n, out_specs=[ANY] * n,
        out_shape=[jax.ShapeDtypeStruct(p.shape, p.dtype) for p in parts],
        scratch_shapes=[pltpu.SemaphoreType.DMA((n,)), pltpu.SemaphoreType.DMA((n,))],
        compiler_params=pltpu.CompilerParams(has_side_effects=True),
    )(*parts)
    return list(outs)


def _fill_copies(group, lands, send_sems, recv_sems, chip, core):
    sibling = (lax.axis_index("x"), lax.axis_index("y"), 1 - lax.axis_index("c"))
    copies = []
    for t, name in enumerate(GROUPS[group]):
        for j in range(3):
            region = _half_in_full(lands[t], name, chip ^ RELATION_XOR[j], core)
            copies.append(pltpu.make_async_remote_copy(
                src_ref=region, dst_ref=region, send_sem=send_sems.at[3 * t + j], recv_sem=recv_sems.at[3 * t + j],
                device_id=sibling, device_id_type=MESH))
    return copies


def _gather_relay(group, started, after):
    nw = len(GROUPS[group])

    def body(*refs):
        srcs_r, lands_r = refs[:nw], refs[nw:2 * nw]
        send_sems, recv_sems, local_sems = refs[2 * nw:2 * nw + 3]
        fill_send, fill_recv, token = refs[-3:]
        for cp in _group_copies("gather", group, srcs_r, lands_r, send_sems, recv_sems, local_sems, 0, 0):
            cp.wait()

        def forward(chip, core):
            for cp in _fill_copies(group, lands_r, fill_send, fill_recv, chip, core):
                cp.start()
        _on_each_device(forward)
        token[...] = jnp.zeros_like(token)

    srcs, lands = list(started["srcs"]), list(started["lands"])
    outs = pl.pallas_call(
        body, name=f"gather_{group}_relay",
        in_specs=[HBM] * (2 * nw) + [SEM, SEM, SEM, ANY],
        out_specs=[HBM] * (2 * nw) + [SEM, SEM, pl.BlockSpec(memory_space=pltpu.VMEM)],
        out_shape=[pltpu.HBM(a.shape, a.dtype) for a in srcs + lands]
        + [pltpu.SemaphoreType.DMA((3 * nw,)), pltpu.SemaphoreType.DMA((3 * nw,)), TOKEN],
        input_output_aliases={t: t for t in range(2 * nw)},
        compiler_params=pltpu.CompilerParams(has_side_effects=DATAFLOW),
    )(*srcs, *lands, *started["sems"], after)
    return dict(lands=outs[nw:2 * nw], sems=outs[2 * nw:2 * nw + 2], token=outs[-1])


def _fill_wait(group, relayed, after):
    names = GROUPS[group]
    nw = len(names)

    def body(*refs):
        lands_r = refs[:nw]
        fill_send, fill_recv = refs[nw:nw + 2]
        for cp in _fill_copies(group, lands_r, fill_send, fill_recv, 0, 0):
            cp.wait()

    lands = list(relayed["lands"])
    outs = pl.pallas_call(
        body, name=f"gather_{group}_filled",
        in_specs=[HBM] * nw + [SEM, SEM, ANY], out_specs=[HBM] * nw,
        out_shape=[pltpu.HBM(a.shape, a.dtype) for a in lands],
        input_output_aliases={t: t for t in range(nw)},
        compiler_params=pltpu.CompilerParams(has_side_effects=DATAFLOW),
    )(*lands, *relayed["sems"], after)
    return dict(zip(names, outs))


N_DEV = 8


def _all_reduce_small(v, after):
    R, C = v.shape

    def body(v_ref, after_ref, o_ref, slots, send_sems, recv_sems):
        x, y, c = lax.axis_index("x"), lax.axis_index("y"), lax.axis_index("c")
        me = 4 * x + 2 * y + c
        slots[me] = v_ref[...]
        copies = []
        for k in range(1, N_DEV):
            peer = me ^ k
            cp = pltpu.make_async_remote_copy(
                src_ref=v_ref, dst_ref=slots.at[me], send_sem=send_sems.at[k - 1], recv_sem=recv_sems.at[k - 1],
                device_id=(peer // 4, (peer // 2) % 2, peer % 2), device_id_type=MESH)
            cp.start()
            copies.append(cp)
        for cp in copies:
            cp.wait()
        acc = slots[0]
        for k in range(1, N_DEV):
            acc = acc + slots[k]
        o_ref[...] = acc

    return pl.pallas_call(
        body, name="all_reduce_small", out_shape=jax.ShapeDtypeStruct((R, C), F32),
        in_specs=[pl.BlockSpec(memory_space=pltpu.VMEM), ANY], out_specs=pl.BlockSpec(memory_space=pltpu.VMEM),
        scratch_shapes=[pltpu.VMEM((N_DEV, R, C), F32), pltpu.SemaphoreType.DMA((N_DEV - 1,)),
                        pltpu.SemaphoreType.DMA((N_DEV - 1,))],
        compiler_params=pltpu.CompilerParams(has_side_effects=True),
    )(v, after)


def _as2d(shape):
    return (int(np.prod(shape[:-1])), shape[-1])


def _row_block(rows, cols, n_arrays):
    budget = V7X_VMEM_LIMIT // 2
    tr = rows
    while tr % 16 == 0 and 2 * n_arrays * tr * cols * 4 > budget:
        tr //= 2
    return tr


def _sum_slots(slots):
    _, R, C = slots.shape
    tr = _row_block(R, C, 5)

    def body(s_ref, o_ref):
        acc = s_ref[0].astype(F32)
        for k in range(1, N_CHIPS):
            acc = acc + s_ref[k].astype(F32)
        o_ref[...] = acc.astype(BF16)

    return pl.pallas_call(
        body, name="sum_slots", grid=(R // tr,),
        in_specs=[pl.BlockSpec((N_CHIPS, tr, C), lambda i: (0, i, 0))],
        out_specs=pl.BlockSpec((tr, C), lambda i: (i, 0)),
        out_shape=jax.ShapeDtypeStruct((R, C), BF16),
        compiler_params=_params("parallel"),
    )(slots)


def _adamw(l, w, m, v, g_a, g_b, prev):
    L, R, C = w.shape
    tr = _row_block(R, C, 9)
    c1 = 1.0 - ADAM_B1 ** ADAM_STEP
    c2 = 1.0 - ADAM_B2 ** ADAM_STEP

    def body(w_ref, m_ref, v_ref, a_ref, b_ref, *rest):
        g_ref, d_ref, nm_ref, nv_ref = rest[-4:]
        g = a_ref[...].astype(F32) + b_ref[...].astype(F32)
        nm = ADAM_B1 * m_ref[...] + (1.0 - ADAM_B1) * g
        nv = ADAM_B2 * v_ref[...] + (1.0 - ADAM_B2) * (g * g)
        g_ref[...] = g
        nm_ref[...] = nm
        nv_ref[...] = nv
        d_ref[...] = -ADAM_LR * ((nm / c1) / (jnp.sqrt(nv / c2) + ADAM_EPS) + ADAM_WD * w_ref[...])

    layer = pl.BlockSpec((None, tr, C), lambda i: (l, i, 0))
    blk = pl.BlockSpec((tr, C), lambda i: (i, 0))
    out = jax.ShapeDtypeStruct((L, R, C), F32)
    prev = [] if prev is None else list(prev)
    return pl.pallas_call(
        body, name="adamw", grid=(R // tr,), in_specs=[layer] * 3 + [blk] * 2 + [ANY] * len(prev),
        out_specs=[layer] * 4, out_shape=[out] * 4,
        input_output_aliases={5 + k: k for k in range(len(prev))},
        compiler_params=_params("parallel"),
    )(w, m, v, g_a, g_b, *prev)


SMALL_ROWS = 16


def _pack_small(pool_scale, g_mix, g_ffn, g_final, attn_sink, rel_bias):
    tail = jnp.concatenate([attn_sink.reshape(-1), rel_bias.reshape(-1)])
    tail = jnp.pad(tail, (0, D_MODEL - tail.shape[0])).reshape(1, D_MODEL)
    rows = jnp.concatenate([pool_scale, g_mix, g_ffn, g_final.reshape(1, D_MODEL), tail], axis=0)
    return jnp.pad(rows, ((0, SMALL_ROWS - rows.shape[0]), (0, 0)))


def _unpack_small(packed):
    n_sink = DEPTH * N_HEADS
    return dict(pool_scale=packed[0:4], g_mix=packed[4:8], g_ffn=packed[8:12], g_final=packed[12],
                attn_sink=packed[13, 0:n_sink].reshape(DEPTH, N_HEADS),
                rel_bias=packed[13, n_sink:n_sink + N_BUCKETS * N_HEADS].reshape(N_BUCKETS, N_HEADS))


def _group_shards(l, group, masters):
    out = {}
    for n in GROUPS[group]:
        w = masters[n][l]
        out[n] = jnp.pad(w.reshape(3, -1), ((0, 5), (0, 0))) if n == "conv_w" else w.astype(BF16)
    return out


def kernel(x, w_in, conv_w, w_a_out, w_pool, pool_scale, w_attn_out, attn_sink, w_o, g_mix, g_ffn, w_gu, w_down, rel_bias, g_final, loss_target, m_w_in, m_conv_w, m_w_a_out, m_w_pool, m_pool_scale, m_w_attn_out, m_attn_sink, m_w_o, m_g_mix, m_g_ffn, m_w_gu, m_w_down, m_rel_bias, m_g_final, v_w_in, v_conv_w, v_w_a_out, v_w_pool, v_pool_scale, v_w_attn_out, v_attn_sink, v_w_o, v_g_mix, v_g_ffn, v_w_gu, v_w_down, v_rel_bias, v_g_final):
    big = dict(w_in=(w_in, m_w_in, v_w_in), conv_w=(conv_w, m_conv_w, v_conv_w), w_a_out=(w_a_out, m_w_a_out, v_w_a_out),
               w_pool=(w_pool, m_w_pool, v_w_pool), w_attn_out=(w_attn_out, m_w_attn_out, v_w_attn_out),
               w_o=(w_o, m_w_o, v_w_o), w_gu=(w_gu, m_w_gu, v_w_gu), w_down=(w_down, m_w_down, v_w_down))

    big3 = {n: tuple(a.reshape((DEPTH,) + _as2d(a.shape[1:])) for a in big[n]) for n in WEIGHT_NAMES}
    masters = {n: big[n][0] for n in WEIGHT_NAMES}

    gathers = {(0, "mix"): _gather_start("mix", _group_shards(0, "mix", masters), rel_bias)}
    newest = {"token": gathers[0, "mix"]["token"]}
    masters = dict(zip(WEIGHT_NAMES, lax.optimization_barrier(
        (tuple(masters[n] for n in WEIGHT_NAMES), newest["token"]))[0]))

    relays = {}

    def weights_of(l, group, a):
        if group == "ffn on its way":
            relays[l, "ffn"] = _gather_relay("ffn", gathers.pop((l, "ffn")), a)
            return None, relays[l, "ffn"]["token"]
        if (l, group) not in relays:
            relays[l, group] = _gather_relay(group, gathers.pop((l, group)), a)
        W = _fill_wait(group, relays.pop((l, group)), a)
        if group == "mix":
            gathers[l, "ffn"] = _gather_start("ffn", _group_shards(l, "ffn", masters), W["w_in"])
            newest["token"] = gathers[l, "ffn"]["token"]
            if l + 1 < DEPTH:
                gathers[l + 1, "mix"] = _gather_start("mix", _group_shards(l + 1, "mix", masters), newest["token"])
                newest["token"] = gathers[l + 1, "mix"]["token"]
        elif l + 1 < DEPTH:
            relays[l + 1, "mix"] = _gather_relay("mix", gathers.pop((l + 1, "mix")), W["w_gu"])
            newest["token"] = relays[l + 1, "mix"]["token"]
        return W, newest["token"]

    results = {n: None for n in WEIGHT_NAMES}
    scatters = {}

    def finish(l, after):
        slots = {}
        for group in GROUPS:
            slots.update(_exchange_wait("scatter", group, scatters.pop((l, group)), after))
        parts = [_sum_slots(slots[n].reshape((N_CHIPS,) + _as2d(slots[n].shape[1:]))) for n in WEIGHT_NAMES]
        others = _sibling_exchange(parts)
        for n, mine, other in zip(WEIGHT_NAMES, parts, others):
            if n == "conv_w":
                mine, other = mine[0:3], other[0:3]
            results[n] = _adamw(l, *big3[n], mine, other, results[n])

    def grads_to(l, group, wgrads, a):
        if group == "done":
            if l + 1 < DEPTH:
                finish(l + 1, a)
            return None
        scatters[l, group] = _scatter_start(group, wgrads, a)
        return scatters[l, group]["token"]

    loss, grad_x, smalls = _local_step(x[0], loss_target[0], weights_of, grads_to, pool_scale, attn_sink, g_mix, g_ffn,
                                       rel_bias, g_final)
    finish(0, results["w_down"][0])
    stacked = {n: [o.reshape(big[n][0].shape) for o in results[n]] for n in WEIGHT_NAMES}

    g_small = _all_reduce_small(_pack_small(smalls["pool_scale"], smalls["g_mix"], smalls["g_ffn"], smalls["g_final"],
                                            smalls["attn_sink"], smalls["rel_bias"]), results["w_in"][0])
    w_small = _pack_small(pool_scale, g_mix, g_ffn, g_final, attn_sink, rel_bias)
    m_small = _pack_small(m_pool_scale, m_g_mix, m_g_ffn, m_g_final, m_attn_sink, m_rel_bias)
    v_small = _pack_small(v_pool_scale, v_g_mix, v_g_ffn, v_g_final, v_attn_sink, v_rel_bias)
    small_out = [_unpack_small(o[0]) for o in
                 _adamw(0, w_small[None], m_small[None], v_small[None], g_small, jnp.zeros_like(g_small), None)]

    total_loss = lax.psum(loss, ("x", "y", "c"))

    order = ("w_in", "conv_w", "w_a_out", "w_pool", "pool_scale", "w_attn_out", "attn_sink", "w_o", "g_mix", "g_ffn",
             "w_gu", "w_down", "rel_bias", "g_final")
    outs = [total_loss, grad_x[None]]
    for k in range(4):
        for n in order:
            outs.append(stacked[n][k] if n in stacked else small_out[k][n])
    return tuple(outs)
```

```python
import functools
import math

import numpy as np
import jax
import jax.numpy as jnp
from jax import lax
from jax.experimental import pallas as pl
from jax.experimental.pallas import tpu as pltpu

F32 = jnp.float32
BF16 = jnp.bfloat16

D_MODEL = 1024
DEPTH = 4
N_HEADS = 16
N_KV_HEADS = 4
HEAD_DIM = 64
GQA = N_HEADS // N_KV_HEADS
WINDOW = 128
BLOCK = 128
N_BUCKETS = 32
MAX_DISTANCE = 128
POOL_GROUPS = 4
POOL_CG = D_MODEL // POOL_GROUPS
POOL_WINDOWS = (2, 4, 8, 16)
D_FF = 2816
IN_TOTAL = 8704
OFF_B, OFF_C, OFF_X, OFF_U, OFF_Q, OFF_K, OFF_V, OFF_GA, OFF_GP, OFF_GT = (
    0, 1024, 2048, 3072, 4096, 5120, 5376, 5632, 6656, 7680)
EPS = 1e-6
NEG_INF = -1e30
SM_SCALE = HEAD_DIM ** -0.5

ADAM_LR = 0.001
ADAM_B1 = 0.9
ADAM_B2 = 0.999
ADAM_EPS = 1e-08
ADAM_WD = 0.01
ADAM_STEP = 10

N_CHIPS = 4
HALO = 8
V7X_VMEM_LIMIT = 56 * 1024 * 1024
MESH = pl.DeviceIdType.MESH
ANY = pl.BlockSpec(memory_space=pl.ANY)


def _params(*sem):
    return pltpu.CompilerParams(dimension_semantics=tuple(sem) if sem else None,
                                vmem_limit_bytes=V7X_VMEM_LIMIT)


def _tile(n, pref):
    t = min(pref, n)
    while n % t or t % 128:
        t -= 128
    return t


def _nt(a, b):
    return lax.dot_general(a, b, (((1,), (1,)), ((), ())), preferred_element_type=F32)


def _tn(a, b):
    return lax.dot_general(a, b, (((0,), (0,)), ((), ())), preferred_element_type=F32)


def _nn(a, b):
    return jnp.dot(a, b, preferred_element_type=F32)


def _sigmoid(v):
    return 1.0 / (1.0 + jnp.exp(-v))


def _norm_matmul(x, g, w, name, token):
    S, Dm = x.shape
    N = w.shape[1]
    tm, tn = _tile(S, 1024), _tile(N, N // 4)

    def body(x_ref, g_ref, w_ref, token_ref, h_ref, o_ref):
        @pl.when(pl.program_id(1) == 0)
        def _():
            xv = x_ref[...]
            r = lax.rsqrt(jnp.mean(xv * xv, axis=-1, keepdims=True) + EPS)
            h_ref[...] = (xv * r * g_ref[...]).astype(BF16)
        o_ref[...] = _nn(h_ref[...], w_ref[...]).astype(BF16)

    return pl.pallas_call(
        body, name=name, grid=(S // tm, N // tn),
        in_specs=[pl.BlockSpec((tm, Dm), lambda i, j: (i, 0)),
                  pl.BlockSpec((1, Dm), lambda i, j: (0, 0)),
                  pl.BlockSpec((Dm, tn), lambda i, j: (0, j)), ANY],
        out_specs=[pl.BlockSpec((tm, Dm), lambda i, j: (i, 0)),
                   pl.BlockSpec((tm, tn), lambda i, j: (i, j))],
        out_shape=[jax.ShapeDtypeStruct((S, Dm), BF16), jax.ShapeDtypeStruct((S, N), BF16)],
        compiler_params=_params("parallel", "arbitrary"),
    )(x, g, w, token)


CB = 128
CBW = 128


def _fill_padded(pad_ref, v, S):
    z = jnp.zeros((HALO, v.shape[1]), F32)
    pad_ref[pl.ds(0, HALO), :] = z
    pad_ref[pl.ds(S + HALO, HALO), :] = z
    pad_ref[pl.ds(HALO, S), :] = v


def _shifted(pad_ref, off, S):
    return pad_ref[pl.ds(HALO + off, S), :]


def _conv_fwd(proj, cw8):
    S = proj.shape[0]
    nblk = D_MODEL // CBW

    def body(b_ref, c_ref, x_ref, w_ref, o_ref, pad):
        u = c_ref[...].astype(F32) * x_ref[...].astype(F32)
        _fill_padded(pad, u, S)
        cv = w_ref[0:1, :] * _shifted(pad, -1, S) + w_ref[1:2, :] * u + w_ref[2:3, :] * _shifted(pad, 1, S)
        o_ref[...] = (b_ref[...].astype(F32) * cv).astype(BF16)

    col = lambda base: pl.BlockSpec((S, CBW), lambda j: (0, base // CBW + j))
    return pl.pallas_call(
        body, name="conv_fwd", grid=(nblk,),
        in_specs=[col(OFF_B), col(OFF_C), col(OFF_X), pl.BlockSpec((8, CBW), lambda j: (0, j))],
        out_specs=pl.BlockSpec((S, CBW), lambda j: (0, j)),
        out_shape=jax.ShapeDtypeStruct((S, D_MODEL), BF16),
        scratch_shapes=[pltpu.VMEM((S + 2 * HALO, CBW), F32)],
        compiler_params=_params("parallel"),
    )(proj, proj, proj, cw8)


def _pool_count(S, lo, hi):
    t = lax.broadcasted_iota(jnp.int32, (S, CBW), 0)
    return (jnp.minimum(t + hi, S - 1) - jnp.maximum(t - lo, 0) + 1).astype(F32)


def _pool_fwd(proj):
    S = proj.shape[0]
    nblk = D_MODEL // CBW
    per_group = POOL_CG // CBW

    def body(u_ref, o_ref, pad):
        u = u_ref[...].astype(F32)
        _fill_padded(pad, u, S)
        grp = pl.program_id(0) // per_group
        for gi, w in enumerate(POOL_WINDOWS):
            @pl.when(grp == gi)
            def _(w=w):
                lo, hi = w // 2, w - 1 - w // 2
                acc = _shifted(pad, -lo, S)
                for off in range(-lo + 1, hi + 1):
                    acc = acc + _shifted(pad, off, S)
                o_ref[...] = (acc / _pool_count(S, lo, hi) - u).astype(BF16)

    return pl.pallas_call(
        body, name="pool_fwd", grid=(nblk,),
        in_specs=[pl.BlockSpec((S, CBW), lambda j: (0, OFF_U // CBW + j))],
        out_specs=pl.BlockSpec((S, CBW), lambda j: (0, j)),
        out_shape=jax.ShapeDtypeStruct((S, D_MODEL), BF16),
        scratch_shapes=[pltpu.VMEM((S + 2 * HALO, CBW), F32)],
        compiler_params=_params("parallel"),
    )(proj)


def _attn_specs(S):
    nb = S // BLOCK
    kcol, vcol = OFF_K // (N_KV_HEADS * HEAD_DIM), OFF_V // (N_KV_HEADS * HEAD_DIM)
    kvw = N_KV_HEADS * HEAD_DIM
    prev = lambda i: jnp.maximum(i - 1, 0)
    nxt = lambda i: jnp.minimum(i + 1, nb - 1)
    return [
        pl.BlockSpec((BLOCK, D_MODEL), lambda i: (i, OFF_Q // D_MODEL)),
        pl.BlockSpec((BLOCK, kvw), lambda i: (prev(i), kcol)),
        pl.BlockSpec((BLOCK, kvw), lambda i: (i, kcol)),
        pl.BlockSpec((BLOCK, kvw), lambda i: (nxt(i), kcol)),
        pl.BlockSpec((BLOCK, kvw), lambda i: (prev(i), vcol)),
        pl.BlockSpec((BLOCK, kvw), lambda i: (i, vcol)),
        pl.BlockSpec((BLOCK, kvw), lambda i: (nxt(i), vcol)),
    ]


def _heads_rows(ref_or_val, hk):
    return jnp.concatenate(
        [ref_or_val[:, (GQA * hk + g) * HEAD_DIM:(GQA * hk + g + 1) * HEAD_DIM] for g in range(GQA)], axis=0)


def _kv_rows(p_ref, c_ref, n_ref, hk):
    sl = slice(hk * HEAD_DIM, (hk + 1) * HEAD_DIM)
    return jnp.concatenate([p_ref[:, sl], c_ref[:, sl], n_ref[:, sl]], axis=0)


def _bias_cols(bias_ref, hk):
    return jnp.concatenate([bias_ref[GQA * hk + g] for g in range(GQA)], axis=1)


def _softmax_keys_on_rows(q4s, kc, bias_blk, sink_row):
    s = _nt(kc, q4s) + bias_blk
    m = jnp.maximum(jnp.max(s, axis=0, keepdims=True), sink_row)
    p = jnp.exp(s - m)
    e_sink = jnp.exp(sink_row - m)
    inv = 1.0 / (jnp.sum(p, axis=0, keepdims=True) + e_sink)
    return p * inv, e_sink * inv


TAB = (N_HEADS, 3 * BLOCK, BLOCK)
TAB_FLAT = 3 * BLOCK * BLOCK


def _bias_spec(nb):
    return pl.BlockSpec((None,) + TAB, lambda i: (jnp.where(i == 0, 0, jnp.where(i == nb - 1, 2, 1)), 0, 0, 0))


def _attn_fwd(proj, bias_tabs, sink_rows):
    S = proj.shape[0]
    nb = S // BLOCK
    assert nb >= 2

    def body(q_ref, kp, kc_, kn, vp, vc_, vn, bias_ref, sink_ref, o_ref):
        outs = []
        for hk in range(N_KV_HEADS):
            q4s = _heads_rows(q_ref, hk) * SM_SCALE
            kc = _kv_rows(kp, kc_, kn, hk)
            vc = _kv_rows(vp, vc_, vn, hk)
            pn, _ = _softmax_keys_on_rows(q4s, kc, _bias_cols(bias_ref, hk), sink_ref[hk:hk + 1, :])
            o4 = _tn(pn.astype(BF16), vc)
            outs += [o4[g * BLOCK:(g + 1) * BLOCK, :] for g in range(GQA)]
        o_ref[...] = jnp.concatenate(outs, axis=1).astype(BF16)

    return pl.pallas_call(
        body, name="attn_fwd", grid=(nb,),
        in_specs=_attn_specs(S) + [_bias_spec(nb), pl.BlockSpec((N_KV_HEADS, GQA * BLOCK), lambda i: (0, 0))],
        out_specs=pl.BlockSpec((BLOCK, D_MODEL), lambda i: (i, 0)),
        out_shape=jax.ShapeDtypeStruct((S, D_MODEL), BF16),
        compiler_params=_params("parallel"),
    )(*([proj] * 7), bias_tabs, sink_rows)


GATE_HALF = D_MODEL // 2


def _gate_specs(tm):
    return [pl.BlockSpec((tm, GATE_HALF), lambda i, c=off // GATE_HALF + k: (i, c))
            for off in (OFF_GA, OFF_GP, OFF_GT) for k in (0, 1)]


def _gate(lo_ref, hi_ref):
    return _sigmoid(jnp.concatenate([lo_ref[...], hi_ref[...]], axis=1).astype(F32))


def _pool_mix(p, wp):
    return jnp.concatenate(
        [_nn(p[:, g * POOL_CG:(g + 1) * POOL_CG], wp[g]) for g in range(POOL_GROUPS)], axis=1)


def _mix_fwd(za, p, att, proj, x, wa, wp, ps, wt, wo, token):
    S = x.shape[0]
    tm = _tile(S, 256)

    def body(za_ref, p_ref, att_ref, ga0, ga1, gp0, gp1, gt0, gt1, x_ref, wa_ref, wp_ref, ps_ref, wt_ref, wo_ref,
             token_ref, ya_ref, yp_ref, yt_ref, mg_ref, x2_ref):
        ya = _nn(za_ref[...], wa_ref[...])
        ypr = _pool_mix(p_ref[...], wp_ref)
        yt = _nn(att_ref[...], wt_ref[...])
        merged = _gate(ga0, ga1) * ya + _gate(gp0, gp1) * (ypr * ps_ref[...]) + _gate(gt0, gt1) * yt
        mb = merged.astype(BF16)
        ya_ref[...] = ya.astype(BF16)
        yp_ref[...] = ypr.astype(BF16)
        yt_ref[...] = yt.astype(BF16)
        mg_ref[...] = mb
        x2_ref[...] = x_ref[...] + _nn(mb, wo_ref[...])

    row = lambda c=0: pl.BlockSpec((tm, D_MODEL), lambda i: (i, c))
    whole = lambda a: pl.BlockSpec(a.shape, lambda i: (0,) * a.ndim)
    act = jax.ShapeDtypeStruct((S, D_MODEL), BF16)
    return pl.pallas_call(
        body, name="mix_fwd", grid=(S // tm,),
        in_specs=[row(), row(), row()] + _gate_specs(tm)
        + [row(), whole(wa), whole(wp), whole(ps), whole(wt), whole(wo), ANY],
        out_specs=[row(), row(), row(), row(), row()],
        out_shape=[act, act, act, act, jax.ShapeDtypeStruct((S, D_MODEL), F32)],
        compiler_params=_params("parallel"),
    )(za, p, att, *([proj] * 6), x, wa, wp, ps, wt, wo, token)


def _ffn_fwd(gu, x2, wd, token):
    S = x2.shape[0]
    tm = _tile(S, 256)

    def body(g_ref, u_ref, x_ref, w_ref, token_ref, a_ref, o_ref):
        g = g_ref[...].astype(F32)
        a = (g * _sigmoid(g) * u_ref[...].astype(F32)).astype(BF16)
        a_ref[...] = a
        o_ref[...] = x_ref[...] + _nn(a, w_ref[...])

    return pl.pallas_call(
        body, name="ffn_fwd", grid=(S // tm,),
        in_specs=[pl.BlockSpec((tm, D_FF), lambda i: (i, 0)), pl.BlockSpec((tm, D_FF), lambda i: (i, 1)),
                  pl.BlockSpec((tm, D_MODEL), lambda i: (i, 0)), pl.BlockSpec((D_FF, D_MODEL), lambda i: (0, 0)), ANY],
        out_specs=[pl.BlockSpec((tm, D_FF), lambda i: (i, 0)), pl.BlockSpec((tm, D_MODEL), lambda i: (i, 0))],
        out_shape=[jax.ShapeDtypeStruct((S, D_FF), BF16), jax.ShapeDtypeStruct((S, D_MODEL), F32)],
        compiler_params=_params("parallel"),
    )(gu, gu, x2, wd, token)


def _loss_bwd(x, g, tgt):
    S, Dm = x.shape
    tm = _tile(S, 512)

    def body(x_ref, g_ref, t_ref, l_ref, dx_ref, dg_ref):
        @pl.when(pl.program_id(0) == 0)
        def _():
            l_ref[...] = jnp.zeros_like(l_ref)
            dg_ref[...] = jnp.zeros_like(dg_ref)
        xv, gv = x_ref[...], g_ref[...]
        r = lax.rsqrt(jnp.mean(xv * xv, axis=-1, keepdims=True) + EPS)
        n = xv * r
        err = n * gv - t_ref[...]
        l_ref[...] += 0.5 * jnp.sum(jnp.mean(err * err, axis=-1, keepdims=True), axis=0, keepdims=True)
        dy = err * (1.0 / Dm)
        dn = dy * gv
        dx_ref[...] = r * (dn - n * jnp.mean(dn * n, axis=-1, keepdims=True))
        dg_ref[...] += jnp.sum(dy * n, axis=0, keepdims=True)

    return pl.pallas_call(
        body, name="loss_bwd", grid=(S // tm,),
        in_specs=[pl.BlockSpec((tm, Dm), lambda i: (i, 0)), pl.BlockSpec((1, Dm), lambda i: (0, 0)),
                  pl.BlockSpec((tm, Dm), lambda i: (i, 0))],
        out_specs=[pl.BlockSpec((8, 128), lambda i: (0, 0)), pl.BlockSpec((tm, Dm), lambda i: (i, 0)),
                   pl.BlockSpec((1, Dm), lambda i: (0, 0))],
        out_shape=[jax.ShapeDtypeStruct((8, 128), F32), jax.ShapeDtypeStruct((S, Dm), F32),
                   jax.ShapeDtypeStruct((1, Dm), F32)],
        compiler_params=_params("arbitrary"),
    )(x, g, tgt)


def _ffn_bwd(dx3, gu, wd, token):
    S = dx3.shape[0]
    tm = _tile(S, 256)

    def body(d_ref, g_ref, u_ref, w_ref, token_ref, o_ref):
        dact = _nt(d_ref[...].astype(BF16), w_ref[...])
        g, u = g_ref[...].astype(F32), u_ref[...].astype(F32)
        sg = _sigmoid(g)
        o_ref[:, 0:D_FF] = (dact * u * (sg * (1.0 + g * (1.0 - sg)))).astype(BF16)
        o_ref[:, D_FF:2 * D_FF] = (dact * (g * sg)).astype(BF16)

    return pl.pallas_call(
        body, name="ffn_bwd", grid=(S // tm,),
        in_specs=[pl.BlockSpec((tm, D_MODEL), lambda i: (i, 0)),
                  pl.BlockSpec((tm, D_FF), lambda i: (i, 0)), pl.BlockSpec((tm, D_FF), lambda i: (i, 1)),
                  pl.BlockSpec((D_FF, D_MODEL), lambda i: (0, 0)), ANY],
        out_specs=pl.BlockSpec((tm, 2 * D_FF), lambda i: (i, 0)),
        out_shape=jax.ShapeDtypeStruct((S, 2 * D_FF), BF16),
        compiler_params=_params("parallel"),
    )(dx3, gu, gu, wd, token)


def _wgrad(a, b, name, tk=512, tn=512, out_dtype=BF16, token=None):
    S, K = a.shape
    N = b.shape[1]
    tk, tn, ts = _tile(K, tk), _tile(N, tn), _tile(S, 1024)
    n_s = S // ts
    extra = [] if token is None else [token]

    def body(a_ref, b_ref, *rest):
        o_ref, acc = rest[-2:]
        s = pl.program_id(2)

        @pl.when(s == 0)
        def _():
            acc[...] = jnp.zeros_like(acc)
        acc[...] += _tn(a_ref[...].astype(BF16), b_ref[...].astype(BF16))

        @pl.when(s == n_s - 1)
        def _():
            o_ref[...] = acc[...].astype(out_dtype)

    return pl.pallas_call(
        body, name=name, grid=(K // tk, N // tn, n_s),
        in_specs=[pl.BlockSpec((ts, tk), lambda k, n, s: (s, k)), pl.BlockSpec((ts, tn), lambda k, n, s: (s, n))]
        + [ANY] * len(extra),
        out_specs=pl.BlockSpec((tk, tn), lambda k, n, s: (k, n)),
        out_shape=jax.ShapeDtypeStruct((K, N), out_dtype),
        scratch_shapes=[pltpu.VMEM((tk, tn), F32)],
        compiler_params=_params("parallel", "parallel", "arbitrary"),
    )(a, b, *extra)


def _wgrad_pool(p, dyps):
    S = p.shape[0]
    ts = _tile(S, 4096)
    n_s = S // ts

    def body(a_ref, b_ref, o_ref, acc):
        s = pl.program_id(1)

        @pl.when(s == 0)
        def _():
            acc[...] = jnp.zeros_like(acc)
        acc[...] += _tn(a_ref[...], b_ref[...])

        @pl.when(s == n_s - 1)
        def _():
            o_ref[...] = acc[...].astype(BF16)

    return pl.pallas_call(
        body, name="wgrad_pool", grid=(POOL_GROUPS, n_s),
        in_specs=[pl.BlockSpec((ts, POOL_CG), lambda g, s: (s, g)), pl.BlockSpec((ts, POOL_CG), lambda g, s: (s, g))],
        out_specs=pl.BlockSpec((None, POOL_CG, POOL_CG), lambda g, s: (g, 0, 0)),
        out_shape=jax.ShapeDtypeStruct((POOL_GROUPS, POOL_CG, POOL_CG), BF16),
        scratch_shapes=[pltpu.VMEM((POOL_CG, POOL_CG), F32)],
        compiler_params=_params("parallel", "arbitrary"),
    )(p, dyps)


def _dgrad_norm_bwd(dy, w, x, g, dres, name, tk, token=None):
    S, K = dy.shape
    Dm = x.shape[1]
    tm, tk = _tile(S, 1024), _tile(K, tk)
    n_k = K // tk
    extra = [] if token is None else [token]

    def body(dy_ref, w_ref, x_ref, g_ref, r_ref, *rest):
        dx_ref, dg_ref, acc = rest[-3:]
        i, k = pl.program_id(0), pl.program_id(1)

        @pl.when((i == 0) & (k == 0))
        def _():
            dg_ref[...] = jnp.zeros_like(dg_ref)

        @pl.when(k == 0)
        def _():
            acc[...] = jnp.zeros_like(acc)
        acc[...] += _nt(dy_ref[...], w_ref[...])

        @pl.when(k == n_k - 1)
        def _():
            dh, xv = acc[...], x_ref[...]
            r = lax.rsqrt(jnp.mean(xv * xv, axis=-1, keepdims=True) + EPS)
            n = xv * r
            dn = dh * g_ref[...]
            dx_ref[...] = r_ref[...] + r * (dn - n * jnp.mean(dn * n, axis=-1, keepdims=True))
            dg_ref[...] += jnp.sum(dh * n, axis=0, keepdims=True)

    rowblk = pl.BlockSpec((tm, Dm), lambda i, k: (i, 0))
    vec = pl.BlockSpec((1, Dm), lambda i, k: (0, 0))
    return pl.pallas_call(
        body, name=name, grid=(S // tm, n_k),
        in_specs=[pl.BlockSpec((tm, tk), lambda i, k: (i, k)), pl.BlockSpec((Dm, tk), lambda i, k: (0, k)),
                  rowblk, vec, rowblk] + [ANY] * len(extra),
        out_specs=[rowblk, vec],
        out_shape=[jax.ShapeDtypeStruct((S, Dm), F32), jax.ShapeDtypeStruct((1, Dm), F32)],
        scratch_shapes=[pltpu.VMEM((tm, Dm), F32)],
        compiler_params=_params("arbitrary", "arbitrary"),
    )(dy, w, x, g, dres, *extra)


def _mix_bwd(dx2, ya, ypr, yt, proj, ps, wa, wp, wt, wo, token):
    S = dx2.shape[0]
    tm = _tile(S, 256)

    n_tiles = S // tm

    def body(dx_ref, ya_ref, yp_ref, yt_ref, ga0, ga1, gp0, gp1, gt0, gt1, ps_ref, wa_ref, wp_ref, wt_ref, wo_ref,
             token_ref, dya_ref, dyt_ref, dyps_ref, dza_ref, datt_ref, dp_ref, dproj_hbm, dps_ref, dgates, sem):
        i = pl.program_id(0)
        to_dproj = pltpu.make_async_copy(
            dgates, dproj_hbm.at[pl.ds(pl.multiple_of(i * tm, tm), tm), pl.ds(OFF_GA, 3 * D_MODEL)], sem)

        @pl.when(i == 0)
        def _():
            dps_ref[...] = jnp.zeros_like(dps_ref)
        dm = _nt(dx_ref[...].astype(BF16), wo_ref[...])
        sa, sp, st = _gate(ga0, ga1), _gate(gp0, gp1), _gate(gt0, gt1)
        psv = ps_ref[...]
        ypr_v = yp_ref[...].astype(F32)
        dya = (sa * dm).astype(BF16)
        dyt = (st * dm).astype(BF16)
        dyp = sp * dm
        dyps = (dyp * psv).astype(BF16)
        dya_ref[...] = dya
        dyt_ref[...] = dyt
        dyps_ref[...] = dyps
        dg = jnp.concatenate(
            [dm * ya_ref[...].astype(F32) * (sa * (1.0 - sa)), dm * (ypr_v * psv) * (sp * (1.0 - sp)),
             dm * yt_ref[...].astype(F32) * (st * (1.0 - st))], axis=1).astype(BF16)

        @pl.when(i > 0)
        def _():
            to_dproj.wait()
        dgates[...] = dg
        to_dproj.start()
        dps_ref[...] += jnp.sum(dyp * ypr_v, axis=0, keepdims=True)
        dza_ref[...] = _nt(dya, wa_ref[...]).astype(BF16)
        datt_ref[...] = _nt(dyt, wt_ref[...]).astype(BF16)
        dp_ref[...] = jnp.concatenate(
            [_nt(dyps[:, g * POOL_CG:(g + 1) * POOL_CG], wp_ref[g]) for g in range(POOL_GROUPS)], axis=1).astype(BF16)

        @pl.when(i == n_tiles - 1)
        def _():
            to_dproj.wait()

    row = lambda c=0: pl.BlockSpec((tm, D_MODEL), lambda i: (i, c))
    whole = lambda a: pl.BlockSpec(a.shape, lambda i: (0,) * a.ndim)
    act = jax.ShapeDtypeStruct((S, D_MODEL), BF16)
    return pl.pallas_call(
        body, name="mix_bwd", grid=(n_tiles,),
        in_specs=[row(), row(), row(), row()] + _gate_specs(tm)
        + [whole(ps), whole(wa), whole(wp), whole(wt), whole(wo), ANY],
        out_specs=[row()] * 6 + [ANY, pl.BlockSpec((1, D_MODEL), lambda i: (0, 0))],
        out_shape=[act] * 6 + [jax.ShapeDtypeStruct((S, IN_TOTAL), BF16), jax.ShapeDtypeStruct((1, D_MODEL), F32)],
        scratch_shapes=[pltpu.VMEM((tm, 3 * D_MODEL), BF16), pltpu.SemaphoreType.DMA],
        compiler_params=_params("arbitrary"),
    )(dx2, ya, ypr, yt, *([proj] * 6), ps, wa, wp, wt, wo, token)


def _conv_bwd(dza, proj, cw8, dproj):
    S = proj.shape[0]
    nblk = D_MODEL // CB

    def body(d_ref, b_ref, c_ref, x_ref, w_ref, dproj_in, dproj_hbm, dw_ref, pad_u, pad_d, parts, sems):
        cb = pl.program_id(0)
        to_dproj = [pltpu.make_async_copy(
            parts.at[k], dproj_hbm.at[:, pl.ds(pl.multiple_of(off + cb * CB, CB), CB)], sems.at[k])
            for k, off in enumerate((OFF_B, OFF_C, OFF_X))]
        c, xa = c_ref[...].astype(F32), x_ref[...].astype(F32)
        u = c * xa
        _fill_padded(pad_u, u, S)
        u_prev, u_next = _shifted(pad_u, -1, S), _shifted(pad_u, 1, S)
        cv = w_ref[0:1, :] * u_prev + w_ref[1:2, :] * u + w_ref[2:3, :] * u_next
        dza_v = d_ref[...].astype(F32)
        dcv = dza_v * b_ref[...].astype(F32)
        _fill_padded(pad_d, dcv, S)
        du = w_ref[0:1, :] * _shifted(pad_d, 1, S) + w_ref[1:2, :] * dcv + w_ref[2:3, :] * _shifted(pad_d, -1, S)

        @pl.when(cb > 0)
        def _():
            for cp in to_dproj:
                cp.wait()
        parts[0] = (dza_v * cv).astype(BF16)
        parts[1] = (du * xa).astype(BF16)
        parts[2] = (du * c).astype(BF16)
        for cp in to_dproj:
            cp.start()
        dw_ref[...] = jnp.concatenate(
            [jnp.sum(dcv * u_prev, axis=0, keepdims=True), jnp.sum(dcv * u, axis=0, keepdims=True),
             jnp.sum(dcv * u_next, axis=0, keepdims=True), jnp.zeros((5, CB), F32)], axis=0)

        @pl.when(cb == nblk - 1)
        def _():
            for cp in to_dproj:
                cp.wait()

    col = lambda base: pl.BlockSpec((S, CB), lambda cb: (0, base // CB + cb))
    taps = pl.BlockSpec((8, CB), lambda cb: (0, cb))
    return pl.pallas_call(
        body, name="conv_bwd", grid=(nblk,),
        in_specs=[col(0), col(OFF_B), col(OFF_C), col(OFF_X), taps, ANY],
        out_specs=[ANY, taps],
        out_shape=[jax.ShapeDtypeStruct(dproj.shape, dproj.dtype), jax.ShapeDtypeStruct((8, D_MODEL), F32)],
        scratch_shapes=[pltpu.VMEM((S + 2 * HALO, CB), F32), pltpu.VMEM((S + 2 * HALO, CB), F32),
                        pltpu.VMEM((3, S, CB), BF16), pltpu.SemaphoreType.DMA((3,))],
        input_output_aliases={5: 0},
        compiler_params=_params("arbitrary"),
    )(dza, proj, proj, proj, cw8, dproj)


def _pool_bwd(dp, dproj):
    S = dp.shape[0]
    nblk = D_MODEL // CBW
    per_group = POOL_CG // CBW

    def body(d_ref, dproj_in, o_ref, pad):
        d = d_ref[...].astype(F32)
        grp = pl.program_id(0) // per_group
        for gi, w in enumerate(POOL_WINDOWS):
            @pl.when(grp == gi)
            def _(w=w):
                lo, hi = w // 2, w - 1 - w // 2
                _fill_padded(pad, d / _pool_count(S, lo, hi), S)
                acc = _shifted(pad, -hi, S)
                for off in range(-hi + 1, lo + 1):
                    acc = acc + _shifted(pad, off, S)
                o_ref[...] = (acc - d).astype(BF16)

    return pl.pallas_call(
        body, name="pool_bwd", grid=(nblk,),
        in_specs=[pl.BlockSpec((S, CBW), lambda j: (0, j)), ANY],
        out_specs=pl.BlockSpec((S, CBW), lambda j: (0, OFF_U // CBW + j)),
        out_shape=jax.ShapeDtypeStruct(dproj.shape, dproj.dtype),
        scratch_shapes=[pltpu.VMEM((S + 2 * HALO, CBW), F32)],
        input_output_aliases={1: 0},
        compiler_params=_params("parallel"),
    )(dp, dproj)


def _attn_bwd(proj, datt, bias_tabs, sink_rows, dbias_in, dproj):
    S = proj.shape[0]
    nb = S // BLOCK
    kvw = N_KV_HEADS * HEAD_DIM

    def body(q_ref, kp, kc_, kn, vp, vc_, vn, do_ref, bias_ref, sink_ref, dbin_ref, dproj_in,
             dq_ref, dk_ref, dv_ref, db_ref, ds_ref):
        i = pl.program_id(0)

        @pl.when(i == 0)
        def _():
            dk_ref[...] = jnp.zeros_like(dk_ref)
            dv_ref[...] = jnp.zeros_like(dv_ref)
            db_ref[...] = dbin_ref[...]
            ds_ref[...] = jnp.zeros_like(ds_ref)
        dqs, dks, dvs = [], [], []
        for hk in range(N_KV_HEADS):
            q4s = _heads_rows(q_ref, hk) * SM_SCALE
            do4 = _heads_rows(do_ref, hk)
            kc = _kv_rows(kp, kc_, kn, hk)
            vc = _kv_rows(vp, vc_, vn, hk)
            pn, p_sink = _softmax_keys_on_rows(q4s, kc, _bias_cols(bias_ref, hk), sink_ref[hk:hk + 1, :])
            dpm = _nt(vc, do4)
            delta = jnp.sum(pn * dpm, axis=0, keepdims=True)
            dsc = pn * (dpm - delta)
            for g in range(GQA):
                db_ref[GQA * hk + g] += dsc[:, g * BLOCK:(g + 1) * BLOCK]
            ds_ref[hk:hk + 1, :] += -p_sink * delta
            dsb = dsc.astype(BF16)
            dq4 = _tn(dsb, kc) * SM_SCALE
            dqs += [dq4[g * BLOCK:(g + 1) * BLOCK, :] for g in range(GQA)]
            dks.append(_nn(dsb, q4s))
            dvs.append(_nn(pn.astype(BF16), do4))
        dq_ref[...] = jnp.concatenate(dqs, axis=1).astype(BF16)
        r0 = pl.multiple_of(i * BLOCK, BLOCK)
        dk_ref[pl.ds(r0, 3 * BLOCK), :] += jnp.concatenate(dks, axis=1)
        dv_ref[pl.ds(r0, 3 * BLOCK), :] += jnp.concatenate(dvs, axis=1)

    const = lambda shape: pl.BlockSpec(shape, lambda i: (0,) * len(shape))
    sink_shape = (N_KV_HEADS, GQA * BLOCK)
    return pl.pallas_call(
        body, name="attn_bwd", grid=(nb,),
        in_specs=_attn_specs(S) + [pl.BlockSpec((BLOCK, D_MODEL), lambda i: (i, 0)),
                                   _bias_spec(nb), const(sink_shape), const(TAB), ANY],
        out_specs=[pl.BlockSpec((BLOCK, D_MODEL), lambda i: (i, OFF_Q // D_MODEL)),
                   const((S + 2 * BLOCK, kvw)), const((S + 2 * BLOCK, kvw)), const(TAB), const(sink_shape)],
        out_shape=[jax.ShapeDtypeStruct(dproj.shape, dproj.dtype),
                   jax.ShapeDtypeStruct((S + 2 * BLOCK, kvw), F32), jax.ShapeDtypeStruct((S + 2 * BLOCK, kvw), F32),
                   jax.ShapeDtypeStruct(TAB, F32), jax.ShapeDtypeStruct(sink_shape, F32)],
        input_output_aliases={11: 0},
        compiler_params=_params("arbitrary"),
    )(*([proj] * 7), datt, bias_tabs, sink_rows, dbias_in, dproj)


def _kv_finish(dkp, dvp, dproj):
    S = dproj.shape[0]
    kvw = N_KV_HEADS * HEAD_DIM

    def body(dk_ref, dv_ref, dproj_in, o_ref):
        o_ref[:, 0:kvw] = dk_ref[pl.ds(BLOCK, S), :].astype(BF16)
        o_ref[:, kvw:2 * kvw] = dv_ref[pl.ds(BLOCK, S), :].astype(BF16)

    whole = pl.BlockSpec((S + 2 * BLOCK, kvw), lambda i: (0, 0))
    return pl.pallas_call(
        body, name="kv_finish", grid=(1,), in_specs=[whole, whole, ANY],
        out_specs=pl.BlockSpec((S, 2 * kvw), lambda i: (0, OFF_K // (2 * kvw))),
        out_shape=jax.ShapeDtypeStruct(dproj.shape, dproj.dtype),
        input_output_aliases={2: 0},
        compiler_params=_params("arbitrary"),
    )(dkp, dvp, dproj)


def _bucket_constants():
    half = N_BUCKETS // 2
    max_exact = half // 2
    qi = np.arange(BLOCK)[None, :]
    kj = np.arange(3 * BLOCK)[:, None]
    rel = kj - BLOCK - qi
    n = np.abs(rel)
    nf = np.maximum(n, 1).astype(np.float32)
    large = max_exact + (np.log(nf / np.float32(max_exact)) / np.float32(math.log(MAX_DISTANCE / max_exact))
                         * np.float32(half - max_exact)).astype(np.int32)
    large = np.minimum(large, half - 1)
    bucket = np.where(rel > 0, half, 0) + np.where(n < max_exact, n, large)
    onehot = (bucket.reshape(1, -1) == np.arange(N_BUCKETS)[:, None]).astype(np.float32)
    window = n <= WINDOW
    first = window & (kj >= BLOCK)
    last = window & (kj < 2 * BLOCK)
    masks = np.stack([np.where(v, 0.0, NEG_INF).astype(np.float32).reshape(-1) for v in (first, window, last)])
    return onehot, masks


def _bias_expand(rel_bias_t, onehot, masks):
    def body(r_ref, oh_ref, m_ref, o_ref):
        tab = jnp.dot(r_ref[...], oh_ref[...], preferred_element_type=F32, precision=lax.Precision.HIGHEST)
        for v in range(3):
            o_ref[v] = tab + m_ref[v:v + 1, :]

    return pl.pallas_call(
        body, name="bias_expand", out_shape=jax.ShapeDtypeStruct((3, N_HEADS, onehot.shape[1]), F32),
        compiler_params=_params(),
    )(rel_bias_t, onehot, masks)


def _bias_reduce(dtab, dsink_rows, onehot):
    def body(d_ref, s_ref, oh_ref, o_ref, so_ref):
        o_ref[...] = lax.dot_general(oh_ref[...], d_ref[...], (((1,), (1,)), ((), ())),
                                     preferred_element_type=F32, precision=lax.Precision.HIGHEST)
        so_ref[...] = jnp.sum(s_ref[...], axis=-1, keepdims=True)

    return pl.pallas_call(
        body, name="bias_reduce",
        out_shape=[jax.ShapeDtypeStruct((N_BUCKETS, N_HEADS), F32),
                   jax.ShapeDtypeStruct((dsink_rows.shape[0], 1), F32)],
        compiler_params=_params(),
    )(dtab, dsink_rows, onehot)


GROUPS = dict(mix=("w_in", "conv_w", "w_a_out", "w_pool", "w_attn_out", "w_o"), ffn=("w_gu", "w_down"))
WEIGHT_NAMES = GROUPS["mix"] + GROUPS["ffn"]


def _layer_fwd(l, x, weights_of, ps, g_mix, g_ffn, bias_tabs, sink_rows):
    W, token = weights_of(l, "mix", x)
    h, proj = _norm_matmul(x, g_mix, W["w_in"], "norm_proj", token)
    za = _conv_fwd(proj, W["conv_w"])
    p = _pool_fwd(proj)
    att = _attn_fwd(proj, bias_tabs, sink_rows)
    _, token = weights_of(l, "ffn on its way", att)
    ya, ypr, yt, merged, x2 = _mix_fwd(za, p, att, proj, x, W["w_a_out"], W["w_pool"], ps, W["w_attn_out"], W["w_o"],
                                       token)
    Wf, token = weights_of(l, "ffn", x2)
    h2, gu = _norm_matmul(x2, g_ffn, Wf["w_gu"], "norm_gu", token)
    _, token = weights_of(l, "next on its way", gu)
    act, x3 = _ffn_fwd(gu, x2, Wf["w_down"], token)
    saved = dict(x=x, h=h, proj=proj, za=za, p=p, att=att, ya=ya, ypr=ypr, yt=yt, merged=merged, x2=x2, h2=h2,
                 gu=gu, act=act, W={**W, **Wf}, sink_rows=sink_rows)
    return x3, saved, token


def _layer_bwd(l, dx3, sv, grads_to, ps, g_mix, g_ffn, bias_tabs, dbias, token):
    W, sink_rows = sv["W"], sv["sink_rows"]
    dgu = _ffn_bwd(dx3, sv["gu"], W["w_down"], token)
    g_w_down = _wgrad(sv["act"], dx3, "wgrad_down", tk=1408, tn=1024, token=token)
    g_w_gu = _wgrad(sv["h2"], dgu, "wgrad_gu", tk=1024, tn=1408)
    dx2, dg_ffn = _dgrad_norm_bwd(dgu, W["w_gu"], sv["x2"], g_ffn, dx3, "dgrad_gu", tk=1408)
    token = grads_to(l, "ffn", dict(w_gu=g_w_gu, w_down=g_w_down), dx2)
    dya, dyt, dyps, dza, datt, dp, dproj, dps = _mix_bwd(
        dx2, sv["ya"], sv["ypr"], sv["yt"], sv["proj"], ps, W["w_a_out"], W["w_pool"], W["w_attn_out"], W["w_o"], token)
    g_w_o = _wgrad(sv["merged"], dx2, "wgrad_sq_f32", tk=1024, tn=1024)
    g_w_a_out = _wgrad(sv["za"], dya, "wgrad_sq", tk=1024, tn=1024)
    g_w_attn_out = _wgrad(sv["att"], dyt, "wgrad_sq", tk=1024, tn=1024)
    g_w_pool = _wgrad_pool(sv["p"], dyps)
    dproj, g_conv = _conv_bwd(dza, sv["proj"], W["conv_w"], dproj)
    dproj = _pool_bwd(dp, dproj)
    dproj, dkp, dvp, dbias, dsink = _attn_bwd(sv["proj"], datt, bias_tabs, sink_rows, dbias, dproj)
    dproj = _kv_finish(dkp, dvp, dproj)
    g_w_in = _wgrad(sv["h"], dproj, "wgrad_in", tk=1024, tn=2176)
    token = grads_to(l, "mix", dict(w_in=g_w_in, conv_w=g_conv, w_a_out=g_w_a_out, w_pool=g_w_pool,
                                    w_attn_out=g_w_attn_out, w_o=g_w_o), dproj)
    dx, dg_mix = _dgrad_norm_bwd(dproj, W["w_in"], sv["x"], g_mix, dx2, "dgrad_in", tk=2176, token=token)
    grads_to(l, "done", None, dx)
    return dx, dict(pool_scale=dps, g_mix=dg_mix, g_ffn=dg_ffn, attn_sink=dsink), dbias, token


def _local_step(x, tgt, weights_of, grads_to, pool_scale, attn_sink, g_mix, g_ffn, rel_bias, g_final):
    onehot_np, masks_np = _bucket_constants()
    onehot, masks = jnp.asarray(onehot_np), jnp.asarray(masks_np)
    bias_tabs = _bias_expand(rel_bias.T, onehot, masks).reshape((3,) + TAB)
    saved = []
    for l in range(DEPTH):
        sink_rows = jnp.repeat(attn_sink[l], BLOCK).reshape(N_KV_HEADS, GQA * BLOCK)
        x, sv, token = _layer_fwd(l, x, weights_of, pool_scale[l:l + 1], g_mix[l:l + 1], g_ffn[l:l + 1], bias_tabs,
                                  sink_rows)
        saved.append(sv)
    loss, dx, dg_final = _loss_bwd(x, g_final.reshape(1, D_MODEL), tgt)
    dbias = jnp.zeros(TAB, F32)
    small = [None] * DEPTH
    for l in reversed(range(DEPTH)):
        dx, small[l], dbias, token = _layer_bwd(
            l, dx, saved[l], grads_to, pool_scale[l:l + 1], g_mix[l:l + 1], g_ffn[l:l + 1], bias_tabs, dbias, token)
    dsink_rows = jnp.concatenate([small[l]["attn_sink"].reshape(N_HEADS, BLOCK) for l in range(DEPTH)], axis=0)
    d_rel_bias, d_sink = _bias_reduce(dbias.reshape(N_HEADS, TAB_FLAT), dsink_rows, onehot)
    cat = lambda k: jnp.concatenate([small[l][k] for l in range(DEPTH)], axis=0)
    smalls = dict(pool_scale=cat("pool_scale"), g_mix=cat("g_mix"), g_ffn=cat("g_ffn"),
                  attn_sink=d_sink.reshape(DEPTH, N_HEADS), rel_bias=d_rel_bias, g_final=dg_final)
    return loss[0, 0], dx, smalls


SHARD_AXIS = dict(w_in=(1, IN_TOTAL // N_CHIPS), conv_w=(1, D_MODEL // N_CHIPS), w_a_out=(0, D_MODEL // N_CHIPS),
                  w_pool=(1, POOL_CG // N_CHIPS), w_attn_out=(0, D_MODEL // N_CHIPS), w_o=(0, D_MODEL // N_CHIPS),
                  w_gu=(1, 2 * D_FF // N_CHIPS), w_down=(0, D_FF // N_CHIPS))
HBM = pl.BlockSpec(memory_space=pltpu.HBM)
SEM = pl.BlockSpec(memory_space=pltpu.SEMAPHORE)
DATAFLOW = pltpu.SideEffectType.DATAFLOW_SIDE_EFFECTING
TOKEN = jax.ShapeDtypeStruct((8, 128), F32)


def _shard_of(ref, name, chip):
    axis, n = SHARD_AXIS[name]
    idx = [slice(None)] * len(ref.shape)
    idx[axis] = pl.ds(chip * n, n)
    return ref.at[tuple(idx)]


def _with_shard_axis(name, shape, size):
    axis, _ = SHARD_AXIS[name]
    s = list(shape)
    s[axis] = size
    return tuple(s)


HALF_AXIS = dict(w_in=0, conv_w=1, w_a_out=0, w_pool=1, w_attn_out=0, w_o=0, w_gu=0, w_down=0)


def _half_of_shard(ref, name, core):
    axis = HALF_AXIS[name]
    n = ref.shape[axis] // 2
    idx = [slice(None)] * len(ref.shape)
    idx[axis] = pl.ds(core * n, n)
    return ref.at[tuple(idx)]


def _half_in_full(ref, name, chip, core):
    saxis, n = SHARD_AXIS[name]
    haxis = HALF_AXIS[name]
    idx = [slice(None)] * len(ref.shape)
    if haxis == saxis:
        idx[saxis] = pl.ds(chip * n + core * (n // 2), n // 2)
    else:
        h = ref.shape[haxis] // 2
        idx[saxis] = pl.ds(chip * n, n)
        idx[haxis] = pl.ds(core * h, h)
    return ref.at[tuple(idx)]


def _on_each_device(fn):
    me = 2 * lax.axis_index("x") + lax.axis_index("y")
    c = lax.axis_index("c")
    for chip in range(N_CHIPS):
        for core in range(2):
            pl.when((me == chip) & (c == core))(functools.partial(fn, chip, core))


def _chip_peers(x, y):
    return [(1 - x, y), (x, 1 - y), (1 - x, 1 - y)]


RELATION_XOR = (2, 1, 3)


def _group_copies(kind, group, srcs, lands, send_sems, recv_sems, local_sems, chip, core):
    x, y, c = lax.axis_index("x"), lax.axis_index("y"), lax.axis_index("c")
    copies = []
    for t, name in enumerate(GROUPS[group]):
        for j, (px, py) in enumerate(_chip_peers(x, y)):
            if kind == "gather":
                src, dst = _half_of_shard(srcs[t], name, core), _half_in_full(lands[t], name, chip, core)
            else:
                src, dst = _shard_of(srcs[t], name, chip ^ RELATION_XOR[j]), lands[t].at[j]
            copies.append(pltpu.make_async_remote_copy(
                src_ref=src, dst_ref=dst, send_sem=send_sems.at[3 * t + j], recv_sem=recv_sems.at[3 * t + j],
                device_id=(px, py, c), device_id_type=MESH))
        if kind == "gather":
            src, dst = srcs[t], _shard_of(lands[t], name, chip)
        else:
            src, dst = _shard_of(srcs[t], name, chip), lands[t].at[N_CHIPS - 1]
        copies.append(pltpu.make_async_copy(src, dst, local_sems.at[t]))
    return copies


def _exchange_start(kind, group, srcs, land_shapes, after):
    nw = len(GROUPS[group])

    def body(*refs):
        srcs_r, lands_r = refs[:nw], refs[nw:2 * nw]
        send_sems, recv_sems, local_sems = refs[2 * nw + 1:2 * nw + 4]
        token = refs[-1]

        def issue(chip, core):
            for cp in _group_copies(kind, group, srcs_r, lands_r, send_sems, recv_sems, local_sems, chip, core):
                cp.start()
        _on_each_device(issue)
        token[...] = jnp.zeros_like(token)

    lands = [pltpu.with_memory_space_constraint(lax.empty(s.shape, s.dtype), pltpu.HBM) for s in land_shapes]
    srcs = [pltpu.with_memory_space_constraint(a, pltpu.HBM) for a in srcs]
    thru = [pltpu.HBM(a.shape, a.dtype) for a in srcs + lands]
    outs = pl.pallas_call(
        body, name=f"{kind}_{group}_start",
        in_specs=[HBM] * (2 * nw) + [ANY],
        out_specs=[SEM, SEM, SEM] + [HBM] * (2 * nw) + [pl.BlockSpec(memory_space=pltpu.VMEM)],
        out_shape=[pltpu.SemaphoreType.DMA((3 * nw,)), pltpu.SemaphoreType.DMA((3 * nw,)),
                   pltpu.SemaphoreType.DMA((nw,))] + thru + [TOKEN],
        input_output_aliases={t: 3 + t for t in range(2 * nw)},
        compiler_params=pltpu.CompilerParams(has_side_effects=DATAFLOW),
    )(*srcs, *lands, after)
    return dict(sems=outs[0:3], srcs=outs[3:3 + nw], lands=outs[3 + nw:3 + 2 * nw], token=outs[-1])


def _exchange_wait(kind, group, started, after):
    nw = len(GROUPS[group])

    def body(*refs):
        srcs_r, lands_r = refs[:nw], refs[nw:2 * nw]
        send_sems, recv_sems, local_sems = refs[2 * nw:2 * nw + 3]
        for cp in _group_copies(kind, group, srcs_r, lands_r, send_sems, recv_sems, local_sems, 0, 0):
            cp.wait()

    srcs, lands = list(started["srcs"]), list(started["lands"])
    outs = pl.pallas_call(
        body, name=f"{kind}_{group}_wait",
        in_specs=[HBM] * (2 * nw) + [SEM, SEM, SEM, ANY],
        out_specs=[HBM] * (2 * nw),
        out_shape=[pltpu.HBM(a.shape, a.dtype) for a in srcs + lands],
        input_output_aliases={t: t for t in range(2 * nw)},
        compiler_params=pltpu.CompilerParams(has_side_effects=DATAFLOW),
    )(*srcs, *lands, *started["sems"], after)
    return dict(zip(GROUPS[group], outs[nw:]))


def _gather_start(group, shards, after):
    names = GROUPS[group]
    shapes = [jax.ShapeDtypeStruct(_with_shard_axis(n, shards[n].shape, SHARD_AXIS[n][1] * N_CHIPS), shards[n].dtype)
              for n in names]
    return _exchange_start("gather", group, [shards[n] for n in names], shapes, after)


def _scatter_start(group, grads, after):
    names = GROUPS[group]
    shapes = [jax.ShapeDtypeStruct((N_CHIPS,) + _with_shard_axis(n, grads[n].shape, SHARD_AXIS[n][1]), grads[n].dtype)
              for n in names]
    return _exchange_start("scatter", group, [grads[n] for n in names], shapes, after)


def _sibling_exchange(parts):
    n = len(parts)

    def body(*refs):
        ins, outs = refs[:n], refs[n:2 * n]
        send_sems, recv_sems = refs[2 * n:]
        sibling = (lax.axis_index("x"), lax.axis_index("y"), 1 - lax.axis_index("c"))
        copies = [pltpu.make_async_remote_copy(src_ref=ins[t], dst_ref=outs[t], send_sem=send_sems.at[t],
                                               recv_sem=recv_sems.at[t], device_id=sibling, device_id_type=MESH)
                  for t in range(n)]
        for cp in copies:
            cp.start()
        for cp in copies:
            cp.wait()

    outs = pl.pallas_call(
        body, name="sibling_exchange", in_specs=[ANY] * n, out_specs=[ANY] * n,
        out_shape=[jax.ShapeDtypeStruct(p.shape, p.dtype) for p in parts],
        scratch_shapes=[pltpu.SemaphoreType.DMA((n,)), pltpu.SemaphoreType.DMA((n,))],
        compiler_params=pltpu.CompilerParams(has_side_effects=True),
    )(*parts)
    return list(outs)


def _fill_copies(group, lands, send_sems, recv_sems, chip, core):
    sibling = (lax.axis_index("x"), lax.axis_index("y"), 1 - lax.axis_index("c"))
    copies = []
    for t, name in enumerate(GROUPS[group]):
        for j in range(3):
            region = _half_in_full(lands[t], name, chip ^ RELATION_XOR[j], core)
            copies.append(pltpu.make_async_remote_copy(
                src_ref=region, dst_ref=region, send_sem=send_sems.at[3 * t + j], recv_sem=recv_sems.at[3 * t + j],
                device_id=sibling, device_id_type=MESH))
    return copies


def _gather_relay(group, started, after):
    nw = len(GROUPS[group])

    def body(*refs):
        srcs_r, lands_r = refs[:nw], refs[nw:2 * nw]
        send_sems, recv_sems, local_sems = refs[2 * nw:2 * nw + 3]
        fill_send, fill_recv, token = refs[-3:]
        for cp in _group_copies("gather", group, srcs_r, lands_r, send_sems, recv_sems, local_sems, 0, 0):
            cp.wait()

        def forward(chip, core):
            for cp in _fill_copies(group, lands_r, fill_send, fill_recv, chip, core):
                cp.start()
        _on_each_device(forward)
        token[...] = jnp.zeros_like(token)

    srcs, lands = list(started["srcs"]), list(started["lands"])
    outs = pl.pallas_call(
        body, name=f"gather_{group}_relay",
        in_specs=[HBM] * (2 * nw) + [SEM, SEM, SEM, ANY],
        out_specs=[HBM] * (2 * nw) + [SEM, SEM, pl.BlockSpec(memory_space=pltpu.VMEM)],
        out_shape=[pltpu.HBM(a.shape, a.dtype) for a in srcs + lands]
        + [pltpu.SemaphoreType.DMA((3 * nw,)), pltpu.SemaphoreType.DMA((3 * nw,)), TOKEN],
        input_output_aliases={t: t for t in range(2 * nw)},
        compiler_params=pltpu.CompilerParams(has_side_effects=DATAFLOW),
    )(*srcs, *lands, *started["sems"], after)
    return dict(lands=outs[nw:2 * nw], sems=outs[2 * nw:2 * nw + 2], token=outs[-1])


def _fill_wait(group, relayed, after):
    names = GROUPS[group]
    nw = len(names)

    def body(*refs):
        lands_r = refs[:nw]
        fill_send, fill_recv = refs[nw:nw + 2]
        for cp in _fill_copies(group, lands_r, fill_send, fill_recv, 0, 0):
            cp.wait()

    lands = list(relayed["lands"])
    outs = pl.pallas_call(
        body, name=f"gather_{group}_filled",
        in_specs=[HBM] * nw + [SEM, SEM, ANY], out_specs=[HBM] * nw,
        out_shape=[pltpu.HBM(a.shape, a.dtype) for a in lands],
        input_output_aliases={t: t for t in range(nw)},
        compiler_params=pltpu.CompilerParams(has_side_effects=DATAFLOW),
    )(*lands, *relayed["sems"], after)
    return dict(zip(names, outs))


N_DEV = 8


def _all_reduce_small(v, after):
    R, C = v.shape

    def body(v_ref, after_ref, o_ref, slots, send_sems, recv_sems):
        x, y, c = lax.axis_index("x"), lax.axis_index("y"), lax.axis_index("c")
        me = 4 * x + 2 * y + c
        slots[me] = v_ref[...]
        copies = []
        for k in range(1, N_DEV):
            peer = me ^ k
            cp = pltpu.make_async_remote_copy(
                src_ref=v_ref, dst_ref=slots.at[me], send_sem=send_sems.at[k - 1], recv_sem=recv_sems.at[k - 1],
                device_id=(peer // 4, (peer // 2) % 2, peer % 2), device_id_type=MESH)
            cp.start()
            copies.append(cp)
        for cp in copies:
            cp.wait()
        acc = slots[0]
        for k in range(1, N_DEV):
            acc = acc + slots[k]
        o_ref[...] = acc

    return pl.pallas_call(
        body, name="all_reduce_small", out_shape=jax.ShapeDtypeStruct((R, C), F32),
        in_specs=[pl.BlockSpec(memory_space=pltpu.VMEM), ANY], out_specs=pl.BlockSpec(memory_space=pltpu.VMEM),
        scratch_shapes=[pltpu.VMEM((N_DEV, R, C), F32), pltpu.SemaphoreType.DMA((N_DEV - 1,)),
                        pltpu.SemaphoreType.DMA((N_DEV - 1,))],
        compiler_params=pltpu.CompilerParams(has_side_effects=True),
    )(v, after)


def _as2d(shape):
    return (int(np.prod(shape[:-1])), shape[-1])


def _row_block(rows, cols, n_arrays):
    budget = V7X_VMEM_LIMIT // 2
    tr = rows
    while tr % 16 == 0 and 2 * n_arrays * tr * cols * 4 > budget:
        tr //= 2
    return tr


def _sum_slots(slots):
    _, R, C = slots.shape
    tr = _row_block(R, C, 5)

    def body(s_ref, o_ref):
        acc = s_ref[0].astype(F32)
        for k in range(1, N_CHIPS):
            acc = acc + s_ref[k].astype(F32)
        o_ref[...] = acc.astype(BF16)

    return pl.pallas_call(
        body, name="sum_slots", grid=(R // tr,),
        in_specs=[pl.BlockSpec((N_CHIPS, tr, C), lambda i: (0, i, 0))],
        out_specs=pl.BlockSpec((tr, C), lambda i: (i, 0)),
        out_shape=jax.ShapeDtypeStruct((R, C), BF16),
        compiler_params=_params("parallel"),
    )(slots)


def _adamw(l, w, m, v, g_a, g_b, prev):
    L, R, C = w.shape
    tr = _row_block(R, C, 9)
    c1 = 1.0 - ADAM_B1 ** ADAM_STEP
    c2 = 1.0 - ADAM_B2 ** ADAM_STEP

    def body(w_ref, m_ref, v_ref, a_ref, b_ref, *rest):
        g_ref, d_ref, nm_ref, nv_ref = rest[-4:]
        g = a_ref[...].astype(F32) + b_ref[...].astype(F32)
        nm = ADAM_B1 * m_ref[...] + (1.0 - ADAM_B1) * g
        nv = ADAM_B2 * v_ref[...] + (1.0 - ADAM_B2) * (g * g)
        g_ref[...] = g
        nm_ref[...] = nm
        nv_ref[...] = nv
        d_ref[...] = -ADAM_LR * ((nm / c1) / (jnp.sqrt(nv / c2) + ADAM_EPS) + ADAM_WD * w_ref[...])

    layer = pl.BlockSpec((None, tr, C), lambda i: (l, i, 0))
    blk = pl.BlockSpec((tr, C), lambda i: (i, 0))
    out = jax.ShapeDtypeStruct((L, R, C), F32)
    prev = [] if prev is None else list(prev)
    return pl.pallas_call(
        body, name="adamw", grid=(R // tr,), in_specs=[layer] * 3 + [blk] * 2 + [ANY] * len(prev),
        out_specs=[layer] * 4, out_shape=[out] * 4,
        input_output_aliases={5 + k: k for k in range(len(prev))},
        compiler_params=_params("parallel"),
    )(w, m, v, g_a, g_b, *prev)


SMALL_ROWS = 16


def _pack_small(pool_scale, g_mix, g_ffn, g_final, attn_sink, rel_bias):
    tail = jnp.concatenate([attn_sink.reshape(-1), rel_bias.reshape(-1)])
    tail = jnp.pad(tail, (0, D_MODEL - tail.shape[0])).reshape(1, D_MODEL)
    rows = jnp.concatenate([pool_scale, g_mix, g_ffn, g_final.reshape(1, D_MODEL), tail], axis=0)
    return jnp.pad(rows, ((0, SMALL_ROWS - rows.shape[0]), (0, 0)))


def _unpack_small(packed):
    n_sink = DEPTH * N_HEADS
    return dict(pool_scale=packed[0:4], g_mix=packed[4:8], g_ffn=packed[8:12], g_final=packed[12],
                attn_sink=packed[13, 0:n_sink].reshape(DEPTH, N_HEADS),
                rel_bias=packed[13, n_sink:n_sink + N_BUCKETS * N_HEADS].reshape(N_BUCKETS, N_HEADS))


def _group_shards(l, group, masters):
    out = {}
    for n in GROUPS[group]:
        w = masters[n][l]
        out[n] = jnp.pad(w.reshape(3, -1), ((0, 5), (0, 0))) if n == "conv_w" else w.astype(BF16)
    return out


def kernel(x, w_in, conv_w, w_a_out, w_pool, pool_scale, w_attn_out, attn_sink, w_o, g_mix, g_ffn, w_gu, w_down, rel_bias, g_final, loss_target, m_w_in, m_conv_w, m_w_a_out, m_w_pool, m_pool_scale, m_w_attn_out, m_attn_sink, m_w_o, m_g_mix, m_g_ffn, m_w_gu, m_w_down, m_rel_bias, m_g_final, v_w_in, v_conv_w, v_w_a_out, v_w_pool, v_pool_scale, v_w_attn_out, v_attn_sink, v_w_o, v_g_mix, v_g_ffn, v_w_gu, v_w_down, v_rel_bias, v_g_final):
    big = dict(w_in=(w_in, m_w_in, v_w_in), conv_w=(conv_w, m_conv_w, v_conv_w), w_a_out=(w_a_out, m_w_a_out, v_w_a_out),
               w_pool=(w_pool, m_w_pool, v_w_pool), w_attn_out=(w_attn_out, m_w_attn_out, v_w_attn_out),
               w_o=(w_o, m_w_o, v_w_o), w_gu=(w_gu, m_w_gu, v_w_gu), w_down=(w_down, m_w_down, v_w_down))

    big3 = {n: tuple(a.reshape((DEPTH,) + _as2d(a.shape[1:])) for a in big[n]) for n in WEIGHT_NAMES}
    masters = {n: big[n][0] for n in WEIGHT_NAMES}

    gathers = {(0, "mix"): _gather_start("mix", _group_shards(0, "mix", masters), rel_bias)}
    newest = {"token": gathers[0, "mix"]["token"]}
    masters = dict(zip(WEIGHT_NAMES, lax.optimization_barrier(
        (tuple(masters[n] for n in WEIGHT_NAMES), newest["token"]))[0]))

    relays = {}

    def weights_of(l, group, a):
        if group == "ffn on its way":
            relays[l, "ffn"] = _gather_relay("ffn", gathers.pop((l, "ffn")), a)
            return None, relays[l, "ffn"]["token"]
        if group == "next on its way":
            if l + 1 < DEPTH:
                relays[l + 1, "mix"] = _gather_relay("mix", gathers.pop((l + 1, "mix")), a)
                newest["token"] = relays[l + 1, "mix"]["token"]
            return None, newest["token"]
        if (l, group) not in relays:
            relays[l, group] = _gather_relay(group, gathers.pop((l, group)), a)
        W = _fill_wait(group, relays.pop((l, group)), a)
        if group == "mix":
            gathers[l, "ffn"] = _gather_start("ffn", _group_shards(l, "ffn", masters), W["w_in"])
            newest["token"] = gathers[l, "ffn"]["token"]
            if l + 1 < DEPTH:
                gathers[l + 1, "mix"] = _gather_start("mix", _group_shards(l + 1, "mix", masters), newest["token"])
                newest["token"] = gathers[l + 1, "mix"]["token"]
        return W, newest["token"]

    results = {n: None for n in WEIGHT_NAMES}
    scatters = {}

    def finish(l, after):
        slots = {}
        for group in GROUPS:
            slots.update(_exchange_wait("scatter", group, scatters.pop((l, group)), after))
        parts = [_sum_slots(slots[n].reshape((N_CHIPS,) + _as2d(slots[n].shape[1:]))) for n in WEIGHT_NAMES]
        others = _sibling_exchange(parts)
        for n, mine, other in zip(WEIGHT_NAMES, parts, others):
            if n == "conv_w":
                mine, other = mine[0:3], other[0:3]
            results[n] = _adamw(l, *big3[n], mine, other, results[n])

    def grads_to(l, group, wgrads, a):
        if group == "done":
            if l + 1 < DEPTH:
                finish(l + 1, a)
            return None
        scatters[l, group] = _scatter_start(group, wgrads, a)
        return scatters[l, group]["token"]

    loss, grad_x, smalls = _local_step(x[0], loss_target[0], weights_of, grads_to, pool_scale, attn_sink, g_mix, g_ffn,
                                       rel_bias, g_final)
    finish(0, results["w_down"][0])
    stacked = {n: [o.reshape(big[n][0].shape) for o in results[n]] for n in WEIGHT_NAMES}

    g_small = _all_reduce_small(_pack_small(smalls["pool_scale"], smalls["g_mix"], smalls["g_ffn"], smalls["g_final"],
                                            smalls["attn_sink"], smalls["rel_bias"]), results["w_in"][0])
    w_small = _pack_small(pool_scale, g_mix, g_ffn, g_final, attn_sink, rel_bias)
    m_small = _pack_small(m_pool_scale, m_g_mix, m_g_ffn, m_g_final, m_attn_sink, m_rel_bias)
    v_small = _pack_small(v_pool_scale, v_g_mix, v_g_ffn, v_g_final, v_attn_sink, v_rel_bias)
    small_out = [_unpack_small(o[0]) for o in
                 _adamw(0, w_small[None], m_small[None], v_small[None], g_small, jnp.zeros_like(g_small), None)]

    total_loss = lax.psum(loss, ("x", "y", "c"))

    order = ("w_in", "conv_w", "w_a_out", "w_pool", "pool_scale", "w_attn_out", "attn_sink", "w_o", "g_mix", "g_ffn",
             "w_gu", "w_down", "rel_bias", "g_final")
    outs = [total_loss, grad_x[None]]
    for k in range(4):
        for n in order:
            outs.append(stacked[n][k] if n in stacked else small_out[k][n])
    return tuple(outs)
```

```python
import functools
import math

import numpy as np
import jax
import jax.numpy as jnp
from jax import lax
from jax.experimental import pallas as pl
from jax.experimental.pallas import tpu as pltpu

F32 = jnp.float32
BF16 = jnp.bfloat16

D_MODEL = 1024
DEPTH = 4
N_HEADS = 16
N_KV_HEADS = 4
HEAD_DIM = 64
GQA = N_HEADS // N_KV_HEADS
WINDOW = 128
BLOCK = 128
N_BUCKETS = 32
MAX_DISTANCE = 128
POOL_GROUPS = 4
POOL_CG = D_MODEL // POOL_GROUPS
POOL_WINDOWS = (2, 4, 8, 16)
D_FF = 2816
IN_TOTAL = 8704
OFF_B, OFF_C, OFF_X, OFF_U, OFF_Q, OFF_K, OFF_V, OFF_GA, OFF_GP, OFF_GT = (
    0, 1024, 2048, 3072, 4096, 5120, 5376, 5632, 6656, 7680)
EPS = 1e-6
NEG_INF = -1e30
SM_SCALE = HEAD_DIM ** -0.5

ADAM_LR = 0.001
ADAM_B1 = 0.9
ADAM_B2 = 0.999
ADAM_EPS = 1e-08
ADAM_WD = 0.01
ADAM_STEP = 10

N_CHIPS = 4
HALO = 8
V7X_VMEM_LIMIT = 56 * 1024 * 1024
MESH = pl.DeviceIdType.MESH
ANY = pl.BlockSpec(memory_space=pl.ANY)


def _params(*sem):
    return pltpu.CompilerParams(dimension_semantics=tuple(sem) if sem else None,
                                vmem_limit_bytes=V7X_VMEM_LIMIT)


def _tile(n, pref):
    t = min(pref, n)
    while n % t or t % 128:
        t -= 128
    return t


def _nt(a, b):
    return lax.dot_general(a, b, (((1,), (1,)), ((), ())), preferred_element_type=F32)


def _tn(a, b):
    return lax.dot_general(a, b, (((0,), (0,)), ((), ())), preferred_element_type=F32)


def _nn(a, b):
    return jnp.dot(a, b, preferred_element_type=F32)


def _sigmoid(v):
    return pl.reciprocal(1.0 + jnp.exp(-v), approx=True)


def _norm_matmul(x, g, w, name, token):
    S, Dm = x.shape
    N = w.shape[1]
    tm, tn = _tile(S, 1024), _tile(N, N // 4)

    def body(x_ref, g_ref, w_ref, token_ref, h_ref, o_ref):
        @pl.when(pl.program_id(1) == 0)
        def _():
            xv = x_ref[...]
            r = lax.rsqrt(jnp.mean(xv * xv, axis=-1, keepdims=True) + EPS)
            h_ref[...] = (xv * r * g_ref[...]).astype(BF16)
        o_ref[...] = _nn(h_ref[...], w_ref[...]).astype(BF16)

    return pl.pallas_call(
        body, name=name, grid=(S // tm, N // tn),
        in_specs=[pl.BlockSpec((tm, Dm), lambda i, j: (i, 0)),
                  pl.BlockSpec((1, Dm), lambda i, j: (0, 0)),
                  pl.BlockSpec((Dm, tn), lambda i, j: (0, j)), ANY],
        out_specs=[pl.BlockSpec((tm, Dm), lambda i, j: (i, 0)),
                   pl.BlockSpec((tm, tn), lambda i, j: (i, j))],
        out_shape=[jax.ShapeDtypeStruct((S, Dm), BF16), jax.ShapeDtypeStruct((S, N), BF16)],
        compiler_params=_params("parallel", "arbitrary"),
    )(x, g, w, token)


CB = 128
CBW = 128


def _fill_padded(pad_ref, v, S):
    z = jnp.zeros((HALO, v.shape[1]), F32)
    pad_ref[pl.ds(0, HALO), :] = z
    pad_ref[pl.ds(S + HALO, HALO), :] = z
    pad_ref[pl.ds(HALO, S), :] = v


def _shifted(pad_ref, off, S):
    return pad_ref[pl.ds(HALO + off, S), :]


def _conv_fwd(proj, cw8):
    S = proj.shape[0]
    nblk = D_MODEL // CBW

    def body(b_ref, c_ref, x_ref, w_ref, o_ref, pad):
        u = c_ref[...].astype(F32) * x_ref[...].astype(F32)
        _fill_padded(pad, u, S)
        cv = w_ref[0:1, :] * _shifted(pad, -1, S) + w_ref[1:2, :] * u + w_ref[2:3, :] * _shifted(pad, 1, S)
        o_ref[...] = (b_ref[...].astype(F32) * cv).astype(BF16)

    col = lambda base: pl.BlockSpec((S, CBW), lambda j: (0, base // CBW + j))
    return pl.pallas_call(
        body, name="conv_fwd", grid=(nblk,),
        in_specs=[col(OFF_B), col(OFF_C), col(OFF_X), pl.BlockSpec((8, CBW), lambda j: (0, j))],
        out_specs=pl.BlockSpec((S, CBW), lambda j: (0, j)),
        out_shape=jax.ShapeDtypeStruct((S, D_MODEL), BF16),
        scratch_shapes=[pltpu.VMEM((S + 2 * HALO, CBW), F32)],
        compiler_params=_params("parallel"),
    )(proj, proj, proj, cw8)


def _pool_count(S, lo, hi):
    t = lax.broadcasted_iota(jnp.int32, (S, CBW), 0)
    return (jnp.minimum(t + hi, S - 1) - jnp.maximum(t - lo, 0) + 1).astype(F32)


def _pool_fwd(proj):
    S = proj.shape[0]
    nblk = D_MODEL // CBW
    per_group = POOL_CG // CBW

    def body(u_ref, o_ref, pad):
        u = u_ref[...].astype(F32)
        _fill_padded(pad, u, S)
        grp = pl.program_id(0) // per_group
        for gi, w in enumerate(POOL_WINDOWS):
            @pl.when(grp == gi)
            def _(w=w):
                lo, hi = w // 2, w - 1 - w // 2
                acc = _shifted(pad, -lo, S)
                for off in range(-lo + 1, hi + 1):
                    acc = acc + _shifted(pad, off, S)
                o_ref[...] = (acc / _pool_count(S, lo, hi) - u).astype(BF16)

    return pl.pallas_call(
        body, name="pool_fwd", grid=(nblk,),
        in_specs=[pl.BlockSpec((S, CBW), lambda j: (0, OFF_U // CBW + j))],
        out_specs=pl.BlockSpec((S, CBW), lambda j: (0, j)),
        out_shape=jax.ShapeDtypeStruct((S, D_MODEL), BF16),
        scratch_shapes=[pltpu.VMEM((S + 2 * HALO, CBW), F32)],
        compiler_params=_params("parallel"),
    )(proj)


def _attn_specs(S):
    nb = S // BLOCK
    kcol, vcol = OFF_K // (N_KV_HEADS * HEAD_DIM), OFF_V // (N_KV_HEADS * HEAD_DIM)
    kvw = N_KV_HEADS * HEAD_DIM
    prev = lambda i: jnp.maximum(i - 1, 0)
    nxt = lambda i: jnp.minimum(i + 1, nb - 1)
    return [
        pl.BlockSpec((BLOCK, D_MODEL), lambda i: (i, OFF_Q // D_MODEL)),
        pl.BlockSpec((BLOCK, kvw), lambda i: (prev(i), kcol)),
        pl.BlockSpec((BLOCK, kvw), lambda i: (i, kcol)),
        pl.BlockSpec((BLOCK, kvw), lambda i: (nxt(i), kcol)),
        pl.BlockSpec((BLOCK, kvw), lambda i: (prev(i), vcol)),
        pl.BlockSpec((BLOCK, kvw), lambda i: (i, vcol)),
        pl.BlockSpec((BLOCK, kvw), lambda i: (nxt(i), vcol)),
    ]


def _heads_rows(ref_or_val, hk):
    return jnp.concatenate(
        [ref_or_val[:, (GQA * hk + g) * HEAD_DIM:(GQA * hk + g + 1) * HEAD_DIM] for g in range(GQA)], axis=0)


def _kv_rows(p_ref, c_ref, n_ref, hk):
    sl = slice(hk * HEAD_DIM, (hk + 1) * HEAD_DIM)
    return jnp.concatenate([p_ref[:, sl], c_ref[:, sl], n_ref[:, sl]], axis=0)


def _bias_cols(bias_ref, hk):
    return jnp.concatenate([bias_ref[GQA * hk + g] for g in range(GQA)], axis=1)


def _softmax_keys_on_rows(q4s, kc, bias_blk, sink_row):
    s = _nt(kc, q4s) + bias_blk
    m = jnp.maximum(jnp.max(s, axis=0, keepdims=True), sink_row)
    p = jnp.exp(s - m)
    e_sink = jnp.exp(sink_row - m)
    inv = 1.0 / (jnp.sum(p, axis=0, keepdims=True) + e_sink)
    return p * inv, e_sink * inv


TAB = (N_HEADS, 3 * BLOCK, BLOCK)
TAB_FLAT = 3 * BLOCK * BLOCK


def _bias_spec(nb):
    return pl.BlockSpec((None,) + TAB, lambda i: (jnp.where(i == 0, 0, jnp.where(i == nb - 1, 2, 1)), 0, 0, 0))


def _attn_fwd(proj, bias_tabs, sink_rows):
    S = proj.shape[0]
    nb = S // BLOCK
    assert nb >= 2

    def body(q_ref, kp, kc_, kn, vp, vc_, vn, bias_ref, sink_ref, o_ref):
        outs = []
        for hk in range(N_KV_HEADS):
            q4s = _heads_rows(q_ref, hk) * SM_SCALE
            kc = _kv_rows(kp, kc_, kn, hk)
            vc = _kv_rows(vp, vc_, vn, hk)
            pn, _ = _softmax_keys_on_rows(q4s, kc, _bias_cols(bias_ref, hk), sink_ref[hk:hk + 1, :])
            o4 = _tn(pn.astype(BF16), vc)
            outs += [o4[g * BLOCK:(g + 1) * BLOCK, :] for g in range(GQA)]
        o_ref[...] = jnp.concatenate(outs, axis=1).astype(BF16)

    return pl.pallas_call(
        body, name="attn_fwd", grid=(nb,),
        in_specs=_attn_specs(S) + [_bias_spec(nb), pl.BlockSpec((N_KV_HEADS, GQA * BLOCK), lambda i: (0, 0))],
        out_specs=pl.BlockSpec((BLOCK, D_MODEL), lambda i: (i, 0)),
        out_shape=jax.ShapeDtypeStruct((S, D_MODEL), BF16),
        compiler_params=_params("parallel"),
    )(*([proj] * 7), bias_tabs, sink_rows)


GATE_HALF = D_MODEL // 2


def _gate_specs(tm):
    return [pl.BlockSpec((tm, GATE_HALF), lambda i, c=off // GATE_HALF + k: (i, c))
            for off in (OFF_GA, OFF_GP, OFF_GT) for k in (0, 1)]


def _gate(lo_ref, hi_ref):
    return _sigmoid(jnp.concatenate([lo_ref[...], hi_ref[...]], axis=1).astype(F32))


def _pool_mix(p, wp):
    return jnp.concatenate(
        [_nn(p[:, g * POOL_CG:(g + 1) * POOL_CG], wp[g]) for g in range(POOL_GROUPS)], axis=1)


def _mix_fwd(za, p, att, proj, x, wa, wp, ps, wt, wo, token):
    S = x.shape[0]
    tm = _tile(S, 256)

    def body(za_ref, p_ref, att_ref, ga0, ga1, gp0, gp1, gt0, gt1, x_ref, wa_ref, wp_ref, ps_ref, wt_ref, wo_ref,
             token_ref, ya_ref, yp_ref, yt_ref, mg_ref, x2_ref):
        ya = _nn(za_ref[...], wa_ref[...])
        ypr = _pool_mix(p_ref[...], wp_ref)
        yt = _nn(att_ref[...], wt_ref[...])
        merged = _gate(ga0, ga1) * ya + _gate(gp0, gp1) * (ypr * ps_ref[...]) + _gate(gt0, gt1) * yt
        mb = merged.astype(BF16)
        ya_ref[...] = ya.astype(BF16)
        yp_ref[...] = ypr.astype(BF16)
        yt_ref[...] = yt.astype(BF16)
        mg_ref[...] = mb
        x2_ref[...] = x_ref[...] + _nn(mb, wo_ref[...])

    row = lambda c=0: pl.BlockSpec((tm, D_MODEL), lambda i: (i, c))
    whole = lambda a: pl.BlockSpec(a.shape, lambda i: (0,) * a.ndim)
    act = jax.ShapeDtypeStruct((S, D_MODEL), BF16)
    return pl.pallas_call(
        body, name="mix_fwd", grid=(S // tm,),
        in_specs=[row(), row(), row()] + _gate_specs(tm)
        + [row(), whole(wa), whole(wp), whole(ps), whole(wt), whole(wo), ANY],
        out_specs=[row(), row(), row(), row(), row()],
        out_shape=[act, act, act, act, jax.ShapeDtypeStruct((S, D_MODEL), F32)],
        compiler_params=_params("parallel"),
    )(za, p, att, *([proj] * 6), x, wa, wp, ps, wt, wo, token)


def _ffn_fwd(gu, x2, wd, token):
    S = x2.shape[0]
    tm = _tile(S, 256)

    def body(g_ref, u_ref, x_ref, w_ref, token_ref, a_ref, o_ref):
        g = g_ref[...].astype(F32)
        a = (g * _sigmoid(g) * u_ref[...].astype(F32)).astype(BF16)
        a_ref[...] = a
        o_ref[...] = x_ref[...] + _nn(a, w_ref[...])

    return pl.pallas_call(
        body, name="ffn_fwd", grid=(S // tm,),
        in_specs=[pl.BlockSpec((tm, D_FF), lambda i: (i, 0)), pl.BlockSpec((tm, D_FF), lambda i: (i, 1)),
                  pl.BlockSpec((tm, D_MODEL), lambda i: (i, 0)), pl.BlockSpec((D_FF, D_MODEL), lambda i: (0, 0)), ANY],
        out_specs=[pl.BlockSpec((tm, D_FF), lambda i: (i, 0)), pl.BlockSpec((tm, D_MODEL), lambda i: (i, 0))],
        out_shape=[jax.ShapeDtypeStruct((S, D_FF), BF16), jax.ShapeDtypeStruct((S, D_MODEL), F32)],
        compiler_params=_params("parallel"),
    )(gu, gu, x2, wd, token)


def _loss_bwd(x, g, tgt):
    S, Dm = x.shape
    tm = _tile(S, 512)

    def body(x_ref, g_ref, t_ref, l_ref, dx_ref, dg_ref):
        @pl.when(pl.program_id(0) == 0)
        def _():
            l_ref[...] = jnp.zeros_like(l_ref)
            dg_ref[...] = jnp.zeros_like(dg_ref)
        xv, gv = x_ref[...], g_ref[...]
        r = lax.rsqrt(jnp.mean(xv * xv, axis=-1, keepdims=True) + EPS)
        n = xv * r
        err = n * gv - t_ref[...]
        l_ref[...] += 0.5 * jnp.sum(jnp.mean(err * err, axis=-1, keepdims=True), axis=0, keepdims=True)
        dy = err * (1.0 / Dm)
        dn = dy * gv
        dx_ref[...] = r * (dn - n * jnp.mean(dn * n, axis=-1, keepdims=True))
        dg_ref[...] += jnp.sum(dy * n, axis=0, keepdims=True)

    return pl.pallas_call(
        body, name="loss_bwd", grid=(S // tm,),
        in_specs=[pl.BlockSpec((tm, Dm), lambda i: (i, 0)), pl.BlockSpec((1, Dm), lambda i: (0, 0)),
                  pl.BlockSpec((tm, Dm), lambda i: (i, 0))],
        out_specs=[pl.BlockSpec((8, 128), lambda i: (0, 0)), pl.BlockSpec((tm, Dm), lambda i: (i, 0)),
                   pl.BlockSpec((1, Dm), lambda i: (0, 0))],
        out_shape=[jax.ShapeDtypeStruct((8, 128), F32), jax.ShapeDtypeStruct((S, Dm), F32),
                   jax.ShapeDtypeStruct((1, Dm), F32)],
        compiler_params=_params("arbitrary"),
    )(x, g, tgt)


def _ffn_bwd(dx3, gu, wd, token):
    S = dx3.shape[0]
    tm = _tile(S, 256)

    def body(d_ref, g_ref, u_ref, w_ref, token_ref, o_ref):
        dact = _nt(d_ref[...].astype(BF16), w_ref[...])
        g, u = g_ref[...].astype(F32), u_ref[...].astype(F32)
        sg = _sigmoid(g)
        o_ref[:, 0:D_FF] = (dact * u * (sg * (1.0 + g * (1.0 - sg)))).astype(BF16)
        o_ref[:, D_FF:2 * D_FF] = (dact * (g * sg)).astype(BF16)

    return pl.pallas_call(
        body, name="ffn_bwd", grid=(S // tm,),
        in_specs=[pl.BlockSpec((tm, D_MODEL), lambda i: (i, 0)),
                  pl.BlockSpec((tm, D_FF), lambda i: (i, 0)), pl.BlockSpec((tm, D_FF), lambda i: (i, 1)),
                  pl.BlockSpec((D_FF, D_MODEL), lambda i: (0, 0)), ANY],
        out_specs=pl.BlockSpec((tm, 2 * D_FF), lambda i: (i, 0)),
        out_shape=jax.ShapeDtypeStruct((S, 2 * D_FF), BF16),
        compiler_params=_params("parallel"),
    )(dx3, gu, gu, wd, token)


def _wgrad(a, b, name, tk=512, tn=512, out_dtype=BF16, token=None):
    S, K = a.shape
    N = b.shape[1]
    tk, tn, ts = _tile(K, tk), _tile(N, tn), _tile(S, 1024)
    n_s = S // ts
    extra = [] if token is None else [token]

    def body(a_ref, b_ref, *rest):
        o_ref, acc = rest[-2:]
        s = pl.program_id(2)

        @pl.when(s == 0)
        def _():
            acc[...] = jnp.zeros_like(acc)
        acc[...] += _tn(a_ref[...].astype(BF16), b_ref[...].astype(BF16))

        @pl.when(s == n_s - 1)
        def _():
            o_ref[...] = acc[...].astype(out_dtype)

    return pl.pallas_call(
        body, name=name, grid=(K // tk, N // tn, n_s),
        in_specs=[pl.BlockSpec((ts, tk), lambda k, n, s: (s, k)), pl.BlockSpec((ts, tn), lambda k, n, s: (s, n))]
        + [ANY] * len(extra),
        out_specs=pl.BlockSpec((tk, tn), lambda k, n, s: (k, n)),
        out_shape=jax.ShapeDtypeStruct((K, N), out_dtype),
        scratch_shapes=[pltpu.VMEM((tk, tn), F32)],
        compiler_params=_params("parallel", "parallel", "arbitrary"),
    )(a, b, *extra)


def _wgrad_pool(p, dyps):
    S = p.shape[0]
    ts = _tile(S, 4096)
    n_s = S // ts

    def body(a_ref, b_ref, o_ref, acc):
        s = pl.program_id(1)

        @pl.when(s == 0)
        def _():
            acc[...] = jnp.zeros_like(acc)
        acc[...] += _tn(a_ref[...], b_ref[...])

        @pl.when(s == n_s - 1)
        def _():
            o_ref[...] = acc[...].astype(BF16)

    return pl.pallas_call(
        body, name="wgrad_pool", grid=(POOL_GROUPS, n_s),
        in_specs=[pl.BlockSpec((ts, POOL_CG), lambda g, s: (s, g)), pl.BlockSpec((ts, POOL_CG), lambda g, s: (s, g))],
        out_specs=pl.BlockSpec((None, POOL_CG, POOL_CG), lambda g, s: (g, 0, 0)),
        out_shape=jax.ShapeDtypeStruct((POOL_GROUPS, POOL_CG, POOL_CG), BF16),
        scratch_shapes=[pltpu.VMEM((POOL_CG, POOL_CG), F32)],
        compiler_params=_params("parallel", "arbitrary"),
    )(p, dyps)


def _dgrad_norm_bwd(dy, w, x, g, dres, name, tk, token=None):
    S, K = dy.shape
    Dm = x.shape[1]
    tm, tk = _tile(S, 1024), _tile(K, tk)
    n_k = K // tk
    extra = [] if token is None else [token]

    def body(dy_ref, w_ref, x_ref, g_ref, r_ref, *rest):
        dx_ref, dg_ref, acc = rest[-3:]
        i, k = pl.program_id(0), pl.program_id(1)

        @pl.when((i == 0) & (k == 0))
        def _():
            dg_ref[...] = jnp.zeros_like(dg_ref)

        @pl.when(k == 0)
        def _():
            acc[...] = jnp.zeros_like(acc)
        acc[...] += _nt(dy_ref[...], w_ref[...])

        @pl.when(k == n_k - 1)
        def _():
            dh, xv = acc[...], x_ref[...]
            r = lax.rsqrt(jnp.mean(xv * xv, axis=-1, keepdims=True) + EPS)
            n = xv * r
            dn = dh * g_ref[...]
            dx_ref[...] = r_ref[...] + r * (dn - n * jnp.mean(dn * n, axis=-1, keepdims=True))
            dg_ref[...] += jnp.sum(dh * n, axis=0, keepdims=True)

    rowblk = pl.BlockSpec((tm, Dm), lambda i, k: (i, 0))
    vec = pl.BlockSpec((1, Dm), lambda i, k: (0, 0))
    return pl.pallas_call(
        body, name=name, grid=(S // tm, n_k),
        in_specs=[pl.BlockSpec((tm, tk), lambda i, k: (i, k)), pl.BlockSpec((Dm, tk), lambda i, k: (0, k)),
                  rowblk, vec, rowblk] + [ANY] * len(extra),
        out_specs=[rowblk, vec],
        out_shape=[jax.ShapeDtypeStruct((S, Dm), F32), jax.ShapeDtypeStruct((1, Dm), F32)],
        scratch_shapes=[pltpu.VMEM((tm, Dm), F32)],
        compiler_params=_params("arbitrary", "arbitrary"),
    )(dy, w, x, g, dres, *extra)


def _mix_bwd(dx2, ya, ypr, yt, proj, ps, wa, wp, wt, wo, token):
    S = dx2.shape[0]
    tm = _tile(S, 256)

    n_tiles = S // tm

    def body(dx_ref, ya_ref, yp_ref, yt_ref, ga0, ga1, gp0, gp1, gt0, gt1, ps_ref, wa_ref, wp_ref, wt_ref, wo_ref,
             token_ref, dya_ref, dyt_ref, dyps_ref, dza_ref, datt_ref, dp_ref, dproj_hbm, dps_ref, dgates, sem):
        i = pl.program_id(0)
        to_dproj = pltpu.make_async_copy(
            dgates, dproj_hbm.at[pl.ds(pl.multiple_of(i * tm, tm), tm), pl.ds(OFF_GA, 3 * D_MODEL)], sem)

        @pl.when(i == 0)
        def _():
            dps_ref[...] = jnp.zeros_like(dps_ref)
        dm = _nt(dx_ref[...].astype(BF16), wo_ref[...])
        sa, sp, st = _gate(ga0, ga1), _gate(gp0, gp1), _gate(gt0, gt1)
        psv = ps_ref[...]
        ypr_v = yp_ref[...].astype(F32)
        dya = (sa * dm).astype(BF16)
        dyt = (st * dm).astype(BF16)
        dyp = sp * dm
        dyps = (dyp * psv).astype(BF16)
        dya_ref[...] = dya
        dyt_ref[...] = dyt
        dyps_ref[...] = dyps
        dg = jnp.concatenate(
            [dm * ya_ref[...].astype(F32) * (sa * (1.0 - sa)), dm * (ypr_v * psv) * (sp * (1.0 - sp)),
             dm * yt_ref[...].astype(F32) * (st * (1.0 - st))], axis=1).astype(BF16)

        @pl.when(i > 0)
        def _():
            to_dproj.wait()
        dgates[...] = dg
        to_dproj.start()
        dps_ref[...] += jnp.sum(dyp * ypr_v, axis=0, keepdims=True)
        dza_ref[...] = _nt(dya, wa_ref[...]).astype(BF16)
        datt_ref[...] = _nt(dyt, wt_ref[...]).astype(BF16)
        dp_ref[...] = jnp.concatenate(
            [_nt(dyps[:, g * POOL_CG:(g + 1) * POOL_CG], wp_ref[g]) for g in range(POOL_GROUPS)], axis=1).astype(BF16)

        @pl.when(i == n_tiles - 1)
        def _():
            to_dproj.wait()

    row = lambda c=0: pl.BlockSpec((tm, D_MODEL), lambda i: (i, c))
    whole = lambda a: pl.BlockSpec(a.shape, lambda i: (0,) * a.ndim)
    act = jax.ShapeDtypeStruct((S, D_MODEL), BF16)
    return pl.pallas_call(
        body, name="mix_bwd", grid=(n_tiles,),
        in_specs=[row(), row(), row(), row()] + _gate_specs(tm)
        + [whole(ps), whole(wa), whole(wp), whole(wt), whole(wo), ANY],
        out_specs=[row()] * 6 + [ANY, pl.BlockSpec((1, D_MODEL), lambda i: (0, 0))],
        out_shape=[act] * 6 + [jax.ShapeDtypeStruct((S, IN_TOTAL), BF16), jax.ShapeDtypeStruct((1, D_MODEL), F32)],
        scratch_shapes=[pltpu.VMEM((tm, 3 * D_MODEL), BF16), pltpu.SemaphoreType.DMA],
        compiler_params=_params("arbitrary"),
    )(dx2, ya, ypr, yt, *([proj] * 6), ps, wa, wp, wt, wo, token)


def _conv_bwd(dza, proj, cw8, dproj):
    S = proj.shape[0]
    nblk = D_MODEL // CB

    def body(d_ref, b_ref, c_ref, x_ref, w_ref, dproj_in, dproj_hbm, dw_ref, pad_u, pad_d, parts, sems):
        cb = pl.program_id(0)
        to_dproj = [pltpu.make_async_copy(
            parts.at[k], dproj_hbm.at[:, pl.ds(pl.multiple_of(off + cb * CB, CB), CB)], sems.at[k])
            for k, off in enumerate((OFF_B, OFF_C, OFF_X))]
        c, xa = c_ref[...].astype(F32), x_ref[...].astype(F32)
        u = c * xa
        _fill_padded(pad_u, u, S)
        u_prev, u_next = _shifted(pad_u, -1, S), _shifted(pad_u, 1, S)
        cv = w_ref[0:1, :] * u_prev + w_ref[1:2, :] * u + w_ref[2:3, :] * u_next
        dza_v = d_ref[...].astype(F32)
        dcv = dza_v * b_ref[...].astype(F32)
        _fill_padded(pad_d, dcv, S)
        du = w_ref[0:1, :] * _shifted(pad_d, 1, S) + w_ref[1:2, :] * dcv + w_ref[2:3, :] * _shifted(pad_d, -1, S)

        @pl.when(cb > 0)
        def _():
            for cp in to_dproj:
                cp.wait()
        parts[0] = (dza_v * cv).astype(BF16)
        parts[1] = (du * xa).astype(BF16)
        parts[2] = (du * c).astype(BF16)
        for cp in to_dproj:
            cp.start()
        dw_ref[...] = jnp.concatenate(
            [jnp.sum(dcv * u_prev, axis=0, keepdims=True), jnp.sum(dcv * u, axis=0, keepdims=True),
             jnp.sum(dcv * u_next, axis=0, keepdims=True), jnp.zeros((5, CB), F32)], axis=0)

        @pl.when(cb == nblk - 1)
        def _():
            for cp in to_dproj:
                cp.wait()

    col = lambda base: pl.BlockSpec((S, CB), lambda cb: (0, base // CB + cb))
    taps = pl.BlockSpec((8, CB), lambda cb: (0, cb))
    return pl.pallas_call(
        body, name="conv_bwd", grid=(nblk,),
        in_specs=[col(0), col(OFF_B), col(OFF_C), col(OFF_X), taps, ANY],
        out_specs=[ANY, taps],
        out_shape=[jax.ShapeDtypeStruct(dproj.shape, dproj.dtype), jax.ShapeDtypeStruct((8, D_MODEL), F32)],
        scratch_shapes=[pltpu.VMEM((S + 2 * HALO, CB), F32), pltpu.VMEM((S + 2 * HALO, CB), F32),
                        pltpu.VMEM((3, S, CB), BF16), pltpu.SemaphoreType.DMA((3,))],
        input_output_aliases={5: 0},
        compiler_params=_params("arbitrary"),
    )(dza, proj, proj, proj, cw8, dproj)


def _pool_bwd(dp, dproj):
    S = dp.shape[0]
    nblk = D_MODEL // CBW
    per_group = POOL_CG // CBW

    def body(d_ref, dproj_in, o_ref, pad):
        d = d_ref[...].astype(F32)
        grp = pl.program_id(0) // per_group
        for gi, w in enumerate(POOL_WINDOWS):
            @pl.when(grp == gi)
            def _(w=w):
                lo, hi = w // 2, w - 1 - w // 2
                _fill_padded(pad, d / _pool_count(S, lo, hi), S)
                acc = _shifted(pad, -hi, S)
                for off in range(-hi + 1, lo + 1):
                    acc = acc + _shifted(pad, off, S)
                o_ref[...] = (acc - d).astype(BF16)

    return pl.pallas_call(
        body, name="pool_bwd", grid=(nblk,),
        in_specs=[pl.BlockSpec((S, CBW), lambda j: (0, j)), ANY],
        out_specs=pl.BlockSpec((S, CBW), lambda j: (0, OFF_U // CBW + j)),
        out_shape=jax.ShapeDtypeStruct(dproj.shape, dproj.dtype),
        scratch_shapes=[pltpu.VMEM((S + 2 * HALO, CBW), F32)],
        input_output_aliases={1: 0},
        compiler_params=_params("parallel"),
    )(dp, dproj)


def _attn_bwd(proj, datt, bias_tabs, sink_rows, dbias_in, dproj):
    S = proj.shape[0]
    nb = S // BLOCK
    kvw = N_KV_HEADS * HEAD_DIM

    def body(q_ref, kp, kc_, kn, vp, vc_, vn, do_ref, bias_ref, sink_ref, dbin_ref, dproj_in,
             dq_ref, dk_ref, dv_ref, db_ref, ds_ref):
        i = pl.program_id(0)

        @pl.when(i == 0)
        def _():
            dk_ref[...] = jnp.zeros_like(dk_ref)
            dv_ref[...] = jnp.zeros_like(dv_ref)
            db_ref[...] = dbin_ref[...]
            ds_ref[...] = jnp.zeros_like(ds_ref)
        dqs, dks, dvs = [], [], []
        for hk in range(N_KV_HEADS):
            q4s = _heads_rows(q_ref, hk) * SM_SCALE
            do4 = _heads_rows(do_ref, hk)
            kc = _kv_rows(kp, kc_, kn, hk)
            vc = _kv_rows(vp, vc_, vn, hk)
            pn, p_sink = _softmax_keys_on_rows(q4s, kc, _bias_cols(bias_ref, hk), sink_ref[hk:hk + 1, :])
            dpm = _nt(vc, do4)
            delta = jnp.sum(pn * dpm, axis=0, keepdims=True)
            dsc = pn * (dpm - delta)
            for g in range(GQA):
                db_ref[GQA * hk + g] += dsc[:, g * BLOCK:(g + 1) * BLOCK]
            ds_ref[hk:hk + 1, :] += -p_sink * delta
            dsb = dsc.astype(BF16)
            dq4 = _tn(dsb, kc) * SM_SCALE
            dqs += [dq4[g * BLOCK:(g + 1) * BLOCK, :] for g in range(GQA)]
            dks.append(_nn(dsb, q4s))
            dvs.append(_nn(pn.astype(BF16), do4))
        dq_ref[...] = jnp.concatenate(dqs, axis=1).astype(BF16)
        r0 = pl.multiple_of(i * BLOCK, BLOCK)
        dk_ref[pl.ds(r0, 3 * BLOCK), :] += jnp.concatenate(dks, axis=1)
        dv_ref[pl.ds(r0, 3 * BLOCK), :] += jnp.concatenate(dvs, axis=1)

    const = lambda shape: pl.BlockSpec(shape, lambda i: (0,) * len(shape))
    sink_shape = (N_KV_HEADS, GQA * BLOCK)
    return pl.pallas_call(
        body, name="attn_bwd", grid=(nb,),
        in_specs=_attn_specs(S) + [pl.BlockSpec((BLOCK, D_MODEL), lambda i: (i, 0)),
                                   _bias_spec(nb), const(sink_shape), const(TAB), ANY],
        out_specs=[pl.BlockSpec((BLOCK, D_MODEL), lambda i: (i, OFF_Q // D_MODEL)),
                   const((S + 2 * BLOCK, kvw)), const((S + 2 * BLOCK, kvw)), const(TAB), const(sink_shape)],
        out_shape=[jax.ShapeDtypeStruct(dproj.shape, dproj.dtype),
                   jax.ShapeDtypeStruct((S + 2 * BLOCK, kvw), F32), jax.ShapeDtypeStruct((S + 2 * BLOCK, kvw), F32),
                   jax.ShapeDtypeStruct(TAB, F32), jax.ShapeDtypeStruct(sink_shape, F32)],
        input_output_aliases={11: 0},
        compiler_params=_params("arbitrary"),
    )(*([proj] * 7), datt, bias_tabs, sink_rows, dbias_in, dproj)


def _kv_finish(dkp, dvp, dproj):
    S = dproj.shape[0]
    kvw = N_KV_HEADS * HEAD_DIM

    def body(dk_ref, dv_ref, dproj_in, o_ref):
        o_ref[:, 0:kvw] = dk_ref[pl.ds(BLOCK, S), :].astype(BF16)
        o_ref[:, kvw:2 * kvw] = dv_ref[pl.ds(BLOCK, S), :].astype(BF16)

    whole = pl.BlockSpec((S + 2 * BLOCK, kvw), lambda i: (0, 0))
    return pl.pallas_call(
        body, name="kv_finish", grid=(1,), in_specs=[whole, whole, ANY],
        out_specs=pl.BlockSpec((S, 2 * kvw), lambda i: (0, OFF_K // (2 * kvw))),
        out_shape=jax.ShapeDtypeStruct(dproj.shape, dproj.dtype),
        input_output_aliases={2: 0},
        compiler_params=_params("arbitrary"),
    )(dkp, dvp, dproj)


def _bucket_constants():
    half = N_BUCKETS // 2
    max_exact = half // 2
    qi = np.arange(BLOCK)[None, :]
    kj = np.arange(3 * BLOCK)[:, None]
    rel = kj - BLOCK - qi
    n = np.abs(rel)
    nf = np.maximum(n, 1).astype(np.float32)
    large = max_exact + (np.log(nf / np.float32(max_exact)) / np.float32(math.log(MAX_DISTANCE / max_exact))
                         * np.float32(half - max_exact)).astype(np.int32)
    large = np.minimum(large, half - 1)
    bucket = np.where(rel > 0, half, 0) + np.where(n < max_exact, n, large)
    onehot = (bucket.reshape(1, -1) == np.arange(N_BUCKETS)[:, None]).astype(np.float32)
    window = n <= WINDOW
    first = window & (kj >= BLOCK)
    last = window & (kj < 2 * BLOCK)
    masks = np.stack([np.where(v, 0.0, NEG_INF).astype(np.float32).reshape(-1) for v in (first, window, last)])
    return onehot, masks


def _bias_expand(rel_bias_t, onehot, masks):
    def body(r_ref, oh_ref, m_ref, o_ref):
        tab = jnp.dot(r_ref[...], oh_ref[...], preferred_element_type=F32, precision=lax.Precision.HIGHEST)
        for v in range(3):
            o_ref[v] = tab + m_ref[v:v + 1, :]

    return pl.pallas_call(
        body, name="bias_expand", out_shape=jax.ShapeDtypeStruct((3, N_HEADS, onehot.shape[1]), F32),
        compiler_params=_params(),
    )(rel_bias_t, onehot, masks)


def _bias_reduce(dtab, dsink_rows, onehot):
    def body(d_ref, s_ref, oh_ref, o_ref, so_ref):
        o_ref[...] = lax.dot_general(oh_ref[...], d_ref[...], (((1,), (1,)), ((), ())),
                                     preferred_element_type=F32, precision=lax.Precision.HIGHEST)
        so_ref[...] = jnp.sum(s_ref[...], axis=-1, keepdims=True)

    return pl.pallas_call(
        body, name="bias_reduce",
        out_shape=[jax.ShapeDtypeStruct((N_BUCKETS, N_HEADS), F32),
                   jax.ShapeDtypeStruct((dsink_rows.shape[0], 1), F32)],
        compiler_params=_params(),
    )(dtab, dsink_rows, onehot)


GROUPS = dict(mix=("w_in", "conv_w", "w_a_out", "w_pool", "w_attn_out", "w_o"), ffn=("w_gu", "w_down"))
WEIGHT_NAMES = GROUPS["mix"] + GROUPS["ffn"]


def _layer_fwd(l, x, weights_of, ps, g_mix, g_ffn, bias_tabs, sink_rows):
    W, token = weights_of(l, "mix", x)
    h, proj = _norm_matmul(x, g_mix, W["w_in"], "norm_proj", token)
    za = _conv_fwd(proj, W["conv_w"])
    p = _pool_fwd(proj)
    att = _attn_fwd(proj, bias_tabs, sink_rows)
    _, token = weights_of(l, "ffn on its way", att)
    ya, ypr, yt, merged, x2 = _mix_fwd(za, p, att, proj, x, W["w_a_out"], W["w_pool"], ps, W["w_attn_out"], W["w_o"],
                                       token)
    Wf, token = weights_of(l, "ffn", x2)
    h2, gu = _norm_matmul(x2, g_ffn, Wf["w_gu"], "norm_gu", token)
    _, token = weights_of(l, "next on its way", gu)
    act, x3 = _ffn_fwd(gu, x2, Wf["w_down"], token)
    saved = dict(x=x, h=h, proj=proj, za=za, p=p, att=att, ya=ya, ypr=ypr, yt=yt, merged=merged, x2=x2, h2=h2,
                 gu=gu, act=act, W={**W, **Wf}, sink_rows=sink_rows)
    return x3, saved, token


def _layer_bwd(l, dx3, sv, grads_to, ps, g_mix, g_ffn, bias_tabs, dbias, token):
    W, sink_rows = sv["W"], sv["sink_rows"]
    dgu = _ffn_bwd(dx3, sv["gu"], W["w_down"], token)
    g_w_down = _wgrad(sv["act"], dx3, "wgrad_down", tk=1408, tn=1024, token=token)
    g_w_gu = _wgrad(sv["h2"], dgu, "wgrad_gu", tk=1024, tn=1408)
    dx2, dg_ffn = _dgrad_norm_bwd(dgu, W["w_gu"], sv["x2"], g_ffn, dx3, "dgrad_gu", tk=1408)
    token = grads_to(l, "ffn", dict(w_gu=g_w_gu, w_down=g_w_down), dx2)
    dya, dyt, dyps, dza, datt, dp, dproj, dps = _mix_bwd(
        dx2, sv["ya"], sv["ypr"], sv["yt"], sv["proj"], ps, W["w_a_out"], W["w_pool"], W["w_attn_out"], W["w_o"], token)
    g_w_o = _wgrad(sv["merged"], dx2, "wgrad_sq_f32", tk=1024, tn=1024)
    g_w_a_out = _wgrad(sv["za"], dya, "wgrad_sq", tk=1024, tn=1024)
    g_w_attn_out = _wgrad(sv["att"], dyt, "wgrad_sq", tk=1024, tn=1024)
    g_w_pool = _wgrad_pool(sv["p"], dyps)
    dproj, g_conv = _conv_bwd(dza, sv["proj"], W["conv_w"], dproj)
    dproj = _pool_bwd(dp, dproj)
    dproj, dkp, dvp, dbias, dsink = _attn_bwd(sv["proj"], datt, bias_tabs, sink_rows, dbias, dproj)
    dproj = _kv_finish(dkp, dvp, dproj)
    g_w_in = _wgrad(sv["h"], dproj, "wgrad_in", tk=1024, tn=2176)
    token = grads_to(l, "mix", dict(w_in=g_w_in, conv_w=g_conv, w_a_out=g_w_a_out, w_pool=g_w_pool,
                                    w_attn_out=g_w_attn_out, w_o=g_w_o), dproj)
    dx, dg_mix = _dgrad_norm_bwd(dproj, W["w_in"], sv["x"], g_mix, dx2, "dgrad_in", tk=2176, token=token)
    grads_to(l, "done", None, dx)
    return dx, dict(pool_scale=dps, g_mix=dg_mix, g_ffn=dg_ffn, attn_sink=dsink), dbias, token


def _local_step(x, tgt, weights_of, grads_to, pool_scale, attn_sink, g_mix, g_ffn, rel_bias, g_final):
    onehot_np, masks_np = _bucket_constants()
    onehot, masks = jnp.asarray(onehot_np), jnp.asarray(masks_np)
    bias_tabs = _bias_expand(rel_bias.T, onehot, masks).reshape((3,) + TAB)
    saved = []
    for l in range(DEPTH):
        sink_rows = jnp.repeat(attn_sink[l], BLOCK).reshape(N_KV_HEADS, GQA * BLOCK)
        x, sv, token = _layer_fwd(l, x, weights_of, pool_scale[l:l + 1], g_mix[l:l + 1], g_ffn[l:l + 1], bias_tabs,
                                  sink_rows)
        saved.append(sv)
    loss, dx, dg_final = _loss_bwd(x, g_final.reshape(1, D_MODEL), tgt)
    dbias = jnp.zeros(TAB, F32)
    small = [None] * DEPTH
    for l in reversed(range(DEPTH)):
        dx, small[l], dbias, token = _layer_bwd(
            l, dx, saved[l], grads_to, pool_scale[l:l + 1], g_mix[l:l + 1], g_ffn[l:l + 1], bias_tabs, dbias, token)
    dsink_rows = jnp.concatenate([small[l]["attn_sink"].reshape(N_HEADS, BLOCK) for l in range(DEPTH)], axis=0)
    d_rel_bias, d_sink = _bias_reduce(dbias.reshape(N_HEADS, TAB_FLAT), dsink_rows, onehot)
    cat = lambda k: jnp.concatenate([small[l][k] for l in range(DEPTH)], axis=0)
    smalls = dict(pool_scale=cat("pool_scale"), g_mix=cat("g_mix"), g_ffn=cat("g_ffn"),
                  attn_sink=d_sink.reshape(DEPTH, N_HEADS), rel_bias=d_rel_bias, g_final=dg_final)
    return loss[0, 0], dx, smalls


SHARD_AXIS = dict(w_in=(1, IN_TOTAL // N_CHIPS), conv_w=(1, D_MODEL // N_CHIPS), w_a_out=(0, D_MODEL // N_CHIPS),
                  w_pool=(1, POOL_CG // N_CHIPS), w_attn_out=(0, D_MODEL // N_CHIPS), w_o=(0, D_MODEL // N_CHIPS),
                  w_gu=(1, 2 * D_FF // N_CHIPS), w_down=(0, D_FF // N_CHIPS))
HBM = pl.BlockSpec(memory_space=pltpu.HBM)
SEM = pl.BlockSpec(memory_space=pltpu.SEMAPHORE)
DATAFLOW = pltpu.SideEffectType.DATAFLOW_SIDE_EFFECTING
TOKEN = jax.ShapeDtypeStruct((8, 128), F32)


def _shard_of(ref, name, chip):
    axis, n = SHARD_AXIS[name]
    idx = [slice(None)] * len(ref.shape)
    idx[axis] = pl.ds(chip * n, n)
    return ref.at[tuple(idx)]


def _with_shard_axis(name, shape, size):
    axis, _ = SHARD_AXIS[name]
    s = list(shape)
    s[axis] = size
    return tuple(s)


HALF_AXIS = dict(w_in=0, conv_w=1, w_a_out=0, w_pool=1, w_attn_out=0, w_o=0, w_gu=0, w_down=0)


def _half_of_shard(ref, name, core):
    axis = HALF_AXIS[name]
    n = ref.shape[axis] // 2
    idx = [slice(None)] * len(ref.shape)
    idx[axis] = pl.ds(core * n, n)
    return ref.at[tuple(idx)]


def _half_in_full(ref, name, chip, core):
    saxis, n = SHARD_AXIS[name]
    haxis = HALF_AXIS[name]
    idx = [slice(None)] * len(ref.shape)
    if haxis == saxis:
        idx[saxis] = pl.ds(chip * n + core * (n // 2), n // 2)
    else:
        h = ref.shape[haxis] // 2
        idx[saxis] = pl.ds(chip * n, n)
        idx[haxis] = pl.ds(core * h, h)
    return ref.at[tuple(idx)]


def _on_each_device(fn):
    me = 2 * lax.axis_index("x") + lax.axis_index("y")
    c = lax.axis_index("c")
    for chip in range(N_CHIPS):
        for core in range(2):
            pl.when((me == chip) & (c == core))(functools.partial(fn, chip, core))


def _chip_peers(x, y):
    return [(1 - x, y), (x, 1 - y), (1 - x, 1 - y)]


RELATION_XOR = (2, 1, 3)


def _group_copies(kind, group, srcs, lands, send_sems, recv_sems, local_sems, chip, core):
    x, y, c = lax.axis_index("x"), lax.axis_index("y"), lax.axis_index("c")
    copies = []
    for t, name in enumerate(GROUPS[group]):
        for j, (px, py) in enumerate(_chip_peers(x, y)):
            if kind == "gather":
                src, dst = _half_of_shard(srcs[t], name, core), _half_in_full(lands[t], name, chip, core)
            else:
                src, dst = _shard_of(srcs[t], name, chip ^ RELATION_XOR[j]), lands[t].at[j]
            copies.append(pltpu.make_async_remote_copy(
                src_ref=src, dst_ref=dst, send_sem=send_sems.at[3 * t + j], recv_sem=recv_sems.at[3 * t + j],
                device_id=(px, py, c), device_id_type=MESH))
        if kind == "gather":
            src, dst = srcs[t], _shard_of(lands[t], name, chip)
        else:
            src, dst = _shard_of(srcs[t], name, chip), lands[t].at[N_CHIPS - 1]
        copies.append(pltpu.make_async_copy(src, dst, local_sems.at[t]))
    return copies


def _exchange_start(kind, group, srcs, land_shapes, after):
    nw = len(GROUPS[group])

    def body(*refs):
        srcs_r, lands_r = refs[:nw], refs[nw:2 * nw]
        send_sems, recv_sems, local_sems = refs[2 * nw + 1:2 * nw + 4]
        token = refs[-1]

        def issue(chip, core):
            for cp in _group_copies(kind, group, srcs_r, lands_r, send_sems, recv_sems, local_sems, chip, core):
                cp.start()
        _on_each_device(issue)
        token[...] = jnp.zeros_like(token)

    lands = [pltpu.with_memory_space_constraint(lax.empty(s.shape, s.dtype), pltpu.HBM) for s in land_shapes]
    srcs = [pltpu.with_memory_space_constraint(a, pltpu.HBM) for a in srcs]
    thru = [pltpu.HBM(a.shape, a.dtype) for a in srcs + lands]
    outs = pl.pallas_call(
        body, name=f"{kind}_{group}_start",
        in_specs=[HBM] * (2 * nw) + [ANY],
        out_specs=[SEM, SEM, SEM] + [HBM] * (2 * nw) + [pl.BlockSpec(memory_space=pltpu.VMEM)],
        out_shape=[pltpu.SemaphoreType.DMA((3 * nw,)), pltpu.SemaphoreType.DMA((3 * nw,)),
                   pltpu.SemaphoreType.DMA((nw,))] + thru + [TOKEN],
        input_output_aliases={t: 3 + t for t in range(2 * nw)},
        compiler_params=pltpu.CompilerParams(has_side_effects=DATAFLOW),
    )(*srcs, *lands, after)
    return dict(sems=outs[0:3], srcs=outs[3:3 + nw], lands=outs[3 + nw:3 + 2 * nw], token=outs[-1])


def _exchange_wait(kind, group, started, after):
    nw = len(GROUPS[group])

    def body(*refs):
        srcs_r, lands_r = refs[:nw], refs[nw:2 * nw]
        send_sems, recv_sems, local_sems = refs[2 * nw:2 * nw + 3]
        for cp in _group_copies(kind, group, srcs_r, lands_r, send_sems, recv_sems, local_sems, 0, 0):
            cp.wait()

    srcs, lands = list(started["srcs"]), list(started["lands"])
    outs = pl.pallas_call(
        body, name=f"{kind}_{group}_wait",
        in_specs=[HBM] * (2 * nw) + [SEM, SEM, SEM, ANY],
        out_specs=[HBM] * (2 * nw),
        out_shape=[pltpu.HBM(a.shape, a.dtype) for a in srcs + lands],
        input_output_aliases={t: t for t in range(2 * nw)},
        compiler_params=pltpu.CompilerParams(has_side_effects=DATAFLOW),
    )(*srcs, *lands, *started["sems"], after)
    return dict(zip(GROUPS[group], outs[nw:]))


def _gather_start(group, shards, after):
    names = GROUPS[group]
    shapes = [jax.ShapeDtypeStruct(_with_shard_axis(n, shards[n].shape, SHARD_AXIS[n][1] * N_CHIPS), shards[n].dtype)
              for n in names]
    return _exchange_start("gather", group, [shards[n] for n in names], shapes, after)


def _scatter_start(group, grads, after):
    names = GROUPS[group]
    shapes = [jax.ShapeDtypeStruct((N_CHIPS,) + _with_shard_axis(n, grads[n].shape, SHARD_AXIS[n][1]), grads[n].dtype)
              for n in names]
    return _exchange_start("scatter", group, [grads[n] for n in names], shapes, after)


def _sibling_exchange(parts):
    n = len(parts)

    def body(*refs):
        ins, outs = refs[:n], refs[n:2 * n]
        send_sems, recv_sems = refs[2 * n:]
        sibling = (lax.axis_index("x"), lax.axis_index("y"), 1 - lax.axis_index("c"))
        copies = [pltpu.make_async_remote_copy(src_ref=ins[t], dst_ref=outs[t], send_sem=send_sems.at[t],
                                               recv_sem=recv_sems.at[t], device_id=sibling, device_id_type=MESH)
                  for t in range(n)]
        for cp in copies:
            cp.start()
        for cp in copies:
            cp.wait()

    outs = pl.pallas_call(
        body, name="sibling_exchange", in_specs=[ANY] * n, out_specs=[ANY] * n,
        out_shape=[jax.ShapeDtypeStruct(p.shape, p.dtype) for p in parts],
        scratch_shapes=[pltpu.SemaphoreType.DMA((n,)), pltpu.SemaphoreType.DMA((n,))],
        compiler_params=pltpu.CompilerParams(has_side_effects=True),
    )(*parts)
    return list(outs)


def _fill_copies(group, lands, send_sems, recv_sems, chip, core):
    sibling = (lax.axis_index("x"), lax.axis_index("y"), 1 - lax.axis_index("c"))
    copies = []
    for t, name in enumerate(GROUPS[group]):
        for j in range(3):
            region = _half_in_full(lands[t], name, chip ^ RELATION_XOR[j], core)
            copies.append(pltpu.make_async_remote_copy(
                src_ref=region, dst_ref=region, send_sem=send_sems.at[3 * t + j], recv_sem=recv_sems.at[3 * t + j],
                device_id=sibling, device_id_type=MESH))
    return copies


def _gather_relay(group, started, after):
    nw = len(GROUPS[group])

    def body(*refs):
        srcs_r, lands_r = refs[:nw], refs[nw:2 * nw]
        send_sems, recv_sems, local_sems = refs[2 * nw:2 * nw + 3]
        fill_send, fill_recv, token = refs[-3:]
        for cp in _group_copies("gather", group, srcs_r, lands_r, send_sems, recv_sems, local_sems, 0, 0):
            cp.wait()

        def forward(chip, core):
            for cp in _fill_copies(group, lands_r, fill_send, fill_recv, chip, core):
                cp.start()
        _on_each_device(forward)
        token[...] = jnp.zeros_like(token)

    srcs, lands = list(started["srcs"]), list(started["lands"])
    outs = pl.pallas_call(
        body, name=f"gather_{group}_relay",
        in_specs=[HBM] * (2 * nw) + [SEM, SEM, SEM, ANY],
        out_specs=[HBM] * (2 * nw) + [SEM, SEM, pl.BlockSpec(memory_space=pltpu.VMEM)],
        out_shape=[pltpu.HBM(a.shape, a.dtype) for a in srcs + lands]
        + [pltpu.SemaphoreType.DMA((3 * nw,)), pltpu.SemaphoreType.DMA((3 * nw,)), TOKEN],
        input_output_aliases={t: t for t in range(2 * nw)},
        compiler_params=pltpu.CompilerParams(has_side_effects=DATAFLOW),
    )(*srcs, *lands, *started["sems"], after)
    return dict(lands=outs[nw:2 * nw], sems=outs[2 * nw:2 * nw + 2], token=outs[-1])


def _fill_wait(group, relayed, after):
    names = GROUPS[group]
    nw = len(names)

    def body(*refs):
        lands_r = refs[:nw]
        fill_send, fill_recv = refs[nw:nw + 2]
        for cp in _fill_copies(group, lands_r, fill_send, fill_recv, 0, 0):
            cp.wait()

    lands = list(relayed["lands"])
    outs = pl.pallas_call(
        body, name=f"gather_{group}_filled",
        in_specs=[HBM] * nw + [SEM, SEM, ANY], out_specs=[HBM] * nw,
        out_shape=[pltpu.HBM(a.shape, a.dtype) for a in lands],
        input_output_aliases={t: t for t in range(nw)},
        compiler_params=pltpu.CompilerParams(has_side_effects=DATAFLOW),
    )(*lands, *relayed["sems"], after)
    return dict(zip(names, outs))


N_DEV = 8


def _all_reduce_small(v, after):
    R, C = v.shape

    def body(v_ref, after_ref, o_ref, slots, send_sems, recv_sems):
        x, y, c = lax.axis_index("x"), lax.axis_index("y"), lax.axis_index("c")
        me = 4 * x + 2 * y + c
        slots[me] = v_ref[...]
        copies = []
        for k in range(1, N_DEV):
            peer = me ^ k
            cp = pltpu.make_async_remote_copy(
                src_ref=v_ref, dst_ref=slots.at[me], send_sem=send_sems.at[k - 1], recv_sem=recv_sems.at[k - 1],
                device_id=(peer // 4, (peer // 2) % 2, peer % 2), device_id_type=MESH)
            cp.start()
            copies.append(cp)
        for cp in copies:
            cp.wait()
        acc = slots[0]
        for k in range(1, N_DEV):
            acc = acc + slots[k]
        o_ref[...] = acc

    return pl.pallas_call(
        body, name="all_reduce_small", out_shape=jax.ShapeDtypeStruct((R, C), F32),
        in_specs=[pl.BlockSpec(memory_space=pltpu.VMEM), ANY], out_specs=pl.BlockSpec(memory_space=pltpu.VMEM),
        scratch_shapes=[pltpu.VMEM((N_DEV, R, C), F32), pltpu.SemaphoreType.DMA((N_DEV - 1,)),
                        pltpu.SemaphoreType.DMA((N_DEV - 1,))],
        compiler_params=pltpu.CompilerParams(has_side_effects=True),
    )(v, after)


def _as2d(shape):
    return (int(np.prod(shape[:-1])), shape[-1])


def _row_block(rows, cols, n_arrays):
    budget = V7X_VMEM_LIMIT // 2
    tr = rows
    while tr % 16 == 0 and 2 * n_arrays * tr * cols * 4 > budget:
        tr //= 2
    return tr


def _sum_slots(slots):
    _, R, C = slots.shape
    tr = _row_block(R, C, 5)

    def body(s_ref, o_ref):
        acc = s_ref[0].astype(F32)
        for k in range(1, N_CHIPS):
            acc = acc + s_ref[k].astype(F32)
        o_ref[...] = acc.astype(BF16)

    return pl.pallas_call(
        body, name="sum_slots", grid=(R // tr,),
        in_specs=[pl.BlockSpec((N_CHIPS, tr, C), lambda i: (0, i, 0))],
        out_specs=pl.BlockSpec((tr, C), lambda i: (i, 0)),
        out_shape=jax.ShapeDtypeStruct((R, C), BF16),
        compiler_params=_params("parallel"),
    )(slots)


def _adamw(l, w, m, v, g_a, g_b, prev):
    L, R, C = w.shape
    tr = _row_block(R, C, 9)
    c1 = 1.0 - ADAM_B1 ** ADAM_STEP
    c2 = 1.0 - ADAM_B2 ** ADAM_STEP

    def body(w_ref, m_ref, v_ref, a_ref, b_ref, *rest):
        g_ref, d_ref, nm_ref, nv_ref = rest[-4:]
        g = a_ref[...].astype(F32) + b_ref[...].astype(F32)
        nm = ADAM_B1 * m_ref[...] + (1.0 - ADAM_B1) * g
        nv = ADAM_B2 * v_ref[...] + (1.0 - ADAM_B2) * (g * g)
        g_ref[...] = g
        nm_ref[...] = nm
        nv_ref[...] = nv
        d_ref[...] = -ADAM_LR * ((nm / c1) / (jnp.sqrt(nv / c2) + ADAM_EPS) + ADAM_WD * w_ref[...])

    layer = pl.BlockSpec((None, tr, C), lambda i: (l, i, 0))
    blk = pl.BlockSpec((tr, C), lambda i: (i, 0))
    out = jax.ShapeDtypeStruct((L, R, C), F32)
    prev = [] if prev is None else list(prev)
    return pl.pallas_call(
        body, name="adamw", grid=(R // tr,), in_specs=[layer] * 3 + [blk] * 2 + [ANY] * len(prev),
        out_specs=[layer] * 4, out_shape=[out] * 4,
        input_output_aliases={5 + k: k for k in range(len(prev))},
        compiler_params=_params("parallel"),
    )(w, m, v, g_a, g_b, *prev)


SMALL_ROWS = 16


def _pack_small(pool_scale, g_mix, g_ffn, g_final, attn_sink, rel_bias):
    tail = jnp.concatenate([attn_sink.reshape(-1), rel_bias.reshape(-1)])
    tail = jnp.pad(tail, (0, D_MODEL - tail.shape[0])).reshape(1, D_MODEL)
    rows = jnp.concatenate([pool_scale, g_mix, g_ffn, g_final.reshape(1, D_MODEL), tail], axis=0)
    return jnp.pad(rows, ((0, SMALL_ROWS - rows.shape[0]), (0, 0)))


def _unpack_small(packed):
    n_sink = DEPTH * N_HEADS
    return dict(pool_scale=packed[0:4], g_mix=packed[4:8], g_ffn=packed[8:12], g_final=packed[12],
                attn_sink=packed[13, 0:n_sink].reshape(DEPTH, N_HEADS),
                rel_bias=packed[13, n_sink:n_sink + N_BUCKETS * N_HEADS].reshape(N_BUCKETS, N_HEADS))


def _group_shards(l, group, masters):
    out = {}
    for n in GROUPS[group]:
        w = masters[n][l]
        out[n] = jnp.pad(w.reshape(3, -1), ((0, 5), (0, 0))) if n == "conv_w" else w.astype(BF16)
    return out


def kernel(x, w_in, conv_w, w_a_out, w_pool, pool_scale, w_attn_out, attn_sink, w_o, g_mix, g_ffn, w_gu, w_down, rel_bias, g_final, loss_target, m_w_in, m_conv_w, m_w_a_out, m_w_pool, m_pool_scale, m_w_attn_out, m_attn_sink, m_w_o, m_g_mix, m_g_ffn, m_w_gu, m_w_down, m_rel_bias, m_g_final, v_w_in, v_conv_w, v_w_a_out, v_w_pool, v_pool_scale, v_w_attn_out, v_attn_sink, v_w_o, v_g_mix, v_g_ffn, v_w_gu, v_w_down, v_rel_bias, v_g_final):
    big = dict(w_in=(w_in, m_w_in, v_w_in), conv_w=(conv_w, m_conv_w, v_conv_w), w_a_out=(w_a_out, m_w_a_out, v_w_a_out),
               w_pool=(w_pool, m_w_pool, v_w_pool), w_attn_out=(w_attn_out, m_w_attn_out, v_w_attn_out),
               w_o=(w_o, m_w_o, v_w_o), w_gu=(w_gu, m_w_gu, v_w_gu), w_down=(w_down, m_w_down, v_w_down))

    big3 = {n: tuple(a.reshape((DEPTH,) + _as2d(a.shape[1:])) for a in big[n]) for n in WEIGHT_NAMES}
    masters = {n: big[n][0] for n in WEIGHT_NAMES}

    gathers = {(0, "mix"): _gather_start("mix", _group_shards(0, "mix", masters), rel_bias)}
    newest = {"token": gathers[0, "mix"]["token"]}
    masters = dict(zip(WEIGHT_NAMES, lax.optimization_barrier(
        (tuple(masters[n] for n in WEIGHT_NAMES), newest["token"]))[0]))

    relays = {}

    def weights_of(l, group, a):
        if group == "ffn on its way":
            relays[l, "ffn"] = _gather_relay("ffn", gathers.pop((l, "ffn")), a)
            return None, relays[l, "ffn"]["token"]
        if group == "next on its way":
            if l + 1 < DEPTH:
                relays[l + 1, "mix"] = _gather_relay("mix", gathers.pop((l + 1, "mix")), a)
                newest["token"] = relays[l + 1, "mix"]["token"]
            return None, newest["token"]
        if (l, group) not in relays:
            relays[l, group] = _gather_relay(group, gathers.pop((l, group)), a)
        W = _fill_wait(group, relays.pop((l, group)), a)
        if group == "mix":
            gathers[l, "ffn"] = _gather_start("ffn", _group_shards(l, "ffn", masters), W["w_in"])
            newest["token"] = gathers[l, "ffn"]["token"]
            if l + 1 < DEPTH:
                gathers[l + 1, "mix"] = _gather_start("mix", _group_shards(l + 1, "mix", masters), newest["token"])
                newest["token"] = gathers[l + 1, "mix"]["token"]
        return W, newest["token"]

    results = {n: None for n in WEIGHT_NAMES}
    scatters = {}

    def finish(l, after):
        slots = {}
        for group in GROUPS:
            slots.update(_exchange_wait("scatter", group, scatters.pop((l, group)), after))
        parts = [_sum_slots(slots[n].reshape((N_CHIPS,) + _as2d(slots[n].shape[1:]))) for n in WEIGHT_NAMES]
        others = _sibling_exchange(parts)
        for n, mine, other in zip(WEIGHT_NAMES, parts, others):
            if n == "conv_w":
                mine, other = mine[0:3], other[0:3]
            results[n] = _adamw(l, *big3[n], mine, other, results[n])

    def grads_to(l, group, wgrads, a):
        if group == "done":
            if l + 1 < DEPTH:
                finish(l + 1, a)
            return None
        scatters[l, group] = _scatter_start(group, wgrads, a)
        return scatters[l, group]["token"]

    loss, grad_x, smalls = _local_step(x[0], loss_target[0], weights_of, grads_to, pool_scale, attn_sink, g_mix, g_ffn,
                                       rel_bias, g_final)
    finish(0, results["w_down"][0])
    stacked = {n: [o.reshape(big[n][0].shape) for o in results[n]] for n in WEIGHT_NAMES}

    g_small = _all_reduce_small(_pack_small(smalls["pool_scale"], smalls["g_mix"], smalls["g_ffn"], smalls["g_final"],
                                            smalls["attn_sink"], smalls["rel_bias"]), results["w_in"][0])
    w_small = _pack_small(pool_scale, g_mix, g_ffn, g_final, attn_sink, rel_bias)
    m_small = _pack_small(m_pool_scale, m_g_mix, m_g_ffn, m_g_final, m_attn_sink, m_rel_bias)
    v_small = _pack_small(v_pool_scale, v_g_mix, v_g_ffn, v_g_final, v_attn_sink, v_rel_bias)
    small_out = [_unpack_small(o[0]) for o in
                 _adamw(0, w_small[None], m_small[None], v_small[None], g_small, jnp.zeros_like(g_small), None)]

    total_loss = lax.psum(loss, ("x", "y", "c"))

    order = ("w_in", "conv_w", "w_a_out", "w_pool", "pool_scale", "w_attn_out", "attn_sink", "w_o", "g_mix", "g_ffn",
             "w_gu", "w_down", "rel_bias", "g_final")
    outs = [total_loss, grad_x[None]]
    for k in range(4):
        for n in order:
            outs.append(stacked[n][k] if n in stacked else small_out[k][n])
    return tuple(outs)
```
